```python
import math
import jax, jax.numpy as jnp
from jax import lax
import numpy as np

D_MODEL = 1024
BATCH = 8
SEQ = 2048
DEPTH = 2
DEC_BATCH = 32
DEC_SEQ = 8
PAST_LEN = 16384
PAGE_SIZE = 128

N_PAIRS = DEPTH // 2
D_A = D_MODEL // 2
CONV_A = 31
D_INNER = D_MODEL
SSM_HEADDIM = 64
SSM_HEADS = D_INNER // SSM_HEADDIM
SSM_GROUPS = 2
SSM_STATE = 128
SSM_CONV = 4
CONV_DIM = D_INNER + 2 * SSM_GROUPS * SSM_STATE
SSD_CHUNK = 128
HEAD_DIM = 64
N_HEADS = (D_MODEL // 2) // HEAD_DIM
N_KV = 2
ROT_DIM = HEAD_DIM // 4
ROPE_THETA = 500000.0
IDX_HEADS = 4
IDX_DIM = 64
TOPK_MAX = 256
Q_BLOCK = 128
D_D = D_MODEL // 2
CONV_D = 3
D_FF = 2816
N_EXPERTS = 8
TOP_K = 2
EPS = 1e-6
NEG = -1e30

E_SPLIT = [D_A, D_A, D_INNER, CONV_DIM, SSM_HEADS]
O_SPLIT = [N_HEADS * HEAD_DIM, N_KV * HEAD_DIM, N_KV * HEAD_DIM, IDX_HEADS * IDX_DIM, IDX_DIM, IDX_HEADS, D_D, D_D, D_D]
E_IN = sum(E_SPLIT)
O_IN = sum(O_SPLIT)
E_MIX = D_A + D_INNER
O_MIX = N_HEADS * HEAD_DIM + D_D

kernel_name = 'hybrid_conformer_ssd_dsa_decoder_step'

F32 = jnp.float32


def split_cols(z, sizes):
    offs = np.cumsum(sizes)[:-1].tolist()
    return jnp.split(z, offs, axis=-1)


def rms_norm(x, g):
    xf = x.astype(F32)
    y = xf * lax.rsqrt(jnp.mean(xf * xf, axis=-1, keepdims=True) + EPS)
    return (y * g.astype(F32)).astype(x.dtype)


def layer_norm(x, g, b):
    xf = x.astype(F32)
    mu = jnp.mean(xf, axis=-1, keepdims=True)
    var = jnp.mean(jnp.square(xf - mu), axis=-1, keepdims=True)
    return ((xf - mu) * lax.rsqrt(var + EPS) * g.astype(F32) + b.astype(F32)).astype(x.dtype)


def rope(x, pos):
    half = ROT_DIM // 2
    inv = ROPE_THETA ** (-jnp.arange(half, dtype=F32) / half)
    ang = pos.astype(F32)[:, None] * inv[None, :]
    cos = jnp.cos(ang)[None, :, None, :]
    sin = jnp.sin(ang)[None, :, None, :]
    xr = x[..., :ROT_DIM].astype(F32)
    x1, x2 = xr[..., :half], xr[..., half:]
    rot = jnp.concatenate([x1 * cos - x2 * sin, x1 * sin + x2 * cos], axis=-1).astype(x.dtype)
    return jnp.concatenate([rot, x[..., ROT_DIM:]], axis=-1)


def causal_dw_conv(x, prev, w, bias=None):
    k_w = w.shape[0]
    xp = jnp.concatenate([prev.astype(x.dtype), x], axis=1)
    y = lax.conv_general_dilated(xp, w[:, None, :].astype(x.dtype), window_strides=(1,), padding='VALID',
                                 dimension_numbers=('NWC', 'WIO', 'NWC'), feature_group_count=x.shape[-1])
    if bias is not None:
        y = y + bias.astype(x.dtype)
    return y, xp[:, xp.shape[1] - (k_w - 1):]


def ssd_scan(x, dt, a, bm, cm, h0, chunk):
    b, L, H, P = x.shape
    G, N = bm.shape[2], bm.shape[3]
    R = H // G
    nc = L // chunk
    xc = x.astype(F32).reshape(b, nc, chunk, G, R, P)
    dtc = dt.astype(F32).reshape(b, nc, chunk, G, R)
    bc = bm.astype(F32).reshape(b, nc, chunk, G, N)
    cc = cm.astype(F32).reshape(b, nc, chunk, G, N)
    a_cum = jnp.cumsum(dtc * a.reshape(G, R), axis=2)
    causal = jnp.tril(jnp.ones((chunk, chunk), bool))[None, None, :, :, None, None]
    seg = a_cum[:, :, :, None] - a_cum[:, :, None, :]
    decay = jnp.where(causal, jnp.exp(jnp.where(causal, seg, 0.0)), 0.0)
    cb = jnp.einsum('bctgn,bcsgn->bctsg', cc, bc)
    wts = cb[..., None] * decay * dtc[:, :, None]
    y = jnp.einsum('bctsgr,bcsgrp->bctgrp', wts, xc)
    to_end = jnp.exp(a_cum[:, :, -1:] - a_cum) * dtc
    states = jnp.einsum('bclgr,bclgrp,bclgn->bcgrpn', to_end, xc, bc)
    chunk_decay = jnp.exp(a_cum[:, :, -1])

    def step(h, inp):
        s, d = inp
        return h * d[..., None, None] + s, h

    h_init = h0.astype(F32).reshape(b, G, R, P, N)
    h_last, h_start = lax.scan(step, h_init, (jnp.moveaxis(states, 1, 0), jnp.moveaxis(chunk_decay, 1, 0)))
    h_start = jnp.moveaxis(h_start, 0, 1)
    y = y + jnp.einsum('bctgn,bcgrpn,bctgr->bctgrp', cc, h_start, jnp.exp(a_cum))
    return y.reshape(b, L, H, P), h_last.reshape(b, H, P, N)


def even_mixer(h, conv_a_prev, ssm_prev, conv_b_prev, chunk, w_in, conv_a_w, conv_a_b, ln_a_g, ln_a_b,
               conv_b_w, conv_b_b, dt_bias, a_log, d_skip, ssm_norm, w_out):
    b, L, _ = h.shape
    a_val, a_gate, z, xbc, dt_raw = split_cols(h @ w_in, E_SPLIT)
    u = a_val * jax.nn.sigmoid(a_gate)
    ua, conv_a_new = causal_dw_conv(u, conv_a_prev, conv_a_w, conv_a_b)
    ya = jax.nn.silu(layer_norm(ua, ln_a_g, ln_a_b))
    xbc_c, conv_b_new = causal_dw_conv(xbc, conv_b_prev, conv_b_w, conv_b_b)
    xs, bm, cm = split_cols(jax.nn.silu(xbc_c), [D_INNER, SSM_GROUPS * SSM_STATE, SSM_GROUPS * SSM_STATE])
    dt = jax.nn.softplus(dt_raw.astype(F32) + dt_bias.astype(F32))
    a = -jnp.exp(a_log.astype(F32))
    xh = xs.reshape(b, L, SSM_HEADS, SSM_HEADDIM)
    y, ssm_new = ssd_scan(xh, dt, a, bm.reshape(b, L, SSM_GROUPS, SSM_STATE),
                          cm.reshape(b, L, SSM_GROUPS, SSM_STATE), ssm_prev, chunk)
    y = y + d_skip.astype(F32)[:, None] * xh.astype(F32)
    y = y.reshape(b, L, D_INNER) * jax.nn.silu(z.astype(F32))
    y = y.reshape(b, L, SSM_GROUPS, D_INNER // SSM_GROUPS)
    y = y * lax.rsqrt(jnp.mean(y * y, axis=-1, keepdims=True) + EPS)
    yb = (y.reshape(b, L, D_INNER) * ssm_norm.astype(F32)).astype(h.dtype)
    out = jnp.concatenate([ya, yb], axis=-1) @ w_out
    return out, conv_a_new, ssm_new.astype(ssm_prev.dtype), conv_b_new


def odd_project(h, pos, w_in, q_norm, k_norm, kidx_g, kidx_b):
    b, L, _ = h.shape
    q, k, v, qi, ki, wi, din, bg, cg = split_cols(h @ w_in, O_SPLIT)
    q = rope(rms_norm(q.reshape(b, L, N_HEADS, HEAD_DIM), q_norm), pos)
    k = rope(rms_norm(k.reshape(b, L, N_KV, HEAD_DIM), k_norm), pos)
    v = v.reshape(b, L, N_KV, HEAD_DIM)
    qi = rope(qi.reshape(b, L, IDX_HEADS, IDX_DIM), pos)
    ki = rope(layer_norm(ki, kidx_g, kidx_b)[:, :, None, :], pos)[:, :, 0, :]
    wi = wi * (IDX_HEADS ** -0.5 * IDX_DIM ** -0.5)
    return q, k, v, qi, ki, wi, din, bg, cg


def indexer_scores(qi, wi, ki):
    dots = jnp.einsum('bthd,bsd->bhts', qi, ki).astype(F32)
    return jnp.einsum('bhts,bth->bts', jax.nn.relu(dots), wi.astype(F32))


def sparse_attend(q, k_sel, v_sel, valid):
    b, t = q.shape[:2]
    qg = q.reshape(b, t, N_KV, N_HEADS // N_KV, HEAD_DIM)
    s = jnp.einsum('btgrd,btkgd->btgrk', qg, k_sel).astype(F32) * (HEAD_DIM ** -0.5)
    s = jnp.where(valid[:, :, None, None, :], s, NEG)
    pr = jax.nn.softmax(s, axis=-1).astype(v_sel.dtype)
    o = jnp.einsum('btgrk,btkgd->btgrd', pr, v_sel)
    return o.reshape(b, t, N_HEADS * HEAD_DIM)


def prompt_dsa(q, k, v, qi, ki, wi):
    b, s_len = q.shape[:2]
    topk = min(TOPK_MAX, s_len // 4)
    nb = s_len // Q_BLOCK
    key_pos = jnp.arange(s_len)
    bi = jnp.arange(b)[:, None, None]

    def blk(arr):
        return arr.reshape((b, nb, Q_BLOCK) + arr.shape[2:]).swapaxes(0, 1)

    def one(args):
        qb, qib, wib, start = args
        qpos = start + jnp.arange(Q_BLOCK)
        sc = indexer_scores(qib, wib, ki)
        sc = jnp.where((key_pos[None, :] <= qpos[:, None])[None], sc, NEG)
        _, idx = lax.top_k(sc, topk)
        valid = idx <= qpos[None, :, None]
        return sparse_attend(qb, k[bi, idx], v[bi, idx], valid)

    starts = jnp.arange(nb, dtype=jnp.int32) * Q_BLOCK
    out = lax.map(one, (blk(q), blk(qi), blk(wi), starts))
    return out.swapaxes(0, 1).reshape(b, s_len, N_HEADS * HEAD_DIM)


def sample_dsa(q, k, v, qi, ki, wi, cache_k, cache_v, cache_kidx, page_table):
    bd, t_len = q.shape[:2]
    past = page_table.shape[1] * PAGE_SIZE
    total = past + t_len
    topk = min(TOPK_MAX, total // 4)
    ki_past = cache_kidx[page_table].reshape(bd, past, IDX_DIM)
    ki_all = jnp.concatenate([ki_past.astype(ki.dtype), ki], axis=1)
    qpos = past + jnp.arange(t_len)
    sc = indexer_scores(qi, wi, ki_all)
    sc = jnp.where((jnp.arange(total)[None, :] <= qpos[:, None])[None], sc, NEG)
    _, idx = lax.top_k(sc, topk)
    bi = jnp.arange(bd)[:, None, None]
    from_past = (idx < past)[..., None, None]
    pidx = jnp.minimum(idx, past - 1)
    page = page_table[bi, pidx // PAGE_SIZE]
    off = pidx % PAGE_SIZE
    nidx = jnp.clip(idx - past, 0, t_len - 1)
    k_sel = jnp.where(from_past, cache_k[page, off].astype(k.dtype), k[bi, nidx])
    v_sel = jnp.where(from_past, cache_v[page, off].astype(v.dtype), v[bi, nidx])
    valid = idx <= qpos[None, :, None]
    return sparse_attend(q, k_sel, v_sel, valid)


def short_conv_mixer(din, bg, cg, prev, conv_w):
    y, new_buf = causal_dw_conv(cg * din, prev, conv_w)
    return bg * y, new_buf


def swiglu(h, w_gate, w_up, w_down):
    return (jax.nn.silu(h @ w_gate) * (h @ w_up)) @ w_down


def moe_swiglu(h, w_router, b_router, we_gate, we_up, we_down):
    logits = (h @ w_router).astype(F32) + b_router.astype(F32)
    top_v, top_i = lax.top_k(logits, TOP_K)
    gates = jax.nn.softmax(top_v, axis=-1)
    comb = jnp.sum(jax.nn.one_hot(top_i, N_EXPERTS, dtype=F32) * gates[..., None], axis=-2)
    out = jnp.zeros(h.shape, F32)
    for e in range(N_EXPERTS):
        out = out + comb[..., e:e + 1] * swiglu(h, we_gate[e], we_up[e], we_down[e]).astype(F32)
    return out.astype(h.dtype)


def setup_inputs(seed: int = 0) -> dict:
    key = jax.random.key(seed)
    keys = jax.random.split(key, 64)
    ctr = [0]

    def nk():
        ctr[0] += 1
        return keys[ctr[0] - 1]

    def nrm(shape, scale):
        return scale * jax.random.normal(nk(), shape, F32)

    def gain(shape):
        return 1.0 + nrm(shape, 0.02)

    NP = N_PAIRS
    n_pages = PAST_LEN // PAGE_SIZE
    n_used = DEC_BATCH * n_pages
    n_pool = (5 * n_used + 3) // 4
    perm = jax.random.permutation(nk(), n_pool)
    page_table = perm[:n_used].reshape(DEC_BATCH, n_pages).astype(jnp.int32)
    dt0 = jnp.exp(jax.random.uniform(nk(), (NP, SSM_HEADS), F32, math.log(1e-3), math.log(1e-1)))
    dt_bias = dt0 + jnp.log(-jnp.expm1(-dt0))
    a_log = jnp.log(jax.random.uniform(nk(), (NP, SSM_HEADS), F32, 1.0, 16.0))
    return {
        'x_prompt': nrm((BATCH, SEQ, D_MODEL), 1.0),
        'x_sample': nrm((DEC_BATCH, DEC_SEQ, D_MODEL), 1.0),
        'state_conv_a': nrm((NP, DEC_BATCH, CONV_A - 1, D_A), 0.5),
        'state_ssm': nrm((NP, DEC_BATCH, SSM_HEADS, SSM_HEADDIM, SSM_STATE), 0.1),
        'state_conv_b': nrm((NP, DEC_BATCH, SSM_CONV - 1, CONV_DIM), 0.5),
        'cache_k': nrm((NP, n_pool, PAGE_SIZE, N_KV, HEAD_DIM), 1.0),
        'cache_v': nrm((NP, n_pool, PAGE_SIZE, N_KV, HEAD_DIM), 1.0),
        'cache_kidx': nrm((NP, n_pool, PAGE_SIZE, IDX_DIM), 1.0),
        'state_conv_d': nrm((NP, DEC_BATCH, CONV_D - 1, D_D), 0.5),
        'page_table': page_table,
        'e_norm_mix': gain((NP, D_MODEL)),
        'e_w_in': nrm((NP, D_MODEL, E_IN), D_MODEL ** -0.5),
        'e_conv_a_w': nrm((NP, CONV_A, D_A), CONV_A ** -0.5),
        'e_conv_a_b': nrm((NP, D_A), 0.02),
        'e_ln_a_g': gain((NP, D_A)),
        'e_ln_a_b': nrm((NP, D_A), 0.02),
        'e_conv_b_w': nrm((NP, SSM_CONV, CONV_DIM), SSM_CONV ** -0.5),
        'e_conv_b_b': nrm((NP, CONV_DIM), 0.02),
        'e_dt_bias': dt_bias,
        'e_a_log': a_log,
        'e_d_skip': 1.0 + nrm((NP, SSM_HEADS), 0.1),
        'e_ssm_norm': gain((NP, D_INNER)),
        'e_w_out': nrm((NP, E_MIX, D_MODEL), E_MIX ** -0.5),
        'e_norm_ffn': gain((NP, D_MODEL)),
        'e_w_gate': nrm((NP, D_MODEL, D_FF), D_MODEL ** -0.5),
        'e_w_up': nrm((NP, D_MODEL, D_FF), D_MODEL ** -0.5),
        'e_w_down': nrm((NP, D_FF, D_MODEL), D_FF ** -0.5),
        'o_norm_mix': gain((NP, D_MODEL)),
        'o_w_in': nrm((NP, D_MODEL, O_IN), D_MODEL ** -0.5),
        'o_q_norm': gain((NP, HEAD_DIM)),
        'o_k_norm': gain((NP, HEAD_DIM)),
        'o_kidx_g': gain((NP, IDX_DIM)),
        'o_kidx_b': nrm((NP, IDX_DIM), 0.02),
        'o_conv_d_w': nrm((NP, CONV_D, D_D), CONV_D ** -0.5),
        'o_w_out': nrm((NP, O_MIX, D_MODEL), O_MIX ** -0.5),
        'o_norm_ffn': gain((NP, D_MODEL)),
        'o_w_router': nrm((NP, D_MODEL, N_EXPERTS), D_MODEL ** -0.5),
        'o_b_router': nrm((NP, N_EXPERTS), 0.01),
        'o_we_gate': nrm((NP, N_EXPERTS, D_MODEL, D_FF), D_MODEL ** -0.5),
        'o_we_up': nrm((NP, N_EXPERTS, D_MODEL, D_FF), D_MODEL ** -0.5),
        'o_we_down': nrm((NP, N_EXPERTS, D_FF, D_MODEL), D_FF ** -0.5),
    }


def reference(x_prompt, x_sample, state_conv_a, state_ssm, state_conv_b, cache_k, cache_v, cache_kidx, state_conv_d,
              page_table, e_norm_mix, e_w_in, e_conv_a_w, e_conv_a_b, e_ln_a_g, e_ln_a_b, e_conv_b_w, e_conv_b_b,
              e_dt_bias, e_a_log, e_d_skip, e_ssm_norm, e_w_out, e_norm_ffn, e_w_gate, e_w_up, e_w_down,
              o_norm_mix, o_w_in, o_q_norm, o_k_norm, o_kidx_g, o_kidx_b, o_conv_d_w, o_w_out, o_norm_ffn,
              o_w_router, o_b_router, o_we_gate, o_we_up, o_we_down):
    bp, s_len = x_prompt.shape[:2]
    bd, t_len = x_sample.shape[:2]
    past = page_table.shape[1] * PAGE_SIZE
    pos_p = jnp.arange(s_len)
    pos_s = past + jnp.arange(t_len)
    xp, xs = x_prompt, x_sample
    p_ca, p_ssm, p_cb, p_k, p_v, p_ki, p_cd = [], [], [], [], [], [], []
    s_ca, s_ssm, s_cb, s_k, s_v, s_ki, s_cd = [], [], [], [], [], [], []
    for layer in range(DEPTH):
        i = layer // 2
        if layer % 2 == 0:
            ew = (e_w_in[i], e_conv_a_w[i], e_conv_a_b[i], e_ln_a_g[i], e_ln_a_b[i], e_conv_b_w[i], e_conv_b_b[i],
                  e_dt_bias[i], e_a_log[i], e_d_skip[i], e_ssm_norm[i], e_w_out[i])
            mp, ca, sm, cb = even_mixer(rms_norm(xp, e_norm_mix[i]),
                                        jnp.zeros((bp, CONV_A - 1, D_A), xp.dtype),
                                        jnp.zeros((bp, SSM_HEADS, SSM_HEADDIM, SSM_STATE), xp.dtype),
                                        jnp.zeros((bp, SSM_CONV - 1, CONV_DIM), xp.dtype),
                                        min(SSD_CHUNK, s_len), *ew)
            xp = xp + mp
            p_ca.append(ca); p_ssm.append(sm); p_cb.append(cb)
            ms, ca, sm, cb = even_mixer(rms_norm(xs, e_norm_mix[i]), state_conv_a[i], state_ssm[i],
                                        state_conv_b[i], t_len, *ew)
            xs = xs + ms
            s_ca.append(ca); s_ssm.append(sm); s_cb.append(cb)
            xp = xp + swiglu(rms_norm(xp, e_norm_ffn[i]), e_w_gate[i], e_w_up[i], e_w_down[i])
            xs = xs + swiglu(rms_norm(xs, e_norm_ffn[i]), e_w_gate[i], e_w_up[i], e_w_down[i])
        else:
            ow = (o_w_in[i], o_q_norm[i], o_k_norm[i], o_kidx_g[i], o_kidx_b[i])
            q, k, v, qi, ki, wi, din, bg, cg = odd_project(rms_norm(xp, o_norm_mix[i]), pos_p, *ow)
            att = prompt_dsa(q, k, v, qi, ki, wi)
            dm, cd = short_conv_mixer(din, bg, cg, jnp.zeros((bp, CONV_D - 1, D_D), xp.dtype), o_conv_d_w[i])
            xp = xp + jnp.concatenate([att, dm], axis=-1) @ o_w_out[i]
            p_k.append(k); p_v.append(v); p_ki.append(ki); p_cd.append(cd)
            q, k, v, qi, ki, wi, din, bg, cg = odd_project(rms_norm(xs, o_norm_mix[i]), pos_s, *ow)
            att = sample_dsa(q, k, v, qi, ki, wi, cache_k[i], cache_v[i], cache_kidx[i], page_table)
            dm, cd = short_conv_mixer(din, bg, cg, state_conv_d[i], o_conv_d_w[i])
            xs = xs + jnp.concatenate([att, dm], axis=-1) @ o_w_out[i]
            s_k.append(k); s_v.append(v); s_ki.append(ki); s_cd.append(cd)
            xp = xp + moe_swiglu(rms_norm(xp, o_norm_ffn[i]), o_w_router[i], o_b_router[i],
                                 o_we_gate[i], o_we_up[i], o_we_down[i])
            xs = xs + moe_swiglu(rms_norm(xs, o_norm_ffn[i]), o_w_router[i], o_b_router[i],
                                 o_we_gate[i], o_we_up[i], o_we_down[i])
    return (xp, xs,
            jnp.stack(p_ca), jnp.stack(p_ssm), jnp.stack(p_cb), jnp.stack(p_k), jnp.stack(p_v), jnp.stack(p_ki), jnp.stack(p_cd),
            jnp.stack(s_ca), jnp.stack(s_ssm), jnp.stack(s_cb), jnp.stack(s_k), jnp.stack(s_v), jnp.stack(s_ki), jnp.stack(s_cd))
```

```python
import functools
import math

import jax
import jax.numpy as jnp
import numpy as np
from jax import lax
from jax.experimental import pallas as pl
from jax.experimental.pallas import tpu as pltpu

F32 = jnp.float32
BF16 = jnp.bfloat16
I32 = jnp.int32

D_MODEL = 1024
D_A = 512
CONV_A = 31
D_INNER = 1024
SSM_HEADDIM = 64
SSM_HEADS = 16
SSM_GROUPS = 2
SSM_STATE = 128
SSM_CONV = 4
CONV_DIM = D_INNER + 2 * SSM_GROUPS * SSM_STATE
SSD_CHUNK = 128
HEAD_DIM = 64
N_HEADS = 8
N_KV = 2
ROT_DIM = 16
ROPE_THETA = 500000.0
IDX_HEADS = 4
IDX_DIM = 64
TOPK_MAX = 256
Q_BLOCK = 128
D_D = 512
CONV_D = 3
D_FF = 2816
N_EXPERTS = 8
PAGE_SIZE = 128
EPS = 1e-6
NEG = -1e30

LANES = 128
SUBLANES = 8
VMEM_LIMIT = 56 * 1024 * 1024

E_MAIN = D_A + D_A + D_INNER + CONV_DIM
E_COL_XBC, E_COL_VAL, E_COL_Z, E_COL_GATE = 0, 1536, 2048, 3072
O_MAIN = 512 + 128 + 128 + 256 + 3 * D_D


def _cparams(sem):
    return pltpu.CompilerParams(dimension_semantics=sem, vmem_limit_bytes=VMEM_LIMIT)


def _bdot(a, b):
    return jnp.dot(a.astype(BF16), b.astype(BF16), preferred_element_type=F32)


def _bdot_nt(a, b):
    return lax.dot_general(a.astype(BF16), b.astype(BF16), (((1,), (1,)), ((), ())),
                           preferred_element_type=F32)


def _split3(a):
    hi = a.astype(BF16)
    r1 = a - hi.astype(F32)
    mid = r1.astype(BF16)
    lo = (r1 - mid.astype(F32)).astype(BF16)
    return hi, mid, lo


def _dot_exact_rhs(a, b_bf16):
    hi, mid, lo = _split3(a)
    out = jnp.dot(lo, b_bf16, preferred_element_type=F32)
    out = out + jnp.dot(mid, b_bf16, preferred_element_type=F32)
    return out + jnp.dot(hi, b_bf16, preferred_element_type=F32)


def _dot_exact_lhs(a_bf16, b):
    hi, mid, lo = _split3(b)
    out = jnp.dot(a_bf16, lo, preferred_element_type=F32)
    out = out + jnp.dot(a_bf16, mid, preferred_element_type=F32)
    return out + jnp.dot(a_bf16, hi, preferred_element_type=F32)


def _sigmoid(x):
    return 1.0 / (1.0 + jnp.exp(-x))


def _silu(x):
    return x * _sigmoid(x)


def _softplus(x):
    return jnp.maximum(x, 0.0) + jnp.log(1.0 + jnp.exp(-jnp.abs(x)))


def _norm_proj_kernel(x_ref, g_ref, w_ref, wt_ref, main_ref, tail_ref, xn_ref):
    j = pl.program_id(1)

    @pl.when(j == 0)
    def _():
        x = x_ref[...]
        y = x * lax.rsqrt(jnp.mean(x * x, axis=-1, keepdims=True) + EPS) * g_ref[...]
        xn = y.astype(BF16)
        xn_ref[...] = xn
        tail_ref[...] = jnp.dot(xn, wt_ref[...], preferred_element_type=F32)

    main_ref[...] = jnp.dot(xn_ref[...], w_ref[...], preferred_element_type=F32)


def norm_proj(x, g, w_main, w_tail, *, tm, tn):
    m, k = x.shape
    n = w_main.shape[1]
    return pl.pallas_call(
        _norm_proj_kernel,
        out_shape=(jax.ShapeDtypeStruct((m, n), F32), jax.ShapeDtypeStruct((m, LANES), F32)),
        grid=(m // tm, n // tn),
        in_specs=[pl.BlockSpec((tm, k), lambda i, j: (i, 0)),
                  pl.BlockSpec((1, k), lambda i, j: (0, 0)),
                  pl.BlockSpec((k, tn), lambda i, j: (0, j)),
                  pl.BlockSpec((k, LANES), lambda i, j: (0, 0))],
        out_specs=(pl.BlockSpec((tm, tn), lambda i, j: (i, j)),
                   pl.BlockSpec((tm, LANES), lambda i, j: (i, 0))),
        scratch_shapes=[pltpu.VMEM((tm, k), BF16)],
        compiler_params=_cparams(("arbitrary", "arbitrary")),
        name="norm_proj",
    )(x, g, w_main, w_tail)


def _dwconv_from_buf(buf_ref, w_ref, kw, carry, tl):
    acc = None
    for k in range(kw):
        start = carry - (kw - 1) + k
        term = buf_ref[pl.ds(start, tl), :] * w_ref[pl.ds(k, 1), :]
        acc = term if acc is None else acc + term
    return acc


CARRY_A = 32


def _conv_a_kernel(val_ref, gate_ref, prev_ref, w_ref, b_ref, lg_ref, lb_ref, ya_ref, new_ref, buf_ref, *, tl):
    l = pl.program_id(1)

    @pl.when(l == 0)
    def _():
        buf_ref[pl.ds(0, CARRY_A), :] = prev_ref[0]

    u = val_ref[0] * _sigmoid(gate_ref[0])
    buf_ref[pl.ds(CARRY_A, tl), :] = u
    y = _dwconv_from_buf(buf_ref, w_ref, CONV_A, CARRY_A, tl) + b_ref[...]
    mu = jnp.mean(y, axis=-1, keepdims=True)
    yc = y - mu
    var = jnp.mean(yc * yc, axis=-1, keepdims=True)
    yn = yc * lax.rsqrt(var + EPS) * lg_ref[...] + lb_ref[...]
    ya_ref[0] = _silu(yn)
    hist = buf_ref[pl.ds(tl, CARRY_A), :]
    new_ref[0] = hist
    buf_ref[pl.ds(0, CARRY_A), :] = hist


def conv_a(zmain, prev32, w, b, lg, lb, *, tl):
    bsz, L, _ = zmain.shape
    kern = functools.partial(_conv_a_kernel, tl=tl)
    return pl.pallas_call(
        kern,
        out_shape=(jax.ShapeDtypeStruct((bsz, L, D_A), F32), jax.ShapeDtypeStruct((bsz, CARRY_A, D_A), F32)),
        grid=(bsz, L // tl),
        in_specs=[pl.BlockSpec((1, tl, D_A), lambda b_, l: (b_, l, E_COL_VAL // D_A)),
                  pl.BlockSpec((1, tl, D_A), lambda b_, l: (b_, l, E_COL_GATE // D_A)),
                  pl.BlockSpec((1, CARRY_A, D_A), lambda b_, l: (b_, 0, 0)),
                  pl.BlockSpec((CONV_A, D_A), lambda b_, l: (0, 0)),
                  pl.BlockSpec((1, D_A), lambda b_, l: (0, 0)),
                  pl.BlockSpec((1, D_A), lambda b_, l: (0, 0)),
                  pl.BlockSpec((1, D_A), lambda b_, l: (0, 0))],
        out_specs=(pl.BlockSpec((1, tl, D_A), lambda b_, l: (b_, l, 0)),
                   pl.BlockSpec((1, CARRY_A, D_A), lambda b_, l: (b_, 0, 0))),
        scratch_shapes=[pltpu.VMEM((CARRY_A + tl, D_A), F32)],
        compiler_params=_cparams(("arbitrary", "arbitrary")),
        name="conv_a",
    )(zmain, zmain, prev32, w, b, lg, lb)


CARRY_B = 8
SSD_LP = 128


def _ssd_kernel(xbc_ref, z_ref, dt_ref, prevb_ref, h0_ref, cw_ref, cb_ref, dtb_ref, a_ref, expand_ref,
                dskip_ref, nrm_ref, y_ref, newb_ref, hout_ref, buf_ref, ht_ref, *, lc):
    c = pl.program_id(1)
    nc = pl.num_programs(1)
    lp = SSD_LP

    @pl.when(c == 0)
    def _():
        buf_ref[pl.ds(0, CARRY_B), :] = prevb_ref[0]
        ht_ref[...] = h0_ref[0].T
        if lc < lp:
            buf_ref[pl.ds(CARRY_B + lc, lp - lc), :] = jnp.zeros((lp - lc, CONV_DIM), F32)

    buf_ref[pl.ds(CARRY_B, lc), :] = xbc_ref[0]
    xbc = _silu(_dwconv_from_buf(buf_ref, cw_ref, SSM_CONV, CARRY_B, lp) + cb_ref[...])
    hist = buf_ref[pl.ds(lc, CARRY_B), :]
    newb_ref[0] = hist
    buf_ref[pl.ds(0, CARRY_B), :] = hist

    xs = xbc[:, :D_INNER]
    if lc < lp:
        dt_raw = jnp.concatenate([dt_ref[0], jnp.zeros((lp - lc, LANES), F32)], axis=0)
        zg = jnp.concatenate([z_ref[0], jnp.zeros((lp - lc, D_INNER), F32)], axis=0)
    else:
        dt_raw = dt_ref[0]
        zg = z_ref[0]
    dt = _softplus(dt_raw + dtb_ref[...])
    if lc < lp:
        row = lax.broadcasted_iota(I32, (lp, LANES), 0)
        dt = jnp.where(row < lc, dt, 0.0)
    lane = lax.broadcasted_iota(I32, (lp, LANES), 1)
    dt = jnp.where(lane < SSM_HEADS, dt, 0.0)
    da = dt * a_ref[...]

    r_i = lax.broadcasted_iota(I32, (lp, lp), 0)
    c_i = lax.broadcasted_iota(I32, (lp, lp), 1)
    causal = r_i >= c_i
    tri = jnp.where(causal, 1.0, 0.0).astype(BF16)
    a_col = _dot_exact_lhs(tri, da)
    a_row = a_col.T
    dt_row = dt.T
    a_last = a_col[lp - 1:lp, :]

    expand = expand_ref[...]
    a_exp = _dot_exact_rhs(a_col, expand)
    dt_exp = _dot_exact_rhs(dt, expand)
    al_exp = _dot_exact_rhs(a_last, expand)

    ht = ht_ref[...]
    y_parts = []
    for g in range(SSM_GROUPS):
        bm = xbc[:, D_INNER + g * SSM_STATE: D_INNER + (g + 1) * SSM_STATE]
        cm = xbc[:, D_INNER + (SSM_GROUPS + g) * SSM_STATE: D_INNER + (SSM_GROUPS + g + 1) * SSM_STATE]
        cb = _bdot_nt(cm, bm)
        gs = slice(g * 512, (g + 1) * 512)
        y_inter = _bdot(cm, ht[:, gs]) * jnp.exp(a_exp[:, gs])
        heads = []
        for r in range(SSM_HEADS // SSM_GROUPS):
            h = g * (SSM_HEADS // SSM_GROUPS) + r
            seg = a_col[:, h:h + 1] - a_row[h:h + 1, :]
            decay = jnp.where(causal, jnp.exp(jnp.where(causal, seg, 0.0)), 0.0)
            wts = cb * decay * dt_row[h:h + 1, :]
            heads.append(_bdot(wts, xs[:, h * SSM_HEADDIM:(h + 1) * SSM_HEADDIM]))
        y_parts.append(jnp.concatenate(heads, axis=-1) + y_inter)
        xw = xs[:, gs] * (jnp.exp(al_exp[:, gs] - a_exp[:, gs]) * dt_exp[:, gs])
        ht_ref[:, gs] = ht[:, gs] * jnp.exp(al_exp[:, gs]) + _bdot(bm.T, xw)
    y = jnp.concatenate(y_parts, axis=-1)
    y = y + dskip_ref[...] * xs
    y = y * _silu(zg)
    outs = []
    for g in range(SSM_GROUPS):
        yg = y[:, g * 512:(g + 1) * 512]
        outs.append(yg * lax.rsqrt(jnp.mean(yg * yg, axis=-1, keepdims=True) + EPS))
    yb = jnp.concatenate(outs, axis=-1) * nrm_ref[...]
    y_ref[0] = yb[:lc, :]

    @pl.when(c == nc - 1)
    def _():
        hout_ref[0] = ht_ref[...].T


def ssd_mixer(zmain, ztail, prev8, h0, cw, cb, dtb, a_neg, expand, dskip_exp, nrm, *, lc):
    bsz, L, _ = zmain.shape
    kern = functools.partial(_ssd_kernel, lc=lc)
    return pl.pallas_call(
        kern,
        out_shape=(jax.ShapeDtypeStruct((bsz, L, D_INNER), F32),
                   jax.ShapeDtypeStruct((bsz, CARRY_B, CONV_DIM), F32),
                   jax.ShapeDtypeStruct((bsz, D_INNER, SSM_STATE), F32)),
        grid=(bsz, L // lc),
        in_specs=[pl.BlockSpec((1, lc, CONV_DIM), lambda b_, c: (b_, c, E_COL_XBC // CONV_DIM)),
                  pl.BlockSpec((1, lc, D_INNER), lambda b_, c: (b_, c, E_COL_Z // D_INNER)),
                  pl.BlockSpec((1, lc, LANES), lambda b_, c: (b_, c, 0)),
                  pl.BlockSpec((1, CARRY_B, CONV_DIM), lambda b_, c: (b_, 0, 0)),
                  pl.BlockSpec((1, D_INNER, SSM_STATE), lambda b_, c: (b_, 0, 0)),
                  pl.BlockSpec((SSM_CONV, CONV_DIM), lambda b_, c: (0, 0)),
                  pl.BlockSpec((1, CONV_DIM), lambda b_, c: (0, 0)),
                  pl.BlockSpec((1, LANES), lambda b_, c: (0, 0)),
                  pl.BlockSpec((1, LANES), lambda b_, c: (0, 0)),
                  pl.BlockSpec((LANES, D_INNER), lambda b_, c: (0, 0)),
                  pl.BlockSpec((1, D_INNER), lambda b_, c: (0, 0)),
                  pl.BlockSpec((1, D_INNER), lambda b_, c: (0, 0))],
        out_specs=(pl.BlockSpec((1, lc, D_INNER), lambda b_, c: (b_, c, 0)),
                   pl.BlockSpec((1, CARRY_B, CONV_DIM), lambda b_, c: (b_, 0, 0)),
                   pl.BlockSpec((1, D_INNER, SSM_STATE), lambda b_, c: (b_, 0, 0))),
        scratch_shapes=[pltpu.VMEM((CARRY_B + SSD_LP, CONV_DIM), F32),
                        pltpu.VMEM((SSM_STATE, D_INNER), F32)],
        compiler_params=_cparams(("arbitrary", "arbitrary")),
        name="ssd_mixer",
    )(zmain, zmain, ztail, prev8, h0, cw, cb, dtb, a_neg, expand, dskip_exp, nrm)


def _out_proj_kernel(x_ref, a_ref, b_ref, wa_ref, wb_ref, o_ref):
    acc = jnp.dot(a_ref[...].astype(BF16), wa_ref[...], preferred_element_type=F32)
    acc = acc + jnp.dot(b_ref[...].astype(BF16), wb_ref[...], preferred_element_type=F32)
    o_ref[...] = x_ref[...] + acc


def out_proj(x, a, b, wa, wb, *, tm):
    m, d = x.shape
    ka, kb = a.shape[1], b.shape[1]
    return pl.pallas_call(
        _out_proj_kernel,
        out_shape=jax.ShapeDtypeStruct((m, d), F32),
        grid=(m // tm,),
        in_specs=[pl.BlockSpec((tm, d), lambda i: (i, 0)),
                  pl.BlockSpec((tm, ka), lambda i: (i, 0)),
                  pl.BlockSpec((tm, kb), lambda i: (i, 0)),
                  pl.BlockSpec((ka, d), lambda i: (0, 0)),
                  pl.BlockSpec((kb, d), lambda i: (0, 0))],
        out_specs=pl.BlockSpec((tm, d), lambda i: (i, 0)),
        compiler_params=_cparams(("arbitrary",)),
        name="out_proj",
    )(x, a, b, wa, wb)


def _swiglu_kernel(x_ref, g_ref, wg_ref, wu_ref, wd_ref, o_ref, xn_ref, acc_ref):
    f = pl.program_id(1)
    nf = pl.num_programs(1)

    @pl.when(f == 0)
    def _():
        x = x_ref[...]
        y = x * lax.rsqrt(jnp.mean(x * x, axis=-1, keepdims=True) + EPS) * g_ref[...]
        xn_ref[...] = y.astype(BF16)
        acc_ref[...] = jnp.zeros_like(acc_ref)

    xn = xn_ref[...]
    gate = jnp.dot(xn, wg_ref[...], preferred_element_type=F32)
    up = jnp.dot(xn, wu_ref[...], preferred_element_type=F32)
    hid = (_silu(gate) * up).astype(BF16)
    acc_ref[...] += jnp.dot(hid, wd_ref[...], preferred_element_type=F32)

    @pl.when(f == nf - 1)
    def _():
        o_ref[...] = x_ref[...] + acc_ref[...]


def swiglu_ffn(x, g, wg, wu, wd, *, tm, tf):
    m, d = x.shape
    ff = wg.shape[1]
    return pl.pallas_call(
        _swiglu_kernel,
        out_shape=jax.ShapeDtypeStruct((m, d), F32),
        grid=(m // tm, ff // tf),
        in_specs=[pl.BlockSpec((tm, d), lambda i, f: (i, 0)),
                  pl.BlockSpec((1, d), lambda i, f: (0, 0)),
                  pl.BlockSpec((d, tf), lambda i, f: (0, f)),
                  pl.BlockSpec((d, tf), lambda i, f: (0, f)),
                  pl.BlockSpec((tf, d), lambda i, f: (f, 0))],
        out_specs=pl.BlockSpec((tm, d), lambda i, f: (i, 0)),
        scratch_shapes=[pltpu.VMEM((tm, d), BF16), pltpu.VMEM((tm, d), F32)],
        compiler_params=_cparams(("arbitrary", "arbitrary")),
        name="swiglu_ffn",
    )(x, g, wg, wu, wd)


O_COL_Q, O_COL_K, O_COL_V, O_COL_QI, O_COL_DIN, O_COL_BG, O_COL_CG = 0, 512, 640, 768, 1024, 1536, 2048


def _rope128(x, cos_t, sin_t, lo_mask):
    up = pltpu.roll(x, LANES - ROT_DIM // 2, 1)
    dn = pltpu.roll(x, ROT_DIM // 2, 1)
    return x * cos_t + jnp.where(lo_mask, up, dn) * sin_t


def _odd_post_kernel(qkvq_ref, tail_ref, cos_ref, sin_ref, qn_ref, kn_ref, kg_ref, kb_ref, bd_ref,
                     q_ref, k_ref, v_ref, qi_ref, kiw_ref):
    cos_t = cos_ref[...]
    sin_t = sin_ref[...]
    lane = lax.broadcasted_iota(I32, cos_t.shape, 1)
    lo_mask = (lane % HEAD_DIM) < (ROT_DIM // 2)
    bd = bd_ref[...]
    z = qkvq_ref[...]
    for c in range(4):
        x = z[:, c * LANES:(c + 1) * LANES]
        ms = _dot_exact_rhs(x * x, bd)
        xn = x * lax.rsqrt(ms + EPS) * qn_ref[...]
        q_ref[:, c * LANES:(c + 1) * LANES] = _rope128(xn, cos_t, sin_t, lo_mask)
    x = z[:, O_COL_K:O_COL_K + LANES]
    ms = _dot_exact_rhs(x * x, bd)
    xn = x * lax.rsqrt(ms + EPS) * kn_ref[...]
    k_ref[...] = _rope128(xn, cos_t, sin_t, lo_mask)
    v_ref[...] = z[:, O_COL_V:O_COL_V + LANES]
    for c in range(2):
        x = z[:, O_COL_QI + c * LANES: O_COL_QI + (c + 1) * LANES]
        qi_ref[:, c * LANES:(c + 1) * LANES] = _rope128(x, cos_t, sin_t, lo_mask)
    t = tail_ref[...]
    lane_t = lax.broadcasted_iota(I32, t.shape, 1)
    is_ki = lane_t < IDX_DIM
    mu = _dot_exact_rhs(t, bd)
    tc = t - mu
    var = _dot_exact_rhs(tc * tc, bd)
    kin = tc * lax.rsqrt(var + EPS) * kg_ref[...] + kb_ref[...]
    kin = _rope128(kin, cos_t, sin_t, lo_mask)
    kiw_ref[...] = jnp.where(is_ki, kin, t * (IDX_HEADS ** -0.5 * IDX_DIM ** -0.5))


def odd_post(zmain, ztail, cos_t, sin_t, qn, kn, kg, kb, bd, *, tm):
    m = zmain.shape[0]
    nrope = cos_t.shape[0] // tm
    row = lambda i: (i, 0)
    rope = lambda i: (i % nrope, 0)
    const = lambda i: (0, 0)
    return pl.pallas_call(
        _odd_post_kernel,
        out_shape=(jax.ShapeDtypeStruct((m, 512), F32), jax.ShapeDtypeStruct((m, LANES), F32),
                   jax.ShapeDtypeStruct((m, LANES), F32), jax.ShapeDtypeStruct((m, 256), F32),
                   jax.ShapeDtypeStruct((m, LANES), F32)),
        grid=(m // tm,),
        in_specs=[pl.BlockSpec((tm, 1024), row), pl.BlockSpec((tm, LANES), row),
                  pl.BlockSpec((tm, LANES), rope), pl.BlockSpec((tm, LANES), rope),
                  pl.BlockSpec((1, LANES), const), pl.BlockSpec((1, LANES), const),
                  pl.BlockSpec((1, LANES), const), pl.BlockSpec((1, LANES), const),
                  pl.BlockSpec((LANES, LANES), const)],
        out_specs=(pl.BlockSpec((tm, 512), row), pl.BlockSpec((tm, LANES), row), pl.BlockSpec((tm, LANES), row),
                   pl.BlockSpec((tm, 256), row), pl.BlockSpec((tm, LANES), row)),
        compiler_params=_cparams(("arbitrary",)),
        name="odd_post",
    )(zmain, ztail, cos_t, sin_t, qn, kn, kg, kb, bd)


INT_MIN = -2 ** 31
F32_MIN_NORMAL_BITS = 0x00800000


def _code_to_float(c):
    return pltpu.bitcast(c ^ ((c >> 31) & 0x7FFFFFFF), F32)


def _kth_largest(count_ge, k, shape):
    ans = jnp.full(shape, INT_MIN, I32)
    cand = jnp.zeros(shape, I32)
    ans = jnp.where(count_ge(_code_to_float(cand)) >= k, cand, ans)

    def body(i, ans):
        cand = ans | jnp.left_shift(jnp.int32(1), 30 - i)
        return jnp.where(count_ge(_code_to_float(cand)) >= k, cand, ans)

    ans = lax.fori_loop(0, 31, body, ans)
    ans = jnp.where((ans > 0) & (ans < F32_MIN_NORMAL_BITS), 0, ans)
    return _code_to_float(ans)


def _upper_tri_bf16(n):
    r = lax.broadcasted_iota(I32, (n, n), 0)
    c = lax.broadcasted_iota(I32, (n, n), 1)
    return jnp.where(r <= c, 1.0, 0.0).astype(BF16)


def _dsa_prompt_kernel(q_ref, qi_ref, wi_ref, k_ref, v_ref, kk_ref, o_ref, keys_ref, sel_ref, *, s_len, topk):
    t0 = pl.program_id(1) * Q_BLOCK
    lane = lax.broadcasted_iota(I32, (Q_BLOCK, LANES), 1)
    row = lax.broadcasted_iota(I32, (Q_BLOCK, LANES), 0)
    lo = lane < HEAD_DIM
    hi = lane >= HEAD_DIM

    kk = kk_ref[0].astype(BF16)
    qi = qi_ref[0]
    wi = wi_ref[0]
    sc = None
    for h in range(IDX_HEADS):
        x = qi[:, (h // 2) * LANES:(h // 2 + 1) * LANES]
        x = jnp.where(lo if h % 2 == 0 else hi, x, 0.0)
        d = _bdot_nt(x, kk)
        term = jnp.maximum(d, 0.0) * wi[:, IDX_DIM + h:IDX_DIM + h + 1]
        sc = term if sc is None else sc + term
    kpos = lax.broadcasted_iota(I32, (Q_BLOCK, s_len), 1)
    qpos = t0 + lax.broadcasted_iota(I32, (Q_BLOCK, s_len), 0)
    sc = jnp.where(sc == 0.0, 0.0, sc)
    sc = jnp.where(kpos <= qpos, sc, NEG)
    keys_ref[...] = sc

    def count_ge(cand):
        return jnp.sum(jnp.where(keys_ref[...] >= cand, 1.0, 0.0), axis=-1, keepdims=True)

    thr = _kth_largest(count_ge, float(topk), (Q_BLOCK, 1))
    n_gt = jnp.sum(jnp.where(keys_ref[...] > thr, 1.0, 0.0), axis=-1, keepdims=True)
    need = float(topk) - n_gt
    ut = _upper_tri_bf16(LANES)
    off = jnp.zeros((Q_BLOCK, 1), F32)
    for j in range(s_len // LANES):
        kj = keys_ref[:, j * LANES:(j + 1) * LANES]
        eq = kj == thr
        pre = jnp.dot(jnp.where(eq, 1.0, 0.0).astype(BF16), ut, preferred_element_type=F32) + off
        adm = (j * LANES + lane) <= (t0 + row)
        keep = jnp.where(eq, jnp.where(pre <= need, 1.0, 0.0), jnp.where(kj > thr, 1.0, 0.0))
        sel_ref[:, j * LANES:(j + 1) * LANES] = jnp.where(adm, keep, 0.0)
        off = pre[:, LANES - 1:LANES]

    kb = k_ref[0].astype(BF16)
    vb = v_ref[0].astype(BF16)
    selm = sel_ref[...]
    sel4 = jnp.concatenate([selm, selm, selm, selm], axis=0) > 0.5
    q = q_ref[0]
    outs = []
    for g in range(N_KV):
        keep = lo if g == 0 else hi
        qg = jnp.concatenate([jnp.where(keep, q[:, r * LANES:(r + 1) * LANES], 0.0) for r in range(4)], axis=0)
        s = _bdot_nt(qg, kb) * (HEAD_DIM ** -0.5)
        s = jnp.where(sel4, s, NEG)
        m = jnp.max(s, axis=-1, keepdims=True)
        e = jnp.exp(s - m)
        l = jnp.sum(e, axis=-1, keepdims=True)
        outs.append(jnp.dot(e.astype(BF16), vb, preferred_element_type=F32) / l)
    lo4 = lax.broadcasted_iota(I32, (4 * Q_BLOCK, LANES), 1) < HEAD_DIM
    oc = jnp.where(lo4, outs[0], outs[1])
    for r in range(4):
        o_ref[0, :, r * LANES:(r + 1) * LANES] = oc[r * Q_BLOCK:(r + 1) * Q_BLOCK, :]


def dsa_prompt(q, qi, wi, k, v, kk):
    bsz, s_len, _ = q.shape
    topk = min(TOPK_MAX, s_len // 4)
    kern = functools.partial(_dsa_prompt_kernel, s_len=s_len, topk=topk)
    qblk = lambda w: pl.BlockSpec((1, Q_BLOCK, w), lambda b_, i: (b_, i, 0))
    full = pl.BlockSpec((1, s_len, LANES), lambda b_, i: (b_, 0, 0))
    return pl.pallas_call(
        kern,
        out_shape=jax.ShapeDtypeStruct((bsz, s_len, 512), F32),
        grid=(bsz, s_len // Q_BLOCK),
        in_specs=[qblk(512), qblk(256), qblk(LANES), full, full, full],
        out_specs=qblk(512),
        scratch_shapes=[pltpu.VMEM((Q_BLOCK, s_len), F32), pltpu.VMEM((Q_BLOCK, s_len), F32)],
        compiler_params=_cparams(("arbitrary", "arbitrary")),
        name="dsa_prompt",
    )(q, qi, wi, k, v, kk)


CARRY_D = 8


def _conv_d_kernel(din_ref, bg_ref, cg_ref, prev_ref, w_ref, y_ref, new_ref, buf_ref, *, tl):
    l = pl.program_id(1)

    @pl.when(l == 0)
    def _():
        buf_ref[pl.ds(0, CARRY_D), :] = prev_ref[0]

    buf_ref[pl.ds(CARRY_D, tl), :] = cg_ref[0] * din_ref[0]
    y_ref[0] = bg_ref[0] * _dwconv_from_buf(buf_ref, w_ref, CONV_D, CARRY_D, tl)
    hist = buf_ref[pl.ds(tl, CARRY_D), :]
    new_ref[0] = hist
    buf_ref[pl.ds(0, CARRY_D), :] = hist


def conv_d(zmain, prev8, w, *, tl):
    bsz, L, _ = zmain.shape
    kern = functools.partial(_conv_d_kernel, tl=tl)
    col = lambda c: pl.BlockSpec((1, tl, D_D), lambda b_, l: (b_, l, c // D_D))
    return pl.pallas_call(
        kern,
        out_shape=(jax.ShapeDtypeStruct((bsz, L, D_D), F32), jax.ShapeDtypeStruct((bsz, CARRY_D, D_D), F32)),
        grid=(bsz, L // tl),
        in_specs=[col(O_COL_DIN), col(O_COL_BG), col(O_COL_CG),
                  pl.BlockSpec((1, CARRY_D, D_D), lambda b_, l: (b_, 0, 0)),
                  pl.BlockSpec((CONV_D, D_D), lambda b_, l: (0, 0))],
        out_specs=(pl.BlockSpec((1, tl, D_D), lambda b_, l: (b_, l, 0)),
                   pl.BlockSpec((1, CARRY_D, D_D), lambda b_, l: (b_, 0, 0))),
        scratch_shapes=[pltpu.VMEM((CARRY_D + tl, D_D), F32)],
        compiler_params=_cparams(("arbitrary", "arbitrary")),
        name="conv_d",
    )(zmain, zmain, zmain, prev8, w)


def _router_kernel(x_ref, g_ref, whi_ref, wlo_ref, br_ref, xn_ref, comb_ref):
    x = x_ref[...]
    y = x * lax.rsqrt(jnp.mean(x * x, axis=-1, keepdims=True) + EPS) * g_ref[...]
    yhi = y.astype(BF16)
    xn_ref[...] = yhi
    ylo = (y - yhi.astype(F32)).astype(BF16)
    logits = jnp.dot(ylo, whi_ref[...], preferred_element_type=F32)
    logits = logits + jnp.dot(yhi, wlo_ref[...], preferred_element_type=F32)
    logits = logits + jnp.dot(yhi, whi_ref[...], preferred_element_type=F32) + br_ref[...]
    lane = lax.broadcasted_iota(I32, logits.shape, 1).astype(F32)
    m1 = jnp.max(logits, axis=-1, keepdims=True)
    i1 = jnp.min(jnp.where(logits == m1, lane, float(LANES)), axis=-1, keepdims=True)
    rest = jnp.where(lane == i1, -3e38, logits)
    m2 = jnp.max(rest, axis=-1, keepdims=True)
    i2 = jnp.min(jnp.where(rest == m2, lane, float(LANES)), axis=-1, keepdims=True)
    e2 = jnp.exp(m2 - m1)
    g1 = 1.0 / (1.0 + e2)
    g2 = e2 / (1.0 + e2)
    comb_ref[...] = jnp.where(lane == i1, g1, 0.0) + jnp.where(lane == i2, g2, 0.0)


def moe_router(x, g, w_hi, w_lo, b_pad, *, tm):
    m, d = x.shape
    return pl.pallas_call(
        _router_kernel,
        out_shape=(jax.ShapeDtypeStruct((m, d), BF16), jax.ShapeDtypeStruct((m, LANES), F32)),
        grid=(m // tm,),
        in_specs=[pl.BlockSpec((tm, d), lambda i: (i, 0)),
                  pl.BlockSpec((1, d), lambda i: (0, 0)),
                  pl.BlockSpec((d, LANES), lambda i: (0, 0)),
                  pl.BlockSpec((d, LANES), lambda i: (0, 0)),
                  pl.BlockSpec((1, LANES), lambda i: (0, 0))],
        out_specs=(pl.BlockSpec((tm, d), lambda i: (i, 0)), pl.BlockSpec((tm, LANES), lambda i: (i, 0))),
        compiler_params=_cparams(("arbitrary",)),
        name="moe_router",
    )(x, g, w_hi, w_lo, b_pad)


def _moe_dense_kernel(x_ref, xn_ref, comb_ref, wg_ref, wu_ref, wd_ref, o_ref, acc_ref, acce_ref):
    e = pl.program_id(1)
    f = pl.program_id(2)
    ne = pl.num_programs(1)
    nf = pl.num_programs(2)

    @pl.when((e == 0) & (f == 0))
    def _():
        acc_ref[...] = jnp.zeros_like(acc_ref)

    @pl.when(f == 0)
    def _():
        acce_ref[...] = jnp.zeros_like(acce_ref)

    xn = xn_ref[...]
    gate = jnp.dot(xn, wg_ref[0], preferred_element_type=F32)
    up = jnp.dot(xn, wu_ref[0], preferred_element_type=F32)
    hid = (_silu(gate) * up).astype(BF16)
    acce_ref[...] += jnp.dot(hid, wd_ref[0], preferred_element_type=F32)

    @pl.when(f == nf - 1)
    def _():
        comb = comb_ref[...]
        lane = lax.broadcasted_iota(I32, comb.shape, 1)
        c = jnp.sum(jnp.where(lane == e, comb, 0.0), axis=-1, keepdims=True)
        acc_ref[...] += c * acce_ref[...]

    @pl.when((e == ne - 1) & (f == nf - 1))
    def _():
        o_ref[...] = x_ref[...] + acc_ref[...]


def moe_dense(x, xn, comb, wg, wu, wd, *, tm, tf):
    m, d = x.shape
    ne, _, ff = wg.shape
    return pl.pallas_call(
        _moe_dense_kernel,
        out_shape=jax.ShapeDtypeStruct((m, d), F32),
        grid=(m // tm, ne, ff // tf),
        in_specs=[pl.BlockSpec((tm, d), lambda i, e, f: (i, 0)),
                  pl.BlockSpec((tm, d), lambda i, e, f: (i, 0)),
                  pl.BlockSpec((tm, LANES), lambda i, e, f: (i, 0)),
                  pl.BlockSpec((1, d, tf), lambda i, e, f: (e, 0, f)),
                  pl.BlockSpec((1, d, tf), lambda i, e, f: (e, 0, f)),
                  pl.BlockSpec((1, tf, d), lambda i, e, f: (e, f, 0))],
        out_specs=pl.BlockSpec((tm, d), lambda i, e, f: (i, 0)),
        scratch_shapes=[pltpu.VMEM((tm, d), F32), pltpu.VMEM((tm, d), F32)],
        compiler_params=_cparams(("arbitrary", "arbitrary", "arbitrary")),
        name="moe_dense",
    )(x, xn, comb, wg, wu, wd)


PAGES_PER_STEP = 32


def _sdsa_index_kernel(pt_ref, qi_ref, wi_ref, kknew_ref, *rest, n_pages, t_len, topk):
    del pt_ref
    page_refs = rest[:PAGES_PER_STEP]
    mpast_ref, mnew_ref, keys_ref, pre_ref = rest[PAGES_PER_STEP:]
    j = pl.program_id(1)
    nj = pl.num_programs(1)
    lane = lax.broadcasted_iota(I32, (t_len, LANES), 1)
    row = lax.broadcasted_iota(I32, (t_len, LANES), 0)
    lo = lane < HEAD_DIM

    qi = qi_ref[0]
    wi = wi_ref[0]
    qs, ws = [], []
    for h in range(IDX_HEADS):
        x = qi[:, (h // 2) * LANES:(h // 2 + 1) * LANES]
        if h % 2 == 1:
            x = pltpu.roll(x, HEAD_DIM, 1)
        qs.append(x[:, :IDX_DIM])
        ws.append(wi[:, IDX_DIM + h:IDX_DIM + h + 1])
    q32 = jnp.concatenate(qs, axis=0).astype(BF16)
    w32 = jnp.concatenate(ws, axis=0)

    def scores(dots):
        d = jnp.maximum(dots, 0.0) * w32
        s = d[0:t_len]
        for h in range(1, IDX_HEADS):
            s = s + d[h * t_len:(h + 1) * t_len]
        return jnp.where(s == 0.0, 0.0, s)

    for i in range(PAGES_PER_STEP):
        page_t = page_refs[i][0].astype(BF16)
        keys_ref[j * PAGES_PER_STEP + i] = scores(jnp.dot(q32, page_t, preferred_element_type=F32))

    @pl.when(j == nj - 1)
    def _():
        knew = jnp.concatenate([kknew_ref[0][:, :IDX_DIM], jnp.zeros((LANES - t_len, IDX_DIM), F32)], axis=0)
        s_new = jnp.where((lane <= row) & (lane < t_len), scores(_bdot_nt(q32, knew)), NEG)
        keys_ref[n_pages] = s_new

        def count_ge(cand):
            part = jnp.sum(jnp.where(keys_ref[...] >= cand, 1.0, 0.0), axis=0)
            return jnp.sum(part, axis=-1, keepdims=True)

        thr = _kth_largest(count_ge, float(topk), (t_len, 1))
        keys = keys_ref[...]
        n_gt = jnp.sum(jnp.sum(jnp.where(keys > thr, 1.0, 0.0), axis=0), axis=-1, keepdims=True)
        need = float(topk) - n_gt
        eqf = jnp.where(keys == thr, 1.0, 0.0).astype(BF16).reshape((n_pages + 1) * t_len, LANES)
        pre_ref[...] = jnp.dot(eqf, _upper_tri_bf16(LANES), preferred_element_type=F32).reshape(
            n_pages + 1, t_len, LANES)

        def body(p, off):
            kp = keys_ref[p]
            pre = pre_ref[p]
            bias = jnp.where(kp == thr, jnp.where((pre + off) <= need, 0.0, NEG), jnp.where(kp > thr, 0.0, NEG))

            @pl.when(p < n_pages)
            def _():
                mpast_ref[0, p] = bias

            @pl.when(p == n_pages)
            def _():
                mnew_ref[0] = bias

            return off + pre[:, LANES - 1:LANES]

        lax.fori_loop(0, n_pages + 1, body, jnp.zeros((t_len, 1), F32))


def sdsa_index(page_table, qi, wi, kk_new, cache_kidx):
    bsz, t_len, _ = qi.shape
    n_pages = page_table.shape[1]
    topk = min(TOPK_MAX, (n_pages * PAGE_SIZE + t_len) // 4)
    kern = functools.partial(_sdsa_index_kernel, n_pages=n_pages, t_len=t_len, topk=topk)
    tok = lambda w: pl.BlockSpec((1, t_len, w), lambda b_, j, pt: (b_, 0, 0))
    page_specs = [pl.BlockSpec((1, IDX_DIM, PAGE_SIZE),
                               functools.partial(lambda b_, j, pt, i: (pt[b_, j * PAGES_PER_STEP + i], 0, 0), i=i))
                  for i in range(PAGES_PER_STEP)]
    return pl.pallas_call(
        kern,
        out_shape=(jax.ShapeDtypeStruct((bsz, n_pages, t_len, LANES), F32),
                   jax.ShapeDtypeStruct((bsz, t_len, LANES), F32)),
        grid_spec=pltpu.PrefetchScalarGridSpec(
            num_scalar_prefetch=1,
            grid=(bsz, n_pages // PAGES_PER_STEP),
            in_specs=[tok(256), tok(LANES), tok(LANES)] + page_specs,
            out_specs=(pl.BlockSpec((1, n_pages, t_len, LANES), lambda b_, j, pt: (b_, 0, 0, 0)),
                       pl.BlockSpec((1, t_len, LANES), lambda b_, j, pt: (b_, 0, 0))),
            scratch_shapes=[pltpu.VMEM((n_pages + 1, t_len, LANES), F32),
                            pltpu.VMEM((n_pages + 1, t_len, LANES), F32)]),
        compiler_params=_cparams(("arbitrary", "arbitrary")),
        name="sdsa_index",
    )(page_table, qi, wi, kk_new, *([cache_kidx] * PAGES_PER_STEP))


def _sdsa_attn_kernel(pt_ref, q_ref, knew_ref, vnew_ref, mpast_ref, mnew_ref, *rest, t_len):
    del pt_ref
    kp_refs = rest[:PAGES_PER_STEP]
    vp_refs = rest[PAGES_PER_STEP:2 * PAGES_PER_STEP]
    o_ref, qg_ref, m_ref, l_ref, acc_ref = rest[2 * PAGES_PER_STEP:]
    j = pl.program_id(1)
    nj = pl.num_programs(1)
    rpg = N_HEADS // N_KV
    reps = rpg

    def group_lanes(x, g):
        if g == 1:
            x = pltpu.roll(x, HEAD_DIM, 1)
        return x[:, :HEAD_DIM]

    @pl.when(j == 0)
    def _():
        q = q_ref[0]
        for g in range(N_KV):
            parts = [group_lanes(q[:, r * LANES:(r + 1) * LANES], g) for r in range(rpg)]
            qg_ref[g] = (jnp.concatenate(parts, axis=0) * (HEAD_DIM ** -0.5)).astype(BF16)
        m_ref[...] = jnp.full(m_ref.shape, NEG, F32)
        l_ref[...] = jnp.zeros(l_ref.shape, F32)
        acc_ref[...] = jnp.zeros(acc_ref.shape, F32)

    def update(g, s_list, pv):
        m_old = m_ref[g]
        m_cur = m_old
        for s in s_list:
            m_cur = jnp.maximum(m_cur, jnp.max(s, axis=-1, keepdims=True))
        alpha = jnp.exp(m_old - m_cur)
        l_new = l_ref[g] * alpha
        acc = acc_ref[g] * alpha
        for i, s in enumerate(s_list):
            p = jnp.exp(s - m_cur)
            l_new = l_new + jnp.sum(p, axis=-1, keepdims=True)
            acc = acc + pv(i, p.astype(BF16))
        m_ref[g] = m_cur
        l_ref[g] = l_new
        acc_ref[g] = acc

    biases = [jnp.concatenate([mpast_ref[0, i]] * reps, axis=0) for i in range(PAGES_PER_STEP)]
    for g in range(N_KV):
        qg = qg_ref[g]
        s_list = [jnp.dot(qg, kp_refs[i][0, g].astype(BF16), preferred_element_type=F32) + biases[i]
                  for i in range(PAGES_PER_STEP)]
        update(g, s_list, lambda i, p, g=g: _bdot_nt(p, vp_refs[i][0, g]))

    @pl.when(j == nj - 1)
    def _():
        pad = jnp.zeros((LANES - t_len, HEAD_DIM), F32)
        bias = jnp.concatenate([mnew_ref[0]] * reps, axis=0)
        for g in range(N_KV):
            knew = jnp.concatenate([group_lanes(knew_ref[0], g), pad], axis=0)
            vnew = jnp.concatenate([group_lanes(vnew_ref[0], g), pad], axis=0)
            update(g, [_bdot_nt(qg_ref[g], knew) + bias], lambda i, p, vnew=vnew: _bdot(p, vnew))
        o0 = acc_ref[0] / l_ref[0]
        o1 = acc_ref[1] / l_ref[1]
        for r in range(rpg):
            o_ref[0, :, r * LANES:(r + 1) * LANES] = jnp.concatenate(
                [o0[r * t_len:(r + 1) * t_len], o1[r * t_len:(r + 1) * t_len]], axis=-1)


def sdsa_attn(page_table, q, k_new, v_new, mask_past, mask_new, cache_k, cache_v):
    bsz, t_len, _ = q.shape
    n_pages = page_table.shape[1]
    kern = functools.partial(_sdsa_attn_kernel, t_len=t_len)
    tok = lambda w: pl.BlockSpec((1, t_len, w), lambda b_, j, pt: (b_, 0, 0))
    page_specs = [pl.BlockSpec((1, N_KV, HEAD_DIM, PAGE_SIZE),
                               functools.partial(lambda b_, j, pt, i: (pt[b_, j * PAGES_PER_STEP + i], 0, 0, 0), i=i))
                  for i in range(PAGES_PER_STEP)]
    rows = (N_HEADS // N_KV) * t_len
    return pl.pallas_call(
        kern,
        out_shape=jax.ShapeDtypeStruct((bsz, t_len, 512), F32),
        grid_spec=pltpu.PrefetchScalarGridSpec(
            num_scalar_prefetch=1,
            grid=(bsz, n_pages // PAGES_PER_STEP),
            in_specs=[tok(512), tok(LANES), tok(LANES),
                      pl.BlockSpec((1, PAGES_PER_STEP, t_len, LANES), lambda b_, j, pt: (b_, j, 0, 0)),
                      tok(LANES)] + page_specs + page_specs,
            out_specs=pl.BlockSpec((1, t_len, 512), lambda b_, j, pt: (b_, 0, 0)),
            scratch_shapes=[pltpu.VMEM((N_KV, rows, HEAD_DIM), BF16), pltpu.VMEM((N_KV, rows, 1), F32),
                            pltpu.VMEM((N_KV, rows, 1), F32), pltpu.VMEM((N_KV, rows, HEAD_DIM), F32)]),
        compiler_params=_cparams(("arbitrary", "arbitrary")),
        name="sdsa_attn",
    )(page_table, q, k_new, v_new, mask_past, mask_new,
      *([cache_k] * PAGES_PER_STEP), *([cache_v] * PAGES_PER_STEP))


HEAD_ORDER = (0, 4, 1, 5, 2, 6, 3, 7)


def _pad_cols(w, n):
    return jnp.pad(w, ((0, 0), (0, n - w.shape[1])))


def _row(v):
    return v.reshape(1, -1).astype(F32)


def _rope_tables(pos):
    half = ROT_DIM // 2
    inv = ROPE_THETA ** (-jnp.arange(half, dtype=F32) / half)
    ang = pos.astype(F32)[:, None] * inv[None, :]
    cos, sin = jnp.cos(ang), jnp.sin(ang)
    n = pos.shape[0]
    rest = HEAD_DIM - ROT_DIM
    c64 = jnp.concatenate([cos, cos, jnp.ones((n, rest), F32)], axis=1)
    s64 = jnp.concatenate([-sin, sin, jnp.zeros((n, rest), F32)], axis=1)
    return jnp.tile(c64, (1, 2)), jnp.tile(s64, (1, 2))


def _even_layer(x, prev_a, prev_ssm, prev_b, lc, p, *, tm, tl_a):
    bsz, L, d = x.shape
    m = bsz * L
    x2 = x.reshape(m, d)
    zmain, ztail = norm_proj(x2, p["e_norm_mix"], p["e_w_main"], p["e_w_tail"], tm=tm, tn=512)
    zmain3 = zmain.reshape(bsz, L, E_MAIN)
    prev32 = jnp.pad(prev_a, ((0, 0), (CARRY_A - (CONV_A - 1), 0), (0, 0)))
    ya, new_a = conv_a(zmain3, prev32, p["e_conv_a_w"], p["e_conv_a_b"], p["e_ln_a_g"], p["e_ln_a_b"], tl=tl_a)
    prev8 = jnp.pad(prev_b, ((0, 0), (CARRY_B - (SSM_CONV - 1), 0), (0, 0)))
    yb, new_b, hout = ssd_mixer(zmain3, ztail.reshape(bsz, L, LANES), prev8,
                                prev_ssm.reshape(bsz, D_INNER, SSM_STATE),
                                p["e_conv_b_w"], p["e_conv_b_b"], p["e_dt_bias"], p["e_a_neg"], p["e_expand"],
                                p["e_dskip"], p["e_ssm_norm"], lc=lc)
    x2 = out_proj(x2, ya.reshape(m, D_A), yb.reshape(m, D_INNER), p["e_wo_a"], p["e_wo_b"], tm=tm)
    x2 = swiglu_ffn(x2, p["e_norm_ffn"], p["e_w_gate"], p["e_w_up"], p["e_w_down"], tm=min(m, 1024), tf=256)
    return (x2.reshape(bsz, L, d), new_a[:, CARRY_A - (CONV_A - 1):], hout.reshape(bsz, SSM_HEADS, SSM_HEADDIM, SSM_STATE),
            new_b[:, CARRY_B - (SSM_CONV - 1):])


def _odd_project(x2, cos_t, sin_t, p, *, tm):
    zmain, ztail = norm_proj(x2, p["o_norm_mix"], p["o_w_main"], p["o_w_tail"], tm=tm, tn=512)
    q, k, v, qi, kiw = odd_post(zmain, ztail, cos_t, sin_t, p["o_q_norm"], p["o_k_norm"], p["o_kidx_g"],
                                p["o_kidx_b"], p["o_bd"], tm=tm)
    return zmain, q, k, v, qi, kiw


def _odd_tail(x2, att, zmain3, prev_d, p, *, tm, tl_d):
    bsz, L, _ = zmain3.shape
    m = bsz * L
    prev8 = jnp.pad(prev_d, ((0, 0), (CARRY_D - (CONV_D - 1), 0), (0, 0)))
    dm, new_d = conv_d(zmain3, prev8, p["o_conv_d_w"], tl=tl_d)
    x2 = out_proj(x2, att.reshape(m, 512), dm.reshape(m, D_D), p["o_wo_a"], p["o_wo_b"], tm=tm)
    xn, comb = moe_router(x2, p["o_norm_ffn"], p["o_wr_hi"], p["o_wr_lo"], p["o_br"], tm=tm)
    x2 = moe_dense(x2, xn, comb, p["o_we_gate"], p["o_we_up"], p["o_we_down"], tm=min(m, 1024), tf=256)
    return x2, new_d[:, CARRY_D - (CONV_D - 1):]


def kernel(x_prompt, x_sample, state_conv_a, state_ssm, state_conv_b, cache_k, cache_v, cache_kidx, state_conv_d,
           page_table, e_norm_mix, e_w_in, e_conv_a_w, e_conv_a_b, e_ln_a_g, e_ln_a_b, e_conv_b_w, e_conv_b_b,
           e_dt_bias, e_a_log, e_d_skip, e_ssm_norm, e_w_out, e_norm_ffn, e_w_gate, e_w_up, e_w_down,
           o_norm_mix, o_w_in, o_q_norm, o_k_norm, o_kidx_g, o_kidx_b, o_conv_d_w, o_w_out, o_norm_ffn,
           o_w_router, o_b_router, o_we_gate, o_we_up, o_we_down):
    bp, s_len, d = x_prompt.shape
    bd, t_len, _ = x_sample.shape
    n_pairs = e_w_in.shape[0]
    past = page_table.shape[1] * PAGE_SIZE
    n_pool = cache_k.shape[1]
    xp, xs = x_prompt, x_sample
    outs_p = [[] for _ in range(7)]
    outs_s = [[] for _ in range(7)]
    cos_p, sin_p = _rope_tables(jnp.arange(s_len))
    cos_s, sin_s = _rope_tables(jnp.tile(past + jnp.arange(t_len), bd))
    perm = np.concatenate([np.arange(h * HEAD_DIM, (h + 1) * HEAD_DIM) for h in HEAD_ORDER])
    expand = (jnp.arange(LANES)[:, None] == (jnp.arange(D_INNER)[None, :] // SSM_HEADDIM)).astype(BF16)
    blk = jnp.arange(LANES) // HEAD_DIM
    bdiag = jnp.where(blk[:, None] == blk[None, :], 1.0 / HEAD_DIM, 0.0).astype(BF16)
    for i in range(n_pairs):
        w = e_w_in[i]
        p = {
            "e_norm_mix": _row(e_norm_mix[i]),
            "e_w_main": jnp.concatenate([w[:, 2048:3584], w[:, 0:512], w[:, 1024:2048], w[:, 512:1024]],
                                        axis=1).astype(BF16),
            "e_w_tail": _pad_cols(w[:, 3584:3600], LANES).astype(BF16),
            "e_conv_a_w": e_conv_a_w[i], "e_conv_a_b": _row(e_conv_a_b[i]),
            "e_ln_a_g": _row(e_ln_a_g[i]), "e_ln_a_b": _row(e_ln_a_b[i]),
            "e_conv_b_w": e_conv_b_w[i], "e_conv_b_b": _row(e_conv_b_b[i]),
            "e_dt_bias": _pad_cols(_row(e_dt_bias[i]), LANES),
            "e_a_neg": _pad_cols(_row(-jnp.exp(e_a_log[i].astype(F32))), LANES),
            "e_expand": expand,
            "e_dskip": _row(jnp.repeat(e_d_skip[i], SSM_HEADDIM)),
            "e_ssm_norm": _row(e_ssm_norm[i]),
            "e_wo_a": e_w_out[i][:D_A].astype(BF16), "e_wo_b": e_w_out[i][D_A:].astype(BF16),
            "e_norm_ffn": _row(e_norm_ffn[i]),
            "e_w_gate": e_w_gate[i].astype(BF16), "e_w_up": e_w_up[i].astype(BF16),
            "e_w_down": e_w_down[i].astype(BF16),
        }
        zero_a = jnp.zeros((bp, CONV_A - 1, D_A), F32)
        zero_h = jnp.zeros((bp, SSM_HEADS, SSM_HEADDIM, SSM_STATE), F32)
        zero_b = jnp.zeros((bp, SSM_CONV - 1, CONV_DIM), F32)
        xp, ca, sm, cb = _even_layer(xp, zero_a, zero_h, zero_b, min(SSD_CHUNK, s_len), p, tm=512, tl_a=256)
        outs_p[0].append(ca); outs_p[1].append(sm); outs_p[2].append(cb)
        xs, ca, sm, cb = _even_layer(xs, state_conv_a[i], state_ssm[i], state_conv_b[i], t_len, p,
                                     tm=bd * t_len, tl_a=t_len)
        outs_s[0].append(ca); outs_s[1].append(sm); outs_s[2].append(cb)

        w = o_w_in[i]
        wo = o_w_out[i]
        p = {
            "o_norm_mix": _row(o_norm_mix[i]),
            "o_w_main": jnp.concatenate([w[:, 0:512][:, perm], w[:, 512:1024], w[:, 1092:2628]], axis=1).astype(BF16),
            "o_w_tail": _pad_cols(w[:, 1024:1092], LANES).astype(BF16),
            "o_q_norm": _row(jnp.tile(o_q_norm[i], 2)), "o_k_norm": _row(jnp.tile(o_k_norm[i], 2)),
            "o_kidx_g": _pad_cols(_row(o_kidx_g[i]), LANES), "o_kidx_b": _pad_cols(_row(o_kidx_b[i]), LANES),
            "o_bd": bdiag,
            "o_conv_d_w": o_conv_d_w[i],
            "o_wo_a": wo[:512][perm].astype(BF16), "o_wo_b": wo[512:].astype(BF16),
            "o_norm_ffn": _row(o_norm_ffn[i]),
            "o_br": jnp.concatenate([_row(o_b_router[i]), jnp.full((1, LANES - N_EXPERTS), NEG, F32)], axis=1),
            "o_we_gate": o_we_gate[i].astype(BF16), "o_we_up": o_we_up[i].astype(BF16),
            "o_we_down": o_we_down[i].astype(BF16),
        }
        wr = _pad_cols(o_w_router[i], LANES)
        p["o_wr_hi"] = wr.astype(BF16)
        p["o_wr_lo"] = (wr - p["o_wr_hi"].astype(F32)).astype(BF16)

        mp = bp * s_len
        x2 = xp.reshape(mp, d)
        zmain, q, k, v, qi, kiw = _odd_project(x2, cos_p, sin_p, p, tm=512)
        r3 = lambda a: a.reshape(bp, s_len, a.shape[-1])
        kk = jnp.concatenate([kiw[:, :IDX_DIM], kiw[:, :IDX_DIM]], axis=1)
        att = dsa_prompt(r3(q), r3(qi), r3(kiw), r3(k), r3(v), r3(kk))
        x2, cd = _odd_tail(x2, att, r3(zmain), jnp.zeros((bp, CONV_D - 1, D_D), F32), p, tm=512, tl_d=512)
        xp = x2.reshape(bp, s_len, d)
        outs_p[3].append(k.reshape(bp, s_len, N_KV, HEAD_DIM)); outs_p[4].append(v.reshape(bp, s_len, N_KV, HEAD_DIM))
        outs_p[5].append(kiw[:, :IDX_DIM].reshape(bp, s_len, IDX_DIM)); outs_p[6].append(cd)

        ms = bd * t_len
        x2 = xs.reshape(ms, d)
        zmain, q, k, v, qi, kiw = _odd_project(x2, cos_s, sin_s, p, tm=ms)
        r3 = lambda a: a.reshape(bd, t_len, a.shape[-1])
        mask_past, mask_new = sdsa_index(page_table, r3(qi), r3(kiw), r3(kiw),
                                         jnp.transpose(cache_kidx[i], (0, 2, 1)))
        att = sdsa_attn(page_table, r3(q), r3(k), r3(v), mask_past, mask_new,
                        jnp.transpose(cache_k[i], (0, 2, 3, 1)), jnp.transpose(cache_v[i], (0, 2, 3, 1)))
        x2, cd = _odd_tail(x2, att, r3(zmain), state_conv_d[i], p, tm=ms, tl_d=t_len)
        xs = x2.reshape(bd, t_len, d)
        outs_s[3].append(k.reshape(bd, t_len, N_KV, HEAD_DIM)); outs_s[4].append(v.reshape(bd, t_len, N_KV, HEAD_DIM))
        outs_s[5].append(kiw[:, :IDX_DIM].reshape(bd, t_len, IDX_DIM)); outs_s[6].append(cd)
    return (xp, xs) + tuple(jnp.stack(o) for o in outs_p) + tuple(jnp.stack(o) for o in outs_s)
```

```python
import functools
import math

import jax
import jax.numpy as jnp
import numpy as np
from jax import lax
from jax.experimental import pallas as pl
from jax.experimental.pallas import tpu as pltpu

F32 = jnp.float32
BF16 = jnp.bfloat16
I32 = jnp.int32

D_MODEL = 1024
D_A = 512
CONV_A = 31
D_INNER = 1024
SSM_HEADDIM = 64
SSM_HEADS = 16
SSM_GROUPS = 2
SSM_STATE = 128
SSM_CONV = 4
CONV_DIM = D_INNER + 2 * SSM_GROUPS * SSM_STATE
SSD_CHUNK = 128
HEAD_DIM = 64
N_HEADS = 8
N_KV = 2
ROT_DIM = 16
ROPE_THETA = 500000.0
IDX_HEADS = 4
IDX_DIM = 64
TOPK_MAX = 256
Q_BLOCK = 128
D_D = 512
CONV_D = 3
D_FF = 2816
N_EXPERTS = 8
PAGE_SIZE = 128
EPS = 1e-6
NEG = -1e30

LANES = 128
SUBLANES = 8
VMEM_LIMIT = 56 * 1024 * 1024

E_MAIN = D_A + D_A + D_INNER + CONV_DIM
E_COL_XBC, E_COL_VAL, E_COL_Z, E_COL_GATE = 0, 1536, 2048, 3072
O_MAIN = 512 + 128 + 128 + 256 + 3 * D_D


def _cparams(sem):
    return pltpu.CompilerParams(dimension_semantics=sem, vmem_limit_bytes=VMEM_LIMIT)


def _bdot(a, b):
    return jnp.dot(a.astype(BF16), b.astype(BF16), preferred_element_type=F32)


def _bdot_nt(a, b):
    return lax.dot_general(a.astype(BF16), b.astype(BF16), (((1,), (1,)), ((), ())),
                           preferred_element_type=F32)


def _split3(a):
    hi = a.astype(BF16)
    r1 = a - hi.astype(F32)
    mid = r1.astype(BF16)
    lo = (r1 - mid.astype(F32)).astype(BF16)
    return hi, mid, lo


def _dot_exact_rhs(a, b_bf16):
    hi, mid, lo = _split3(a)
    out = jnp.dot(lo, b_bf16, preferred_element_type=F32)
    out = out + jnp.dot(mid, b_bf16, preferred_element_type=F32)
    return out + jnp.dot(hi, b_bf16, preferred_element_type=F32)


def _dot_exact_lhs(a_bf16, b):
    hi, mid, lo = _split3(b)
    out = jnp.dot(a_bf16, lo, preferred_element_type=F32)
    out = out + jnp.dot(a_bf16, mid, preferred_element_type=F32)
    return out + jnp.dot(a_bf16, hi, preferred_element_type=F32)


def _sigmoid(x):
    return 1.0 / (1.0 + jnp.exp(-x))


def _silu(x):
    return x * _sigmoid(x)


def _softplus(x):
    return jnp.maximum(x, 0.0) + jnp.log(1.0 + jnp.exp(-jnp.abs(x)))


def _norm_proj_kernel(x_ref, g_ref, w_ref, wt_ref, main_ref, tail_ref, xn_ref):
    j = pl.program_id(1)

    @pl.when(j == 0)
    def _():
        x = x_ref[...]
        y = x * lax.rsqrt(jnp.mean(x * x, axis=-1, keepdims=True) + EPS) * g_ref[...]
        xn = y.astype(BF16)
        xn_ref[...] = xn
        tail_ref[...] = jnp.dot(xn, wt_ref[...], preferred_element_type=F32)

    main_ref[...] = jnp.dot(xn_ref[...], w_ref[...], preferred_element_type=F32)


def norm_proj(x, g, w_main, w_tail, *, tm, tn):
    m, k = x.shape
    n = w_main.shape[1]
    return pl.pallas_call(
        _norm_proj_kernel,
        out_shape=(jax.ShapeDtypeStruct((m, n), F32), jax.ShapeDtypeStruct((m, LANES), F32)),
        grid=(m // tm, n // tn),
        in_specs=[pl.BlockSpec((tm, k), lambda i, j: (i, 0)),
                  pl.BlockSpec((1, k), lambda i, j: (0, 0)),
                  pl.BlockSpec((k, tn), lambda i, j: (0, j)),
                  pl.BlockSpec((k, LANES), lambda i, j: (0, 0))],
        out_specs=(pl.BlockSpec((tm, tn), lambda i, j: (i, j)),
                   pl.BlockSpec((tm, LANES), lambda i, j: (i, 0))),
        scratch_shapes=[pltpu.VMEM((tm, k), BF16)],
        compiler_params=_cparams(("arbitrary", "arbitrary")),
        name="norm_proj",
    )(x, g, w_main, w_tail)


def _dwconv_from_buf(buf_ref, w_ref, kw, carry, tl):
    acc = None
    for k in range(kw):
        start = carry - (kw - 1) + k
        term = buf_ref[pl.ds(start, tl), :] * w_ref[pl.ds(k, 1), :]
        acc = term if acc is None else acc + term
    return acc


CARRY_A = 32


def _conv_a_kernel(val_ref, gate_ref, prev_ref, w_ref, b_ref, lg_ref, lb_ref, ya_ref, new_ref, buf_ref, *, tl):
    l = pl.program_id(1)

    @pl.when(l == 0)
    def _():
        buf_ref[pl.ds(0, CARRY_A), :] = prev_ref[0]

    u = val_ref[0] * _sigmoid(gate_ref[0])
    buf_ref[pl.ds(CARRY_A, tl), :] = u
    y = _dwconv_from_buf(buf_ref, w_ref, CONV_A, CARRY_A, tl) + b_ref[...]
    mu = jnp.mean(y, axis=-1, keepdims=True)
    yc = y - mu
    var = jnp.mean(yc * yc, axis=-1, keepdims=True)
    yn = yc * lax.rsqrt(var + EPS) * lg_ref[...] + lb_ref[...]
    ya_ref[0] = _silu(yn)
    hist = buf_ref[pl.ds(tl, CARRY_A), :]
    new_ref[0] = hist
    buf_ref[pl.ds(0, CARRY_A), :] = hist


def conv_a(zmain, prev32, w, b, lg, lb, *, tl):
    bsz, L, _ = zmain.shape
    kern = functools.partial(_conv_a_kernel, tl=tl)
    return pl.pallas_call(
        kern,
        out_shape=(jax.ShapeDtypeStruct((bsz, L, D_A), F32), jax.ShapeDtypeStruct((bsz, CARRY_A, D_A), F32)),
        grid=(bsz, L // tl),
        in_specs=[pl.BlockSpec((1, tl, D_A), lambda b_, l: (b_, l, E_COL_VAL // D_A)),
                  pl.BlockSpec((1, tl, D_A), lambda b_, l: (b_, l, E_COL_GATE // D_A)),
                  pl.BlockSpec((1, CARRY_A, D_A), lambda b_, l: (b_, 0, 0)),
                  pl.BlockSpec((CONV_A, D_A), lambda b_, l: (0, 0)),
                  pl.BlockSpec((1, D_A), lambda b_, l: (0, 0)),
                  pl.BlockSpec((1, D_A), lambda b_, l: (0, 0)),
                  pl.BlockSpec((1, D_A), lambda b_, l: (0, 0))],
        out_specs=(pl.BlockSpec((1, tl, D_A), lambda b_, l: (b_, l, 0)),
                   pl.BlockSpec((1, CARRY_A, D_A), lambda b_, l: (b_, 0, 0))),
        scratch_shapes=[pltpu.VMEM((CARRY_A + tl, D_A), F32)],
        compiler_params=_cparams(("arbitrary", "arbitrary")),
        name="conv_a",
    )(zmain, zmain, prev32, w, b, lg, lb)


CARRY_B = 8
SSD_LP = 128


def _ssd_kernel(xbc_ref, z_ref, dt_ref, prevb_ref, h0_ref, cw_ref, cb_ref, dtb_ref, a_ref, expand_ref,
                dskip_ref, nrm_ref, y_ref, newb_ref, hout_ref, buf_ref, ht_ref, *, lc):
    c = pl.program_id(1)
    nc = pl.num_programs(1)
    lp = SSD_LP

    @pl.when(c == 0)
    def _():
        buf_ref[pl.ds(0, CARRY_B), :] = prevb_ref[0]
        ht_ref[...] = h0_ref[0].T
        if lc < lp:
            buf_ref[pl.ds(CARRY_B + lc, lp - lc), :] = jnp.zeros((lp - lc, CONV_DIM), F32)

    buf_ref[pl.ds(CARRY_B, lc), :] = xbc_ref[0]
    xbc = _silu(_dwconv_from_buf(buf_ref, cw_ref, SSM_CONV, CARRY_B, lp) + cb_ref[...])
    hist = buf_ref[pl.ds(lc, CARRY_B), :]
    newb_ref[0] = hist
    buf_ref[pl.ds(0, CARRY_B), :] = hist

    xs = xbc[:, :D_INNER]
    if lc < lp:
        dt_raw = jnp.concatenate([dt_ref[0], jnp.zeros((lp - lc, LANES), F32)], axis=0)
        zg = jnp.concatenate([z_ref[0], jnp.zeros((lp - lc, D_INNER), F32)], axis=0)
    else:
        dt_raw = dt_ref[0]
        zg = z_ref[0]
    dt = _softplus(dt_raw + dtb_ref[...])
    if lc < lp:
        row = lax.broadcasted_iota(I32, (lp, LANES), 0)
        dt = jnp.where(row < lc, dt, 0.0)
    lane = lax.broadcasted_iota(I32, (lp, LANES), 1)
    dt = jnp.where(lane < SSM_HEADS, dt, 0.0)
    da = dt * a_ref[...]

    r_i = lax.broadcasted_iota(I32, (lp, lp), 0)
    c_i = lax.broadcasted_iota(I32, (lp, lp), 1)
    causal = r_i >= c_i
    tri = jnp.where(causal, 1.0, 0.0).astype(BF16)
    a_col = _dot_exact_lhs(tri, da)
    a_row = a_col.T
    dt_row = dt.T
    a_last = a_col[lp - 1:lp, :]

    expand = expand_ref[...]
    a_exp = _dot_exact_rhs(a_col, expand)
    dt_exp = _dot_exact_rhs(dt, expand)
    al_exp = _dot_exact_rhs(a_last, expand)

    ht = ht_ref[...]
    y_parts = []
    for g in range(SSM_GROUPS):
        bm = xbc[:, D_INNER + g * SSM_STATE: D_INNER + (g + 1) * SSM_STATE]
        cm = xbc[:, D_INNER + (SSM_GROUPS + g) * SSM_STATE: D_INNER + (SSM_GROUPS + g + 1) * SSM_STATE]
        cb = _bdot_nt(cm, bm)
        gs = slice(g * 512, (g + 1) * 512)
        y_inter = _bdot(cm, ht[:, gs]) * jnp.exp(a_exp[:, gs])
        heads = []
        for r in range(SSM_HEADS // SSM_GROUPS):
            h = g * (SSM_HEADS // SSM_GROUPS) + r
            seg = a_col[:, h:h + 1] - a_row[h:h + 1, :]
            decay = jnp.where(causal, jnp.exp(jnp.where(causal, seg, 0.0)), 0.0)
            wts = cb * decay * dt_row[h:h + 1, :]
            heads.append(_bdot(wts, xs[:, h * SSM_HEADDIM:(h + 1) * SSM_HEADDIM]))
        y_parts.append(jnp.concatenate(heads, axis=-1) + y_inter)
        xw = xs[:, gs] * (jnp.exp(al_exp[:, gs] - a_exp[:, gs]) * dt_exp[:, gs])
        ht_ref[:, gs] = ht[:, gs] * jnp.exp(al_exp[:, gs]) + _bdot(bm.T, xw)
    y = jnp.concatenate(y_parts, axis=-1)
    y = y + dskip_ref[...] * xs
    y = y * _silu(zg)
    outs = []
    for g in range(SSM_GROUPS):
        yg = y[:, g * 512:(g + 1) * 512]
        outs.append(yg * lax.rsqrt(jnp.mean(yg * yg, axis=-1, keepdims=True) + EPS))
    yb = jnp.concatenate(outs, axis=-1) * nrm_ref[...]
    y_ref[0] = yb[:lc, :]

    @pl.when(c == nc - 1)
    def _():
        hout_ref[0] = ht_ref[...].T


def ssd_mixer(zmain, ztail, prev8, h0, cw, cb, dtb, a_neg, expand, dskip_exp, nrm, *, lc):
    bsz, L, _ = zmain.shape
    kern = functools.partial(_ssd_kernel, lc=lc)
    return pl.pallas_call(
        kern,
        out_shape=(jax.ShapeDtypeStruct((bsz, L, D_INNER), F32),
                   jax.ShapeDtypeStruct((bsz, CARRY_B, CONV_DIM), F32),
                   jax.ShapeDtypeStruct((bsz, D_INNER, SSM_STATE), F32)),
        grid=(bsz, L // lc),
        in_specs=[pl.BlockSpec((1, lc, CONV_DIM), lambda b_, c: (b_, c, E_COL_XBC // CONV_DIM)),
                  pl.BlockSpec((1, lc, D_INNER), lambda b_, c: (b_, c, E_COL_Z // D_INNER)),
                  pl.BlockSpec((1, lc, LANES), lambda b_, c: (b_, c, 0)),
                  pl.BlockSpec((1, CARRY_B, CONV_DIM), lambda b_, c: (b_, 0, 0)),
                  pl.BlockSpec((1, D_INNER, SSM_STATE), lambda b_, c: (b_, 0, 0)),
                  pl.BlockSpec((SSM_CONV, CONV_DIM), lambda b_, c: (0, 0)),
                  pl.BlockSpec((1, CONV_DIM), lambda b_, c: (0, 0)),
                  pl.BlockSpec((1, LANES), lambda b_, c: (0, 0)),
                  pl.BlockSpec((1, LANES), lambda b_, c: (0, 0)),
                  pl.BlockSpec((LANES, D_INNER), lambda b_, c: (0, 0)),
                  pl.BlockSpec((1, D_INNER), lambda b_, c: (0, 0)),
                  pl.BlockSpec((1, D_INNER), lambda b_, c: (0, 0))],
        out_specs=(pl.BlockSpec((1, lc, D_INNER), lambda b_, c: (b_, c, 0)),
                   pl.BlockSpec((1, CARRY_B, CONV_DIM), lambda b_, c: (b_, 0, 0)),
                   pl.BlockSpec((1, D_INNER, SSM_STATE), lambda b_, c: (b_, 0, 0))),
        scratch_shapes=[pltpu.VMEM((CARRY_B + SSD_LP, CONV_DIM), F32),
                        pltpu.VMEM((SSM_STATE, D_INNER), F32)],
        compiler_params=_cparams(("arbitrary", "arbitrary")),
        name="ssd_mixer",
    )(zmain, zmain, ztail, prev8, h0, cw, cb, dtb, a_neg, expand, dskip_exp, nrm)


def _out_proj_kernel(x_ref, a_ref, b_ref, wa_ref, wb_ref, o_ref):
    acc = jnp.dot(a_ref[...].astype(BF16), wa_ref[...], preferred_element_type=F32)
    acc = acc + jnp.dot(b_ref[...].astype(BF16), wb_ref[...], preferred_element_type=F32)
    o_ref[...] = x_ref[...] + acc


def out_proj(x, a, b, wa, wb, *, tm):
    m, d = x.shape
    ka, kb = a.shape[1], b.shape[1]
    return pl.pallas_call(
        _out_proj_kernel,
        out_shape=jax.ShapeDtypeStruct((m, d), F32),
        grid=(m // tm,),
        in_specs=[pl.BlockSpec((tm, d), lambda i: (i, 0)),
                  pl.BlockSpec((tm, ka), lambda i: (i, 0)),
                  pl.BlockSpec((tm, kb), lambda i: (i, 0)),
                  pl.BlockSpec((ka, d), lambda i: (0, 0)),
                  pl.BlockSpec((kb, d), lambda i: (0, 0))],
        out_specs=pl.BlockSpec((tm, d), lambda i: (i, 0)),
        compiler_params=_cparams(("arbitrary",)),
        name="out_proj",
    )(x, a, b, wa, wb)


def _swiglu_kernel(x_ref, g_ref, wg_ref, wu_ref, wd_ref, o_ref, xn_ref, acc_ref):
    f = pl.program_id(1)
    nf = pl.num_programs(1)

    @pl.when(f == 0)
    def _():
        x = x_ref[...]
        y = x * lax.rsqrt(jnp.mean(x * x, axis=-1, keepdims=True) + EPS) * g_ref[...]
        xn_ref[...] = y.astype(BF16)
        acc_ref[...] = jnp.zeros_like(acc_ref)

    xn = xn_ref[...]
    gate = jnp.dot(xn, wg_ref[...], preferred_element_type=F32)
    up = jnp.dot(xn, wu_ref[...], preferred_element_type=F32)
    hid = (_silu(gate) * up).astype(BF16)
    acc_ref[...] += jnp.dot(hid, wd_ref[...], preferred_element_type=F32)

    @pl.when(f == nf - 1)
    def _():
        o_ref[...] = x_ref[...] + acc_ref[...]


def swiglu_ffn(x, g, wg, wu, wd, *, tm, tf):
    m, d = x.shape
    ff = wg.shape[1]
    return pl.pallas_call(
        _swiglu_kernel,
        out_shape=jax.ShapeDtypeStruct((m, d), F32),
        grid=(m // tm, ff // tf),
        in_specs=[pl.BlockSpec((tm, d), lambda i, f: (i, 0)),
                  pl.BlockSpec((1, d), lambda i, f: (0, 0)),
                  pl.BlockSpec((d, tf), lambda i, f: (0, f)),
                  pl.BlockSpec((d, tf), lambda i, f: (0, f)),
                  pl.BlockSpec((tf, d), lambda i, f: (f, 0))],
        out_specs=pl.BlockSpec((tm, d), lambda i, f: (i, 0)),
        scratch_shapes=[pltpu.VMEM((tm, d), BF16), pltpu.VMEM((tm, d), F32)],
        compiler_params=_cparams(("arbitrary", "arbitrary")),
        name="swiglu_ffn",
    )(x, g, wg, wu, wd)


O_COL_Q, O_COL_K, O_COL_V, O_COL_QI, O_COL_DIN, O_COL_BG, O_COL_CG = 0, 512, 640, 768, 1024, 1536, 2048


def _rope128(x, cos_t, sin_t, lo_mask):
    up = pltpu.roll(x, LANES - ROT_DIM // 2, 1)
    dn = pltpu.roll(x, ROT_DIM // 2, 1)
    return x * cos_t + jnp.where(lo_mask, up, dn) * sin_t


def _odd_post_kernel(qkvq_ref, tail_ref, cos_ref, sin_ref, qn_ref, kn_ref, kg_ref, kb_ref, bd_ref,
                     q_ref, k_ref, v_ref, qi_ref, kiw_ref):
    cos_t = cos_ref[...]
    sin_t = sin_ref[...]
    lane = lax.broadcasted_iota(I32, cos_t.shape, 1)
    lo_mask = (lane % HEAD_DIM) < (ROT_DIM // 2)
    bd = bd_ref[...]
    z = qkvq_ref[...]
    for c in range(4):
        x = z[:, c * LANES:(c + 1) * LANES]
        ms = _dot_exact_rhs(x * x, bd)
        xn = x * lax.rsqrt(ms + EPS) * qn_ref[...]
        q_ref[:, c * LANES:(c + 1) * LANES] = _rope128(xn, cos_t, sin_t, lo_mask)
    x = z[:, O_COL_K:O_COL_K + LANES]
    ms = _dot_exact_rhs(x * x, bd)
    xn = x * lax.rsqrt(ms + EPS) * kn_ref[...]
    k_ref[...] = _rope128(xn, cos_t, sin_t, lo_mask)
    v_ref[...] = z[:, O_COL_V:O_COL_V + LANES]
    for c in range(2):
        x = z[:, O_COL_QI + c * LANES: O_COL_QI + (c + 1) * LANES]
        qi_ref[:, c * LANES:(c + 1) * LANES] = _rope128(x, cos_t, sin_t, lo_mask)
    t = tail_ref[...]
    lane_t = lax.broadcasted_iota(I32, t.shape, 1)
    is_ki = lane_t < IDX_DIM
    mu = _dot_exact_rhs(t, bd)
    tc = t - mu
    var = _dot_exact_rhs(tc * tc, bd)
    kin = tc * lax.rsqrt(var + EPS) * kg_ref[...] + kb_ref[...]
    kin = _rope128(kin, cos_t, sin_t, lo_mask)
    kiw_ref[...] = jnp.where(is_ki, kin, t * (IDX_HEADS ** -0.5 * IDX_DIM ** -0.5))


def odd_post(zmain, ztail, cos_t, sin_t, qn, kn, kg, kb, bd, *, tm):
    m = zmain.shape[0]
    nrope = cos_t.shape[0] // tm
    row = lambda i: (i, 0)
    rope = lambda i: (i % nrope, 0)
    const = lambda i: (0, 0)
    return pl.pallas_call(
        _odd_post_kernel,
        out_shape=(jax.ShapeDtypeStruct((m, 512), F32), jax.ShapeDtypeStruct((m, LANES), F32),
                   jax.ShapeDtypeStruct((m, LANES), F32), jax.ShapeDtypeStruct((m, 256), F32),
                   jax.ShapeDtypeStruct((m, LANES), F32)),
        grid=(m // tm,),
        in_specs=[pl.BlockSpec((tm, 1024), row), pl.BlockSpec((tm, LANES), row),
                  pl.BlockSpec((tm, LANES), rope), pl.BlockSpec((tm, LANES), rope),
                  pl.BlockSpec((1, LANES), const), pl.BlockSpec((1, LANES), const),
                  pl.BlockSpec((1, LANES), const), pl.BlockSpec((1, LANES), const),
                  pl.BlockSpec((LANES, LANES), const)],
        out_specs=(pl.BlockSpec((tm, 512), row), pl.BlockSpec((tm, LANES), row), pl.BlockSpec((tm, LANES), row),
                   pl.BlockSpec((tm, 256), row), pl.BlockSpec((tm, LANES), row)),
        compiler_params=_cparams(("arbitrary",)),
        name="odd_post",
    )(zmain, ztail, cos_t, sin_t, qn, kn, kg, kb, bd)


INT_MIN = -2 ** 31
F32_MIN_NORMAL_BITS = 0x00800000


def _code_to_float(c):
    return pltpu.bitcast(c ^ ((c >> 31) & 0x7FFFFFFF), F32)


def _kth_largest(count_ge, k, shape):
    ans = jnp.full(shape, INT_MIN, I32)
    cand = jnp.zeros(shape, I32)
    ans = jnp.where(count_ge(_code_to_float(cand)) >= k, cand, ans)

    def body(i, ans):
        cand = ans | jnp.left_shift(jnp.int32(1), 30 - i)
        return jnp.where(count_ge(_code_to_float(cand)) >= k, cand, ans)

    ans = lax.fori_loop(0, 31, body, ans)
    ans = jnp.where((ans > 0) & (ans < F32_MIN_NORMAL_BITS), 0, ans)
    return _code_to_float(ans)


def _sum_leading(x):
    n = x.shape[0]
    extra = None
    while n > 1:
        if n % 2:
            extra = x[n - 1] if extra is None else extra + x[n - 1]
            n -= 1
        x = x[:n // 2] + x[n // 2:n]
        n //= 2
    return x[0] if extra is None else x[0] + extra


def _upper_tri_bf16(n):
    r = lax.broadcasted_iota(I32, (n, n), 0)
    c = lax.broadcasted_iota(I32, (n, n), 1)
    return jnp.where(r <= c, 1.0, 0.0).astype(BF16)


def _dsa_prompt_kernel(q_ref, qi_ref, wi_ref, k_ref, v_ref, kk_ref, o_ref, keys_ref, sel_ref, *, s_len, topk):
    t0 = pl.program_id(1) * Q_BLOCK
    lane = lax.broadcasted_iota(I32, (Q_BLOCK, LANES), 1)
    row = lax.broadcasted_iota(I32, (Q_BLOCK, LANES), 0)
    lo = lane < HEAD_DIM
    hi = lane >= HEAD_DIM

    kk = kk_ref[0].astype(BF16)
    qi = qi_ref[0]
    wi = wi_ref[0]
    sc = None
    for h in range(IDX_HEADS):
        x = qi[:, (h // 2) * LANES:(h // 2 + 1) * LANES]
        x = jnp.where(lo if h % 2 == 0 else hi, x, 0.0)
        d = _bdot_nt(x, kk)
        term = jnp.maximum(d, 0.0) * wi[:, IDX_DIM + h:IDX_DIM + h + 1]
        sc = term if sc is None else sc + term
    kpos = lax.broadcasted_iota(I32, (Q_BLOCK, s_len), 1)
    qpos = t0 + lax.broadcasted_iota(I32, (Q_BLOCK, s_len), 0)
    sc = jnp.where(sc == 0.0, 0.0, sc)
    sc = jnp.where(kpos <= qpos, sc, NEG)
    keys_ref[...] = sc

    def count_ge(cand):
        return jnp.sum(jnp.where(keys_ref[...] >= cand, 1.0, 0.0), axis=-1, keepdims=True)

    thr = _kth_largest(count_ge, float(topk), (Q_BLOCK, 1))
    n_gt = jnp.sum(jnp.where(keys_ref[...] > thr, 1.0, 0.0), axis=-1, keepdims=True)
    need = float(topk) - n_gt
    ut = _upper_tri_bf16(LANES)
    off = jnp.zeros((Q_BLOCK, 1), F32)
    for j in range(s_len // LANES):
        kj = keys_ref[:, j * LANES:(j + 1) * LANES]
        eq = kj == thr
        pre = jnp.dot(jnp.where(eq, 1.0, 0.0).astype(BF16), ut, preferred_element_type=F32) + off
        adm = (j * LANES + lane) <= (t0 + row)
        keep = jnp.where(eq, jnp.where(pre <= need, 1.0, 0.0), jnp.where(kj > thr, 1.0, 0.0))
        sel_ref[:, j * LANES:(j + 1) * LANES] = jnp.where(adm, keep, 0.0)
        off = pre[:, LANES - 1:LANES]

    kb = k_ref[0].astype(BF16)
    vb = v_ref[0].astype(BF16)
    selm = sel_ref[...]
    sel4 = jnp.concatenate([selm, selm, selm, selm], axis=0) > 0.5
    q = q_ref[0]
    outs = []
    for g in range(N_KV):
        keep = lo if g == 0 else hi
        qg = jnp.concatenate([jnp.where(keep, q[:, r * LANES:(r + 1) * LANES], 0.0) for r in range(4)], axis=0)
        s = _bdot_nt(qg, kb) * (HEAD_DIM ** -0.5)
        s = jnp.where(sel4, s, NEG)
        m = jnp.max(s, axis=-1, keepdims=True)
        e = jnp.exp(s - m)
        l = jnp.sum(e, axis=-1, keepdims=True)
        outs.append(jnp.dot(e.astype(BF16), vb, preferred_element_type=F32) / l)
    lo4 = lax.broadcasted_iota(I32, (4 * Q_BLOCK, LANES), 1) < HEAD_DIM
    oc = jnp.where(lo4, outs[0], outs[1])
    for r in range(4):
        o_ref[0, :, r * LANES:(r + 1) * LANES] = oc[r * Q_BLOCK:(r + 1) * Q_BLOCK, :]


def dsa_prompt(q, qi, wi, k, v, kk):
    bsz, s_len, _ = q.shape
    topk = min(TOPK_MAX, s_len // 4)
    kern = functools.partial(_dsa_prompt_kernel, s_len=s_len, topk=topk)
    qblk = lambda w: pl.BlockSpec((1, Q_BLOCK, w), lambda b_, i: (b_, i, 0))
    full = pl.BlockSpec((1, s_len, LANES), lambda b_, i: (b_, 0, 0))
    return pl.pallas_call(
        kern,
        out_shape=jax.ShapeDtypeStruct((bsz, s_len, 512), F32),
        grid=(bsz, s_len // Q_BLOCK),
        in_specs=[qblk(512), qblk(256), qblk(LANES), full, full, full],
        out_specs=qblk(512),
        scratch_shapes=[pltpu.VMEM((Q_BLOCK, s_len), F32), pltpu.VMEM((Q_BLOCK, s_len), F32)],
        compiler_params=_cparams(("arbitrary", "arbitrary")),
        name="dsa_prompt",
    )(q, qi, wi, k, v, kk)


CARRY_D = 8


def _conv_d_kernel(din_ref, bg_ref, cg_ref, prev_ref, w_ref, y_ref, new_ref, buf_ref, *, tl):
    l = pl.program_id(1)

    @pl.when(l == 0)
    def _():
        buf_ref[pl.ds(0, CARRY_D), :] = prev_ref[0]

    buf_ref[pl.ds(CARRY_D, tl), :] = cg_ref[0] * din_ref[0]
    y_ref[0] = bg_ref[0] * _dwconv_from_buf(buf_ref, w_ref, CONV_D, CARRY_D, tl)
    hist = buf_ref[pl.ds(tl, CARRY_D), :]
    new_ref[0] = hist
    buf_ref[pl.ds(0, CARRY_D), :] = hist


def conv_d(zmain, prev8, w, *, tl):
    bsz, L, _ = zmain.shape
    kern = functools.partial(_conv_d_kernel, tl=tl)
    col = lambda c: pl.BlockSpec((1, tl, D_D), lambda b_, l: (b_, l, c // D_D))
    return pl.pallas_call(
        kern,
        out_shape=(jax.ShapeDtypeStruct((bsz, L, D_D), F32), jax.ShapeDtypeStruct((bsz, CARRY_D, D_D), F32)),
        grid=(bsz, L // tl),
        in_specs=[col(O_COL_DIN), col(O_COL_BG), col(O_COL_CG),
                  pl.BlockSpec((1, CARRY_D, D_D), lambda b_, l: (b_, 0, 0)),
                  pl.BlockSpec((CONV_D, D_D), lambda b_, l: (0, 0))],
        out_specs=(pl.BlockSpec((1, tl, D_D), lambda b_, l: (b_, l, 0)),
                   pl.BlockSpec((1, CARRY_D, D_D), lambda b_, l: (b_, 0, 0))),
        scratch_shapes=[pltpu.VMEM((CARRY_D + tl, D_D), F32)],
        compiler_params=_cparams(("arbitrary", "arbitrary")),
        name="conv_d",
    )(zmain, zmain, zmain, prev8, w)


def _router_kernel(x_ref, g_ref, whi_ref, wlo_ref, br_ref, xn_ref, comb_ref):
    x = x_ref[...]
    y = x * lax.rsqrt(jnp.mean(x * x, axis=-1, keepdims=True) + EPS) * g_ref[...]
    yhi = y.astype(BF16)
    xn_ref[...] = yhi
    ylo = (y - yhi.astype(F32)).astype(BF16)
    logits = jnp.dot(ylo, whi_ref[...], preferred_element_type=F32)
    logits = logits + jnp.dot(yhi, wlo_ref[...], preferred_element_type=F32)
    logits = logits + jnp.dot(yhi, whi_ref[...], preferred_element_type=F32) + br_ref[...]
    lane = lax.broadcasted_iota(I32, logits.shape, 1).astype(F32)
    m1 = jnp.max(logits, axis=-1, keepdims=True)
    i1 = jnp.min(jnp.where(logits == m1, lane, float(LANES)), axis=-1, keepdims=True)
    rest = jnp.where(lane == i1, -3e38, logits)
    m2 = jnp.max(rest, axis=-1, keepdims=True)
    i2 = jnp.min(jnp.where(rest == m2, lane, float(LANES)), axis=-1, keepdims=True)
    e2 = jnp.exp(m2 - m1)
    g1 = 1.0 / (1.0 + e2)
    g2 = e2 / (1.0 + e2)
    comb_ref[...] = jnp.where(lane == i1, g1, 0.0) + jnp.where(lane == i2, g2, 0.0)


def moe_router(x, g, w_hi, w_lo, b_pad, *, tm):
    m, d = x.shape
    return pl.pallas_call(
        _router_kernel,
        out_shape=(jax.ShapeDtypeStruct((m, d), BF16), jax.ShapeDtypeStruct((m, LANES), F32)),
        grid=(m // tm,),
        in_specs=[pl.BlockSpec((tm, d), lambda i: (i, 0)),
                  pl.BlockSpec((1, d), lambda i: (0, 0)),
                  pl.BlockSpec((d, LANES), lambda i: (0, 0)),
                  pl.BlockSpec((d, LANES), lambda i: (0, 0)),
                  pl.BlockSpec((1, LANES), lambda i: (0, 0))],
        out_specs=(pl.BlockSpec((tm, d), lambda i: (i, 0)), pl.BlockSpec((tm, LANES), lambda i: (i, 0))),
        compiler_params=_cparams(("arbitrary",)),
        name="moe_router",
    )(x, g, w_hi, w_lo, b_pad)


def _moe_dense_kernel(x_ref, xn_ref, comb_ref, wg_ref, wu_ref, wd_ref, o_ref, acc_ref, acce_ref):
    e = pl.program_id(1)
    f = pl.program_id(2)
    ne = pl.num_programs(1)
    nf = pl.num_programs(2)

    @pl.when((e == 0) & (f == 0))
    def _():
        acc_ref[...] = jnp.zeros_like(acc_ref)

    @pl.when(f == 0)
    def _():
        acce_ref[...] = jnp.zeros_like(acce_ref)

    xn = xn_ref[...]
    gate = jnp.dot(xn, wg_ref[0], preferred_element_type=F32)
    up = jnp.dot(xn, wu_ref[0], preferred_element_type=F32)
    hid = (_silu(gate) * up).astype(BF16)
    acce_ref[...] += jnp.dot(hid, wd_ref[0], preferred_element_type=F32)

    @pl.when(f == nf - 1)
    def _():
        comb = comb_ref[...]
        lane = lax.broadcasted_iota(I32, comb.shape, 1)
        c = jnp.sum(jnp.where(lane == e, comb, 0.0), axis=-1, keepdims=True)
        acc_ref[...] += c * acce_ref[...]

    @pl.when((e == ne - 1) & (f == nf - 1))
    def _():
        o_ref[...] = x_ref[...] + acc_ref[...]


def moe_dense(x, xn, comb, wg, wu, wd, *, tm, tf):
    m, d = x.shape
    ne, _, ff = wg.shape
    return pl.pallas_call(
        _moe_dense_kernel,
        out_shape=jax.ShapeDtypeStruct((m, d), F32),
        grid=(m // tm, ne, ff // tf),
        in_specs=[pl.BlockSpec((tm, d), lambda i, e, f: (i, 0)),
                  pl.BlockSpec((tm, d), lambda i, e, f: (i, 0)),
                  pl.BlockSpec((tm, LANES), lambda i, e, f: (i, 0)),
                  pl.BlockSpec((1, d, tf), lambda i, e, f: (e, 0, f)),
                  pl.BlockSpec((1, d, tf), lambda i, e, f: (e, 0, f)),
                  pl.BlockSpec((1, tf, d), lambda i, e, f: (e, f, 0))],
        out_specs=pl.BlockSpec((tm, d), lambda i, e, f: (i, 0)),
        scratch_shapes=[pltpu.VMEM((tm, d), F32), pltpu.VMEM((tm, d), F32)],
        compiler_params=_cparams(("arbitrary", "arbitrary", "arbitrary")),
        name="moe_dense",
    )(x, xn, comb, wg, wu, wd)


MOE_TILE = 512
R_I1, R_I2, R_G1, R_G2, R_RANK1, R_RANK2 = 0, 1, 2, 3, 4, 5


def _router_rank_kernel(x_ref, g_ref, whi_ref, wlo_ref, br_ref, info_ref, cnt_ref, run_ref):
    i = pl.program_id(0)

    @pl.when(i == 0)
    def _():
        run_ref[...] = jnp.zeros_like(run_ref)

    x = x_ref[...]
    tm = x.shape[0]
    y = x * lax.rsqrt(jnp.mean(x * x, axis=-1, keepdims=True) + EPS) * g_ref[...]
    yhi = y.astype(BF16)
    ylo = (y - yhi.astype(F32)).astype(BF16)
    logits = jnp.dot(ylo, whi_ref[...], preferred_element_type=F32)
    logits = logits + jnp.dot(yhi, wlo_ref[...], preferred_element_type=F32)
    logits = logits + jnp.dot(yhi, whi_ref[...], preferred_element_type=F32) + br_ref[...]
    lane = lax.broadcasted_iota(I32, logits.shape, 1).astype(F32)
    m1 = jnp.max(logits, axis=-1, keepdims=True)
    i1 = jnp.min(jnp.where(logits == m1, lane, float(LANES)), axis=-1, keepdims=True)
    rest = jnp.where(lane == i1, -3e38, logits)
    m2 = jnp.max(rest, axis=-1, keepdims=True)
    i2 = jnp.min(jnp.where(rest == m2, lane, float(LANES)), axis=-1, keepdims=True)
    e2 = jnp.exp(m2 - m1)
    g1 = 1.0 / (1.0 + e2)
    g2 = e2 / (1.0 + e2)
    oh1 = jnp.where(lane == i1, 1.0, 0.0)
    oh2 = jnp.where(lane == i2, 1.0, 0.0)
    both = oh1 + oh2
    r_i = lax.broadcasted_iota(I32, (tm, tm), 0)
    c_i = lax.broadcasted_iota(I32, (tm, tm), 1)
    strict_lower = jnp.where(c_i < r_i, 1.0, 0.0).astype(BF16)
    before = jnp.dot(strict_lower, both.astype(BF16), preferred_element_type=F32) + run_ref[...]
    rank1 = jnp.sum(before * oh1, axis=-1, keepdims=True)
    rank2 = jnp.sum(before * oh2, axis=-1, keepdims=True)
    run_ref[...] += jnp.sum(both, axis=0, keepdims=True)
    info = jnp.where(lane == R_I1, i1, 0.0)
    for col, val in ((R_I2, i2), (R_G1, g1), (R_G2, g2), (R_RANK1, rank1), (R_RANK2, rank2)):
        info = jnp.where(lane == col, val, info)
    info_ref[...] = info
    cnt_ref[...] = run_ref[...]


def moe_router_rank(x, g, w_hi, w_lo, b_pad, *, tm):
    m, d = x.shape
    return pl.pallas_call(
        _router_rank_kernel,
        out_shape=(jax.ShapeDtypeStruct((m, LANES), F32), jax.ShapeDtypeStruct((1, LANES), F32)),
        grid=(m // tm,),
        in_specs=[pl.BlockSpec((tm, d), lambda i: (i, 0)),
                  pl.BlockSpec((1, d), lambda i: (0, 0)),
                  pl.BlockSpec((d, LANES), lambda i: (0, 0)),
                  pl.BlockSpec((d, LANES), lambda i: (0, 0)),
                  pl.BlockSpec((1, LANES), lambda i: (0, 0))],
        out_specs=(pl.BlockSpec((tm, LANES), lambda i: (i, 0)), pl.BlockSpec((1, LANES), lambda i: (0, 0))),
        scratch_shapes=[pltpu.VMEM((1, LANES), F32)],
        compiler_params=_cparams(("arbitrary",)),
        name="moe_router_rank",
    )(x, g, w_hi, w_lo, b_pad)


def _row_copy(src_ref, si, dst_ref, di, sem):
    return pltpu.make_async_copy(src_ref.at[pl.ds(si, 1)], dst_ref.at[pl.ds(di, 1)], sem)


def _moe_dispatch_kernel(d1_ref, d2_ref, x_ref, xs_ref, sem, *, ct):
    def issue(t, carry):
        _row_copy(x_ref, t, xs_ref, d1_ref[0, 0, t], sem).start()
        _row_copy(x_ref, t, xs_ref, d2_ref[0, 0, t], sem).start()
        return carry

    lax.fori_loop(0, ct, issue, 0)

    def drain(t, carry):
        _row_copy(x_ref, 0, xs_ref, 0, sem).wait()
        _row_copy(x_ref, 0, xs_ref, 0, sem).wait()
        return carry

    lax.fori_loop(0, ct, drain, 0)


def moe_dispatch(x, dest1, dest2, *, ct):
    m, d = x.shape
    smem_row = pl.BlockSpec((1, 1, ct), lambda i: (i, 0, 0), memory_space=pltpu.SMEM)
    return pl.pallas_call(
        functools.partial(_moe_dispatch_kernel, ct=ct),
        out_shape=jax.ShapeDtypeStruct((2 * m, d), F32),
        grid=(m // ct,),
        in_specs=[smem_row, smem_row, pl.BlockSpec((ct, d), lambda i: (i, 0))],
        out_specs=pl.BlockSpec(memory_space=pl.ANY),
        scratch_shapes=[pltpu.SemaphoreType.DMA],
        compiler_params=_cparams(("arbitrary",)),
        name="moe_dispatch",
    )(dest1, dest2, x)


def _moe_grouped_kernel(tile_ref, exp_ref, first_ref, valid_ref, gs_ref, xs_ref, g_ref, wg_ref, wu_ref, wd_ref,
                        ys_ref, xn_ref, acc_ref):
    w = pl.program_id(0)
    f = pl.program_id(1)
    nf = pl.num_programs(1)

    @pl.when(valid_ref[w] == 1)
    def _():
        @pl.when(f == 0)
        def _():
            x = xs_ref[...]
            y = x * lax.rsqrt(jnp.mean(x * x, axis=-1, keepdims=True) + EPS) * g_ref[...]
            xn_ref[...] = y.astype(BF16)
            acc_ref[...] = jnp.zeros_like(acc_ref)

        xn = xn_ref[...]
        gate = jnp.dot(xn, wg_ref[0], preferred_element_type=F32)
        up = jnp.dot(xn, wu_ref[0], preferred_element_type=F32)
        hid = (_silu(gate) * up).astype(BF16)
        acc_ref[...] += jnp.dot(hid, wd_ref[0], preferred_element_type=F32)

        @pl.when(f == nf - 1)
        def _():
            e = exp_ref[w]
            row = tile_ref[w] * MOE_TILE + lax.broadcasted_iota(I32, acc_ref.shape, 0)
            mine = (row >= gs_ref[e]) & (row < gs_ref[e + 1])
            part = jnp.where(mine, acc_ref[...], 0.0)

            @pl.when(first_ref[w] == 1)
            def _():
                ys_ref[...] = part

            @pl.when(first_ref[w] == 0)
            def _():
                ys_ref[...] += part


def moe_grouped(item_tile, item_expert, item_first, item_valid, group_starts, xs, g, wg, wu, wd, *, tf):
    r, d = xs.shape
    ff = wg.shape[2]
    n_items = item_tile.shape[0]
    return pl.pallas_call(
        _moe_grouped_kernel,
        out_shape=jax.ShapeDtypeStruct((r, d), F32),
        grid_spec=pltpu.PrefetchScalarGridSpec(
            num_scalar_prefetch=5,
            grid=(n_items, ff // tf),
            in_specs=[pl.BlockSpec((MOE_TILE, d), lambda w, f, it, ie, i1, iv, gs: (it[w], 0)),
                      pl.BlockSpec((1, d), lambda w, f, it, ie, i1, iv, gs: (0, 0)),
                      pl.BlockSpec((1, d, tf), lambda w, f, it, ie, i1, iv, gs: (ie[w], 0, f)),
                      pl.BlockSpec((1, d, tf), lambda w, f, it, ie, i1, iv, gs: (ie[w], 0, f)),
                      pl.BlockSpec((1, tf, d), lambda w, f, it, ie, i1, iv, gs: (ie[w], f, 0))],
            out_specs=pl.BlockSpec((MOE_TILE, d), lambda w, f, it, ie, i1, iv, gs: (it[w], 0)),
            scratch_shapes=[pltpu.VMEM((MOE_TILE, d), BF16), pltpu.VMEM((MOE_TILE, d), F32)]),
        compiler_params=_cparams(("arbitrary", "arbitrary")),
        name="moe_grouped",
    )(item_tile, item_expert, item_first, item_valid, group_starts, xs, g, wg, wu, wd)


def _moe_combine_kernel(d1_ref, d2_ref, x_ref, info_ref, ys_ref, o_ref, y1_ref, y2_ref, sem, *, ct):
    def issue(t, carry):
        _row_copy(ys_ref, d1_ref[0, 0, t], y1_ref, t, sem).start()
        _row_copy(ys_ref, d2_ref[0, 0, t], y2_ref, t, sem).start()
        return carry

    lax.fori_loop(0, ct, issue, 0)

    def drain(t, carry):
        _row_copy(ys_ref, 0, y1_ref, 0, sem).wait()
        _row_copy(ys_ref, 0, y2_ref, 0, sem).wait()
        return carry

    lax.fori_loop(0, ct, drain, 0)
    info = info_ref[...]
    g1 = info[:, R_G1:R_G1 + 1]
    g2 = info[:, R_G2:R_G2 + 1]
    o_ref[...] = x_ref[...] + (g1 * y1_ref[...] + g2 * y2_ref[...])


def moe_combine(x, info, ys, dest1, dest2, *, ct):
    m, d = x.shape
    smem_row = pl.BlockSpec((1, 1, ct), lambda i: (i, 0, 0), memory_space=pltpu.SMEM)
    return pl.pallas_call(
        functools.partial(_moe_combine_kernel, ct=ct),
        out_shape=jax.ShapeDtypeStruct((m, d), F32),
        grid=(m // ct,),
        in_specs=[smem_row, smem_row, pl.BlockSpec((ct, d), lambda i: (i, 0)),
                  pl.BlockSpec((ct, LANES), lambda i: (i, 0)), pl.BlockSpec(memory_space=pl.ANY)],
        out_specs=pl.BlockSpec((ct, d), lambda i: (i, 0)),
        scratch_shapes=[pltpu.VMEM((ct, d), F32), pltpu.VMEM((ct, d), F32), pltpu.SemaphoreType.DMA],
        compiler_params=_cparams(("arbitrary",)),
        name="moe_combine",
    )(dest1, dest2, x, info, ys)


def _moe_work_items(counts, n_tiles):
    n_exp = counts.shape[0]
    n_items = n_tiles + n_exp - 1
    ends = jnp.cumsum(counts)
    starts = ends - counts
    expert_of_row = lambda rr: jnp.minimum(jnp.sum(ends[None, :] <= rr[:, None], axis=1), n_exp - 1).astype(I32)
    t = jnp.arange(n_tiles, dtype=I32)
    e_lo = expert_of_row(t * MOE_TILE)
    e_hi = expert_of_row(t * MOE_TILE + (MOE_TILE - 1))
    per_tile = e_hi - e_lo + 1
    item_end = jnp.cumsum(per_tile)
    item_start = item_end - per_tile
    total = item_end[-1]
    w = jnp.minimum(jnp.arange(n_items, dtype=I32), total - 1)
    tile = jnp.sum(item_end[None, :] <= w[:, None], axis=1).astype(I32)
    expert = (e_lo[tile] + (w - item_start[tile])).astype(I32)
    first = (w == item_start[tile]).astype(I32)
    valid = (jnp.arange(n_items, dtype=I32) < total).astype(I32)
    group_starts = jnp.concatenate([starts, ends[-1:]]).astype(I32)
    return tile, expert, first, valid, group_starts


def moe_routed(x, g, w_hi, w_lo, b_pad, wg, wu, wd, *, tm):
    m, d = x.shape
    n_exp = wg.shape[0]
    assert (2 * m) % MOE_TILE == 0
    info, cnt = moe_router_rank(x, g, w_hi, w_lo, b_pad, tm=tm)
    counts = cnt[0, :n_exp].astype(I32)
    starts = jnp.cumsum(counts) - counts
    i1 = info[:, R_I1].astype(I32)
    i2 = info[:, R_I2].astype(I32)
    dest1 = (starts[i1] + info[:, R_RANK1].astype(I32)).reshape(m // tm, 1, tm)
    dest2 = (starts[i2] + info[:, R_RANK2].astype(I32)).reshape(m // tm, 1, tm)
    xs = moe_dispatch(x, dest1, dest2, ct=tm)
    items = _moe_work_items(counts, (2 * m) // MOE_TILE)
    ys = moe_grouped(*items, xs, g, wg, wu, wd, tf=256)
    return moe_combine(x, info, ys, dest1, dest2, ct=tm)


PAGES_PER_STEP = 32


def _sdsa_index_kernel(pt_ref, qi_ref, wi_ref, kknew_ref, *rest, n_pages, t_len, topk):
    del pt_ref
    page_refs = rest[:PAGES_PER_STEP]
    mpast_ref, mnew_ref, keys_ref, pre_ref = rest[PAGES_PER_STEP:]
    j = pl.program_id(1)
    nj = pl.num_programs(1)
    lane = lax.broadcasted_iota(I32, (t_len, LANES), 1)
    row = lax.broadcasted_iota(I32, (t_len, LANES), 0)
    lo = lane < HEAD_DIM

    qi = qi_ref[0]
    wi = wi_ref[0]
    qs, ws = [], []
    for h in range(IDX_HEADS):
        x = qi[:, (h // 2) * LANES:(h // 2 + 1) * LANES]
        if h % 2 == 1:
            x = pltpu.roll(x, HEAD_DIM, 1)
        qs.append(x[:, :IDX_DIM])
        ws.append(wi[:, IDX_DIM + h:IDX_DIM + h + 1])
    q32 = jnp.concatenate(qs, axis=0).astype(BF16)
    w32 = jnp.concatenate(ws, axis=0)

    def scores(dots):
        d = jnp.maximum(dots, 0.0) * w32
        s = d[0:t_len]
        for h in range(1, IDX_HEADS):
            s = s + d[h * t_len:(h + 1) * t_len]
        return jnp.where(s == 0.0, 0.0, s)

    for i in range(PAGES_PER_STEP):
        page_t = page_refs[i][0].astype(BF16)
        keys_ref[j * PAGES_PER_STEP + i] = scores(jnp.dot(q32, page_t, preferred_element_type=F32))

    @pl.when(j == nj - 1)
    def _():
        knew = jnp.concatenate([kknew_ref[0][:, :IDX_DIM], jnp.zeros((LANES - t_len, IDX_DIM), F32)], axis=0)
        s_new = jnp.where((lane <= row) & (lane < t_len), scores(_bdot_nt(q32, knew)), NEG)
        keys_ref[n_pages] = s_new

        def count_ge(cand):
            part = _sum_leading(jnp.where(keys_ref[...] >= cand, 1.0, 0.0))
            return jnp.sum(part, axis=-1, keepdims=True)

        thr = _kth_largest(count_ge, float(topk), (t_len, 1))
        n_ge = count_ge(thr)
        no_cut = jnp.max(jnp.abs(n_ge - float(topk))) == 0.0

        @pl.when(no_cut)
        def _():
            bias = jnp.where(keys_ref[...] >= thr, 0.0, NEG)
            mpast_ref[0] = bias[:n_pages]
            mnew_ref[0] = bias[n_pages]

        @pl.when(jnp.logical_not(no_cut))
        def _():
            keys = keys_ref[...]
            n_gt = jnp.sum(_sum_leading(jnp.where(keys > thr, 1.0, 0.0)), axis=-1, keepdims=True)
            need = float(topk) - n_gt
            eqf = jnp.where(keys == thr, 1.0, 0.0).astype(BF16).reshape((n_pages + 1) * t_len, LANES)
            pre_ref[...] = jnp.dot(eqf, _upper_tri_bf16(LANES), preferred_element_type=F32).reshape(
                n_pages + 1, t_len, LANES)

            def body(p, off):
                kp = keys_ref[p]
                pre = pre_ref[p]
                bias = jnp.where(kp == thr, jnp.where((pre + off) <= need, 0.0, NEG),
                                 jnp.where(kp > thr, 0.0, NEG))

                @pl.when(p < n_pages)
                def _():
                    mpast_ref[0, p] = bias

                @pl.when(p == n_pages)
                def _():
                    mnew_ref[0] = bias

                return off + pre[:, LANES - 1:LANES]

            lax.fori_loop(0, n_pages + 1, body, jnp.zeros((t_len, 1), F32))


def sdsa_index(page_table, qi, wi, kk_new, cache_kidx):
    bsz, t_len, _ = qi.shape
    n_pages = page_table.shape[1]
    topk = min(TOPK_MAX, (n_pages * PAGE_SIZE + t_len) // 4)
    kern = functools.partial(_sdsa_index_kernel, n_pages=n_pages, t_len=t_len, topk=topk)
    tok = lambda w: pl.BlockSpec((1, t_len, w), lambda b_, j, pt: (b_, 0, 0))
    page_specs = [pl.BlockSpec((1, IDX_DIM, PAGE_SIZE),
                               functools.partial(lambda b_, j, pt, i: (pt[b_, j * PAGES_PER_STEP + i], 0, 0), i=i))
                  for i in range(PAGES_PER_STEP)]
    return pl.pallas_call(
        kern,
        out_shape=(jax.ShapeDtypeStruct((bsz, n_pages, t_len, LANES), F32),
                   jax.ShapeDtypeStruct((bsz, t_len, LANES), F32)),
        grid_spec=pltpu.PrefetchScalarGridSpec(
            num_scalar_prefetch=1,
            grid=(bsz, n_pages // PAGES_PER_STEP),
            in_specs=[tok(256), tok(LANES), tok(LANES)] + page_specs,
            out_specs=(pl.BlockSpec((1, n_pages, t_len, LANES), lambda b_, j, pt: (b_, 0, 0, 0)),
                       pl.BlockSpec((1, t_len, LANES), lambda b_, j, pt: (b_, 0, 0))),
            scratch_shapes=[pltpu.VMEM((n_pages + 1, t_len, LANES), F32),
                            pltpu.VMEM((n_pages + 1, t_len, LANES), F32)]),
        compiler_params=_cparams(("arbitrary", "arbitrary")),
        name="sdsa_index",
    )(page_table, qi, wi, kk_new, *([cache_kidx] * PAGES_PER_STEP))


def _sdsa_attn_kernel(pt_ref, q_ref, knew_ref, vnew_ref, mpast_ref, mnew_ref, *rest, t_len):
    del pt_ref
    kp_refs = rest[:PAGES_PER_STEP]
    vp_refs = rest[PAGES_PER_STEP:2 * PAGES_PER_STEP]
    o_ref, qg_ref, m_ref, l_ref, acc_ref = rest[2 * PAGES_PER_STEP:]
    j = pl.program_id(1)
    nj = pl.num_programs(1)
    rpg = N_HEADS // N_KV
    reps = rpg

    def group_lanes(x, g):
        if g == 1:
            x = pltpu.roll(x, HEAD_DIM, 1)
        return x[:, :HEAD_DIM]

    @pl.when(j == 0)
    def _():
        q = q_ref[0]
        for g in range(N_KV):
            parts = [group_lanes(q[:, r * LANES:(r + 1) * LANES], g) for r in range(rpg)]
            qg_ref[g] = (jnp.concatenate(parts, axis=0) * (HEAD_DIM ** -0.5)).astype(BF16)
        m_ref[...] = jnp.full(m_ref.shape, NEG, F32)
        l_ref[...] = jnp.zeros(l_ref.shape, F32)
        acc_ref[...] = jnp.zeros(acc_ref.shape, F32)

    def update(g, s_list, pv):
        m_old = m_ref[g]
        m_cur = m_old
        for s in s_list:
            m_cur = jnp.maximum(m_cur, jnp.max(s, axis=-1, keepdims=True))
        alpha = jnp.exp(m_old - m_cur)
        l_new = l_ref[g] * alpha
        acc = acc_ref[g] * alpha
        for i, s in enumerate(s_list):
            p = jnp.exp(s - m_cur)
            l_new = l_new + jnp.sum(p, axis=-1, keepdims=True)
            acc = acc + pv(i, p.astype(BF16))
        m_ref[g] = m_cur
        l_ref[g] = l_new
        acc_ref[g] = acc

    biases = [jnp.concatenate([mpast_ref[0, i]] * reps, axis=0) for i in range(PAGES_PER_STEP)]
    for g in range(N_KV):
        qg = qg_ref[g]
        s_list = [jnp.dot(qg, kp_refs[i][0, g].astype(BF16), preferred_element_type=F32) + biases[i]
                  for i in range(PAGES_PER_STEP)]
        update(g, s_list, lambda i, p, g=g: _bdot_nt(p, vp_refs[i][0, g]))

    @pl.when(j == nj - 1)
    def _():
        pad = jnp.zeros((LANES - t_len, HEAD_DIM), F32)
        bias = jnp.concatenate([mnew_ref[0]] * reps, axis=0)
        for g in range(N_KV):
            knew = jnp.concatenate([group_lanes(knew_ref[0], g), pad], axis=0)
            vnew = jnp.concatenate([group_lanes(vnew_ref[0], g), pad], axis=0)
            update(g, [_bdot_nt(qg_ref[g], knew) + bias], lambda i, p, vnew=vnew: _bdot(p, vnew))
        o0 = acc_ref[0] / l_ref[0]
        o1 = acc_ref[1] / l_ref[1]
        for r in range(rpg):
            o_ref[0, :, r * LANES:(r + 1) * LANES] = jnp.concatenate(
                [o0[r * t_len:(r + 1) * t_len], o1[r * t_len:(r + 1) * t_len]], axis=-1)


def sdsa_attn(page_table, q, k_new, v_new, mask_past, mask_new, cache_k, cache_v):
    bsz, t_len, _ = q.shape
    n_pages = page_table.shape[1]
    kern = functools.partial(_sdsa_attn_kernel, t_len=t_len)
    tok = lambda w: pl.BlockSpec((1, t_len, w), lambda b_, j, pt: (b_, 0, 0))
    page_specs = [pl.BlockSpec((1, N_KV, HEAD_DIM, PAGE_SIZE),
                               functools.partial(lambda b_, j, pt, i: (pt[b_, j * PAGES_PER_STEP + i], 0, 0, 0), i=i))
                  for i in range(PAGES_PER_STEP)]
    rows = (N_HEADS // N_KV) * t_len
    return pl.pallas_call(
        kern,
        out_shape=jax.ShapeDtypeStruct((bsz, t_len, 512), F32),
        grid_spec=pltpu.PrefetchScalarGridSpec(
            num_scalar_prefetch=1,
            grid=(bsz, n_pages // PAGES_PER_STEP),
            in_specs=[tok(512), tok(LANES), tok(LANES),
                      pl.BlockSpec((1, PAGES_PER_STEP, t_len, LANES), lambda b_, j, pt: (b_, j, 0, 0)),
                      tok(LANES)] + page_specs + page_specs,
            out_specs=pl.BlockSpec((1, t_len, 512), lambda b_, j, pt: (b_, 0, 0)),
            scratch_shapes=[pltpu.VMEM((N_KV, rows, HEAD_DIM), BF16), pltpu.VMEM((N_KV, rows, 1), F32),
                            pltpu.VMEM((N_KV, rows, 1), F32), pltpu.VMEM((N_KV, rows, HEAD_DIM), F32)]),
        compiler_params=_cparams(("arbitrary", "arbitrary")),
        name="sdsa_attn",
    )(page_table, q, k_new, v_new, mask_past, mask_new,
      *([cache_k] * PAGES_PER_STEP), *([cache_v] * PAGES_PER_STEP))


HEAD_ORDER = (0, 4, 1, 5, 2, 6, 3, 7)


def _pad_cols(w, n):
    return jnp.pad(w, ((0, 0), (0, n - w.shape[1])))


def _row(v):
    return v.reshape(1, -1).astype(F32)


def _rope_tables(pos):
    half = ROT_DIM // 2
    inv = ROPE_THETA ** (-jnp.arange(half, dtype=F32) / half)
    ang = pos.astype(F32)[:, None] * inv[None, :]
    cos, sin = jnp.cos(ang), jnp.sin(ang)
    n = pos.shape[0]
    rest = HEAD_DIM - ROT_DIM
    c64 = jnp.concatenate([cos, cos, jnp.ones((n, rest), F32)], axis=1)
    s64 = jnp.concatenate([-sin, sin, jnp.zeros((n, rest), F32)], axis=1)
    return jnp.tile(c64, (1, 2)), jnp.tile(s64, (1, 2))


def _even_layer(x, prev_a, prev_ssm, prev_b, lc, p, *, tm, tl_a):
    bsz, L, d = x.shape
    m = bsz * L
    x2 = x.reshape(m, d)
    zmain, ztail = norm_proj(x2, p["e_norm_mix"], p["e_w_main"], p["e_w_tail"], tm=tm, tn=512)
    zmain3 = zmain.reshape(bsz, L, E_MAIN)
    prev32 = jnp.pad(prev_a, ((0, 0), (CARRY_A - (CONV_A - 1), 0), (0, 0)))
    ya, new_a = conv_a(zmain3, prev32, p["e_conv_a_w"], p["e_conv_a_b"], p["e_ln_a_g"], p["e_ln_a_b"], tl=tl_a)
    prev8 = jnp.pad(prev_b, ((0, 0), (CARRY_B - (SSM_CONV - 1), 0), (0, 0)))
    yb, new_b, hout = ssd_mixer(zmain3, ztail.reshape(bsz, L, LANES), prev8,
                                prev_ssm.reshape(bsz, D_INNER, SSM_STATE),
                                p["e_conv_b_w"], p["e_conv_b_b"], p["e_dt_bias"], p["e_a_neg"], p["e_expand"],
                                p["e_dskip"], p["e_ssm_norm"], lc=lc)
    x2 = out_proj(x2, ya.reshape(m, D_A), yb.reshape(m, D_INNER), p["e_wo_a"], p["e_wo_b"], tm=tm)
    x2 = swiglu_ffn(x2, p["e_norm_ffn"], p["e_w_gate"], p["e_w_up"], p["e_w_down"], tm=min(m, 1024), tf=256)
    return (x2.reshape(bsz, L, d), new_a[:, CARRY_A - (CONV_A - 1):], hout.reshape(bsz, SSM_HEADS, SSM_HEADDIM, SSM_STATE),
            new_b[:, CARRY_B - (SSM_CONV - 1):])


def _odd_project(x2, cos_t, sin_t, p, *, tm):
    zmain, ztail = norm_proj(x2, p["o_norm_mix"], p["o_w_main"], p["o_w_tail"], tm=tm, tn=512)
    q, k, v, qi, kiw = odd_post(zmain, ztail, cos_t, sin_t, p["o_q_norm"], p["o_k_norm"], p["o_kidx_g"],
                                p["o_kidx_b"], p["o_bd"], tm=tm)
    return zmain, q, k, v, qi, kiw


def _odd_tail(x2, att, zmain3, prev_d, p, *, tm, tl_d):
    bsz, L, _ = zmain3.shape
    m = bsz * L
    prev8 = jnp.pad(prev_d, ((0, 0), (CARRY_D - (CONV_D - 1), 0), (0, 0)))
    dm, new_d = conv_d(zmain3, prev8, p["o_conv_d_w"], tl=tl_d)
    x2 = out_proj(x2, att.reshape(m, 512), dm.reshape(m, D_D), p["o_wo_a"], p["o_wo_b"], tm=tm)
    if (2 * m) % MOE_TILE == 0 and 2 * m >= N_EXPERTS * MOE_TILE:
        x2 = moe_routed(x2, p["o_norm_ffn"], p["o_wr_hi"], p["o_wr_lo"], p["o_br"],
                        p["o_we_gate"], p["o_we_up"], p["o_we_down"], tm=tm)
    else:
        xn, comb = moe_router(x2, p["o_norm_ffn"], p["o_wr_hi"], p["o_wr_lo"], p["o_br"], tm=tm)
        x2 = moe_dense(x2, xn, comb, p["o_we_gate"], p["o_we_up"], p["o_we_down"], tm=min(m, 1024), tf=256)
    return x2, new_d[:, CARRY_D - (CONV_D - 1):]


def kernel(x_prompt, x_sample, state_conv_a, state_ssm, state_conv_b, cache_k, cache_v, cache_kidx, state_conv_d,
           page_table, e_norm_mix, e_w_in, e_conv_a_w, e_conv_a_b, e_ln_a_g, e_ln_a_b, e_conv_b_w, e_conv_b_b,
           e_dt_bias, e_a_log, e_d_skip, e_ssm_norm, e_w_out, e_norm_ffn, e_w_gate, e_w_up, e_w_down,
           o_norm_mix, o_w_in, o_q_norm, o_k_norm, o_kidx_g, o_kidx_b, o_conv_d_w, o_w_out, o_norm_ffn,
           o_w_router, o_b_router, o_we_gate, o_we_up, o_we_down):
    bp, s_len, d = x_prompt.shape
    bd, t_len, _ = x_sample.shape
    n_pairs = e_w_in.shape[0]
    past = page_table.shape[1] * PAGE_SIZE
    n_pool = cache_k.shape[1]
    xp, xs = x_prompt, x_sample
    outs_p = [[] for _ in range(7)]
    outs_s = [[] for _ in range(7)]
    cos_p, sin_p = _rope_tables(jnp.arange(s_len))
    cos_s, sin_s = _rope_tables(jnp.tile(past + jnp.arange(t_len), bd))
    perm = np.concatenate([np.arange(h * HEAD_DIM, (h + 1) * HEAD_DIM) for h in HEAD_ORDER])
    expand = (jnp.arange(LANES)[:, None] == (jnp.arange(D_INNER)[None, :] // SSM_HEADDIM)).astype(BF16)
    blk = jnp.arange(LANES) // HEAD_DIM
    bdiag = jnp.where(blk[:, None] == blk[None, :], 1.0 / HEAD_DIM, 0.0).astype(BF16)
    for i in range(n_pairs):
        w = e_w_in[i]
        p = {
            "e_norm_mix": _row(e_norm_mix[i]),
            "e_w_main": jnp.concatenate([w[:, 2048:3584], w[:, 0:512], w[:, 1024:2048], w[:, 512:1024]],
                                        axis=1).astype(BF16),
            "e_w_tail": _pad_cols(w[:, 3584:3600], LANES).astype(BF16),
            "e_conv_a_w": e_conv_a_w[i], "e_conv_a_b": _row(e_conv_a_b[i]),
            "e_ln_a_g": _row(e_ln_a_g[i]), "e_ln_a_b": _row(e_ln_a_b[i]),
            "e_conv_b_w": e_conv_b_w[i], "e_conv_b_b": _row(e_conv_b_b[i]),
            "e_dt_bias": _pad_cols(_row(e_dt_bias[i]), LANES),
            "e_a_neg": _pad_cols(_row(-jnp.exp(e_a_log[i].astype(F32))), LANES),
            "e_expand": expand,
            "e_dskip": _row(jnp.repeat(e_d_skip[i], SSM_HEADDIM)),
            "e_ssm_norm": _row(e_ssm_norm[i]),
            "e_wo_a": e_w_out[i][:D_A].astype(BF16), "e_wo_b": e_w_out[i][D_A:].astype(BF16),
            "e_norm_ffn": _row(e_norm_ffn[i]),
            "e_w_gate": e_w_gate[i].astype(BF16), "e_w_up": e_w_up[i].astype(BF16),
            "e_w_down": e_w_down[i].astype(BF16),
        }
        zero_a = jnp.zeros((bp, CONV_A - 1, D_A), F32)
        zero_h = jnp.zeros((bp, SSM_HEADS, SSM_HEADDIM, SSM_STATE), F32)
        zero_b = jnp.zeros((bp, SSM_CONV - 1, CONV_DIM), F32)
        xp, ca, sm, cb = _even_layer(xp, zero_a, zero_h, zero_b, min(SSD_CHUNK, s_len), p, tm=512, tl_a=256)
        outs_p[0].append(ca); outs_p[1].append(sm); outs_p[2].append(cb)
        xs, ca, sm, cb = _even_layer(xs, state_conv_a[i], state_ssm[i], state_conv_b[i], t_len, p,
                                     tm=bd * t_len, tl_a=t_len)
        outs_s[0].append(ca); outs_s[1].append(sm); outs_s[2].append(cb)

        w = o_w_in[i]
        wo = o_w_out[i]
        p = {
            "o_norm_mix": _row(o_norm_mix[i]),
            "o_w_main": jnp.concatenate([w[:, 0:512][:, perm], w[:, 512:1024], w[:, 1092:2628]], axis=1).astype(BF16),
            "o_w_tail": _pad_cols(w[:, 1024:1092], LANES).astype(BF16),
            "o_q_norm": _row(jnp.tile(o_q_norm[i], 2)), "o_k_norm": _row(jnp.tile(o_k_norm[i], 2)),
            "o_kidx_g": _pad_cols(_row(o_kidx_g[i]), LANES), "o_kidx_b": _pad_cols(_row(o_kidx_b[i]), LANES),
            "o_bd": bdiag,
            "o_conv_d_w": o_conv_d_w[i],
            "o_wo_a": wo[:512][perm].astype(BF16), "o_wo_b": wo[512:].astype(BF16),
            "o_norm_ffn": _row(o_norm_ffn[i]),
            "o_br": jnp.concatenate([_row(o_b_router[i]), jnp.full((1, LANES - N_EXPERTS), NEG, F32)], axis=1),
            "o_we_gate": o_we_gate[i].astype(BF16), "o_we_up": o_we_up[i].astype(BF16),
            "o_we_down": o_we_down[i].astype(BF16),
        }
        wr = _pad_cols(o_w_router[i], LANES)
        p["o_wr_hi"] = wr.astype(BF16)
        p["o_wr_lo"] = (wr - p["o_wr_hi"].astype(F32)).astype(BF16)

        mp = bp * s_len
        x2 = xp.reshape(mp, d)
        zmain, q, k, v, qi, kiw = _odd_project(x2, cos_p, sin_p, p, tm=512)
        r3 = lambda a: a.reshape(bp, s_len, a.shape[-1])
        kk = jnp.concatenate([kiw[:, :IDX_DIM], kiw[:, :IDX_DIM]], axis=1)
        att = dsa_prompt(r3(q), r3(qi), r3(kiw), r3(k), r3(v), r3(kk))
        x2, cd = _odd_tail(x2, att, r3(zmain), jnp.zeros((bp, CONV_D - 1, D_D), F32), p, tm=512, tl_d=512)
        xp = x2.reshape(bp, s_len, d)
        outs_p[3].append(k.reshape(bp, s_len, N_KV, HEAD_DIM)); outs_p[4].append(v.reshape(bp, s_len, N_KV, HEAD_DIM))
        outs_p[5].append(kiw[:, :IDX_DIM].reshape(bp, s_len, IDX_DIM)); outs_p[6].append(cd)

        ms = bd * t_len
        x2 = xs.reshape(ms, d)
        zmain, q, k, v, qi, kiw = _odd_project(x2, cos_s, sin_s, p, tm=ms)
        r3 = lambda a: a.reshape(bd, t_len, a.shape[-1])
        mask_past, mask_new = sdsa_index(page_table, r3(qi), r3(kiw), r3(kiw),
                                         jnp.transpose(cache_kidx[i], (0, 2, 1)))
        att = sdsa_attn(page_table, r3(q), r3(k), r3(v), mask_past, mask_new,
                        jnp.transpose(cache_k[i], (0, 2, 3, 1)), jnp.transpose(cache_v[i], (0, 2, 3, 1)))
        x2, cd = _odd_tail(x2, att, r3(zmain), state_conv_d[i], p, tm=ms, tl_d=t_len)
        xs = x2.reshape(bd, t_len, d)
        outs_s[3].append(k.reshape(bd, t_len, N_KV, HEAD_DIM)); outs_s[4].append(v.reshape(bd, t_len, N_KV, HEAD_DIM))
        outs_s[5].append(kiw[:, :IDX_DIM].reshape(bd, t_len, IDX_DIM)); outs_s[6].append(cd)
    return (xp, xs) + tuple(jnp.stack(o) for o in outs_p) + tuple(jnp.stack(o) for o in outs_s)
```

```python
import functools
import math

import jax
import jax.numpy as jnp
import numpy as np
from jax import lax
from jax.experimental import pallas as pl
from jax.experimental.pallas import tpu as pltpu

F32 = jnp.float32
BF16 = jnp.bfloat16
I32 = jnp.int32

D_MODEL = 1024
D_A = 512
CONV_A = 31
D_INNER = 1024
SSM_HEADDIM = 64
SSM_HEADS = 16
SSM_GROUPS = 2
SSM_STATE = 128
SSM_CONV = 4
CONV_DIM = D_INNER + 2 * SSM_GROUPS * SSM_STATE
SSD_CHUNK = 128
HEAD_DIM = 64
N_HEADS = 8
N_KV = 2
ROT_DIM = 16
ROPE_THETA = 500000.0
IDX_HEADS = 4
IDX_DIM = 64
TOPK_MAX = 256
Q_BLOCK = 128
D_D = 512
CONV_D = 3
D_FF = 2816
N_EXPERTS = 8
PAGE_SIZE = 128
EPS = 1e-6
NEG = -1e30

LANES = 128
SUBLANES = 8
VMEM_LIMIT = 56 * 1024 * 1024

E_MAIN = D_A + D_A + D_INNER + CONV_DIM
E_COL_XBC, E_COL_VAL, E_COL_Z, E_COL_GATE = 0, 1536, 2048, 3072
O_MAIN = 512 + 128 + 128 + 256 + 3 * D_D


def _cparams(sem):
    return pltpu.CompilerParams(dimension_semantics=sem, vmem_limit_bytes=VMEM_LIMIT)


def _bdot(a, b):
    return jnp.dot(a.astype(BF16), b.astype(BF16), preferred_element_type=F32)


def _bdot_nt(a, b):
    return lax.dot_general(a.astype(BF16), b.astype(BF16), (((1,), (1,)), ((), ())),
                           preferred_element_type=F32)


def _split3(a):
    hi = a.astype(BF16)
    r1 = a - hi.astype(F32)
    mid = r1.astype(BF16)
    lo = (r1 - mid.astype(F32)).astype(BF16)
    return hi, mid, lo


def _dot_exact_rhs(a, b_bf16):
    hi, mid, lo = _split3(a)
    out = jnp.dot(lo, b_bf16, preferred_element_type=F32)
    out = out + jnp.dot(mid, b_bf16, preferred_element_type=F32)
    return out + jnp.dot(hi, b_bf16, preferred_element_type=F32)


def _dot_exact_lhs(a_bf16, b):
    hi, mid, lo = _split3(b)
    out = jnp.dot(a_bf16, lo, preferred_element_type=F32)
    out = out + jnp.dot(a_bf16, mid, preferred_element_type=F32)
    return out + jnp.dot(a_bf16, hi, preferred_element_type=F32)


def _sigmoid(x):
    return 1.0 / (1.0 + jnp.exp(-x))


def _silu(x):
    return x * _sigmoid(x)


def _softplus(x):
    return jnp.maximum(x, 0.0) + jnp.log(1.0 + jnp.exp(-jnp.abs(x)))


def _norm_proj_kernel(x_ref, g_ref, w_ref, wt_ref, main_ref, tail_ref, xn_ref):
    j = pl.program_id(1)

    @pl.when(j == 0)
    def _():
        x = x_ref[...]
        y = x * lax.rsqrt(jnp.mean(x * x, axis=-1, keepdims=True) + EPS) * g_ref[...]
        xn = y.astype(BF16)
        xn_ref[...] = xn
        tail_ref[...] = jnp.dot(xn, wt_ref[...], preferred_element_type=F32)

    main_ref[...] = jnp.dot(xn_ref[...], w_ref[...], preferred_element_type=F32)


def norm_proj(x, g, w_main, w_tail, *, tm, tn):
    m, k = x.shape
    n = w_main.shape[1]
    return pl.pallas_call(
        _norm_proj_kernel,
        out_shape=(jax.ShapeDtypeStruct((m, n), F32), jax.ShapeDtypeStruct((m, LANES), F32)),
        grid=(m // tm, n // tn),
        in_specs=[pl.BlockSpec((tm, k), lambda i, j: (i, 0)),
                  pl.BlockSpec((1, k), lambda i, j: (0, 0)),
                  pl.BlockSpec((k, tn), lambda i, j: (0, j)),
                  pl.BlockSpec((k, LANES), lambda i, j: (0, 0))],
        out_specs=(pl.BlockSpec((tm, tn), lambda i, j: (i, j)),
                   pl.BlockSpec((tm, LANES), lambda i, j: (i, 0))),
        scratch_shapes=[pltpu.VMEM((tm, k), BF16)],
        compiler_params=_cparams(("arbitrary", "arbitrary")),
        name="norm_proj",
    )(x, g, w_main, w_tail)


def _dwconv_from_buf(buf_ref, w_ref, kw, carry, tl):
    acc = None
    for k in range(kw):
        start = carry - (kw - 1) + k
        term = buf_ref[pl.ds(start, tl), :] * w_ref[pl.ds(k, 1), :]
        acc = term if acc is None else acc + term
    return acc


CARRY_A = 32


def _conv_a_kernel(val_ref, gate_ref, prev_ref, w_ref, b_ref, lg_ref, lb_ref, ya_ref, new_ref, buf_ref, *, tl):
    l = pl.program_id(1)

    @pl.when(l == 0)
    def _():
        buf_ref[pl.ds(0, CARRY_A), :] = prev_ref[0]

    u = val_ref[0] * _sigmoid(gate_ref[0])
    buf_ref[pl.ds(CARRY_A, tl), :] = u
    y = _dwconv_from_buf(buf_ref, w_ref, CONV_A, CARRY_A, tl) + b_ref[...]
    mu = jnp.mean(y, axis=-1, keepdims=True)
    yc = y - mu
    var = jnp.mean(yc * yc, axis=-1, keepdims=True)
    yn = yc * lax.rsqrt(var + EPS) * lg_ref[...] + lb_ref[...]
    ya_ref[0] = _silu(yn)
    hist = buf_ref[pl.ds(tl, CARRY_A), :]
    new_ref[0] = hist
    buf_ref[pl.ds(0, CARRY_A), :] = hist


def conv_a(zmain, prev32, w, b, lg, lb, *, tl):
    bsz, L, _ = zmain.shape
    kern = functools.partial(_conv_a_kernel, tl=tl)
    return pl.pallas_call(
        kern,
        out_shape=(jax.ShapeDtypeStruct((bsz, L, D_A), F32), jax.ShapeDtypeStruct((bsz, CARRY_A, D_A), F32)),
        grid=(bsz, L // tl),
        in_specs=[pl.BlockSpec((1, tl, D_A), lambda b_, l: (b_, l, E_COL_VAL // D_A)),
                  pl.BlockSpec((1, tl, D_A), lambda b_, l: (b_, l, E_COL_GATE // D_A)),
                  pl.BlockSpec((1, CARRY_A, D_A), lambda b_, l: (b_, 0, 0)),
                  pl.BlockSpec((CONV_A, D_A), lambda b_, l: (0, 0)),
                  pl.BlockSpec((1, D_A), lambda b_, l: (0, 0)),
                  pl.BlockSpec((1, D_A), lambda b_, l: (0, 0)),
                  pl.BlockSpec((1, D_A), lambda b_, l: (0, 0))],
        out_specs=(pl.BlockSpec((1, tl, D_A), lambda b_, l: (b_, l, 0)),
                   pl.BlockSpec((1, CARRY_A, D_A), lambda b_, l: (b_, 0, 0))),
        scratch_shapes=[pltpu.VMEM((CARRY_A + tl, D_A), F32)],
        compiler_params=_cparams(("arbitrary", "arbitrary")),
        name="conv_a",
    )(zmain, zmain, prev32, w, b, lg, lb)


CARRY_B = 8
SSD_LP = 128


def _ssd_kernel(xbc_ref, z_ref, dt_ref, prevb_ref, h0_ref, cw_ref, cb_ref, dtb_ref, a_ref, expand_ref,
                dskip_ref, nrm_ref, y_ref, newb_ref, hout_ref, buf_ref, ht_ref, *, lc):
    c = pl.program_id(1)
    nc = pl.num_programs(1)
    lp = SSD_LP

    @pl.when(c == 0)
    def _():
        buf_ref[pl.ds(0, CARRY_B), :] = prevb_ref[0]
        ht_ref[...] = h0_ref[0].T
        if lc < lp:
            buf_ref[pl.ds(CARRY_B + lc, lp - lc), :] = jnp.zeros((lp - lc, CONV_DIM), F32)

    buf_ref[pl.ds(CARRY_B, lc), :] = xbc_ref[0]
    xbc = _silu(_dwconv_from_buf(buf_ref, cw_ref, SSM_CONV, CARRY_B, lp) + cb_ref[...])
    hist = buf_ref[pl.ds(lc, CARRY_B), :]
    newb_ref[0] = hist
    buf_ref[pl.ds(0, CARRY_B), :] = hist

    xs = xbc[:, :D_INNER]
    if lc < lp:
        dt_raw = jnp.concatenate([dt_ref[0], jnp.zeros((lp - lc, LANES), F32)], axis=0)
        zg = jnp.concatenate([z_ref[0], jnp.zeros((lp - lc, D_INNER), F32)], axis=0)
    else:
        dt_raw = dt_ref[0]
        zg = z_ref[0]
    dt = _softplus(dt_raw + dtb_ref[...])
    if lc < lp:
        row = lax.broadcasted_iota(I32, (lp, LANES), 0)
        dt = jnp.where(row < lc, dt, 0.0)
    lane = lax.broadcasted_iota(I32, (lp, LANES), 1)
    dt = jnp.where(lane < SSM_HEADS, dt, 0.0)
    da = dt * a_ref[...]

    r_i = lax.broadcasted_iota(I32, (lp, lp), 0)
    c_i = lax.broadcasted_iota(I32, (lp, lp), 1)
    causal = r_i >= c_i
    tri = jnp.where(causal, 1.0, 0.0).astype(BF16)
    a_col = _dot_exact_lhs(tri, da)
    a_row = a_col.T
    dt_row = dt.T
    a_last = a_col[lp - 1:lp, :]

    expand = expand_ref[...]
    a_exp = _dot_exact_rhs(a_col, expand)
    dt_exp = _dot_exact_rhs(dt, expand)
    al_exp = _dot_exact_rhs(a_last, expand)

    ht = ht_ref[...]
    y_parts = []
    for g in range(SSM_GROUPS):
        bm = xbc[:, D_INNER + g * SSM_STATE: D_INNER + (g + 1) * SSM_STATE]
        cm = xbc[:, D_INNER + (SSM_GROUPS + g) * SSM_STATE: D_INNER + (SSM_GROUPS + g + 1) * SSM_STATE]
        cb = _bdot_nt(cm, bm)
        gs = slice(g * 512, (g + 1) * 512)
        y_inter = _bdot(cm, ht[:, gs]) * jnp.exp(a_exp[:, gs])
        heads = []
        for r in range(SSM_HEADS // SSM_GROUPS):
            h = g * (SSM_HEADS // SSM_GROUPS) + r
            seg = a_col[:, h:h + 1] - a_row[h:h + 1, :]
            decay = jnp.where(causal, jnp.exp(jnp.where(causal, seg, 0.0)), 0.0)
            wts = cb * decay * dt_row[h:h + 1, :]
            heads.append(_bdot(wts, xs[:, h * SSM_HEADDIM:(h + 1) * SSM_HEADDIM]))
        y_parts.append(jnp.concatenate(heads, axis=-1) + y_inter)
        xw = xs[:, gs] * (jnp.exp(al_exp[:, gs] - a_exp[:, gs]) * dt_exp[:, gs])
        ht_ref[:, gs] = ht[:, gs] * jnp.exp(al_exp[:, gs]) + _bdot(bm.T, xw)
    y = jnp.concatenate(y_parts, axis=-1)
    y = y + dskip_ref[...] * xs
    y = y * _silu(zg)
    outs = []
    for g in range(SSM_GROUPS):
        yg = y[:, g * 512:(g + 1) * 512]
        outs.append(yg * lax.rsqrt(jnp.mean(yg * yg, axis=-1, keepdims=True) + EPS))
    yb = jnp.concatenate(outs, axis=-1) * nrm_ref[...]
    y_ref[0] = yb[:lc, :]

    @pl.when(c == nc - 1)
    def _():
        hout_ref[0] = ht_ref[...].T


def ssd_mixer(zmain, ztail, prev8, h0, cw, cb, dtb, a_neg, expand, dskip_exp, nrm, *, lc):
    bsz, L, _ = zmain.shape
    kern = functools.partial(_ssd_kernel, lc=lc)
    return pl.pallas_call(
        kern,
        out_shape=(jax.ShapeDtypeStruct((bsz, L, D_INNER), F32),
                   jax.ShapeDtypeStruct((bsz, CARRY_B, CONV_DIM), F32),
                   jax.ShapeDtypeStruct((bsz, D_INNER, SSM_STATE), F32)),
        grid=(bsz, L // lc),
        in_specs=[pl.BlockSpec((1, lc, CONV_DIM), lambda b_, c: (b_, c, E_COL_XBC // CONV_DIM)),
                  pl.BlockSpec((1, lc, D_INNER), lambda b_, c: (b_, c, E_COL_Z // D_INNER)),
                  pl.BlockSpec((1, lc, LANES), lambda b_, c: (b_, c, 0)),
                  pl.BlockSpec((1, CARRY_B, CONV_DIM), lambda b_, c: (b_, 0, 0)),
                  pl.BlockSpec((1, D_INNER, SSM_STATE), lambda b_, c: (b_, 0, 0)),
                  pl.BlockSpec((SSM_CONV, CONV_DIM), lambda b_, c: (0, 0)),
                  pl.BlockSpec((1, CONV_DIM), lambda b_, c: (0, 0)),
                  pl.BlockSpec((1, LANES), lambda b_, c: (0, 0)),
                  pl.BlockSpec((1, LANES), lambda b_, c: (0, 0)),
                  pl.BlockSpec((LANES, D_INNER), lambda b_, c: (0, 0)),
                  pl.BlockSpec((1, D_INNER), lambda b_, c: (0, 0)),
                  pl.BlockSpec((1, D_INNER), lambda b_, c: (0, 0))],
        out_specs=(pl.BlockSpec((1, lc, D_INNER), lambda b_, c: (b_, c, 0)),
                   pl.BlockSpec((1, CARRY_B, CONV_DIM), lambda b_, c: (b_, 0, 0)),
                   pl.BlockSpec((1, D_INNER, SSM_STATE), lambda b_, c: (b_, 0, 0))),
        scratch_shapes=[pltpu.VMEM((CARRY_B + SSD_LP, CONV_DIM), F32),
                        pltpu.VMEM((SSM_STATE, D_INNER), F32)],
        compiler_params=_cparams(("arbitrary", "arbitrary")),
        name="ssd_mixer",
    )(zmain, zmain, ztail, prev8, h0, cw, cb, dtb, a_neg, expand, dskip_exp, nrm)


def _out_proj_kernel(x_ref, a_ref, b_ref, wa_ref, wb_ref, o_ref):
    acc = jnp.dot(a_ref[...].astype(BF16), wa_ref[...], preferred_element_type=F32)
    acc = acc + jnp.dot(b_ref[...].astype(BF16), wb_ref[...], preferred_element_type=F32)
    o_ref[...] = x_ref[...] + acc


def out_proj(x, a, b, wa, wb, *, tm):
    m, d = x.shape
    ka, kb = a.shape[1], b.shape[1]
    return pl.pallas_call(
        _out_proj_kernel,
        out_shape=jax.ShapeDtypeStruct((m, d), F32),
        grid=(m // tm,),
        in_specs=[pl.BlockSpec((tm, d), lambda i: (i, 0)),
                  pl.BlockSpec((tm, ka), lambda i: (i, 0)),
                  pl.BlockSpec((tm, kb), lambda i: (i, 0)),
                  pl.BlockSpec((ka, d), lambda i: (0, 0)),
                  pl.BlockSpec((kb, d), lambda i: (0, 0))],
        out_specs=pl.BlockSpec((tm, d), lambda i: (i, 0)),
        compiler_params=_cparams(("arbitrary",)),
        name="out_proj",
    )(x, a, b, wa, wb)


def _swiglu_kernel(x_ref, g_ref, wg_ref, wu_ref, wd_ref, o_ref, xn_ref, acc_ref):
    f = pl.program_id(1)
    nf = pl.num_programs(1)

    @pl.when(f == 0)
    def _():
        x = x_ref[...]
        y = x * lax.rsqrt(jnp.mean(x * x, axis=-1, keepdims=True) + EPS) * g_ref[...]
        xn_ref[...] = y.astype(BF16)
        acc_ref[...] = jnp.zeros_like(acc_ref)

    xn = xn_ref[...]
    gate = jnp.dot(xn, wg_ref[...], preferred_element_type=F32)
    up = jnp.dot(xn, wu_ref[...], preferred_element_type=F32)
    hid = (_silu(gate) * up).astype(BF16)
    acc_ref[...] += jnp.dot(hid, wd_ref[...], preferred_element_type=F32)

    @pl.when(f == nf - 1)
    def _():
        o_ref[...] = x_ref[...] + acc_ref[...]


def swiglu_ffn(x, g, wg, wu, wd, *, tm, tf):
    m, d = x.shape
    ff = wg.shape[1]
    return pl.pallas_call(
        _swiglu_kernel,
        out_shape=jax.ShapeDtypeStruct((m, d), F32),
        grid=(m // tm, ff // tf),
        in_specs=[pl.BlockSpec((tm, d), lambda i, f: (i, 0)),
                  pl.BlockSpec((1, d), lambda i, f: (0, 0)),
                  pl.BlockSpec((d, tf), lambda i, f: (0, f)),
                  pl.BlockSpec((d, tf), lambda i, f: (0, f)),
                  pl.BlockSpec((tf, d), lambda i, f: (f, 0))],
        out_specs=pl.BlockSpec((tm, d), lambda i, f: (i, 0)),
        scratch_shapes=[pltpu.VMEM((tm, d), BF16), pltpu.VMEM((tm, d), F32)],
        compiler_params=_cparams(("arbitrary", "arbitrary")),
        name="swiglu_ffn",
    )(x, g, wg, wu, wd)


O_COL_Q, O_COL_K, O_COL_V, O_COL_QI, O_COL_DIN, O_COL_BG, O_COL_CG = 0, 512, 640, 768, 1024, 1536, 2048


def _rope128(x, cos_t, sin_t, lo_mask):
    up = pltpu.roll(x, LANES - ROT_DIM // 2, 1)
    dn = pltpu.roll(x, ROT_DIM // 2, 1)
    return x * cos_t + jnp.where(lo_mask, up, dn) * sin_t


def _odd_post_kernel(qkvq_ref, tail_ref, cos_ref, sin_ref, qn_ref, kn_ref, kg_ref, kb_ref, bd_ref,
                     q_ref, k_ref, v_ref, qi_ref, kiw_ref, k16_ref, v16_ref, kk16_ref):
    cos_t = cos_ref[...]
    sin_t = sin_ref[...]
    lane = lax.broadcasted_iota(I32, cos_t.shape, 1)
    lo_mask = (lane % HEAD_DIM) < (ROT_DIM // 2)
    bd = bd_ref[...]
    z = qkvq_ref[...]
    for c in range(4):
        x = z[:, c * LANES:(c + 1) * LANES]
        ms = _dot_exact_rhs(x * x, bd)
        xn = x * lax.rsqrt(ms + EPS) * qn_ref[...]
        q_ref[:, c * LANES:(c + 1) * LANES] = _rope128(xn, cos_t, sin_t, lo_mask)
    x = z[:, O_COL_K:O_COL_K + LANES]
    ms = _dot_exact_rhs(x * x, bd)
    xn = x * lax.rsqrt(ms + EPS) * kn_ref[...]
    kr = _rope128(xn, cos_t, sin_t, lo_mask)
    k_ref[...] = kr
    k16_ref[...] = kr.astype(BF16)
    vv = z[:, O_COL_V:O_COL_V + LANES]
    v_ref[...] = vv
    v16_ref[...] = vv.astype(BF16)
    for c in range(2):
        x = z[:, O_COL_QI + c * LANES: O_COL_QI + (c + 1) * LANES]
        qi_ref[:, c * LANES:(c + 1) * LANES] = _rope128(x, cos_t, sin_t, lo_mask)
    t = tail_ref[...]
    lane_t = lax.broadcasted_iota(I32, t.shape, 1)
    is_ki = lane_t < IDX_DIM
    mu = _dot_exact_rhs(t, bd)
    tc = t - mu
    var = _dot_exact_rhs(tc * tc, bd)
    kin = tc * lax.rsqrt(var + EPS) * kg_ref[...] + kb_ref[...]
    kin = _rope128(kin, cos_t, sin_t, lo_mask)
    kiw_ref[...] = jnp.where(is_ki, kin, t * (IDX_HEADS ** -0.5 * IDX_DIM ** -0.5))
    kk16_ref[...] = jnp.where(is_ki, kin, pltpu.roll(kin, IDX_DIM, 1)).astype(BF16)


def odd_post(zmain, ztail, cos_t, sin_t, qn, kn, kg, kb, bd, *, tm):
    m = zmain.shape[0]
    nrope = cos_t.shape[0] // tm
    row = lambda i: (i, 0)
    rope = lambda i: (i % nrope, 0)
    const = lambda i: (0, 0)
    return pl.pallas_call(
        _odd_post_kernel,
        out_shape=(jax.ShapeDtypeStruct((m, 512), F32), jax.ShapeDtypeStruct((m, LANES), F32),
                   jax.ShapeDtypeStruct((m, LANES), F32), jax.ShapeDtypeStruct((m, 256), F32),
                   jax.ShapeDtypeStruct((m, LANES), F32), jax.ShapeDtypeStruct((m, LANES), BF16),
                   jax.ShapeDtypeStruct((m, LANES), BF16), jax.ShapeDtypeStruct((m, LANES), BF16)),
        grid=(m // tm,),
        in_specs=[pl.BlockSpec((tm, 1024), row), pl.BlockSpec((tm, LANES), row),
                  pl.BlockSpec((tm, LANES), rope), pl.BlockSpec((tm, LANES), rope),
                  pl.BlockSpec((1, LANES), const), pl.BlockSpec((1, LANES), const),
                  pl.BlockSpec((1, LANES), const), pl.BlockSpec((1, LANES), const),
                  pl.BlockSpec((LANES, LANES), const)],
        out_specs=(pl.BlockSpec((tm, 512), row), pl.BlockSpec((tm, LANES), row), pl.BlockSpec((tm, LANES), row),
                   pl.BlockSpec((tm, 256), row), pl.BlockSpec((tm, LANES), row), pl.BlockSpec((tm, LANES), row),
                   pl.BlockSpec((tm, LANES), row), pl.BlockSpec((tm, LANES), row)),
        compiler_params=_cparams(("arbitrary",)),
        name="odd_post",
    )(zmain, ztail, cos_t, sin_t, qn, kn, kg, kb, bd)


INT_MIN = -2 ** 31
F32_MIN_NORMAL_BITS = 0x00800000


def _code_to_float(c):
    return pltpu.bitcast(c ^ ((c >> 31) & 0x7FFFFFFF), F32)


def _kth_largest(count_ge, k, shape):
    ans = jnp.full(shape, INT_MIN, I32)
    cand = jnp.zeros(shape, I32)
    ans = jnp.where(count_ge(_code_to_float(cand)) >= k, cand, ans)

    def body(i, ans):
        cand = ans | jnp.left_shift(jnp.int32(1), 30 - i)
        return jnp.where(count_ge(_code_to_float(cand)) >= k, cand, ans)

    ans = lax.fori_loop(0, 31, body, ans)
    ans = jnp.where((ans > 0) & (ans < F32_MIN_NORMAL_BITS), 0, ans)
    return _code_to_float(ans)


def _sum_leading(x):
    n = x.shape[0]
    extra = None
    while n > 1:
        if n % 2:
            extra = x[n - 1] if extra is None else extra + x[n - 1]
            n -= 1
        x = x[:n // 2] + x[n // 2:n]
        n //= 2
    return x[0] if extra is None else x[0] + extra


def _upper_tri_bf16(n):
    r = lax.broadcasted_iota(I32, (n, n), 0)
    c = lax.broadcasted_iota(I32, (n, n), 1)
    return jnp.where(r <= c, 1.0, 0.0).astype(BF16)


def _dsa_prompt_kernel(q_ref, qi_ref, wi_ref, k_ref, v_ref, kk_ref, o_ref, keys_ref, bias_ref, *,
                       q_base, s_eff, topk):
    t0 = q_base + pl.program_id(1) * Q_BLOCK
    lane = lax.broadcasted_iota(I32, (Q_BLOCK, LANES), 1)
    row = lax.broadcasted_iota(I32, (Q_BLOCK, LANES), 0)
    lo = lane < HEAD_DIM
    hi = lane >= HEAD_DIM
    rpg = N_HEADS // N_KV

    kk = kk_ref[0]
    qi = qi_ref[0]
    wi = wi_ref[0]
    sc = None
    for h in range(IDX_HEADS):
        x = qi[:, (h // 2) * LANES:(h // 2 + 1) * LANES]
        x = jnp.where(lo if h % 2 == 0 else hi, x, 0.0)
        d = _bdot_nt(x, kk)
        term = jnp.maximum(d, 0.0) * wi[:, IDX_DIM + h:IDX_DIM + h + 1]
        sc = term if sc is None else sc + term
    kpos = lax.broadcasted_iota(I32, (Q_BLOCK, s_eff), 1)
    qpos = t0 + lax.broadcasted_iota(I32, (Q_BLOCK, s_eff), 0)
    adm_all = kpos <= qpos
    sc = jnp.where(sc == 0.0, 0.0, sc)
    keys_ref[...] = jnp.where(adm_all, sc, NEG)

    def count_ge(cand):
        return jnp.sum(jnp.where(keys_ref[...] >= cand, 1.0, 0.0), axis=-1, keepdims=True)

    thr = _kth_largest(count_ge, float(topk), (Q_BLOCK, 1))
    n_ge = count_ge(thr)
    n_adm = t0 + 1 + lax.broadcasted_iota(I32, (Q_BLOCK, 1), 0)
    simple = jnp.where((n_ge == float(topk)) | (n_adm < topk), 0.0, 1.0)
    no_cut = jnp.max(simple) == 0.0

    @pl.when(no_cut)
    def _():
        bias_ref[...] = jnp.where(adm_all, jnp.where(keys_ref[...] >= thr, 0.0, NEG), NEG)

    @pl.when(jnp.logical_not(no_cut))
    def _():
        n_gt = jnp.sum(jnp.where(keys_ref[...] > thr, 1.0, 0.0), axis=-1, keepdims=True)
        need = float(topk) - n_gt
        ut = _upper_tri_bf16(LANES)
        off = jnp.zeros((Q_BLOCK, 1), F32)
        for j in range(s_eff // LANES):
            kj = keys_ref[:, j * LANES:(j + 1) * LANES]
            eq = kj == thr
            pre = jnp.dot(jnp.where(eq, 1.0, 0.0).astype(BF16), ut, preferred_element_type=F32) + off
            adm = (j * LANES + lane) <= (t0 + row)
            keep = jnp.where(eq, jnp.where(pre <= need, 0.0, NEG), jnp.where(kj > thr, 0.0, NEG))
            bias_ref[:, j * LANES:(j + 1) * LANES] = jnp.where(adm, keep, NEG)
            off = pre[:, LANES - 1:LANES]

    kb = k_ref[0]
    vb = v_ref[0]
    bias = bias_ref[...]
    q = q_ref[0] * (HEAD_DIM ** -0.5)
    outs = []
    for g in range(N_KV):
        keep = lo if g == 0 else hi
        qg = jnp.concatenate([jnp.where(keep, q[:, r * LANES:(r + 1) * LANES], 0.0) for r in range(rpg)], axis=0)
        s = _bdot_nt(qg, kb).reshape(rpg, Q_BLOCK, s_eff) + bias[None]
        m = jnp.max(s, axis=-1, keepdims=True)
        e = jnp.exp(s - m)
        l = jnp.sum(e, axis=-1, keepdims=True).reshape(rpg * Q_BLOCK, 1)
        pv = jnp.dot(e.reshape(rpg * Q_BLOCK, s_eff).astype(BF16), vb, preferred_element_type=F32)
        outs.append(pv / l)
    lo4 = lax.broadcasted_iota(I32, (rpg * Q_BLOCK, LANES), 1) < HEAD_DIM
    oc = jnp.where(lo4, outs[0], outs[1])
    for r in range(rpg):
        o_ref[0, :, r * LANES:(r + 1) * LANES] = oc[r * Q_BLOCK:(r + 1) * Q_BLOCK, :]


DSA_SEGMENT = 256


def dsa_prompt(q, qi, wi, k16, v16, kk16):
    bsz, s_len, _ = q.shape
    topk = min(TOPK_MAX, s_len // 4)
    seg = min(DSA_SEGMENT, s_len)
    nq = seg // Q_BLOCK
    outs = []
    for si in range(s_len // seg):
        s_eff = (si + 1) * seg
        kern = functools.partial(_dsa_prompt_kernel, q_base=si * seg, s_eff=s_eff, topk=topk)
        qblk = lambda w, si=si: pl.BlockSpec((1, Q_BLOCK, w), lambda b_, i: (b_, si * nq + i, 0))
        keys = pl.BlockSpec((1, s_eff, LANES), lambda b_, i: (b_, 0, 0))
        outs.append(pl.pallas_call(
            kern,
            out_shape=jax.ShapeDtypeStruct((bsz, seg, 512), F32),
            grid=(bsz, nq),
            in_specs=[qblk(512), qblk(256), qblk(LANES), keys, keys, keys],
            out_specs=pl.BlockSpec((1, Q_BLOCK, 512), lambda b_, i: (b_, i, 0)),
            scratch_shapes=[pltpu.VMEM((Q_BLOCK, s_eff), F32), pltpu.VMEM((Q_BLOCK, s_eff), F32)],
            compiler_params=_cparams(("arbitrary", "arbitrary")),
            name=f"dsa_prompt_{si}",
        )(q, qi, wi, k16, v16, kk16))
    return jnp.concatenate(outs, axis=1)


CARRY_D = 8


def _conv_d_kernel(din_ref, bg_ref, cg_ref, prev_ref, w_ref, y_ref, new_ref, buf_ref, *, tl):
    l = pl.program_id(1)

    @pl.when(l == 0)
    def _():
        buf_ref[pl.ds(0, CARRY_D), :] = prev_ref[0]

    buf_ref[pl.ds(CARRY_D, tl), :] = cg_ref[0] * din_ref[0]
    y_ref[0] = bg_ref[0] * _dwconv_from_buf(buf_ref, w_ref, CONV_D, CARRY_D, tl)
    hist = buf_ref[pl.ds(tl, CARRY_D), :]
    new_ref[0] = hist
    buf_ref[pl.ds(0, CARRY_D), :] = hist


def conv_d(zmain, prev8, w, *, tl):
    bsz, L, _ = zmain.shape
    kern = functools.partial(_conv_d_kernel, tl=tl)
    col = lambda c: pl.BlockSpec((1, tl, D_D), lambda b_, l: (b_, l, c // D_D))
    return pl.pallas_call(
        kern,
        out_shape=(jax.ShapeDtypeStruct((bsz, L, D_D), F32), jax.ShapeDtypeStruct((bsz, CARRY_D, D_D), F32)),
        grid=(bsz, L // tl),
        in_specs=[col(O_COL_DIN), col(O_COL_BG), col(O_COL_CG),
                  pl.BlockSpec((1, CARRY_D, D_D), lambda b_, l: (b_, 0, 0)),
                  pl.BlockSpec((CONV_D, D_D), lambda b_, l: (0, 0))],
        out_specs=(pl.BlockSpec((1, tl, D_D), lambda b_, l: (b_, l, 0)),
                   pl.BlockSpec((1, CARRY_D, D_D), lambda b_, l: (b_, 0, 0))),
        scratch_shapes=[pltpu.VMEM((CARRY_D + tl, D_D), F32)],
        compiler_params=_cparams(("arbitrary", "arbitrary")),
        name="conv_d",
    )(zmain, zmain, zmain, prev8, w)


def _router_kernel(x_ref, g_ref, whi_ref, wlo_ref, br_ref, xn_ref, comb_ref):
    x = x_ref[...]
    y = x * lax.rsqrt(jnp.mean(x * x, axis=-1, keepdims=True) + EPS) * g_ref[...]
    yhi = y.astype(BF16)
    xn_ref[...] = yhi
    ylo = (y - yhi.astype(F32)).astype(BF16)
    logits = jnp.dot(ylo, whi_ref[...], preferred_element_type=F32)
    logits = logits + jnp.dot(yhi, wlo_ref[...], preferred_element_type=F32)
    logits = logits + jnp.dot(yhi, whi_ref[...], preferred_element_type=F32) + br_ref[...]
    lane = lax.broadcasted_iota(I32, logits.shape, 1).astype(F32)
    m1 = jnp.max(logits, axis=-1, keepdims=True)
    i1 = jnp.min(jnp.where(logits == m1, lane, float(LANES)), axis=-1, keepdims=True)
    rest = jnp.where(lane == i1, -3e38, logits)
    m2 = jnp.max(rest, axis=-1, keepdims=True)
    i2 = jnp.min(jnp.where(rest == m2, lane, float(LANES)), axis=-1, keepdims=True)
    e2 = jnp.exp(m2 - m1)
    g1 = 1.0 / (1.0 + e2)
    g2 = e2 / (1.0 + e2)
    comb_ref[...] = jnp.where(lane == i1, g1, 0.0) + jnp.where(lane == i2, g2, 0.0)


def moe_router(x, g, w_hi, w_lo, b_pad, *, tm):
    m, d = x.shape
    return pl.pallas_call(
        _router_kernel,
        out_shape=(jax.ShapeDtypeStruct((m, d), BF16), jax.ShapeDtypeStruct((m, LANES), F32)),
        grid=(m // tm,),
        in_specs=[pl.BlockSpec((tm, d), lambda i: (i, 0)),
                  pl.BlockSpec((1, d), lambda i: (0, 0)),
                  pl.BlockSpec((d, LANES), lambda i: (0, 0)),
                  pl.BlockSpec((d, LANES), lambda i: (0, 0)),
                  pl.BlockSpec((1, LANES), lambda i: (0, 0))],
        out_specs=(pl.BlockSpec((tm, d), lambda i: (i, 0)), pl.BlockSpec((tm, LANES), lambda i: (i, 0))),
        compiler_params=_cparams(("arbitrary",)),
        name="moe_router",
    )(x, g, w_hi, w_lo, b_pad)


def _moe_dense_kernel(x_ref, xn_ref, comb_ref, wg_ref, wu_ref, wd_ref, o_ref, acc_ref, acce_ref):
    e = pl.program_id(1)
    f = pl.program_id(2)
    ne = pl.num_programs(1)
    nf = pl.num_programs(2)

    @pl.when((e == 0) & (f == 0))
    def _():
        acc_ref[...] = jnp.zeros_like(acc_ref)

    @pl.when(f == 0)
    def _():
        acce_ref[...] = jnp.zeros_like(acce_ref)

    xn = xn_ref[...]
    gate = jnp.dot(xn, wg_ref[0], preferred_element_type=F32)
    up = jnp.dot(xn, wu_ref[0], preferred_element_type=F32)
    hid = (_silu(gate) * up).astype(BF16)
    acce_ref[...] += jnp.dot(hid, wd_ref[0], preferred_element_type=F32)

    @pl.when(f == nf - 1)
    def _():
        comb = comb_ref[...]
        lane = lax.broadcasted_iota(I32, comb.shape, 1)
        c = jnp.sum(jnp.where(lane == e, comb, 0.0), axis=-1, keepdims=True)
        acc_ref[...] += c * acce_ref[...]

    @pl.when((e == ne - 1) & (f == nf - 1))
    def _():
        o_ref[...] = x_ref[...] + acc_ref[...]


def moe_dense(x, xn, comb, wg, wu, wd, *, tm, tf):
    m, d = x.shape
    ne, _, ff = wg.shape
    return pl.pallas_call(
        _moe_dense_kernel,
        out_shape=jax.ShapeDtypeStruct((m, d), F32),
        grid=(m // tm, ne, ff // tf),
        in_specs=[pl.BlockSpec((tm, d), lambda i, e, f: (i, 0)),
                  pl.BlockSpec((tm, d), lambda i, e, f: (i, 0)),
                  pl.BlockSpec((tm, LANES), lambda i, e, f: (i, 0)),
                  pl.BlockSpec((1, d, tf), lambda i, e, f: (e, 0, f)),
                  pl.BlockSpec((1, d, tf), lambda i, e, f: (e, 0, f)),
                  pl.BlockSpec((1, tf, d), lambda i, e, f: (e, f, 0))],
        out_specs=pl.BlockSpec((tm, d), lambda i, e, f: (i, 0)),
        scratch_shapes=[pltpu.VMEM((tm, d), F32), pltpu.VMEM((tm, d), F32)],
        compiler_params=_cparams(("arbitrary", "arbitrary", "arbitrary")),
        name="moe_dense",
    )(x, xn, comb, wg, wu, wd)


MOE_TILE = 512
DMA_ISSUE_UNROLL = 8
FF_TILE = D_FF // 2
R_I1, R_I2, R_G1, R_G2, R_RANK1, R_RANK2 = 0, 1, 2, 3, 4, 5


def _router_rank_kernel(x_ref, g_ref, whi_ref, wlo_ref, br_ref, info_ref, cnt_ref, run_ref):
    i = pl.program_id(0)

    @pl.when(i == 0)
    def _():
        run_ref[...] = jnp.zeros_like(run_ref)

    x = x_ref[...]
    tm = x.shape[0]
    y = x * lax.rsqrt(jnp.mean(x * x, axis=-1, keepdims=True) + EPS) * g_ref[...]
    yhi = y.astype(BF16)
    ylo = (y - yhi.astype(F32)).astype(BF16)
    logits = jnp.dot(ylo, whi_ref[...], preferred_element_type=F32)
    logits = logits + jnp.dot(yhi, wlo_ref[...], preferred_element_type=F32)
    logits = logits + jnp.dot(yhi, whi_ref[...], preferred_element_type=F32) + br_ref[...]
    lane = lax.broadcasted_iota(I32, logits.shape, 1).astype(F32)
    m1 = jnp.max(logits, axis=-1, keepdims=True)
    i1 = jnp.min(jnp.where(logits == m1, lane, float(LANES)), axis=-1, keepdims=True)
    rest = jnp.where(lane == i1, -3e38, logits)
    m2 = jnp.max(rest, axis=-1, keepdims=True)
    i2 = jnp.min(jnp.where(rest == m2, lane, float(LANES)), axis=-1, keepdims=True)
    e2 = jnp.exp(m2 - m1)
    g1 = 1.0 / (1.0 + e2)
    g2 = e2 / (1.0 + e2)
    oh1 = jnp.where(lane == i1, 1.0, 0.0)
    oh2 = jnp.where(lane == i2, 1.0, 0.0)
    both = oh1 + oh2
    r_i = lax.broadcasted_iota(I32, (tm, tm), 0)
    c_i = lax.broadcasted_iota(I32, (tm, tm), 1)
    strict_lower = jnp.where(c_i < r_i, 1.0, 0.0).astype(BF16)
    before = jnp.dot(strict_lower, both.astype(BF16), preferred_element_type=F32) + run_ref[...]
    rank1 = jnp.sum(before * oh1, axis=-1, keepdims=True)
    rank2 = jnp.sum(before * oh2, axis=-1, keepdims=True)
    run_ref[...] += jnp.sum(both, axis=0, keepdims=True)
    info = jnp.where(lane == R_I1, i1, 0.0)
    for col, val in ((R_I2, i2), (R_G1, g1), (R_G2, g2), (R_RANK1, rank1), (R_RANK2, rank2)):
        info = jnp.where(lane == col, val, info)
    info_ref[...] = info
    cnt_ref[...] = run_ref[...]


def moe_router_rank(x, g, w_hi, w_lo, b_pad, *, tm):
    m, d = x.shape
    return pl.pallas_call(
        _router_rank_kernel,
        out_shape=(jax.ShapeDtypeStruct((m, LANES), F32), jax.ShapeDtypeStruct((1, LANES), F32)),
        grid=(m // tm,),
        in_specs=[pl.BlockSpec((tm, d), lambda i: (i, 0)),
                  pl.BlockSpec((1, d), lambda i: (0, 0)),
                  pl.BlockSpec((d, LANES), lambda i: (0, 0)),
                  pl.BlockSpec((d, LANES), lambda i: (0, 0)),
                  pl.BlockSpec((1, LANES), lambda i: (0, 0))],
        out_specs=(pl.BlockSpec((tm, LANES), lambda i: (i, 0)), pl.BlockSpec((1, LANES), lambda i: (0, 0))),
        scratch_shapes=[pltpu.VMEM((1, LANES), F32)],
        compiler_params=_cparams(("arbitrary",)),
        name="moe_router_rank",
    )(x, g, w_hi, w_lo, b_pad)


def _row_copy(src_ref, si, dst_ref, di, sem):
    return pltpu.make_async_copy(src_ref.at[pl.ds(si, 1)], dst_ref.at[pl.ds(di, 1)], sem)


def _moe_dispatch_kernel(d1_ref, d2_ref, x_ref, xs_ref, sem, *, ct):
    def issue(t, carry):
        _row_copy(x_ref, t, xs_ref, d1_ref[0, 0, t], sem).start(priority=0)
        _row_copy(x_ref, t, xs_ref, d2_ref[0, 0, t], sem).start(priority=1)
        return carry

    lax.fori_loop(0, ct, issue, 0, unroll=DMA_ISSUE_UNROLL)
    tile_copy = pltpu.make_async_copy(x_ref, xs_ref.at[pl.ds(0, ct)], sem)
    tile_copy.wait()
    tile_copy.wait()


def moe_dispatch(x, dest1, dest2, *, ct):
    m, d = x.shape
    smem_row = pl.BlockSpec((1, 1, ct), lambda i: (i, 0, 0), memory_space=pltpu.SMEM)
    return pl.pallas_call(
        functools.partial(_moe_dispatch_kernel, ct=ct),
        out_shape=jax.ShapeDtypeStruct((2 * m, d), F32),
        grid=(m // ct,),
        in_specs=[smem_row, smem_row, pl.BlockSpec((ct, d), lambda i: (i, 0))],
        out_specs=pl.BlockSpec(memory_space=pl.ANY),
        scratch_shapes=[pltpu.SemaphoreType.DMA],
        compiler_params=_cparams(("arbitrary",)),
        name="moe_dispatch",
    )(dest1, dest2, x)


def _moe_grouped_kernel(tile_ref, exp_ref, first_ref, valid_ref, gs_ref, xs_ref, g_ref, wg_ref, wu_ref, wd_ref,
                        ys_ref, xn_ref, acc_ref):
    w = pl.program_id(0)
    f = pl.program_id(1)
    nf = pl.num_programs(1)

    @pl.when(valid_ref[w] == 1)
    def _():
        @pl.when(f == 0)
        def _():
            x = xs_ref[...]
            y = x * lax.rsqrt(jnp.mean(x * x, axis=-1, keepdims=True) + EPS) * g_ref[...]
            xn_ref[...] = y.astype(BF16)
            acc_ref[...] = jnp.zeros_like(acc_ref)

        xn = xn_ref[...]
        gate = jnp.dot(xn, wg_ref[0], preferred_element_type=F32)
        up = jnp.dot(xn, wu_ref[0], preferred_element_type=F32)
        hid = (_silu(gate) * up).astype(BF16)
        acc_ref[...] += jnp.dot(hid, wd_ref[0], preferred_element_type=F32)

        @pl.when(f == nf - 1)
        def _():
            e = exp_ref[w]
            row = tile_ref[w] * MOE_TILE + lax.broadcasted_iota(I32, acc_ref.shape, 0)
            mine = (row >= gs_ref[e]) & (row < gs_ref[e + 1])
            part = jnp.where(mine, acc_ref[...], 0.0)

            @pl.when(first_ref[w] == 1)
            def _():
                ys_ref[...] = part

            @pl.when(first_ref[w] == 0)
            def _():
                ys_ref[...] += part


def moe_grouped(item_tile, item_expert, item_first, item_valid, group_starts, xs, g, wg, wu, wd, *, tf):
    r, d = xs.shape
    ff = wg.shape[2]
    n_items = item_tile.shape[0]
    return pl.pallas_call(
        _moe_grouped_kernel,
        out_shape=jax.ShapeDtypeStruct((r, d), F32),
        grid_spec=pltpu.PrefetchScalarGridSpec(
            num_scalar_prefetch=5,
            grid=(n_items, ff // tf),
            in_specs=[pl.BlockSpec((MOE_TILE, d), lambda w, f, it, ie, i1, iv, gs: (it[w], 0)),
                      pl.BlockSpec((1, d), lambda w, f, it, ie, i1, iv, gs: (0, 0)),
                      pl.BlockSpec((1, d, tf), lambda w, f, it, ie, i1, iv, gs: (ie[w], 0, f)),
                      pl.BlockSpec((1, d, tf), lambda w, f, it, ie, i1, iv, gs: (ie[w], 0, f)),
                      pl.BlockSpec((1, tf, d), lambda w, f, it, ie, i1, iv, gs: (ie[w], f, 0))],
            out_specs=pl.BlockSpec((MOE_TILE, d), lambda w, f, it, ie, i1, iv, gs: (it[w], 0)),
            scratch_shapes=[pltpu.VMEM((MOE_TILE, d), BF16), pltpu.VMEM((MOE_TILE, d), F32)]),
        compiler_params=_cparams(("arbitrary", "arbitrary")),
        name="moe_grouped",
    )(item_tile, item_expert, item_first, item_valid, group_starts, xs, g, wg, wu, wd)


def _moe_combine_kernel(d1_ref, d2_ref, x_ref, info_ref, ys_ref, o_ref, y1_ref, y2_ref, sem, *, ct):
    def issue(t, carry):
        _row_copy(ys_ref, d1_ref[0, 0, t], y1_ref, t, sem).start(priority=0)
        _row_copy(ys_ref, d2_ref[0, 0, t], y2_ref, t, sem).start(priority=1)
        return carry

    lax.fori_loop(0, ct, issue, 0, unroll=DMA_ISSUE_UNROLL)
    pltpu.make_async_copy(ys_ref.at[pl.ds(0, ct)], y1_ref, sem).wait()
    pltpu.make_async_copy(ys_ref.at[pl.ds(0, ct)], y2_ref, sem).wait()
    info = info_ref[...]
    g1 = info[:, R_G1:R_G1 + 1]
    g2 = info[:, R_G2:R_G2 + 1]
    o_ref[...] = x_ref[...] + (g1 * y1_ref[...] + g2 * y2_ref[...])


def moe_combine(x, info, ys, dest1, dest2, *, ct):
    m, d = x.shape
    smem_row = pl.BlockSpec((1, 1, ct), lambda i: (i, 0, 0), memory_space=pltpu.SMEM)
    return pl.pallas_call(
        functools.partial(_moe_combine_kernel, ct=ct),
        out_shape=jax.ShapeDtypeStruct((m, d), F32),
        grid=(m // ct,),
        in_specs=[smem_row, smem_row, pl.BlockSpec((ct, d), lambda i: (i, 0)),
                  pl.BlockSpec((ct, LANES), lambda i: (i, 0)), pl.BlockSpec(memory_space=pl.ANY)],
        out_specs=pl.BlockSpec((ct, d), lambda i: (i, 0)),
        scratch_shapes=[pltpu.VMEM((ct, d), F32), pltpu.VMEM((ct, d), F32), pltpu.SemaphoreType.DMA],
        compiler_params=_cparams(("arbitrary",)),
        name="moe_combine",
    )(dest1, dest2, x, info, ys)


def _moe_work_items(counts, n_tiles):
    n_exp = counts.shape[0]
    n_items = n_tiles + n_exp - 1
    ends = jnp.cumsum(counts)
    starts = ends - counts
    expert_of_row = lambda rr: jnp.minimum(jnp.sum(ends[None, :] <= rr[:, None], axis=1), n_exp - 1).astype(I32)
    t = jnp.arange(n_tiles, dtype=I32)
    e_lo = expert_of_row(t * MOE_TILE)
    e_hi = expert_of_row(t * MOE_TILE + (MOE_TILE - 1))
    per_tile = e_hi - e_lo + 1
    item_end = jnp.cumsum(per_tile)
    item_start = item_end - per_tile
    total = item_end[-1]
    w = jnp.minimum(jnp.arange(n_items, dtype=I32), total - 1)
    tile = jnp.sum(item_end[None, :] <= w[:, None], axis=1).astype(I32)
    expert = (e_lo[tile] + (w - item_start[tile])).astype(I32)
    first = (w == item_start[tile]).astype(I32)
    valid = (jnp.arange(n_items, dtype=I32) < total).astype(I32)
    group_starts = jnp.concatenate([starts, ends[-1:]]).astype(I32)
    return tile, expert, first, valid, group_starts


def moe_routed(x, g, w_hi, w_lo, b_pad, wg, wu, wd, *, tm):
    m, d = x.shape
    n_exp = wg.shape[0]
    assert (2 * m) % MOE_TILE == 0
    info, cnt = moe_router_rank(x, g, w_hi, w_lo, b_pad, tm=tm)
    counts = cnt[0, :n_exp].astype(I32)
    starts = jnp.cumsum(counts) - counts
    i1 = info[:, R_I1].astype(I32)
    i2 = info[:, R_I2].astype(I32)
    dest1 = (starts[i1] + info[:, R_RANK1].astype(I32)).reshape(m // tm, 1, tm)
    dest2 = (starts[i2] + info[:, R_RANK2].astype(I32)).reshape(m // tm, 1, tm)
    xs = moe_dispatch(x, dest1, dest2, ct=tm)
    items = _moe_work_items(counts, (2 * m) // MOE_TILE)
    ys = moe_grouped(*items, xs, g, wg, wu, wd, tf=FF_TILE)
    return moe_combine(x, info, ys, dest1, dest2, ct=tm)


PAGES_PER_STEP = 32


def _sdsa_index_kernel(pt_ref, qi_ref, wi_ref, kknew_ref, *rest, n_pages, t_len, topk):
    del pt_ref
    page_refs = rest[:PAGES_PER_STEP]
    mpast_ref, mnew_ref, keys_ref, pre_ref = rest[PAGES_PER_STEP:]
    j = pl.program_id(1)
    nj = pl.num_programs(1)
    lane = lax.broadcasted_iota(I32, (t_len, LANES), 1)
    row = lax.broadcasted_iota(I32, (t_len, LANES), 0)
    lo = lane < HEAD_DIM

    qi = qi_ref[0]
    wi = wi_ref[0]
    qs, ws = [], []
    for h in range(IDX_HEADS):
        x = qi[:, (h // 2) * LANES:(h // 2 + 1) * LANES]
        if h % 2 == 1:
            x = pltpu.roll(x, HEAD_DIM, 1)
        qs.append(x[:, :IDX_DIM])
        ws.append(wi[:, IDX_DIM + h:IDX_DIM + h + 1])
    q32 = jnp.concatenate(qs, axis=0).astype(BF16)
    w32 = jnp.concatenate(ws, axis=0)

    def scores(dots):
        d = jnp.maximum(dots, 0.0) * w32
        s = d[0:t_len]
        for h in range(1, IDX_HEADS):
            s = s + d[h * t_len:(h + 1) * t_len]
        return jnp.where(s == 0.0, 0.0, s)

    pages_t = jnp.concatenate([page_refs[i][0].astype(BF16) for i in range(PAGES_PER_STEP)], axis=1)
    s_wide = scores(jnp.dot(q32, pages_t, preferred_element_type=F32))
    for i in range(PAGES_PER_STEP):
        keys_ref[j * PAGES_PER_STEP + i] = s_wide[:, i * PAGE_SIZE:(i + 1) * PAGE_SIZE]

    @pl.when(j == nj - 1)
    def _():
        knew = jnp.concatenate([kknew_ref[0][:, :IDX_DIM], jnp.zeros((LANES - t_len, IDX_DIM), F32)], axis=0)
        s_new = jnp.where((lane <= row) & (lane < t_len), scores(_bdot_nt(q32, knew)), NEG)
        keys_ref[n_pages] = s_new

        def count_ge(cand):
            part = _sum_leading(jnp.where(keys_ref[...] >= cand, 1.0, 0.0))
            return jnp.sum(part, axis=-1, keepdims=True)

        thr = _kth_largest(count_ge, float(topk), (t_len, 1))
        n_ge = count_ge(thr)
        no_cut = jnp.max(jnp.abs(n_ge - float(topk))) == 0.0

        @pl.when(no_cut)
        def _():
            bias = jnp.where(keys_ref[...] >= thr, 0.0, NEG)
            mpast_ref[0] = bias[:n_pages]
            mnew_ref[0] = bias[n_pages]

        @pl.when(jnp.logical_not(no_cut))
        def _():
            keys = keys_ref[...]
            n_gt = jnp.sum(_sum_leading(jnp.where(keys > thr, 1.0, 0.0)), axis=-1, keepdims=True)
            need = float(topk) - n_gt
            eqf = jnp.where(keys == thr, 1.0, 0.0).astype(BF16).reshape((n_pages + 1) * t_len, LANES)
            pre_ref[...] = jnp.dot(eqf, _upper_tri_bf16(LANES), preferred_element_type=F32).reshape(
                n_pages + 1, t_len, LANES)

            def body(p, off):
                kp = keys_ref[p]
                pre = pre_ref[p]
                bias = jnp.where(kp == thr, jnp.where((pre + off) <= need, 0.0, NEG),
                                 jnp.where(kp > thr, 0.0, NEG))

                @pl.when(p < n_pages)
                def _():
                    mpast_ref[0, p] = bias

                @pl.when(p == n_pages)
                def _():
                    mnew_ref[0] = bias

                return off + pre[:, LANES - 1:LANES]

            lax.fori_loop(0, n_pages + 1, body, jnp.zeros((t_len, 1), F32))


def sdsa_index(page_table, qi, wi, kk_new, cache_kidx):
    bsz, t_len, _ = qi.shape
    n_pages = page_table.shape[1]
    topk = min(TOPK_MAX, (n_pages * PAGE_SIZE + t_len) // 4)
    kern = functools.partial(_sdsa_index_kernel, n_pages=n_pages, t_len=t_len, topk=topk)
    tok = lambda w: pl.BlockSpec((1, t_len, w), lambda b_, j, pt: (b_, 0, 0))
    page_specs = [pl.BlockSpec((1, IDX_DIM, PAGE_SIZE),
                               functools.partial(lambda b_, j, pt, i: (pt[b_, j * PAGES_PER_STEP + i], 0, 0), i=i))
                  for i in range(PAGES_PER_STEP)]
    return pl.pallas_call(
        kern,
        out_shape=(jax.ShapeDtypeStruct((bsz, n_pages, t_len, LANES), F32),
                   jax.ShapeDtypeStruct((bsz, t_len, LANES), F32)),
        grid_spec=pltpu.PrefetchScalarGridSpec(
            num_scalar_prefetch=1,
            grid=(bsz, n_pages // PAGES_PER_STEP),
            in_specs=[tok(256), tok(LANES), tok(LANES)] + page_specs,
            out_specs=(pl.BlockSpec((1, n_pages, t_len, LANES), lambda b_, j, pt: (b_, 0, 0, 0)),
                       pl.BlockSpec((1, t_len, LANES), lambda b_, j, pt: (b_, 0, 0))),
            scratch_shapes=[pltpu.VMEM((n_pages + 1, t_len, LANES), F32),
                            pltpu.VMEM((n_pages + 1, t_len, LANES), F32)]),
        compiler_params=_cparams(("arbitrary", "arbitrary")),
        name="sdsa_index",
    )(page_table, qi, wi, kk_new, *([cache_kidx] * PAGES_PER_STEP))


def _sdsa_attn_kernel(pt_ref, q_ref, knew_ref, vnew_ref, mpast_ref, mnew_ref, *rest, t_len):
    del pt_ref
    kp_refs = rest[:PAGES_PER_STEP]
    vp_refs = rest[PAGES_PER_STEP:2 * PAGES_PER_STEP]
    o_ref, qg_ref, m_ref, l_ref, acc_ref = rest[2 * PAGES_PER_STEP:]
    j = pl.program_id(1)
    nj = pl.num_programs(1)
    rpg = N_HEADS // N_KV
    reps = rpg

    def group_lanes(x, g):
        if g == 1:
            x = pltpu.roll(x, HEAD_DIM, 1)
        return x[:, :HEAD_DIM]

    @pl.when(j == 0)
    def _():
        q = q_ref[0]
        for g in range(N_KV):
            parts = [group_lanes(q[:, r * LANES:(r + 1) * LANES], g) for r in range(rpg)]
            qg_ref[g] = (jnp.concatenate(parts, axis=0) * (HEAD_DIM ** -0.5)).astype(BF16)
        m_ref[...] = jnp.full(m_ref.shape, NEG, F32)
        l_ref[...] = jnp.zeros(l_ref.shape, F32)
        acc_ref[...] = jnp.zeros(acc_ref.shape, F32)

    def update(g, s, pv):
        m_old = m_ref[g]
        m_cur = jnp.maximum(m_old, jnp.max(s, axis=-1, keepdims=True))
        alpha = jnp.exp(m_old - m_cur)
        p = jnp.exp(s - m_cur)
        m_ref[g] = m_cur
        l_ref[g] = l_ref[g] * alpha + jnp.sum(p, axis=-1, keepdims=True)
        acc_ref[g] = acc_ref[g] * alpha + pv(p.astype(BF16))

    bias8 = jnp.concatenate([mpast_ref[0, i] for i in range(PAGES_PER_STEP)], axis=1)
    bias_wide = jnp.concatenate([bias8] * reps, axis=0)
    for g in range(N_KV):
        k_wide = jnp.concatenate([kp_refs[i][0, g].astype(BF16) for i in range(PAGES_PER_STEP)], axis=1)
        v_wide = jnp.concatenate([vp_refs[i][0, g].astype(BF16) for i in range(PAGES_PER_STEP)], axis=1)
        s = jnp.dot(qg_ref[g], k_wide, preferred_element_type=F32) + bias_wide
        update(g, s, lambda p, v_wide=v_wide: _bdot_nt(p, v_wide))

    @pl.when(j == nj - 1)
    def _():
        pad = jnp.zeros((LANES - t_len, HEAD_DIM), F32)
        bias = jnp.concatenate([mnew_ref[0]] * reps, axis=0)
        for g in range(N_KV):
            knew = jnp.concatenate([group_lanes(knew_ref[0], g), pad], axis=0)
            vnew = jnp.concatenate([group_lanes(vnew_ref[0], g), pad], axis=0)
            update(g, _bdot_nt(qg_ref[g], knew) + bias, lambda p, vnew=vnew: _bdot(p, vnew))
        o0 = acc_ref[0] / l_ref[0]
        o1 = acc_ref[1] / l_ref[1]
        for r in range(rpg):
            o_ref[0, :, r * LANES:(r + 1) * LANES] = jnp.concatenate(
                [o0[r * t_len:(r + 1) * t_len], o1[r * t_len:(r + 1) * t_len]], axis=-1)


def sdsa_attn(page_table, q, k_new, v_new, mask_past, mask_new, cache_k, cache_v):
    bsz, t_len, _ = q.shape
    n_pages = page_table.shape[1]
    kern = functools.partial(_sdsa_attn_kernel, t_len=t_len)
    tok = lambda w: pl.BlockSpec((1, t_len, w), lambda b_, j, pt: (b_, 0, 0))
    page_specs = [pl.BlockSpec((1, N_KV, HEAD_DIM, PAGE_SIZE),
                               functools.partial(lambda b_, j, pt, i: (pt[b_, j * PAGES_PER_STEP + i], 0, 0, 0), i=i))
                  for i in range(PAGES_PER_STEP)]
    rows = (N_HEADS // N_KV) * t_len
    return pl.pallas_call(
        kern,
        out_shape=jax.ShapeDtypeStruct((bsz, t_len, 512), F32),
        grid_spec=pltpu.PrefetchScalarGridSpec(
            num_scalar_prefetch=1,
            grid=(bsz, n_pages // PAGES_PER_STEP),
            in_specs=[tok(512), tok(LANES), tok(LANES),
                      pl.BlockSpec((1, PAGES_PER_STEP, t_len, LANES), lambda b_, j, pt: (b_, j, 0, 0)),
                      tok(LANES)] + page_specs + page_specs,
            out_specs=pl.BlockSpec((1, t_len, 512), lambda b_, j, pt: (b_, 0, 0)),
            scratch_shapes=[pltpu.VMEM((N_KV, rows, HEAD_DIM), BF16), pltpu.VMEM((N_KV, rows, 1), F32),
                            pltpu.VMEM((N_KV, rows, 1), F32), pltpu.VMEM((N_KV, rows, HEAD_DIM), F32)]),
        compiler_params=_cparams(("arbitrary", "arbitrary")),
        name="sdsa_attn",
    )(page_table, q, k_new, v_new, mask_past, mask_new,
      *([cache_k] * PAGES_PER_STEP), *([cache_v] * PAGES_PER_STEP))


HEAD_ORDER = (0, 4, 1, 5, 2, 6, 3, 7)


def _pad_cols(w, n):
    return jnp.pad(w, ((0, 0), (0, n - w.shape[1])))


def _row(v):
    return v.reshape(1, -1).astype(F32)


def _rope_tables(pos):
    half = ROT_DIM // 2
    inv = ROPE_THETA ** (-jnp.arange(half, dtype=F32) / half)
    ang = pos.astype(F32)[:, None] * inv[None, :]
    cos, sin = jnp.cos(ang), jnp.sin(ang)
    n = pos.shape[0]
    rest = HEAD_DIM - ROT_DIM
    c64 = jnp.concatenate([cos, cos, jnp.ones((n, rest), F32)], axis=1)
    s64 = jnp.concatenate([-sin, sin, jnp.zeros((n, rest), F32)], axis=1)
    return jnp.tile(c64, (1, 2)), jnp.tile(s64, (1, 2))


def _even_layer(x, prev_a, prev_ssm, prev_b, lc, p, *, tm, tl_a):
    bsz, L, d = x.shape
    m = bsz * L
    x2 = x.reshape(m, d)
    zmain, ztail = norm_proj(x2, p["e_norm_mix"], p["e_w_main"], p["e_w_tail"], tm=tm, tn=E_MAIN // 2)
    zmain3 = zmain.reshape(bsz, L, E_MAIN)
    prev32 = jnp.pad(prev_a, ((0, 0), (CARRY_A - (CONV_A - 1), 0), (0, 0)))
    ya, new_a = conv_a(zmain3, prev32, p["e_conv_a_w"], p["e_conv_a_b"], p["e_ln_a_g"], p["e_ln_a_b"], tl=tl_a)
    prev8 = jnp.pad(prev_b, ((0, 0), (CARRY_B - (SSM_CONV - 1), 0), (0, 0)))
    yb, new_b, hout = ssd_mixer(zmain3, ztail.reshape(bsz, L, LANES), prev8,
                                prev_ssm.reshape(bsz, D_INNER, SSM_STATE),
                                p["e_conv_b_w"], p["e_conv_b_b"], p["e_dt_bias"], p["e_a_neg"], p["e_expand"],
                                p["e_dskip"], p["e_ssm_norm"], lc=lc)
    x2 = out_proj(x2, ya.reshape(m, D_A), yb.reshape(m, D_INNER), p["e_wo_a"], p["e_wo_b"], tm=tm)
    x2 = swiglu_ffn(x2, p["e_norm_ffn"], p["e_w_gate"], p["e_w_up"], p["e_w_down"], tm=min(m, 512), tf=FF_TILE)
    return (x2.reshape(bsz, L, d), new_a[:, CARRY_A - (CONV_A - 1):], hout.reshape(bsz, SSM_HEADS, SSM_HEADDIM, SSM_STATE),
            new_b[:, CARRY_B - (SSM_CONV - 1):])


def _odd_project(x2, cos_t, sin_t, p, *, tm):
    zmain, ztail = norm_proj(x2, p["o_norm_mix"], p["o_w_main"], p["o_w_tail"], tm=tm, tn=O_MAIN)
    q, k, v, qi, kiw, k16, v16, kk16 = odd_post(zmain, ztail, cos_t, sin_t, p["o_q_norm"], p["o_k_norm"],
                                                p["o_kidx_g"], p["o_kidx_b"], p["o_bd"], tm=tm)
    return zmain, q, k, v, qi, kiw, (k16, v16, kk16)


def _odd_tail(x2, att, zmain3, prev_d, p, *, tm, tl_d):
    bsz, L, _ = zmain3.shape
    m = bsz * L
    prev8 = jnp.pad(prev_d, ((0, 0), (CARRY_D - (CONV_D - 1), 0), (0, 0)))
    dm, new_d = conv_d(zmain3, prev8, p["o_conv_d_w"], tl=tl_d)
    x2 = out_proj(x2, att.reshape(m, 512), dm.reshape(m, D_D), p["o_wo_a"], p["o_wo_b"], tm=tm)
    if (2 * m) % MOE_TILE == 0 and 2 * m >= N_EXPERTS * MOE_TILE:
        x2 = moe_routed(x2, p["o_norm_ffn"], p["o_wr_hi"], p["o_wr_lo"], p["o_br"],
                        p["o_we_gate"], p["o_we_up"], p["o_we_down"], tm=tm)
    else:
        xn, comb = moe_router(x2, p["o_norm_ffn"], p["o_wr_hi"], p["o_wr_lo"], p["o_br"], tm=tm)
        x2 = moe_dense(x2, xn, comb, p["o_we_gate"], p["o_we_up"], p["o_we_down"], tm=min(m, 1024), tf=256)
    return x2, new_d[:, CARRY_D - (CONV_D - 1):]


def kernel(x_prompt, x_sample, state_conv_a, state_ssm, state_conv_b, cache_k, cache_v, cache_kidx, state_conv_d,
           page_table, e_norm_mix, e_w_in, e_conv_a_w, e_conv_a_b, e_ln_a_g, e_ln_a_b, e_conv_b_w, e_conv_b_b,
           e_dt_bias, e_a_log, e_d_skip, e_ssm_norm, e_w_out, e_norm_ffn, e_w_gate, e_w_up, e_w_down,
           o_norm_mix, o_w_in, o_q_norm, o_k_norm, o_kidx_g, o_kidx_b, o_conv_d_w, o_w_out, o_norm_ffn,
           o_w_router, o_b_router, o_we_gate, o_we_up, o_we_down):
    bp, s_len, d = x_prompt.shape
    bd, t_len, _ = x_sample.shape
    n_pairs = e_w_in.shape[0]
    past = page_table.shape[1] * PAGE_SIZE
    n_pool = cache_k.shape[1]
    xp, xs = x_prompt, x_sample
    outs_p = [[] for _ in range(7)]
    outs_s = [[] for _ in range(7)]
    cos_p, sin_p = _rope_tables(jnp.arange(s_len))
    cos_s, sin_s = _rope_tables(jnp.tile(past + jnp.arange(t_len), bd))
    perm = np.concatenate([np.arange(h * HEAD_DIM, (h + 1) * HEAD_DIM) for h in HEAD_ORDER])
    expand = (jnp.arange(LANES)[:, None] == (jnp.arange(D_INNER)[None, :] // SSM_HEADDIM)).astype(BF16)
    blk = jnp.arange(LANES) // HEAD_DIM
    bdiag = jnp.where(blk[:, None] == blk[None, :], 1.0 / HEAD_DIM, 0.0).astype(BF16)
    for i in range(n_pairs):
        w = e_w_in[i]
        p = {
            "e_norm_mix": _row(e_norm_mix[i]),
            "e_w_main": jnp.concatenate([w[:, 2048:3584], w[:, 0:512], w[:, 1024:2048], w[:, 512:1024]],
                                        axis=1).astype(BF16),
            "e_w_tail": _pad_cols(w[:, 3584:3600], LANES).astype(BF16),
            "e_conv_a_w": e_conv_a_w[i], "e_conv_a_b": _row(e_conv_a_b[i]),
            "e_ln_a_g": _row(e_ln_a_g[i]), "e_ln_a_b": _row(e_ln_a_b[i]),
            "e_conv_b_w": e_conv_b_w[i], "e_conv_b_b": _row(e_conv_b_b[i]),
            "e_dt_bias": _pad_cols(_row(e_dt_bias[i]), LANES),
            "e_a_neg": _pad_cols(_row(-jnp.exp(e_a_log[i].astype(F32))), LANES),
            "e_expand": expand,
            "e_dskip": _row(jnp.repeat(e_d_skip[i], SSM_HEADDIM)),
            "e_ssm_norm": _row(e_ssm_norm[i]),
            "e_wo_a": e_w_out[i][:D_A].astype(BF16), "e_wo_b": e_w_out[i][D_A:].astype(BF16),
            "e_norm_ffn": _row(e_norm_ffn[i]),
            "e_w_gate": e_w_gate[i].astype(BF16), "e_w_up": e_w_up[i].astype(BF16),
            "e_w_down": e_w_down[i].astype(BF16),
        }
        zero_a = jnp.zeros((bp, CONV_A - 1, D_A), F32)
        zero_h = jnp.zeros((bp, SSM_HEADS, SSM_HEADDIM, SSM_STATE), F32)
        zero_b = jnp.zeros((bp, SSM_CONV - 1, CONV_DIM), F32)
        xp, ca, sm, cb = _even_layer(xp, zero_a, zero_h, zero_b, min(SSD_CHUNK, s_len), p, tm=512, tl_a=256)
        outs_p[0].append(ca); outs_p[1].append(sm); outs_p[2].append(cb)
        xs, ca, sm, cb = _even_layer(xs, state_conv_a[i], state_ssm[i], state_conv_b[i], t_len, p,
                                     tm=bd * t_len, tl_a=t_len)
        outs_s[0].append(ca); outs_s[1].append(sm); outs_s[2].append(cb)

        w = o_w_in[i]
        wo = o_w_out[i]
        p = {
            "o_norm_mix": _row(o_norm_mix[i]),
            "o_w_main": jnp.concatenate([w[:, 0:512][:, perm], w[:, 512:1024], w[:, 1092:2628]], axis=1).astype(BF16),
            "o_w_tail": _pad_cols(w[:, 1024:1092], LANES).astype(BF16),
            "o_q_norm": _row(jnp.tile(o_q_norm[i], 2)), "o_k_norm": _row(jnp.tile(o_k_norm[i], 2)),
            "o_kidx_g": _pad_cols(_row(o_kidx_g[i]), LANES), "o_kidx_b": _pad_cols(_row(o_kidx_b[i]), LANES),
            "o_bd": bdiag,
            "o_conv_d_w": o_conv_d_w[i],
            "o_wo_a": wo[:512][perm].astype(BF16), "o_wo_b": wo[512:].astype(BF16),
            "o_norm_ffn": _row(o_norm_ffn[i]),
            "o_br": jnp.concatenate([_row(o_b_router[i]), jnp.full((1, LANES - N_EXPERTS), NEG, F32)], axis=1),
            "o_we_gate": o_we_gate[i].astype(BF16), "o_we_up": o_we_up[i].astype(BF16),
            "o_we_down": o_we_down[i].astype(BF16),
        }
        wr = _pad_cols(o_w_router[i], LANES)
        p["o_wr_hi"] = wr.astype(BF16)
        p["o_wr_lo"] = (wr - p["o_wr_hi"].astype(F32)).astype(BF16)

        mp = bp * s_len
        x2 = xp.reshape(mp, d)
        zmain, q, k, v, qi, kiw, (k16, v16, kk16) = _odd_project(x2, cos_p, sin_p, p, tm=512)
        r3 = lambda a: a.reshape(bp, s_len, a.shape[-1])
        att = dsa_prompt(r3(q), r3(qi), r3(kiw), r3(k16), r3(v16), r3(kk16))
        x2, cd = _odd_tail(x2, att, r3(zmain), jnp.zeros((bp, CONV_D - 1, D_D), F32), p, tm=512, tl_d=512)
        xp = x2.reshape(bp, s_len, d)
        outs_p[3].append(k.reshape(bp, s_len, N_KV, HEAD_DIM)); outs_p[4].append(v.reshape(bp, s_len, N_KV, HEAD_DIM))
        outs_p[5].append(kiw[:, :IDX_DIM].reshape(bp, s_len, IDX_DIM)); outs_p[6].append(cd)

        ms = bd * t_len
        x2 = xs.reshape(ms, d)
        zmain, q, k, v, qi, kiw, _ = _odd_project(x2, cos_s, sin_s, p, tm=ms)
        r3 = lambda a: a.reshape(bd, t_len, a.shape[-1])
        mask_past, mask_new = sdsa_index(page_table, r3(qi), r3(kiw), r3(kiw),
                                         jnp.transpose(cache_kidx[i], (0, 2, 1)))
        att = sdsa_attn(page_table, r3(q), r3(k), r3(v), mask_past, mask_new,
                        jnp.transpose(cache_k[i], (0, 2, 3, 1)), jnp.transpose(cache_v[i], (0, 2, 3, 1)))
        x2, cd = _odd_tail(x2, att, r3(zmain), state_conv_d[i], p, tm=ms, tl_d=t_len)
        xs = x2.reshape(bd, t_len, d)
        outs_s[3].append(k.reshape(bd, t_len, N_KV, HEAD_DIM)); outs_s[4].append(v.reshape(bd, t_len, N_KV, HEAD_DIM))
        outs_s[5].append(kiw[:, :IDX_DIM].reshape(bd, t_len, IDX_DIM)); outs_s[6].append(cd)
    return (xp, xs) + tuple(jnp.stack(o) for o in outs_p) + tuple(jnp.stack(o) for o in outs_s)
```

```python
import functools
import math

import jax
import jax.numpy as jnp
import numpy as np
from jax import lax
from jax.experimental import pallas as pl
from jax.experimental.pallas import tpu as pltpu

F32 = jnp.float32
BF16 = jnp.bfloat16
I32 = jnp.int32

D_MODEL = 1024
D_A = 512
CONV_A = 31
D_INNER = 1024
SSM_HEADDIM = 64
SSM_HEADS = 16
SSM_GROUPS = 2
SSM_STATE = 128
SSM_CONV = 4
CONV_DIM = D_INNER + 2 * SSM_GROUPS * SSM_STATE
SSD_CHUNK = 128
HEAD_DIM = 64
N_HEADS = 8
N_KV = 2
ROT_DIM = 16
ROPE_THETA = 500000.0
IDX_HEADS = 4
IDX_DIM = 64
TOPK_MAX = 256
Q_BLOCK = 128
D_D = 512
CONV_D = 3
D_FF = 2816
N_EXPERTS = 8
PAGE_SIZE = 128
EPS = 1e-6
NEG = -1e30

LANES = 128
SUBLANES = 8
VMEM_LIMIT = 56 * 1024 * 1024

E_MAIN = D_A + D_A + D_INNER + CONV_DIM
E_COL_XBC, E_COL_VAL, E_COL_Z, E_COL_GATE = 0, 1536, 2048, 3072
O_MAIN = 512 + 128 + 128 + 256 + 3 * D_D


def _cparams(sem):
    return pltpu.CompilerParams(dimension_semantics=sem, vmem_limit_bytes=VMEM_LIMIT)


def _bdot(a, b):
    return jnp.dot(a.astype(BF16), b.astype(BF16), preferred_element_type=F32)


def _bdot_nt(a, b):
    return lax.dot_general(a.astype(BF16), b.astype(BF16), (((1,), (1,)), ((), ())),
                           preferred_element_type=F32)


def _split3(a):
    hi = a.astype(BF16)
    r1 = a - hi.astype(F32)
    mid = r1.astype(BF16)
    lo = (r1 - mid.astype(F32)).astype(BF16)
    return hi, mid, lo


def _dot_exact_rhs(a, b_bf16):
    hi, mid, lo = _split3(a)
    out = jnp.dot(lo, b_bf16, preferred_element_type=F32)
    out = out + jnp.dot(mid, b_bf16, preferred_element_type=F32)
    return out + jnp.dot(hi, b_bf16, preferred_element_type=F32)


def _dot_exact_lhs(a_bf16, b):
    hi, mid, lo = _split3(b)
    out = jnp.dot(a_bf16, lo, preferred_element_type=F32)
    out = out + jnp.dot(a_bf16, mid, preferred_element_type=F32)
    return out + jnp.dot(a_bf16, hi, preferred_element_type=F32)


def _sigmoid(x):
    return 1.0 / (1.0 + jnp.exp(-x))


def _silu(x):
    return x * _sigmoid(x)


def _softplus(x):
    return jnp.maximum(x, 0.0) + jnp.log(1.0 + jnp.exp(-jnp.abs(x)))


def _norm_proj_kernel(x_ref, g_ref, w_ref, wt_ref, main_ref, tail_ref, xn_ref):
    j = pl.program_id(1)

    @pl.when(j == 0)
    def _():
        x = x_ref[...]
        y = x * lax.rsqrt(jnp.mean(x * x, axis=-1, keepdims=True) + EPS) * g_ref[...]
        xn = y.astype(BF16)
        xn_ref[...] = xn
        tail_ref[...] = jnp.dot(xn, wt_ref[...], preferred_element_type=F32)

    main_ref[...] = jnp.dot(xn_ref[...], w_ref[...], preferred_element_type=F32)


def norm_proj(x, g, w_main, w_tail, *, tm, tn):
    m, k = x.shape
    n = w_main.shape[1]
    return pl.pallas_call(
        _norm_proj_kernel,
        out_shape=(jax.ShapeDtypeStruct((m, n), F32), jax.ShapeDtypeStruct((m, LANES), F32)),
        grid=(m // tm, n // tn),
        in_specs=[pl.BlockSpec((tm, k), lambda i, j: (i, 0)),
                  pl.BlockSpec((1, k), lambda i, j: (0, 0)),
                  pl.BlockSpec((k, tn), lambda i, j: (0, j)),
                  pl.BlockSpec((k, LANES), lambda i, j: (0, 0))],
        out_specs=(pl.BlockSpec((tm, tn), lambda i, j: (i, j)),
                   pl.BlockSpec((tm, LANES), lambda i, j: (i, 0))),
        scratch_shapes=[pltpu.VMEM((tm, k), BF16)],
        compiler_params=_cparams(("arbitrary", "arbitrary")),
        name="norm_proj",
    )(x, g, w_main, w_tail)


def _dwconv_from_buf(buf_ref, w_ref, kw, carry, tl):
    acc = None
    for k in range(kw):
        start = carry - (kw - 1) + k
        term = buf_ref[pl.ds(start, tl), :] * w_ref[pl.ds(k, 1), :]
        acc = term if acc is None else acc + term
    return acc


CARRY_A = 32


def _conv_a_kernel(val_ref, gate_ref, prev_ref, w_ref, b_ref, lg_ref, lb_ref, ya_ref, new_ref, buf_ref, *, tl):
    l = pl.program_id(1)

    @pl.when(l == 0)
    def _():
        buf_ref[pl.ds(0, CARRY_A), :] = prev_ref[0]

    u = val_ref[0] * _sigmoid(gate_ref[0])
    buf_ref[pl.ds(CARRY_A, tl), :] = u
    y = _dwconv_from_buf(buf_ref, w_ref, CONV_A, CARRY_A, tl) + b_ref[...]
    mu = jnp.mean(y, axis=-1, keepdims=True)
    yc = y - mu
    var = jnp.mean(yc * yc, axis=-1, keepdims=True)
    yn = yc * lax.rsqrt(var + EPS) * lg_ref[...] + lb_ref[...]
    ya_ref[0] = _silu(yn)
    hist = buf_ref[pl.ds(tl, CARRY_A), :]
    new_ref[0] = hist
    buf_ref[pl.ds(0, CARRY_A), :] = hist


def conv_a(zmain, prev32, w, b, lg, lb, *, tl):
    bsz, L, _ = zmain.shape
    kern = functools.partial(_conv_a_kernel, tl=tl)
    return pl.pallas_call(
        kern,
        out_shape=(jax.ShapeDtypeStruct((bsz, L, D_A), F32), jax.ShapeDtypeStruct((bsz, CARRY_A, D_A), F32)),
        grid=(bsz, L // tl),
        in_specs=[pl.BlockSpec((1, tl, D_A), lambda b_, l: (b_, l, E_COL_VAL // D_A)),
                  pl.BlockSpec((1, tl, D_A), lambda b_, l: (b_, l, E_COL_GATE // D_A)),
                  pl.BlockSpec((1, CARRY_A, D_A), lambda b_, l: (b_, 0, 0)),
                  pl.BlockSpec((CONV_A, D_A), lambda b_, l: (0, 0)),
                  pl.BlockSpec((1, D_A), lambda b_, l: (0, 0)),
                  pl.BlockSpec((1, D_A), lambda b_, l: (0, 0)),
                  pl.BlockSpec((1, D_A), lambda b_, l: (0, 0))],
        out_specs=(pl.BlockSpec((1, tl, D_A), lambda b_, l: (b_, l, 0)),
                   pl.BlockSpec((1, CARRY_A, D_A), lambda b_, l: (b_, 0, 0))),
        scratch_shapes=[pltpu.VMEM((CARRY_A + tl, D_A), F32)],
        compiler_params=_cparams(("arbitrary", "arbitrary")),
        name="conv_a",
    )(zmain, zmain, prev32, w, b, lg, lb)


CARRY_B = 8
SSD_LP = 128


def _ssd_kernel(xbc_ref, z_ref, dt_ref, prevb_ref, h0_ref, cw_ref, cb_ref, dtb_ref, a_ref, expand_ref,
                dskip_ref, nrm_ref, y_ref, newb_ref, hout_ref, buf_ref, ht_ref, *, lc):
    c = pl.program_id(1)
    nc = pl.num_programs(1)
    lp = SSD_LP

    @pl.when(c == 0)
    def _():
        buf_ref[pl.ds(0, CARRY_B), :] = prevb_ref[0]
        ht_ref[...] = h0_ref[0].T
        if lc < lp:
            buf_ref[pl.ds(CARRY_B + lc, lp - lc), :] = jnp.zeros((lp - lc, CONV_DIM), F32)

    buf_ref[pl.ds(CARRY_B, lc), :] = xbc_ref[0]
    xbc = _silu(_dwconv_from_buf(buf_ref, cw_ref, SSM_CONV, CARRY_B, lp) + cb_ref[...])
    hist = buf_ref[pl.ds(lc, CARRY_B), :]
    newb_ref[0] = hist
    buf_ref[pl.ds(0, CARRY_B), :] = hist

    xs = xbc[:, :D_INNER]
    if lc < lp:
        dt_raw = jnp.concatenate([dt_ref[0], jnp.zeros((lp - lc, LANES), F32)], axis=0)
        zg = jnp.concatenate([z_ref[0], jnp.zeros((lp - lc, D_INNER), F32)], axis=0)
    else:
        dt_raw = dt_ref[0]
        zg = z_ref[0]
    dt = _softplus(dt_raw + dtb_ref[...])
    if lc < lp:
        row = lax.broadcasted_iota(I32, (lp, LANES), 0)
        dt = jnp.where(row < lc, dt, 0.0)
    lane = lax.broadcasted_iota(I32, (lp, LANES), 1)
    dt = jnp.where(lane < SSM_HEADS, dt, 0.0)
    da = dt * a_ref[...]

    r_i = lax.broadcasted_iota(I32, (lp, lp), 0)
    c_i = lax.broadcasted_iota(I32, (lp, lp), 1)
    causal = r_i >= c_i
    tri = jnp.where(causal, 1.0, 0.0).astype(BF16)
    a_col = _dot_exact_lhs(tri, da)
    a_row = a_col.T
    dt_row = dt.T
    a_last = a_col[lp - 1:lp, :]

    expand = expand_ref[...]
    a_exp = _dot_exact_rhs(a_col, expand)
    dt_exp = _dot_exact_rhs(dt, expand)
    al_exp = _dot_exact_rhs(a_last, expand)

    ht = ht_ref[...]
    y_parts = []
    for g in range(SSM_GROUPS):
        bm = xbc[:, D_INNER + g * SSM_STATE: D_INNER + (g + 1) * SSM_STATE]
        cm = xbc[:, D_INNER + (SSM_GROUPS + g) * SSM_STATE: D_INNER + (SSM_GROUPS + g + 1) * SSM_STATE]
        cb = _bdot_nt(cm, bm)
        gs = slice(g * 512, (g + 1) * 512)
        y_inter = _bdot(cm, ht[:, gs]) * jnp.exp(a_exp[:, gs])
        heads = []
        for r in range(SSM_HEADS // SSM_GROUPS):
            h = g * (SSM_HEADS // SSM_GROUPS) + r
            seg = a_col[:, h:h + 1] - a_row[h:h + 1, :]
            decay = jnp.where(causal, jnp.exp(jnp.where(causal, seg, 0.0)), 0.0)
            wts = cb * decay * dt_row[h:h + 1, :]
            heads.append(_bdot(wts, xs[:, h * SSM_HEADDIM:(h + 1) * SSM_HEADDIM]))
        y_parts.append(jnp.concatenate(heads, axis=-1) + y_inter)
        xw = xs[:, gs] * (jnp.exp(al_exp[:, gs] - a_exp[:, gs]) * dt_exp[:, gs])
        ht_ref[:, gs] = ht[:, gs] * jnp.exp(al_exp[:, gs]) + _bdot(bm.T, xw)
    y = jnp.concatenate(y_parts, axis=-1)
    y = y + dskip_ref[...] * xs
    y = y * _silu(zg)
    outs = []
    for g in range(SSM_GROUPS):
        yg = y[:, g * 512:(g + 1) * 512]
        outs.append(yg * lax.rsqrt(jnp.mean(yg * yg, axis=-1, keepdims=True) + EPS))
    yb = jnp.concatenate(outs, axis=-1) * nrm_ref[...]
    y_ref[0] = yb[:lc, :]

    @pl.when(c == nc - 1)
    def _():
        hout_ref[0] = ht_ref[...].T


def ssd_mixer(zmain, ztail, prev8, h0, cw, cb, dtb, a_neg, expand, dskip_exp, nrm, *, lc):
    bsz, L, _ = zmain.shape
    kern = functools.partial(_ssd_kernel, lc=lc)
    return pl.pallas_call(
        kern,
        out_shape=(jax.ShapeDtypeStruct((bsz, L, D_INNER), F32),
                   jax.ShapeDtypeStruct((bsz, CARRY_B, CONV_DIM), F32),
                   jax.ShapeDtypeStruct((bsz, D_INNER, SSM_STATE), F32)),
        grid=(bsz, L // lc),
        in_specs=[pl.BlockSpec((1, lc, CONV_DIM), lambda b_, c: (b_, c, E_COL_XBC // CONV_DIM)),
                  pl.BlockSpec((1, lc, D_INNER), lambda b_, c: (b_, c, E_COL_Z // D_INNER)),
                  pl.BlockSpec((1, lc, LANES), lambda b_, c: (b_, c, 0)),
                  pl.BlockSpec((1, CARRY_B, CONV_DIM), lambda b_, c: (b_, 0, 0)),
                  pl.BlockSpec((1, D_INNER, SSM_STATE), lambda b_, c: (b_, 0, 0)),
                  pl.BlockSpec((SSM_CONV, CONV_DIM), lambda b_, c: (0, 0)),
                  pl.BlockSpec((1, CONV_DIM), lambda b_, c: (0, 0)),
                  pl.BlockSpec((1, LANES), lambda b_, c: (0, 0)),
                  pl.BlockSpec((1, LANES), lambda b_, c: (0, 0)),
                  pl.BlockSpec((LANES, D_INNER), lambda b_, c: (0, 0)),
                  pl.BlockSpec((1, D_INNER), lambda b_, c: (0, 0)),
                  pl.BlockSpec((1, D_INNER), lambda b_, c: (0, 0))],
        out_specs=(pl.BlockSpec((1, lc, D_INNER), lambda b_, c: (b_, c, 0)),
                   pl.BlockSpec((1, CARRY_B, CONV_DIM), lambda b_, c: (b_, 0, 0)),
                   pl.BlockSpec((1, D_INNER, SSM_STATE), lambda b_, c: (b_, 0, 0))),
        scratch_shapes=[pltpu.VMEM((CARRY_B + SSD_LP, CONV_DIM), F32),
                        pltpu.VMEM((SSM_STATE, D_INNER), F32)],
        compiler_params=_cparams(("arbitrary", "arbitrary")),
        name="ssd_mixer",
    )(zmain, zmain, ztail, prev8, h0, cw, cb, dtb, a_neg, expand, dskip_exp, nrm)


def _out_proj_kernel(x_ref, a_ref, b_ref, wa_ref, wb_ref, o_ref):
    acc = jnp.dot(a_ref[...].astype(BF16), wa_ref[...], preferred_element_type=F32)
    acc = acc + jnp.dot(b_ref[...].astype(BF16), wb_ref[...], preferred_element_type=F32)
    o_ref[...] = x_ref[...] + acc


def out_proj(x, a, b, wa, wb, *, tm):
    m, d = x.shape
    ka, kb = a.shape[1], b.shape[1]
    return pl.pallas_call(
        _out_proj_kernel,
        out_shape=jax.ShapeDtypeStruct((m, d), F32),
        grid=(m // tm,),
        in_specs=[pl.BlockSpec((tm, d), lambda i: (i, 0)),
                  pl.BlockSpec((tm, ka), lambda i: (i, 0)),
                  pl.BlockSpec((tm, kb), lambda i: (i, 0)),
                  pl.BlockSpec((ka, d), lambda i: (0, 0)),
                  pl.BlockSpec((kb, d), lambda i: (0, 0))],
        out_specs=pl.BlockSpec((tm, d), lambda i: (i, 0)),
        compiler_params=_cparams(("arbitrary",)),
        name="out_proj",
    )(x, a, b, wa, wb)


def _swiglu_kernel(x_ref, g_ref, wg_ref, wu_ref, wd_ref, o_ref, xn_ref, acc_ref):
    f = pl.program_id(1)
    nf = pl.num_programs(1)

    @pl.when(f == 0)
    def _():
        x = x_ref[...]
        y = x * lax.rsqrt(jnp.mean(x * x, axis=-1, keepdims=True) + EPS) * g_ref[...]
        xn_ref[...] = y.astype(BF16)
        acc_ref[...] = jnp.zeros_like(acc_ref)

    xn = xn_ref[...]
    gate = jnp.dot(xn, wg_ref[...], preferred_element_type=F32)
    up = jnp.dot(xn, wu_ref[...], preferred_element_type=F32)
    hid = (_silu(gate) * up).astype(BF16)
    acc_ref[...] += jnp.dot(hid, wd_ref[...], preferred_element_type=F32)

    @pl.when(f == nf - 1)
    def _():
        o_ref[...] = x_ref[...] + acc_ref[...]


def swiglu_ffn(x, g, wg, wu, wd, *, tm, tf):
    m, d = x.shape
    ff = wg.shape[1]
    return pl.pallas_call(
        _swiglu_kernel,
        out_shape=jax.ShapeDtypeStruct((m, d), F32),
        grid=(m // tm, ff // tf),
        in_specs=[pl.BlockSpec((tm, d), lambda i, f: (i, 0)),
                  pl.BlockSpec((1, d), lambda i, f: (0, 0)),
                  pl.BlockSpec((d, tf), lambda i, f: (0, f)),
                  pl.BlockSpec((d, tf), lambda i, f: (0, f)),
                  pl.BlockSpec((tf, d), lambda i, f: (f, 0))],
        out_specs=pl.BlockSpec((tm, d), lambda i, f: (i, 0)),
        scratch_shapes=[pltpu.VMEM((tm, d), BF16), pltpu.VMEM((tm, d), F32)],
        compiler_params=_cparams(("arbitrary", "arbitrary")),
        name="swiglu_ffn",
    )(x, g, wg, wu, wd)


O_COL_Q, O_COL_K, O_COL_V, O_COL_QI, O_COL_DIN, O_COL_BG, O_COL_CG = 0, 512, 640, 768, 1024, 1536, 2048


def _rope128(x, cos_t, sin_t, lo_mask):
    up = pltpu.roll(x, LANES - ROT_DIM // 2, 1)
    dn = pltpu.roll(x, ROT_DIM // 2, 1)
    return x * cos_t + jnp.where(lo_mask, up, dn) * sin_t


def _odd_post_kernel(qkvq_ref, tail_ref, cos_ref, sin_ref, qn_ref, kn_ref, kg_ref, kb_ref, bd_ref,
                     q_ref, k_ref, v_ref, qi_ref, kiw_ref, k16_ref, v16_ref, kk16_ref):
    cos_t = cos_ref[...]
    sin_t = sin_ref[...]
    lane = lax.broadcasted_iota(I32, cos_t.shape, 1)
    lo_mask = (lane % HEAD_DIM) < (ROT_DIM // 2)
    bd = bd_ref[...]
    z = qkvq_ref[...]
    for c in range(4):
        x = z[:, c * LANES:(c + 1) * LANES]
        ms = _dot_exact_rhs(x * x, bd)
        xn = x * lax.rsqrt(ms + EPS) * qn_ref[...]
        q_ref[:, c * LANES:(c + 1) * LANES] = _rope128(xn, cos_t, sin_t, lo_mask)
    x = z[:, O_COL_K:O_COL_K + LANES]
    ms = _dot_exact_rhs(x * x, bd)
    xn = x * lax.rsqrt(ms + EPS) * kn_ref[...]
    kr = _rope128(xn, cos_t, sin_t, lo_mask)
    k_ref[...] = kr
    k16_ref[...] = kr.astype(BF16)
    vv = z[:, O_COL_V:O_COL_V + LANES]
    v_ref[...] = vv
    v16_ref[...] = vv.astype(BF16)
    for c in range(2):
        x = z[:, O_COL_QI + c * LANES: O_COL_QI + (c + 1) * LANES]
        qi_ref[:, c * LANES:(c + 1) * LANES] = _rope128(x, cos_t, sin_t, lo_mask)
    t = tail_ref[...]
    lane_t = lax.broadcasted_iota(I32, t.shape, 1)
    is_ki = lane_t < IDX_DIM
    mu = _dot_exact_rhs(t, bd)
    tc = t - mu
    var = _dot_exact_rhs(tc * tc, bd)
    kin = tc * lax.rsqrt(var + EPS) * kg_ref[...] + kb_ref[...]
    kin = _rope128(kin, cos_t, sin_t, lo_mask)
    kiw_ref[...] = jnp.where(is_ki, kin, t * (IDX_HEADS ** -0.5 * IDX_DIM ** -0.5))
    kk16_ref[...] = jnp.where(is_ki, kin, pltpu.roll(kin, IDX_DIM, 1)).astype(BF16)


def odd_post(zmain, ztail, cos_t, sin_t, qn, kn, kg, kb, bd, *, tm):
    m = zmain.shape[0]
    nrope = cos_t.shape[0] // tm
    row = lambda i: (i, 0)
    rope = lambda i: (i % nrope, 0)
    const = lambda i: (0, 0)
    return pl.pallas_call(
        _odd_post_kernel,
        out_shape=(jax.ShapeDtypeStruct((m, 512), F32), jax.ShapeDtypeStruct((m, LANES), F32),
                   jax.ShapeDtypeStruct((m, LANES), F32), jax.ShapeDtypeStruct((m, 256), F32),
                   jax.ShapeDtypeStruct((m, LANES), F32), jax.ShapeDtypeStruct((m, LANES), BF16),
                   jax.ShapeDtypeStruct((m, LANES), BF16), jax.ShapeDtypeStruct((m, LANES), BF16)),
        grid=(m // tm,),
        in_specs=[pl.BlockSpec((tm, 1024), row), pl.BlockSpec((tm, LANES), row),
                  pl.BlockSpec((tm, LANES), rope), pl.BlockSpec((tm, LANES), rope),
                  pl.BlockSpec((1, LANES), const), pl.BlockSpec((1, LANES), const),
                  pl.BlockSpec((1, LANES), const), pl.BlockSpec((1, LANES), const),
                  pl.BlockSpec((LANES, LANES), const)],
        out_specs=(pl.BlockSpec((tm, 512), row), pl.BlockSpec((tm, LANES), row), pl.BlockSpec((tm, LANES), row),
                   pl.BlockSpec((tm, 256), row), pl.BlockSpec((tm, LANES), row), pl.BlockSpec((tm, LANES), row),
                   pl.BlockSpec((tm, LANES), row), pl.BlockSpec((tm, LANES), row)),
        compiler_params=_cparams(("arbitrary",)),
        name="odd_post",
    )(zmain, ztail, cos_t, sin_t, qn, kn, kg, kb, bd)


INT_MIN = -2 ** 31
F32_MIN_NORMAL_BITS = 0x00800000


def _code_to_float(c):
    return pltpu.bitcast(c ^ ((c >> 31) & 0x7FFFFFFF), F32)


def _kth_largest(count_ge, k, shape):
    ans = jnp.full(shape, INT_MIN, I32)
    cand = jnp.zeros(shape, I32)
    ans = jnp.where(count_ge(_code_to_float(cand)) >= k, cand, ans)

    def body(i, ans):
        cand = ans | jnp.left_shift(jnp.int32(1), 30 - i)
        return jnp.where(count_ge(_code_to_float(cand)) >= k, cand, ans)

    ans = lax.fori_loop(0, 31, body, ans)
    ans = jnp.where((ans > 0) & (ans < F32_MIN_NORMAL_BITS), 0, ans)
    return _code_to_float(ans)


def _sum_leading(x):
    n = x.shape[0]
    extra = None
    while n > 1:
        if n % 2:
            extra = x[n - 1] if extra is None else extra + x[n - 1]
            n -= 1
        x = x[:n // 2] + x[n // 2:n]
        n //= 2
    return x[0] if extra is None else x[0] + extra


def _upper_tri_bf16(n):
    r = lax.broadcasted_iota(I32, (n, n), 0)
    c = lax.broadcasted_iota(I32, (n, n), 1)
    return jnp.where(r <= c, 1.0, 0.0).astype(BF16)


def _dsa_prompt_kernel(q_ref, qi_ref, wi_ref, k_ref, v_ref, kk_ref, o_ref, keys_ref, bias_ref, *,
                       q_base, s_eff, topk, qb):
    t0 = q_base + pl.program_id(1) * qb
    lane = lax.broadcasted_iota(I32, (qb, LANES), 1)
    row = lax.broadcasted_iota(I32, (qb, LANES), 0)
    lo = lane < HEAD_DIM
    hi = lane >= HEAD_DIM
    rpg = N_HEADS // N_KV

    kk = kk_ref[0]
    qi = qi_ref[0]
    wi = wi_ref[0]
    sc = None
    for h in range(IDX_HEADS):
        x = qi[:, (h // 2) * LANES:(h // 2 + 1) * LANES]
        x = jnp.where(lo if h % 2 == 0 else hi, x, 0.0)
        d = _bdot_nt(x, kk)
        term = jnp.maximum(d, 0.0) * wi[:, IDX_DIM + h:IDX_DIM + h + 1]
        sc = term if sc is None else sc + term
    kpos = lax.broadcasted_iota(I32, (qb, s_eff), 1)
    qpos = t0 + lax.broadcasted_iota(I32, (qb, s_eff), 0)
    adm_all = kpos <= qpos
    sc = jnp.where(sc == 0.0, 0.0, sc)
    keys_ref[...] = jnp.where(adm_all, sc, NEG)

    def count_ge(cand):
        return jnp.sum(jnp.where(keys_ref[...] >= cand, 1.0, 0.0), axis=-1, keepdims=True)

    thr = _kth_largest(count_ge, float(topk), (qb, 1))
    n_ge = count_ge(thr)
    n_adm = t0 + 1 + lax.broadcasted_iota(I32, (qb, 1), 0)
    simple = jnp.where((n_ge == float(topk)) | (n_adm < topk), 0.0, 1.0)
    no_cut = jnp.max(simple) == 0.0

    @pl.when(no_cut)
    def _():
        bias_ref[...] = jnp.where(adm_all, jnp.where(keys_ref[...] >= thr, 0.0, NEG), NEG)

    @pl.when(jnp.logical_not(no_cut))
    def _():
        n_gt = jnp.sum(jnp.where(keys_ref[...] > thr, 1.0, 0.0), axis=-1, keepdims=True)
        need = float(topk) - n_gt
        ut = _upper_tri_bf16(LANES)
        off = jnp.zeros((qb, 1), F32)
        for j in range(s_eff // LANES):
            kj = keys_ref[:, j * LANES:(j + 1) * LANES]
            eq = kj == thr
            pre = jnp.dot(jnp.where(eq, 1.0, 0.0).astype(BF16), ut, preferred_element_type=F32) + off
            adm = (j * LANES + lane) <= (t0 + row)
            keep = jnp.where(eq, jnp.where(pre <= need, 0.0, NEG), jnp.where(kj > thr, 0.0, NEG))
            bias_ref[:, j * LANES:(j + 1) * LANES] = jnp.where(adm, keep, NEG)
            off = pre[:, LANES - 1:LANES]

    kb = k_ref[0]
    vb = v_ref[0]
    lane_b = lax.broadcasted_iota(I32, (Q_BLOCK, LANES), 1)
    lo_b = lane_b < HEAD_DIM
    hi_b = lane_b >= HEAD_DIM
    lo4 = lax.broadcasted_iota(I32, (rpg * Q_BLOCK, LANES), 1) < HEAD_DIM
    for sub in range(qb // Q_BLOCK):
        rows = pl.ds(sub * Q_BLOCK, Q_BLOCK)
        bias = bias_ref[rows, :]
        q = q_ref[0, rows, :] * (HEAD_DIM ** -0.5)
        outs = []
        for g in range(N_KV):
            keep = lo_b if g == 0 else hi_b
            qg = jnp.concatenate([jnp.where(keep, q[:, r * LANES:(r + 1) * LANES], 0.0) for r in range(rpg)],
                                 axis=0)
            s = _bdot_nt(qg, kb).reshape(rpg, Q_BLOCK, s_eff) + bias[None]
            m = jnp.max(s, axis=-1, keepdims=True)
            e = jnp.exp(s - m)
            l = jnp.sum(e, axis=-1, keepdims=True).reshape(rpg * Q_BLOCK, 1)
            pv = jnp.dot(e.reshape(rpg * Q_BLOCK, s_eff).astype(BF16), vb, preferred_element_type=F32)
            outs.append(pv / l)
        oc = jnp.where(lo4, outs[0], outs[1])
        for r in range(rpg):
            o_ref[0, rows, r * LANES:(r + 1) * LANES] = oc[r * Q_BLOCK:(r + 1) * Q_BLOCK, :]


DSA_SEGMENT = 256


def dsa_prompt(q, qi, wi, k16, v16, kk16):
    bsz, s_len, _ = q.shape
    topk = min(TOPK_MAX, s_len // 4)
    seg = min(DSA_SEGMENT, s_len)
    outs = []
    for si in range(s_len // seg):
        s_eff = (si + 1) * seg
        kern = functools.partial(_dsa_prompt_kernel, q_base=si * seg, s_eff=s_eff, topk=topk, qb=seg)
        qblk = lambda w, si=si: pl.BlockSpec((1, seg, w), lambda b_, i: (b_, si, 0))
        keys = pl.BlockSpec((1, s_eff, LANES), lambda b_, i: (b_, 0, 0))
        outs.append(pl.pallas_call(
            kern,
            out_shape=jax.ShapeDtypeStruct((bsz, seg, 512), F32),
            grid=(bsz, 1),
            in_specs=[qblk(512), qblk(256), qblk(LANES), keys, keys, keys],
            out_specs=pl.BlockSpec((1, seg, 512), lambda b_, i: (b_, 0, 0)),
            scratch_shapes=[pltpu.VMEM((seg, s_eff), F32), pltpu.VMEM((seg, s_eff), F32)],
            compiler_params=_cparams(("arbitrary", "arbitrary")),
            name=f"dsa_prompt_{si}",
        )(q, qi, wi, k16, v16, kk16))
    return jnp.concatenate(outs, axis=1)


CARRY_D = 8


def _conv_d_kernel(din_ref, bg_ref, cg_ref, prev_ref, w_ref, y_ref, new_ref, buf_ref, *, tl):
    l = pl.program_id(1)

    @pl.when(l == 0)
    def _():
        buf_ref[pl.ds(0, CARRY_D), :] = prev_ref[0]

    buf_ref[pl.ds(CARRY_D, tl), :] = cg_ref[0] * din_ref[0]
    y_ref[0] = bg_ref[0] * _dwconv_from_buf(buf_ref, w_ref, CONV_D, CARRY_D, tl)
    hist = buf_ref[pl.ds(tl, CARRY_D), :]
    new_ref[0] = hist
    buf_ref[pl.ds(0, CARRY_D), :] = hist


def conv_d(zmain, prev8, w, *, tl):
    bsz, L, _ = zmain.shape
    kern = functools.partial(_conv_d_kernel, tl=tl)
    col = lambda c: pl.BlockSpec((1, tl, D_D), lambda b_, l: (b_, l, c // D_D))
    return pl.pallas_call(
        kern,
        out_shape=(jax.ShapeDtypeStruct((bsz, L, D_D), F32), jax.ShapeDtypeStruct((bsz, CARRY_D, D_D), F32)),
        grid=(bsz, L // tl),
        in_specs=[col(O_COL_DIN), col(O_COL_BG), col(O_COL_CG),
                  pl.BlockSpec((1, CARRY_D, D_D), lambda b_, l: (b_, 0, 0)),
                  pl.BlockSpec((CONV_D, D_D), lambda b_, l: (0, 0))],
        out_specs=(pl.BlockSpec((1, tl, D_D), lambda b_, l: (b_, l, 0)),
                   pl.BlockSpec((1, CARRY_D, D_D), lambda b_, l: (b_, 0, 0))),
        scratch_shapes=[pltpu.VMEM((CARRY_D + tl, D_D), F32)],
        compiler_params=_cparams(("arbitrary", "arbitrary")),
        name="conv_d",
    )(zmain, zmain, zmain, prev8, w)


def _router_kernel(x_ref, g_ref, whi_ref, wlo_ref, br_ref, xn_ref, comb_ref):
    x = x_ref[...]
    y = x * lax.rsqrt(jnp.mean(x * x, axis=-1, keepdims=True) + EPS) * g_ref[...]
    yhi = y.astype(BF16)
    xn_ref[...] = yhi
    ylo = (y - yhi.astype(F32)).astype(BF16)
    logits = jnp.dot(ylo, whi_ref[...], preferred_element_type=F32)
    logits = logits + jnp.dot(yhi, wlo_ref[...], preferred_element_type=F32)
    logits = logits + jnp.dot(yhi, whi_ref[...], preferred_element_type=F32) + br_ref[...]
    lane = lax.broadcasted_iota(I32, logits.shape, 1).astype(F32)
    m1 = jnp.max(logits, axis=-1, keepdims=True)
    i1 = jnp.min(jnp.where(logits == m1, lane, float(LANES)), axis=-1, keepdims=True)
    rest = jnp.where(lane == i1, -3e38, logits)
    m2 = jnp.max(rest, axis=-1, keepdims=True)
    i2 = jnp.min(jnp.where(rest == m2, lane, float(LANES)), axis=-1, keepdims=True)
    e2 = jnp.exp(m2 - m1)
    g1 = 1.0 / (1.0 + e2)
    g2 = e2 / (1.0 + e2)
    comb_ref[...] = jnp.where(lane == i1, g1, 0.0) + jnp.where(lane == i2, g2, 0.0)


def moe_router(x, g, w_hi, w_lo, b_pad, *, tm):
    m, d = x.shape
    return pl.pallas_call(
        _router_kernel,
        out_shape=(jax.ShapeDtypeStruct((m, d), BF16), jax.ShapeDtypeStruct((m, LANES), F32)),
        grid=(m // tm,),
        in_specs=[pl.BlockSpec((tm, d), lambda i: (i, 0)),
                  pl.BlockSpec((1, d), lambda i: (0, 0)),
                  pl.BlockSpec((d, LANES), lambda i: (0, 0)),
                  pl.BlockSpec((d, LANES), lambda i: (0, 0)),
                  pl.BlockSpec((1, LANES), lambda i: (0, 0))],
        out_specs=(pl.BlockSpec((tm, d), lambda i: (i, 0)), pl.BlockSpec((tm, LANES), lambda i: (i, 0))),
        compiler_params=_cparams(("arbitrary",)),
        name="moe_router",
    )(x, g, w_hi, w_lo, b_pad)


def _moe_dense_kernel(x_ref, xn_ref, comb_ref, wg_ref, wu_ref, wd_ref, o_ref, acc_ref, acce_ref):
    e = pl.program_id(1)
    f = pl.program_id(2)
    ne = pl.num_programs(1)
    nf = pl.num_programs(2)

    @pl.when((e == 0) & (f == 0))
    def _():
        acc_ref[...] = jnp.zeros_like(acc_ref)

    @pl.when(f == 0)
    def _():
        acce_ref[...] = jnp.zeros_like(acce_ref)

    xn = xn_ref[...]
    gate = jnp.dot(xn, wg_ref[0], preferred_element_type=F32)
    up = jnp.dot(xn, wu_ref[0], preferred_element_type=F32)
    hid = (_silu(gate) * up).astype(BF16)
    acce_ref[...] += jnp.dot(hid, wd_ref[0], preferred_element_type=F32)

    @pl.when(f == nf - 1)
    def _():
        comb = comb_ref[...]
        lane = lax.broadcasted_iota(I32, comb.shape, 1)
        c = jnp.sum(jnp.where(lane == e, comb, 0.0), axis=-1, keepdims=True)
        acc_ref[...] += c * acce_ref[...]

    @pl.when((e == ne - 1) & (f == nf - 1))
    def _():
        o_ref[...] = x_ref[...] + acc_ref[...]


def moe_dense(x, xn, comb, wg, wu, wd, *, tm, tf):
    m, d = x.shape
    ne, _, ff = wg.shape
    return pl.pallas_call(
        _moe_dense_kernel,
        out_shape=jax.ShapeDtypeStruct((m, d), F32),
        grid=(m // tm, ne, ff // tf),
        in_specs=[pl.BlockSpec((tm, d), lambda i, e, f: (i, 0)),
                  pl.BlockSpec((tm, d), lambda i, e, f: (i, 0)),
                  pl.BlockSpec((tm, LANES), lambda i, e, f: (i, 0)),
                  pl.BlockSpec((1, d, tf), lambda i, e, f: (e, 0, f)),
                  pl.BlockSpec((1, d, tf), lambda i, e, f: (e, 0, f)),
                  pl.BlockSpec((1, tf, d), lambda i, e, f: (e, f, 0))],
        out_specs=pl.BlockSpec((tm, d), lambda i, e, f: (i, 0)),
        scratch_shapes=[pltpu.VMEM((tm, d), F32), pltpu.VMEM((tm, d), F32)],
        compiler_params=_cparams(("arbitrary", "arbitrary", "arbitrary")),
        name="moe_dense",
    )(x, xn, comb, wg, wu, wd)


MOE_TILE = 512
DMA_ISSUE_UNROLL = 8
FF_TILE = D_FF // 2
R_I1, R_I2, R_G1, R_G2, R_RANK1, R_RANK2 = 0, 1, 2, 3, 4, 5


def _router_rank_kernel(x_ref, g_ref, whi_ref, wlo_ref, br_ref, info_ref, cnt_ref, run_ref):
    i = pl.program_id(0)

    @pl.when(i == 0)
    def _():
        run_ref[...] = jnp.zeros_like(run_ref)

    x = x_ref[...]
    tm = x.shape[0]
    y = x * lax.rsqrt(jnp.mean(x * x, axis=-1, keepdims=True) + EPS) * g_ref[...]
    yhi = y.astype(BF16)
    ylo = (y - yhi.astype(F32)).astype(BF16)
    logits = jnp.dot(ylo, whi_ref[...], preferred_element_type=F32)
    logits = logits + jnp.dot(yhi, wlo_ref[...], preferred_element_type=F32)
    logits = logits + jnp.dot(yhi, whi_ref[...], preferred_element_type=F32) + br_ref[...]
    lane = lax.broadcasted_iota(I32, logits.shape, 1).astype(F32)
    m1 = jnp.max(logits, axis=-1, keepdims=True)
    i1 = jnp.min(jnp.where(logits == m1, lane, float(LANES)), axis=-1, keepdims=True)
    rest = jnp.where(lane == i1, -3e38, logits)
    m2 = jnp.max(rest, axis=-1, keepdims=True)
    i2 = jnp.min(jnp.where(rest == m2, lane, float(LANES)), axis=-1, keepdims=True)
    e2 = jnp.exp(m2 - m1)
    g1 = 1.0 / (1.0 + e2)
    g2 = e2 / (1.0 + e2)
    oh1 = jnp.where(lane == i1, 1.0, 0.0)
    oh2 = jnp.where(lane == i2, 1.0, 0.0)
    both = oh1 + oh2
    r_i = lax.broadcasted_iota(I32, (tm, tm), 0)
    c_i = lax.broadcasted_iota(I32, (tm, tm), 1)
    strict_lower = jnp.where(c_i < r_i, 1.0, 0.0).astype(BF16)
    before = jnp.dot(strict_lower, both.astype(BF16), preferred_element_type=F32) + run_ref[...]
    rank1 = jnp.sum(before * oh1, axis=-1, keepdims=True)
    rank2 = jnp.sum(before * oh2, axis=-1, keepdims=True)
    run_ref[...] += jnp.sum(both, axis=0, keepdims=True)
    info = jnp.where(lane == R_I1, i1, 0.0)
    for col, val in ((R_I2, i2), (R_G1, g1), (R_G2, g2), (R_RANK1, rank1), (R_RANK2, rank2)):
        info = jnp.where(lane == col, val, info)
    info_ref[...] = info
    cnt_ref[...] = run_ref[...]


def moe_router_rank(x, g, w_hi, w_lo, b_pad, *, tm):
    m, d = x.shape
    return pl.pallas_call(
        _router_rank_kernel,
        out_shape=(jax.ShapeDtypeStruct((m, LANES), F32), jax.ShapeDtypeStruct((1, LANES), F32)),
        grid=(m // tm,),
        in_specs=[pl.BlockSpec((tm, d), lambda i: (i, 0)),
                  pl.BlockSpec((1, d), lambda i: (0, 0)),
                  pl.BlockSpec((d, LANES), lambda i: (0, 0)),
                  pl.BlockSpec((d, LANES), lambda i: (0, 0)),
                  pl.BlockSpec((1, LANES), lambda i: (0, 0))],
        out_specs=(pl.BlockSpec((tm, LANES), lambda i: (i, 0)), pl.BlockSpec((1, LANES), lambda i: (0, 0))),
        scratch_shapes=[pltpu.VMEM((1, LANES), F32)],
        compiler_params=_cparams(("arbitrary",)),
        name="moe_router_rank",
    )(x, g, w_hi, w_lo, b_pad)


def _row_copy(src_ref, si, dst_ref, di, sem):
    return pltpu.make_async_copy(src_ref.at[pl.ds(si, 1)], dst_ref.at[pl.ds(di, 1)], sem)


def _moe_dispatch_kernel(d1_ref, d2_ref, x_ref, xs_ref, sem, *, ct):
    def issue(t, carry):
        _row_copy(x_ref, t, xs_ref, d1_ref[0, 0, t], sem).start(priority=0)
        _row_copy(x_ref, t, xs_ref, d2_ref[0, 0, t], sem).start(priority=1)
        return carry

    lax.fori_loop(0, ct, issue, 0, unroll=DMA_ISSUE_UNROLL)
    tile_copy = pltpu.make_async_copy(x_ref, xs_ref.at[pl.ds(0, ct)], sem)
    tile_copy.wait()
    tile_copy.wait()


def moe_dispatch(x, dest1, dest2, *, ct):
    m, d = x.shape
    smem_row = pl.BlockSpec((1, 1, ct), lambda i: (i, 0, 0), memory_space=pltpu.SMEM)
    return pl.pallas_call(
        functools.partial(_moe_dispatch_kernel, ct=ct),
        out_shape=jax.ShapeDtypeStruct((2 * m, d), F32),
        grid=(m // ct,),
        in_specs=[smem_row, smem_row, pl.BlockSpec((ct, d), lambda i: (i, 0))],
        out_specs=pl.BlockSpec(memory_space=pl.ANY),
        scratch_shapes=[pltpu.SemaphoreType.DMA],
        compiler_params=_cparams(("arbitrary",)),
        name="moe_dispatch",
    )(dest1, dest2, x)


def _moe_grouped_kernel(tile_ref, exp_ref, first_ref, valid_ref, gs_ref, xs_ref, g_ref, wg_ref, wu_ref, wd_ref,
                        ys_ref, xn_ref, acc_ref):
    w = pl.program_id(0)
    f = pl.program_id(1)
    nf = pl.num_programs(1)

    @pl.when(valid_ref[w] == 1)
    def _():
        @pl.when(f == 0)
        def _():
            x = xs_ref[...]
            y = x * lax.rsqrt(jnp.mean(x * x, axis=-1, keepdims=True) + EPS) * g_ref[...]
            xn_ref[...] = y.astype(BF16)
            acc_ref[...] = jnp.zeros_like(acc_ref)

        xn = xn_ref[...]
        gate = jnp.dot(xn, wg_ref[0], preferred_element_type=F32)
        up = jnp.dot(xn, wu_ref[0], preferred_element_type=F32)
        hid = (_silu(gate) * up).astype(BF16)
        acc_ref[...] += jnp.dot(hid, wd_ref[0], preferred_element_type=F32)

        @pl.when(f == nf - 1)
        def _():
            e = exp_ref[w]
            row = tile_ref[w] * MOE_TILE + lax.broadcasted_iota(I32, acc_ref.shape, 0)
            mine = (row >= gs_ref[e]) & (row < gs_ref[e + 1])
            part = jnp.where(mine, acc_ref[...], 0.0)

            @pl.when(first_ref[w] == 1)
            def _():
                ys_ref[...] = part

            @pl.when(first_ref[w] == 0)
            def _():
                ys_ref[...] += part


def moe_grouped(item_tile, item_expert, item_first, item_valid, group_starts, xs, g, wg, wu, wd, *, tf):
    r, d = xs.shape
    ff = wg.shape[2]
    n_items = item_tile.shape[0]
    return pl.pallas_call(
        _moe_grouped_kernel,
        out_shape=jax.ShapeDtypeStruct((r, d), F32),
        grid_spec=pltpu.PrefetchScalarGridSpec(
            num_scalar_prefetch=5,
            grid=(n_items, ff // tf),
            in_specs=[pl.BlockSpec((MOE_TILE, d), lambda w, f, it, ie, i1, iv, gs: (it[w], 0)),
                      pl.BlockSpec((1, d), lambda w, f, it, ie, i1, iv, gs: (0, 0)),
                      pl.BlockSpec((1, d, tf), lambda w, f, it, ie, i1, iv, gs: (ie[w], 0, f)),
                      pl.BlockSpec((1, d, tf), lambda w, f, it, ie, i1, iv, gs: (ie[w], 0, f)),
                      pl.BlockSpec((1, tf, d), lambda w, f, it, ie, i1, iv, gs: (ie[w], f, 0))],
            out_specs=pl.BlockSpec((MOE_TILE, d), lambda w, f, it, ie, i1, iv, gs: (it[w], 0)),
            scratch_shapes=[pltpu.VMEM((MOE_TILE, d), BF16), pltpu.VMEM((MOE_TILE, d), F32)]),
        compiler_params=_cparams(("arbitrary", "arbitrary")),
        name="moe_grouped",
    )(item_tile, item_expert, item_first, item_valid, group_starts, xs, g, wg, wu, wd)


def _moe_combine_kernel(d1_ref, d2_ref, x_ref, info_ref, ys_ref, o_ref, y1_ref, y2_ref, sem, *, ct):
    def issue(t, carry):
        _row_copy(ys_ref, d1_ref[0, 0, t], y1_ref, t, sem).start(priority=0)
        _row_copy(ys_ref, d2_ref[0, 0, t], y2_ref, t, sem).start(priority=1)
        return carry

    lax.fori_loop(0, ct, issue, 0, unroll=DMA_ISSUE_UNROLL)
    pltpu.make_async_copy(ys_ref.at[pl.ds(0, ct)], y1_ref, sem).wait()
    pltpu.make_async_copy(ys_ref.at[pl.ds(0, ct)], y2_ref, sem).wait()
    info = info_ref[...]
    g1 = info[:, R_G1:R_G1 + 1]
    g2 = info[:, R_G2:R_G2 + 1]
    o_ref[...] = x_ref[...] + (g1 * y1_ref[...] + g2 * y2_ref[...])


def moe_combine(x, info, ys, dest1, dest2, *, ct):
    m, d = x.shape
    smem_row = pl.BlockSpec((1, 1, ct), lambda i: (i, 0, 0), memory_space=pltpu.SMEM)
    return pl.pallas_call(
        functools.partial(_moe_combine_kernel, ct=ct),
        out_shape=jax.ShapeDtypeStruct((m, d), F32),
        grid=(m // ct,),
        in_specs=[smem_row, smem_row, pl.BlockSpec((ct, d), lambda i: (i, 0)),
                  pl.BlockSpec((ct, LANES), lambda i: (i, 0)), pl.BlockSpec(memory_space=pl.ANY)],
        out_specs=pl.BlockSpec((ct, d), lambda i: (i, 0)),
        scratch_shapes=[pltpu.VMEM((ct, d), F32), pltpu.VMEM((ct, d), F32), pltpu.SemaphoreType.DMA],
        compiler_params=_cparams(("arbitrary",)),
        name="moe_combine",
    )(dest1, dest2, x, info, ys)


def _moe_work_items(counts, n_tiles):
    n_exp = counts.shape[0]
    n_items = n_tiles + n_exp - 1
    ends = jnp.cumsum(counts)
    starts = ends - counts
    expert_of_row = lambda rr: jnp.minimum(jnp.sum(ends[None, :] <= rr[:, None], axis=1), n_exp - 1).astype(I32)
    t = jnp.arange(n_tiles, dtype=I32)
    e_lo = expert_of_row(t * MOE_TILE)
    e_hi = expert_of_row(t * MOE_TILE + (MOE_TILE - 1))
    per_tile = e_hi - e_lo + 1
    item_end = jnp.cumsum(per_tile)
    item_start = item_end - per_tile
    total = item_end[-1]
    w = jnp.minimum(jnp.arange(n_items, dtype=I32), total - 1)
    tile = jnp.sum(item_end[None, :] <= w[:, None], axis=1).astype(I32)
    expert = (e_lo[tile] + (w - item_start[tile])).astype(I32)
    first = (w == item_start[tile]).astype(I32)
    valid = (jnp.arange(n_items, dtype=I32) < total).astype(I32)
    group_starts = jnp.concatenate([starts, ends[-1:]]).astype(I32)
    return tile, expert, first, valid, group_starts


def moe_routed(x, g, w_hi, w_lo, b_pad, wg, wu, wd, *, tm):
    m, d = x.shape
    n_exp = wg.shape[0]
    assert (2 * m) % MOE_TILE == 0
    info, cnt = moe_router_rank(x, g, w_hi, w_lo, b_pad, tm=tm)
    counts = cnt[0, :n_exp].astype(I32)
    starts = jnp.cumsum(counts) - counts
    i1 = info[:, R_I1].astype(I32)
    i2 = info[:, R_I2].astype(I32)
    dest1 = (starts[i1] + info[:, R_RANK1].astype(I32)).reshape(m // tm, 1, tm)
    dest2 = (starts[i2] + info[:, R_RANK2].astype(I32)).reshape(m // tm, 1, tm)
    xs = moe_dispatch(x, dest1, dest2, ct=tm)
    items = _moe_work_items(counts, (2 * m) // MOE_TILE)
    ys = moe_grouped(*items, xs, g, wg, wu, wd, tf=FF_TILE)
    return moe_combine(x, info, ys, dest1, dest2, ct=tm)


PAGES_PER_STEP = 32
PAGE_SLOTS = 2


def _fetch_pages(pt_ref, cache_ref, buf_ref, sem_ref, step, nj):
    b = step // nj
    j0 = (step % nj) * PAGES_PER_STEP
    slot = step % PAGE_SLOTS
    for i in range(PAGES_PER_STEP):
        pltpu.make_async_copy(cache_ref.at[pt_ref[b, j0 + i]], buf_ref.at[slot, i], sem_ref.at[slot]).start()


def _wait_pages(cache_ref, buf_ref, sem_ref, step):
    slot = step % PAGE_SLOTS
    pltpu.make_async_copy(cache_ref.at[pl.ds(0, PAGES_PER_STEP)], buf_ref.at[slot], sem_ref.at[slot]).wait()


def _page_pipeline_step(pt_ref, streams, nj):
    step = pl.program_id(0) * nj + pl.program_id(1)
    total = pl.num_programs(0) * nj

    @pl.when(step == 0)
    def _():
        for cache_ref, buf_ref, sem_ref in streams:
            _fetch_pages(pt_ref, cache_ref, buf_ref, sem_ref, step, nj)

    @pl.when(step + 1 < total)
    def _():
        for cache_ref, buf_ref, sem_ref in streams:
            _fetch_pages(pt_ref, cache_ref, buf_ref, sem_ref, step + 1, nj)

    for cache_ref, buf_ref, sem_ref in streams:
        _wait_pages(cache_ref, buf_ref, sem_ref, step)
    return step % PAGE_SLOTS


def _sdsa_index_kernel(pt_ref, qi_ref, wi_ref, kknew_ref, cache_ref, mpast_ref, mnew_ref,
                       keys_ref, pre_ref, pbuf_ref, psem_ref, *, n_pages, t_len, topk):
    j = pl.program_id(1)
    nj = pl.num_programs(1)
    slot = _page_pipeline_step(pt_ref, [(cache_ref, pbuf_ref, psem_ref)], n_pages // PAGES_PER_STEP)
    lane = lax.broadcasted_iota(I32, (t_len, LANES), 1)
    row = lax.broadcasted_iota(I32, (t_len, LANES), 0)
    lo = lane < HEAD_DIM

    qi = qi_ref[0]
    wi = wi_ref[0]
    qs, ws = [], []
    for h in range(IDX_HEADS):
        x = qi[:, (h // 2) * LANES:(h // 2 + 1) * LANES]
        if h % 2 == 1:
            x = pltpu.roll(x, HEAD_DIM, 1)
        qs.append(x[:, :IDX_DIM])
        ws.append(wi[:, IDX_DIM + h:IDX_DIM + h + 1])
    q32 = jnp.concatenate(qs, axis=0).astype(BF16)
    w32 = jnp.concatenate(ws, axis=0)

    def scores(dots):
        d = jnp.maximum(dots, 0.0) * w32
        s = d[0:t_len]
        for h in range(1, IDX_HEADS):
            s = s + d[h * t_len:(h + 1) * t_len]
        return jnp.where(s == 0.0, 0.0, s)

    pages_t = jnp.concatenate([pbuf_ref[slot, i].astype(BF16) for i in range(PAGES_PER_STEP)], axis=1)
    s_wide = scores(jnp.dot(q32, pages_t, preferred_element_type=F32))
    for i in range(PAGES_PER_STEP):
        keys_ref[j * PAGES_PER_STEP + i] = s_wide[:, i * PAGE_SIZE:(i + 1) * PAGE_SIZE]

    @pl.when(j == nj - 1)
    def _():
        knew = jnp.concatenate([kknew_ref[0][:, :IDX_DIM], jnp.zeros((LANES - t_len, IDX_DIM), F32)], axis=0)
        s_new = jnp.where((lane <= row) & (lane < t_len), scores(_bdot_nt(q32, knew)), NEG)
        keys_ref[n_pages] = s_new

        def count_ge(cand):
            part = _sum_leading(jnp.where(keys_ref[...] >= cand, 1.0, 0.0))
            return jnp.sum(part, axis=-1, keepdims=True)

        thr = _kth_largest(count_ge, float(topk), (t_len, 1))
        n_ge = count_ge(thr)
        no_cut = jnp.max(jnp.abs(n_ge - float(topk))) == 0.0

        @pl.when(no_cut)
        def _():
            bias = jnp.where(keys_ref[...] >= thr, 0.0, NEG)
            mpast_ref[0] = bias[:n_pages]
            mnew_ref[0] = bias[n_pages]

        @pl.when(jnp.logical_not(no_cut))
        def _():
            keys = keys_ref[...]
            n_gt = jnp.sum(_sum_leading(jnp.where(keys > thr, 1.0, 0.0)), axis=-1, keepdims=True)
            need = float(topk) - n_gt
            eqf = jnp.where(keys == thr, 1.0, 0.0).astype(BF16).reshape((n_pages + 1) * t_len, LANES)
            pre_ref[...] = jnp.dot(eqf, _upper_tri_bf16(LANES), preferred_element_type=F32).reshape(
                n_pages + 1, t_len, LANES)

            def body(p, off):
                kp = keys_ref[p]
                pre = pre_ref[p]
                bias = jnp.where(kp == thr, jnp.where((pre + off) <= need, 0.0, NEG),
                                 jnp.where(kp > thr, 0.0, NEG))

                @pl.when(p < n_pages)
                def _():
                    mpast_ref[0, p] = bias

                @pl.when(p == n_pages)
                def _():
                    mnew_ref[0] = bias

                return off + pre[:, LANES - 1:LANES]

            lax.fori_loop(0, n_pages + 1, body, jnp.zeros((t_len, 1), F32))


def sdsa_index(page_table, qi, wi, kk_new, cache_kidx):
    bsz, t_len, _ = qi.shape
    n_pages = page_table.shape[1]
    topk = min(TOPK_MAX, (n_pages * PAGE_SIZE + t_len) // 4)
    kern = functools.partial(_sdsa_index_kernel, n_pages=n_pages, t_len=t_len, topk=topk)
    tok = lambda w: pl.BlockSpec((1, t_len, w), lambda b_, j, pt: (b_, 0, 0))
    return pl.pallas_call(
        kern,
        out_shape=(jax.ShapeDtypeStruct((bsz, n_pages, t_len, LANES), F32),
                   jax.ShapeDtypeStruct((bsz, t_len, LANES), F32)),
        grid_spec=pltpu.PrefetchScalarGridSpec(
            num_scalar_prefetch=1,
            grid=(bsz, n_pages // PAGES_PER_STEP),
            in_specs=[tok(256), tok(LANES), tok(LANES), pl.BlockSpec(memory_space=pl.ANY)],
            out_specs=(pl.BlockSpec((1, n_pages, t_len, LANES), lambda b_, j, pt: (b_, 0, 0, 0)),
                       pl.BlockSpec((1, t_len, LANES), lambda b_, j, pt: (b_, 0, 0))),
            scratch_shapes=[pltpu.VMEM((n_pages + 1, t_len, LANES), F32),
                            pltpu.VMEM((n_pages + 1, t_len, LANES), F32),
                            pltpu.VMEM((PAGE_SLOTS, PAGES_PER_STEP, IDX_DIM, PAGE_SIZE), F32),
                            pltpu.SemaphoreType.DMA((PAGE_SLOTS,))]),
        compiler_params=_cparams(("arbitrary", "arbitrary")),
        name="sdsa_index",
    )(page_table, qi, wi, kk_new, cache_kidx)


def _sdsa_attn_kernel(pt_ref, q_ref, knew_ref, vnew_ref, mpast_ref, mnew_ref, ck_ref, cv_ref,
                      o_ref, qg_ref, m_ref, l_ref, acc_ref, kbuf_ref, vbuf_ref, ksem_ref, vsem_ref, *, t_len, n_pages):
    j = pl.program_id(1)
    nj = pl.num_programs(1)
    slot = _page_pipeline_step(pt_ref, [(ck_ref, kbuf_ref, ksem_ref), (cv_ref, vbuf_ref, vsem_ref)],
                               n_pages // PAGES_PER_STEP)
    rpg = N_HEADS // N_KV
    reps = rpg

    def group_lanes(x, g):
        if g == 1:
            x = pltpu.roll(x, HEAD_DIM, 1)
        return x[:, :HEAD_DIM]

    @pl.when(j == 0)
    def _():
        q = q_ref[0]
        for g in range(N_KV):
            parts = [group_lanes(q[:, r * LANES:(r + 1) * LANES], g) for r in range(rpg)]
            qg_ref[g] = (jnp.concatenate(parts, axis=0) * (HEAD_DIM ** -0.5)).astype(BF16)
        m_ref[...] = jnp.full(m_ref.shape, NEG, F32)
        l_ref[...] = jnp.zeros(l_ref.shape, F32)
        acc_ref[...] = jnp.zeros(acc_ref.shape, F32)

    def update(g, s, pv):
        m_old = m_ref[g]
        m_cur = jnp.maximum(m_old, jnp.max(s, axis=-1, keepdims=True))
        alpha = jnp.exp(m_old - m_cur)
        p = jnp.exp(s - m_cur)
        m_ref[g] = m_cur
        l_ref[g] = l_ref[g] * alpha + jnp.sum(p, axis=-1, keepdims=True)
        acc_ref[g] = acc_ref[g] * alpha + pv(p.astype(BF16))

    bias8 = jnp.concatenate([mpast_ref[0, i] for i in range(PAGES_PER_STEP)], axis=1)
    bias_wide = jnp.concatenate([bias8] * reps, axis=0)
    for g in range(N_KV):
        k_wide = jnp.concatenate([kbuf_ref[slot, i, g].astype(BF16) for i in range(PAGES_PER_STEP)], axis=1)
        v_wide = jnp.concatenate([vbuf_ref[slot, i, g].astype(BF16) for i in range(PAGES_PER_STEP)], axis=1)
        s = jnp.dot(qg_ref[g], k_wide, preferred_element_type=F32) + bias_wide
        update(g, s, lambda p, v_wide=v_wide: _bdot_nt(p, v_wide))

    @pl.when(j == nj - 1)
    def _():
        pad = jnp.zeros((LANES - t_len, HEAD_DIM), F32)
        bias = jnp.concatenate([mnew_ref[0]] * reps, axis=0)
        for g in range(N_KV):
            knew = jnp.concatenate([group_lanes(knew_ref[0], g), pad], axis=0)
            vnew = jnp.concatenate([group_lanes(vnew_ref[0], g), pad], axis=0)
            update(g, _bdot_nt(qg_ref[g], knew) + bias, lambda p, vnew=vnew: _bdot(p, vnew))
        o0 = acc_ref[0] / l_ref[0]
        o1 = acc_ref[1] / l_ref[1]
        for r in range(rpg):
            o_ref[0, :, r * LANES:(r + 1) * LANES] = jnp.concatenate(
                [o0[r * t_len:(r + 1) * t_len], o1[r * t_len:(r + 1) * t_len]], axis=-1)


def sdsa_attn(page_table, q, k_new, v_new, mask_past, mask_new, cache_k, cache_v):
    bsz, t_len, _ = q.shape
    n_pages = page_table.shape[1]
    kern = functools.partial(_sdsa_attn_kernel, t_len=t_len, n_pages=n_pages)
    tok = lambda w: pl.BlockSpec((1, t_len, w), lambda b_, j, pt: (b_, 0, 0))
    rows = (N_HEADS // N_KV) * t_len
    page_buf = pltpu.VMEM((PAGE_SLOTS, PAGES_PER_STEP, N_KV, HEAD_DIM, PAGE_SIZE), F32)
    return pl.pallas_call(
        kern,
        out_shape=jax.ShapeDtypeStruct((bsz, t_len, 512), F32),
        grid_spec=pltpu.PrefetchScalarGridSpec(
            num_scalar_prefetch=1,
            grid=(bsz, n_pages // PAGES_PER_STEP),
            in_specs=[tok(512), tok(LANES), tok(LANES),
                      pl.BlockSpec((1, PAGES_PER_STEP, t_len, LANES), lambda b_, j, pt: (b_, j, 0, 0)),
                      tok(LANES), pl.BlockSpec(memory_space=pl.ANY), pl.BlockSpec(memory_space=pl.ANY)],
            out_specs=pl.BlockSpec((1, t_len, 512), lambda b_, j, pt: (b_, 0, 0)),
            scratch_shapes=[pltpu.VMEM((N_KV, rows, HEAD_DIM), BF16), pltpu.VMEM((N_KV, rows, 1), F32),
                            pltpu.VMEM((N_KV, rows, 1), F32), pltpu.VMEM((N_KV, rows, HEAD_DIM), F32),
                            page_buf, page_buf,
                            pltpu.SemaphoreType.DMA((PAGE_SLOTS,)), pltpu.SemaphoreType.DMA((PAGE_SLOTS,))]),
        compiler_params=_cparams(("arbitrary", "arbitrary")),
        name="sdsa_attn",
    )(page_table, q, k_new, v_new, mask_past, mask_new, cache_k, cache_v)


HEAD_ORDER = (0, 4, 1, 5, 2, 6, 3, 7)


def _pad_cols(w, n):
    return jnp.pad(w, ((0, 0), (0, n - w.shape[1])))


def _row(v):
    return v.reshape(1, -1).astype(F32)


def _rope_tables(pos):
    half = ROT_DIM // 2
    inv = ROPE_THETA ** (-jnp.arange(half, dtype=F32) / half)
    ang = pos.astype(F32)[:, None] * inv[None, :]
    cos, sin = jnp.cos(ang), jnp.sin(ang)
    n = pos.shape[0]
    rest = HEAD_DIM - ROT_DIM
    c64 = jnp.concatenate([cos, cos, jnp.ones((n, rest), F32)], axis=1)
    s64 = jnp.concatenate([-sin, sin, jnp.zeros((n, rest), F32)], axis=1)
    return jnp.tile(c64, (1, 2)), jnp.tile(s64, (1, 2))


def _even_layer(x, prev_a, prev_ssm, prev_b, lc, p, *, tm, tl_a):
    bsz, L, d = x.shape
    m = bsz * L
    x2 = x.reshape(m, d)
    zmain, ztail = norm_proj(x2, p["e_norm_mix"], p["e_w_main"], p["e_w_tail"], tm=tm, tn=E_MAIN // 2)
    zmain3 = zmain.reshape(bsz, L, E_MAIN)
    prev32 = jnp.pad(prev_a, ((0, 0), (CARRY_A - (CONV_A - 1), 0), (0, 0)))
    ya, new_a = conv_a(zmain3, prev32, p["e_conv_a_w"], p["e_conv_a_b"], p["e_ln_a_g"], p["e_ln_a_b"], tl=tl_a)
    prev8 = jnp.pad(prev_b, ((0, 0), (CARRY_B - (SSM_CONV - 1), 0), (0, 0)))
    yb, new_b, hout = ssd_mixer(zmain3, ztail.reshape(bsz, L, LANES), prev8,
                                prev_ssm.reshape(bsz, D_INNER, SSM_STATE),
                                p["e_conv_b_w"], p["e_conv_b_b"], p["e_dt_bias"], p["e_a_neg"], p["e_expand"],
                                p["e_dskip"], p["e_ssm_norm"], lc=lc)
    x2 = out_proj(x2, ya.reshape(m, D_A), yb.reshape(m, D_INNER), p["e_wo_a"], p["e_wo_b"], tm=min(m, 1024))
    x2 = swiglu_ffn(x2, p["e_norm_ffn"], p["e_w_gate"], p["e_w_up"], p["e_w_down"], tm=min(m, 512), tf=FF_TILE)
    return (x2.reshape(bsz, L, d), new_a[:, CARRY_A - (CONV_A - 1):], hout.reshape(bsz, SSM_HEADS, SSM_HEADDIM, SSM_STATE),
            new_b[:, CARRY_B - (SSM_CONV - 1):])


def _odd_project(x2, cos_t, sin_t, p, *, tm):
    zmain, ztail = norm_proj(x2, p["o_norm_mix"], p["o_w_main"], p["o_w_tail"], tm=tm, tn=O_MAIN)
    q, k, v, qi, kiw, k16, v16, kk16 = odd_post(zmain, ztail, cos_t, sin_t, p["o_q_norm"], p["o_k_norm"],
                                                p["o_kidx_g"], p["o_kidx_b"], p["o_bd"], tm=tm)
    return zmain, q, k, v, qi, kiw, (k16, v16, kk16)


def _odd_tail(x2, att, zmain3, prev_d, p, *, tm, tl_d):
    bsz, L, _ = zmain3.shape
    m = bsz * L
    prev8 = jnp.pad(prev_d, ((0, 0), (CARRY_D - (CONV_D - 1), 0), (0, 0)))
    dm, new_d = conv_d(zmain3, prev8, p["o_conv_d_w"], tl=tl_d)
    x2 = out_proj(x2, att.reshape(m, 512), dm.reshape(m, D_D), p["o_wo_a"], p["o_wo_b"], tm=min(m, 1024))
    if (2 * m) % MOE_TILE == 0 and 2 * m >= N_EXPERTS * MOE_TILE:
        x2 = moe_routed(x2, p["o_norm_ffn"], p["o_wr_hi"], p["o_wr_lo"], p["o_br"],
                        p["o_we_gate"], p["o_we_up"], p["o_we_down"], tm=tm)
    else:
        xn, comb = moe_router(x2, p["o_norm_ffn"], p["o_wr_hi"], p["o_wr_lo"], p["o_br"], tm=tm)
        x2 = moe_dense(x2, xn, comb, p["o_we_gate"], p["o_we_up"], p["o_we_down"], tm=min(m, 1024), tf=256)
    return x2, new_d[:, CARRY_D - (CONV_D - 1):]


def kernel(x_prompt, x_sample, state_conv_a, state_ssm, state_conv_b, cache_k, cache_v, cache_kidx, state_conv_d,
           page_table, e_norm_mix, e_w_in, e_conv_a_w, e_conv_a_b, e_ln_a_g, e_ln_a_b, e_conv_b_w, e_conv_b_b,
           e_dt_bias, e_a_log, e_d_skip, e_ssm_norm, e_w_out, e_norm_ffn, e_w_gate, e_w_up, e_w_down,
           o_norm_mix, o_w_in, o_q_norm, o_k_norm, o_kidx_g, o_kidx_b, o_conv_d_w, o_w_out, o_norm_ffn,
           o_w_router, o_b_router, o_we_gate, o_we_up, o_we_down):
    bp, s_len, d = x_prompt.shape
    bd, t_len, _ = x_sample.shape
    n_pairs = e_w_in.shape[0]
    past = page_table.shape[1] * PAGE_SIZE
    n_pool = cache_k.shape[1]
    xp, xs = x_prompt, x_sample
    outs_p = [[] for _ in range(7)]
    outs_s = [[] for _ in range(7)]
    cos_p, sin_p = _rope_tables(jnp.arange(s_len))
    cos_s, sin_s = _rope_tables(jnp.tile(past + jnp.arange(t_len), bd))
    perm = np.concatenate([np.arange(h * HEAD_DIM, (h + 1) * HEAD_DIM) for h in HEAD_ORDER])
    expand = (jnp.arange(LANES)[:, None] == (jnp.arange(D_INNER)[None, :] // SSM_HEADDIM)).astype(BF16)
    blk = jnp.arange(LANES) // HEAD_DIM
    bdiag = jnp.where(blk[:, None] == blk[None, :], 1.0 / HEAD_DIM, 0.0).astype(BF16)
    for i in range(n_pairs):
        w = e_w_in[i]
        p = {
            "e_norm_mix": _row(e_norm_mix[i]),
            "e_w_main": jnp.concatenate([w[:, 2048:3584], w[:, 0:512], w[:, 1024:2048], w[:, 512:1024]],
                                        axis=1).astype(BF16),
            "e_w_tail": _pad_cols(w[:, 3584:3600], LANES).astype(BF16),
            "e_conv_a_w": e_conv_a_w[i], "e_conv_a_b": _row(e_conv_a_b[i]),
            "e_ln_a_g": _row(e_ln_a_g[i]), "e_ln_a_b": _row(e_ln_a_b[i]),
            "e_conv_b_w": e_conv_b_w[i], "e_conv_b_b": _row(e_conv_b_b[i]),
            "e_dt_bias": _pad_cols(_row(e_dt_bias[i]), LANES),
            "e_a_neg": _pad_cols(_row(-jnp.exp(e_a_log[i].astype(F32))), LANES),
            "e_expand": expand,
            "e_dskip": _row(jnp.repeat(e_d_skip[i], SSM_HEADDIM)),
            "e_ssm_norm": _row(e_ssm_norm[i]),
            "e_wo_a": e_w_out[i][:D_A].astype(BF16), "e_wo_b": e_w_out[i][D_A:].astype(BF16),
            "e_norm_ffn": _row(e_norm_ffn[i]),
            "e_w_gate": e_w_gate[i].astype(BF16), "e_w_up": e_w_up[i].astype(BF16),
            "e_w_down": e_w_down[i].astype(BF16),
        }
        zero_a = jnp.zeros((bp, CONV_A - 1, D_A), F32)
        zero_h = jnp.zeros((bp, SSM_HEADS, SSM_HEADDIM, SSM_STATE), F32)
        zero_b = jnp.zeros((bp, SSM_CONV - 1, CONV_DIM), F32)
        xp, ca, sm, cb = _even_layer(xp, zero_a, zero_h, zero_b, min(SSD_CHUNK, s_len), p, tm=512, tl_a=256)
        outs_p[0].append(ca); outs_p[1].append(sm); outs_p[2].append(cb)
        xs, ca, sm, cb = _even_layer(xs, state_conv_a[i], state_ssm[i], state_conv_b[i], t_len, p,
                                     tm=bd * t_len, tl_a=t_len)
        outs_s[0].append(ca); outs_s[1].append(sm); outs_s[2].append(cb)

        w = o_w_in[i]
        wo = o_w_out[i]
        p = {
            "o_norm_mix": _row(o_norm_mix[i]),
            "o_w_main": jnp.concatenate([w[:, 0:512][:, perm], w[:, 512:1024], w[:, 1092:2628]], axis=1).astype(BF16),
            "o_w_tail": _pad_cols(w[:, 1024:1092], LANES).astype(BF16),
            "o_q_norm": _row(jnp.tile(o_q_norm[i], 2)), "o_k_norm": _row(jnp.tile(o_k_norm[i], 2)),
            "o_kidx_g": _pad_cols(_row(o_kidx_g[i]), LANES), "o_kidx_b": _pad_cols(_row(o_kidx_b[i]), LANES),
            "o_bd": bdiag,
            "o_conv_d_w": o_conv_d_w[i],
            "o_wo_a": wo[:512][perm].astype(BF16), "o_wo_b": wo[512:].astype(BF16),
            "o_norm_ffn": _row(o_norm_ffn[i]),
            "o_br": jnp.concatenate([_row(o_b_router[i]), jnp.full((1, LANES - N_EXPERTS), NEG, F32)], axis=1),
            "o_we_gate": o_we_gate[i].astype(BF16), "o_we_up": o_we_up[i].astype(BF16),
            "o_we_down": o_we_down[i].astype(BF16),
        }
        wr = _pad_cols(o_w_router[i], LANES)
        p["o_wr_hi"] = wr.astype(BF16)
        p["o_wr_lo"] = (wr - p["o_wr_hi"].astype(F32)).astype(BF16)

        mp = bp * s_len
        x2 = xp.reshape(mp, d)
        zmain, q, k, v, qi, kiw, (k16, v16, kk16) = _odd_project(x2, cos_p, sin_p, p, tm=512)
        r3 = lambda a: a.reshape(bp, s_len, a.shape[-1])
        att = dsa_prompt(r3(q), r3(qi), r3(kiw), r3(k16), r3(v16), r3(kk16))
        x2, cd = _odd_tail(x2, att, r3(zmain), jnp.zeros((bp, CONV_D - 1, D_D), F32), p, tm=512, tl_d=512)
        xp = x2.reshape(bp, s_len, d)
        outs_p[3].append(k.reshape(bp, s_len, N_KV, HEAD_DIM)); outs_p[4].append(v.reshape(bp, s_len, N_KV, HEAD_DIM))
        outs_p[5].append(kiw[:, :IDX_DIM].reshape(bp, s_len, IDX_DIM)); outs_p[6].append(cd)

        ms = bd * t_len
        x2 = xs.reshape(ms, d)
        zmain, q, k, v, qi, kiw, _ = _odd_project(x2, cos_s, sin_s, p, tm=ms)
        r3 = lambda a: a.reshape(bd, t_len, a.shape[-1])
        mask_past, mask_new = sdsa_index(page_table, r3(qi), r3(kiw), r3(kiw),
                                         jnp.transpose(cache_kidx[i], (0, 2, 1)))
        att = sdsa_attn(page_table, r3(q), r3(k), r3(v), mask_past, mask_new,
                        jnp.transpose(cache_k[i], (0, 2, 3, 1)), jnp.transpose(cache_v[i], (0, 2, 3, 1)))
        x2, cd = _odd_tail(x2, att, r3(zmain), state_conv_d[i], p, tm=ms, tl_d=t_len)
        xs = x2.reshape(bd, t_len, d)
        outs_s[3].append(k.reshape(bd, t_len, N_KV, HEAD_DIM)); outs_s[4].append(v.reshape(bd, t_len, N_KV, HEAD_DIM))
        outs_s[5].append(kiw[:, :IDX_DIM].reshape(bd, t_len, IDX_DIM)); outs_s[6].append(cd)
    return (xp, xs) + tuple(jnp.stack(o) for o in outs_p) + tuple(jnp.stack(o) for o in outs_s)
```

```python
import functools
import math

import jax
import jax.numpy as jnp
import numpy as np
from jax import lax
from jax.experimental import pallas as pl
from jax.experimental.pallas import tpu as pltpu

F32 = jnp.float32
BF16 = jnp.bfloat16
I32 = jnp.int32

D_MODEL = 1024
D_A = 512
CONV_A = 31
D_INNER = 1024
SSM_HEADDIM = 64
SSM_HEADS = 16
SSM_GROUPS = 2
SSM_STATE = 128
SSM_CONV = 4
CONV_DIM = D_INNER + 2 * SSM_GROUPS * SSM_STATE
SSD_CHUNK = 128
HEAD_DIM = 64
N_HEADS = 8
N_KV = 2
ROT_DIM = 16
ROPE_THETA = 500000.0
IDX_HEADS = 4
IDX_DIM = 64
TOPK_MAX = 256
Q_BLOCK = 128
D_D = 512
CONV_D = 3
D_FF = 2816
N_EXPERTS = 8
PAGE_SIZE = 128
EPS = 1e-6
NEG = -1e30

LANES = 128
SUBLANES = 8
VMEM_LIMIT = 56 * 1024 * 1024

E_MAIN = D_A + D_A + D_INNER + CONV_DIM
E_COL_XBC, E_COL_VAL, E_COL_Z, E_COL_GATE = 0, 1536, 2048, 3072
O_MAIN = 512 + 128 + 128 + 256 + 3 * D_D


def _cparams(sem):
    return pltpu.CompilerParams(dimension_semantics=sem, vmem_limit_bytes=VMEM_LIMIT)


def _bdot(a, b):
    return jnp.dot(a.astype(BF16), b.astype(BF16), preferred_element_type=F32)


def _bdot_nt(a, b):
    return lax.dot_general(a.astype(BF16), b.astype(BF16), (((1,), (1,)), ((), ())),
                           preferred_element_type=F32)


def _split3(a):
    hi = a.astype(BF16)
    r1 = a - hi.astype(F32)
    mid = r1.astype(BF16)
    lo = (r1 - mid.astype(F32)).astype(BF16)
    return hi, mid, lo


def _dot_exact_rhs(a, b_bf16):
    hi, mid, lo = _split3(a)
    out = jnp.dot(lo, b_bf16, preferred_element_type=F32)
    out = out + jnp.dot(mid, b_bf16, preferred_element_type=F32)
    return out + jnp.dot(hi, b_bf16, preferred_element_type=F32)


def _dot_exact_lhs(a_bf16, b):
    hi, mid, lo = _split3(b)
    out = jnp.dot(a_bf16, lo, preferred_element_type=F32)
    out = out + jnp.dot(a_bf16, mid, preferred_element_type=F32)
    return out + jnp.dot(a_bf16, hi, preferred_element_type=F32)


def _sigmoid(x):
    return 1.0 / (1.0 + jnp.exp(-x))


def _silu(x):
    return x * _sigmoid(x)


def _softplus(x):
    return jnp.maximum(x, 0.0) + jnp.log(1.0 + jnp.exp(-jnp.abs(x)))


def _norm_proj_kernel(x_ref, g_ref, w_ref, wt_ref, main_ref, tail_ref, xn_ref):
    j = pl.program_id(1)

    @pl.when(j == 0)
    def _():
        x = x_ref[...]
        y = x * lax.rsqrt(jnp.mean(x * x, axis=-1, keepdims=True) + EPS) * g_ref[...]
        xn = y.astype(BF16)
        xn_ref[...] = xn
        tail_ref[...] = jnp.dot(xn, wt_ref[...], preferred_element_type=F32)

    main_ref[...] = jnp.dot(xn_ref[...], w_ref[...], preferred_element_type=F32)


def norm_proj(x, g, w_main, w_tail, *, tm, tn):
    m, k = x.shape
    n = w_main.shape[1]
    return pl.pallas_call(
        _norm_proj_kernel,
        out_shape=(jax.ShapeDtypeStruct((m, n), F32), jax.ShapeDtypeStruct((m, LANES), F32)),
        grid=(m // tm, n // tn),
        in_specs=[pl.BlockSpec((tm, k), lambda i, j: (i, 0)),
                  pl.BlockSpec((1, k), lambda i, j: (0, 0)),
                  pl.BlockSpec((k, tn), lambda i, j: (0, j)),
                  pl.BlockSpec((k, LANES), lambda i, j: (0, 0))],
        out_specs=(pl.BlockSpec((tm, tn), lambda i, j: (i, j)),
                   pl.BlockSpec((tm, LANES), lambda i, j: (i, 0))),
        scratch_shapes=[pltpu.VMEM((tm, k), BF16)],
        compiler_params=_cparams(("arbitrary", "arbitrary")),
        name="norm_proj",
    )(x, g, w_main, w_tail)


def _dwconv_from_buf(buf_ref, w_ref, kw, carry, tl):
    acc = None
    for k in range(kw):
        start = carry - (kw - 1) + k
        term = buf_ref[pl.ds(start, tl), :] * w_ref[pl.ds(k, 1), :]
        acc = term if acc is None else acc + term
    return acc


CARRY_A = 32


def _conv_a_kernel(val_ref, gate_ref, prev_ref, w_ref, b_ref, lg_ref, lb_ref, ya_ref, new_ref, buf_ref, *, tl):
    l = pl.program_id(1)

    @pl.when(l == 0)
    def _():
        buf_ref[pl.ds(0, CARRY_A), :] = prev_ref[0]

    u = val_ref[0] * _sigmoid(gate_ref[0])
    buf_ref[pl.ds(CARRY_A, tl), :] = u
    y = _dwconv_from_buf(buf_ref, w_ref, CONV_A, CARRY_A, tl) + b_ref[...]
    mu = jnp.mean(y, axis=-1, keepdims=True)
    yc = y - mu
    var = jnp.mean(yc * yc, axis=-1, keepdims=True)
    yn = yc * lax.rsqrt(var + EPS) * lg_ref[...] + lb_ref[...]
    ya_ref[0] = _silu(yn)
    hist = buf_ref[pl.ds(tl, CARRY_A), :]
    new_ref[0] = hist
    buf_ref[pl.ds(0, CARRY_A), :] = hist


def conv_a(zmain, prev32, w, b, lg, lb, *, tl):
    bsz, L, _ = zmain.shape
    kern = functools.partial(_conv_a_kernel, tl=tl)
    return pl.pallas_call(
        kern,
        out_shape=(jax.ShapeDtypeStruct((bsz, L, D_A), F32), jax.ShapeDtypeStruct((bsz, CARRY_A, D_A), F32)),
        grid=(bsz, L // tl),
        in_specs=[pl.BlockSpec((1, tl, D_A), lambda b_, l: (b_, l, E_COL_VAL // D_A)),
                  pl.BlockSpec((1, tl, D_A), lambda b_, l: (b_, l, E_COL_GATE // D_A)),
                  pl.BlockSpec((1, CARRY_A, D_A), lambda b_, l: (b_, 0, 0)),
                  pl.BlockSpec((CONV_A, D_A), lambda b_, l: (0, 0)),
                  pl.BlockSpec((1, D_A), lambda b_, l: (0, 0)),
                  pl.BlockSpec((1, D_A), lambda b_, l: (0, 0)),
                  pl.BlockSpec((1, D_A), lambda b_, l: (0, 0))],
        out_specs=(pl.BlockSpec((1, tl, D_A), lambda b_, l: (b_, l, 0)),
                   pl.BlockSpec((1, CARRY_A, D_A), lambda b_, l: (b_, 0, 0))),
        scratch_shapes=[pltpu.VMEM((CARRY_A + tl, D_A), F32)],
        compiler_params=_cparams(("arbitrary", "arbitrary")),
        name="conv_a",
    )(zmain, zmain, prev32, w, b, lg, lb)


CARRY_B = 8
SSD_LP = 128


def _ssd_kernel(xbc_ref, z_ref, dt_ref, prevb_ref, h0_ref, cw_ref, cb_ref, dtb_ref, a_ref, expand_ref,
                dskip_ref, nrm_ref, y_ref, newb_ref, hout_ref, buf_ref, ht_ref, *, lc):
    c = pl.program_id(1)
    nc = pl.num_programs(1)
    lp = SSD_LP

    @pl.when(c == 0)
    def _():
        buf_ref[pl.ds(0, CARRY_B), :] = prevb_ref[0]
        ht_ref[...] = h0_ref[0].T
        if lc < lp:
            buf_ref[pl.ds(CARRY_B + lc, lp - lc), :] = jnp.zeros((lp - lc, CONV_DIM), F32)

    buf_ref[pl.ds(CARRY_B, lc), :] = xbc_ref[0]
    xbc = _silu(_dwconv_from_buf(buf_ref, cw_ref, SSM_CONV, CARRY_B, lp) + cb_ref[...])
    hist = buf_ref[pl.ds(lc, CARRY_B), :]
    newb_ref[0] = hist
    buf_ref[pl.ds(0, CARRY_B), :] = hist

    xs = xbc[:, :D_INNER]
    if lc < lp:
        dt_raw = jnp.concatenate([dt_ref[0], jnp.zeros((lp - lc, LANES), F32)], axis=0)
        zg = jnp.concatenate([z_ref[0], jnp.zeros((lp - lc, D_INNER), F32)], axis=0)
    else:
        dt_raw = dt_ref[0]
        zg = z_ref[0]
    dt = _softplus(dt_raw + dtb_ref[...])
    if lc < lp:
        row = lax.broadcasted_iota(I32, (lp, LANES), 0)
        dt = jnp.where(row < lc, dt, 0.0)
    lane = lax.broadcasted_iota(I32, (lp, LANES), 1)
    dt = jnp.where(lane < SSM_HEADS, dt, 0.0)
    da = dt * a_ref[...]

    r_i = lax.broadcasted_iota(I32, (lp, lp), 0)
    c_i = lax.broadcasted_iota(I32, (lp, lp), 1)
    causal = r_i >= c_i
    tri = jnp.where(causal, 1.0, 0.0).astype(BF16)
    a_col = _dot_exact_lhs(tri, da)
    a_row = a_col.T
    dt_row = dt.T
    a_last = a_col[lp - 1:lp, :]

    expand = expand_ref[...]
    a_exp = _dot_exact_rhs(a_col, expand)
    dt_exp = _dot_exact_rhs(dt, expand)
    al_exp = _dot_exact_rhs(a_last, expand)

    ht = ht_ref[...]
    y_parts = []
    for g in range(SSM_GROUPS):
        bm = xbc[:, D_INNER + g * SSM_STATE: D_INNER + (g + 1) * SSM_STATE]
        cm = xbc[:, D_INNER + (SSM_GROUPS + g) * SSM_STATE: D_INNER + (SSM_GROUPS + g + 1) * SSM_STATE]
        cb = _bdot_nt(cm, bm)
        gs = slice(g * 512, (g + 1) * 512)
        y_inter = _bdot(cm, ht[:, gs]) * jnp.exp(a_exp[:, gs])
        heads = []
        for r in range(SSM_HEADS // SSM_GROUPS):
            h = g * (SSM_HEADS // SSM_GROUPS) + r
            seg = a_col[:, h:h + 1] - a_row[h:h + 1, :]
            decay = jnp.where(causal, jnp.exp(jnp.where(causal, seg, 0.0)), 0.0)
            wts = cb * decay * dt_row[h:h + 1, :]
            heads.append(_bdot(wts, xs[:, h * SSM_HEADDIM:(h + 1) * SSM_HEADDIM]))
        y_parts.append(jnp.concatenate(heads, axis=-1) + y_inter)
        xw = xs[:, gs] * (jnp.exp(al_exp[:, gs] - a_exp[:, gs]) * dt_exp[:, gs])
        ht_ref[:, gs] = ht[:, gs] * jnp.exp(al_exp[:, gs]) + _bdot(bm.T, xw)
    y = jnp.concatenate(y_parts, axis=-1)
    y = y + dskip_ref[...] * xs
    y = y * _silu(zg)
    outs = []
    for g in range(SSM_GROUPS):
        yg = y[:, g * 512:(g + 1) * 512]
        outs.append(yg * lax.rsqrt(jnp.mean(yg * yg, axis=-1, keepdims=True) + EPS))
    yb = jnp.concatenate(outs, axis=-1) * nrm_ref[...]
    y_ref[0] = yb[:lc, :]

    @pl.when(c == nc - 1)
    def _():
        hout_ref[0] = ht_ref[...].T


def ssd_mixer(zmain, ztail, prev8, h0, cw, cb, dtb, a_neg, expand, dskip_exp, nrm, *, lc):
    bsz, L, _ = zmain.shape
    kern = functools.partial(_ssd_kernel, lc=lc)
    return pl.pallas_call(
        kern,
        out_shape=(jax.ShapeDtypeStruct((bsz, L, D_INNER), F32),
                   jax.ShapeDtypeStruct((bsz, CARRY_B, CONV_DIM), F32),
                   jax.ShapeDtypeStruct((bsz, D_INNER, SSM_STATE), F32)),
        grid=(bsz, L // lc),
        in_specs=[pl.BlockSpec((1, lc, CONV_DIM), lambda b_, c: (b_, c, E_COL_XBC // CONV_DIM)),
                  pl.BlockSpec((1, lc, D_INNER), lambda b_, c: (b_, c, E_COL_Z // D_INNER)),
                  pl.BlockSpec((1, lc, LANES), lambda b_, c: (b_, c, 0)),
                  pl.BlockSpec((1, CARRY_B, CONV_DIM), lambda b_, c: (b_, 0, 0)),
                  pl.BlockSpec((1, D_INNER, SSM_STATE), lambda b_, c: (b_, 0, 0)),
                  pl.BlockSpec((SSM_CONV, CONV_DIM), lambda b_, c: (0, 0)),
                  pl.BlockSpec((1, CONV_DIM), lambda b_, c: (0, 0)),
                  pl.BlockSpec((1, LANES), lambda b_, c: (0, 0)),
                  pl.BlockSpec((1, LANES), lambda b_, c: (0, 0)),
                  pl.BlockSpec((LANES, D_INNER), lambda b_, c: (0, 0)),
                  pl.BlockSpec((1, D_INNER), lambda b_, c: (0, 0)),
                  pl.BlockSpec((1, D_INNER), lambda b_, c: (0, 0))],
        out_specs=(pl.BlockSpec((1, lc, D_INNER), lambda b_, c: (b_, c, 0)),
                   pl.BlockSpec((1, CARRY_B, CONV_DIM), lambda b_, c: (b_, 0, 0)),
                   pl.BlockSpec((1, D_INNER, SSM_STATE), lambda b_, c: (b_, 0, 0))),
        scratch_shapes=[pltpu.VMEM((CARRY_B + SSD_LP, CONV_DIM), F32),
                        pltpu.VMEM((SSM_STATE, D_INNER), F32)],
        compiler_params=_cparams(("arbitrary", "arbitrary")),
        name="ssd_mixer",
    )(zmain, zmain, ztail, prev8, h0, cw, cb, dtb, a_neg, expand, dskip_exp, nrm)


def _out_proj_kernel(x_ref, a_ref, b_ref, wa_ref, wb_ref, o_ref):
    acc = jnp.dot(a_ref[...].astype(BF16), wa_ref[...], preferred_element_type=F32)
    acc = acc + jnp.dot(b_ref[...].astype(BF16), wb_ref[...], preferred_element_type=F32)
    o_ref[...] = x_ref[...] + acc


def out_proj(x, a, b, wa, wb, *, tm):
    m, d = x.shape
    ka, kb = a.shape[1], b.shape[1]
    return pl.pallas_call(
        _out_proj_kernel,
        out_shape=jax.ShapeDtypeStruct((m, d), F32),
        grid=(m // tm,),
        in_specs=[pl.BlockSpec((tm, d), lambda i: (i, 0)),
                  pl.BlockSpec((tm, ka), lambda i: (i, 0)),
                  pl.BlockSpec((tm, kb), lambda i: (i, 0)),
                  pl.BlockSpec((ka, d), lambda i: (0, 0)),
                  pl.BlockSpec((kb, d), lambda i: (0, 0))],
        out_specs=pl.BlockSpec((tm, d), lambda i: (i, 0)),
        compiler_params=_cparams(("arbitrary",)),
        name="out_proj",
    )(x, a, b, wa, wb)


def _swiglu_kernel(x_ref, g_ref, wg_ref, wu_ref, wd_ref, o_ref, xn_ref, acc_ref):
    f = pl.program_id(1)
    nf = pl.num_programs(1)

    @pl.when(f == 0)
    def _():
        x = x_ref[...]
        y = x * lax.rsqrt(jnp.mean(x * x, axis=-1, keepdims=True) + EPS) * g_ref[...]
        xn_ref[...] = y.astype(BF16)
        acc_ref[...] = jnp.zeros_like(acc_ref)

    xn = xn_ref[...]
    gate = jnp.dot(xn, wg_ref[...], preferred_element_type=F32)
    up = jnp.dot(xn, wu_ref[...], preferred_element_type=F32)
    hid = (_silu(gate) * up).astype(BF16)
    acc_ref[...] += jnp.dot(hid, wd_ref[...], preferred_element_type=F32)

    @pl.when(f == nf - 1)
    def _():
        o_ref[...] = x_ref[...] + acc_ref[...]


def swiglu_ffn(x, g, wg, wu, wd, *, tm, tf):
    m, d = x.shape
    ff = wg.shape[1]
    return pl.pallas_call(
        _swiglu_kernel,
        out_shape=jax.ShapeDtypeStruct((m, d), F32),
        grid=(m // tm, ff // tf),
        in_specs=[pl.BlockSpec((tm, d), lambda i, f: (i, 0)),
                  pl.BlockSpec((1, d), lambda i, f: (0, 0)),
                  pl.BlockSpec((d, tf), lambda i, f: (0, f)),
                  pl.BlockSpec((d, tf), lambda i, f: (0, f)),
                  pl.BlockSpec((tf, d), lambda i, f: (f, 0))],
        out_specs=pl.BlockSpec((tm, d), lambda i, f: (i, 0)),
        scratch_shapes=[pltpu.VMEM((tm, d), BF16), pltpu.VMEM((tm, d), F32)],
        compiler_params=_cparams(("arbitrary", "arbitrary")),
        name="swiglu_ffn",
    )(x, g, wg, wu, wd)


O_COL_Q, O_COL_K, O_COL_V, O_COL_QI, O_COL_DIN, O_COL_BG, O_COL_CG = 0, 512, 640, 768, 1024, 1536, 2048


def _rope128(x, cos_t, sin_t, lo_mask):
    up = pltpu.roll(x, LANES - ROT_DIM // 2, 1)
    dn = pltpu.roll(x, ROT_DIM // 2, 1)
    return x * cos_t + jnp.where(lo_mask, up, dn) * sin_t


def _odd_post_kernel(qkvq_ref, tail_ref, cos_ref, sin_ref, qn_ref, kn_ref, kg_ref, kb_ref, bd_ref,
                     q_ref, k_ref, v_ref, qi_ref, kiw_ref, k16_ref, v16_ref, kk16_ref):
    cos_t = cos_ref[...]
    sin_t = sin_ref[...]
    lane = lax.broadcasted_iota(I32, cos_t.shape, 1)
    lo_mask = (lane % HEAD_DIM) < (ROT_DIM // 2)
    bd = bd_ref[...]
    z = qkvq_ref[...]
    for c in range(4):
        x = z[:, c * LANES:(c + 1) * LANES]
        ms = _dot_exact_rhs(x * x, bd)
        xn = x * lax.rsqrt(ms + EPS) * qn_ref[...]
        q_ref[:, c * LANES:(c + 1) * LANES] = _rope128(xn, cos_t, sin_t, lo_mask)
    x = z[:, O_COL_K:O_COL_K + LANES]
    ms = _dot_exact_rhs(x * x, bd)
    xn = x * lax.rsqrt(ms + EPS) * kn_ref[...]
    kr = _rope128(xn, cos_t, sin_t, lo_mask)
    k_ref[...] = kr
    k16_ref[...] = kr.astype(BF16)
    vv = z[:, O_COL_V:O_COL_V + LANES]
    v_ref[...] = vv
    v16_ref[...] = vv.astype(BF16)
    for c in range(2):
        x = z[:, O_COL_QI + c * LANES: O_COL_QI + (c + 1) * LANES]
        qi_ref[:, c * LANES:(c + 1) * LANES] = _rope128(x, cos_t, sin_t, lo_mask)
    t = tail_ref[...]
    lane_t = lax.broadcasted_iota(I32, t.shape, 1)
    is_ki = lane_t < IDX_DIM
    mu = _dot_exact_rhs(t, bd)
    tc = t - mu
    var = _dot_exact_rhs(tc * tc, bd)
    kin = tc * lax.rsqrt(var + EPS) * kg_ref[...] + kb_ref[...]
    kin = _rope128(kin, cos_t, sin_t, lo_mask)
    kiw_ref[...] = jnp.where(is_ki, kin, t * (IDX_HEADS ** -0.5 * IDX_DIM ** -0.5))
    kk16_ref[...] = jnp.where(is_ki, kin, pltpu.roll(kin, IDX_DIM, 1)).astype(BF16)


def odd_post(zmain, ztail, cos_t, sin_t, qn, kn, kg, kb, bd, *, tm):
    m = zmain.shape[0]
    nrope = cos_t.shape[0] // tm
    row = lambda i: (i, 0)
    rope = lambda i: (i % nrope, 0)
    const = lambda i: (0, 0)
    return pl.pallas_call(
        _odd_post_kernel,
        out_shape=(jax.ShapeDtypeStruct((m, 512), F32), jax.ShapeDtypeStruct((m, LANES), F32),
                   jax.ShapeDtypeStruct((m, LANES), F32), jax.ShapeDtypeStruct((m, 256), F32),
                   jax.ShapeDtypeStruct((m, LANES), F32), jax.ShapeDtypeStruct((m, LANES), BF16),
                   jax.ShapeDtypeStruct((m, LANES), BF16), jax.ShapeDtypeStruct((m, LANES), BF16)),
        grid=(m // tm,),
        in_specs=[pl.BlockSpec((tm, 1024), row), pl.BlockSpec((tm, LANES), row),
                  pl.BlockSpec((tm, LANES), rope), pl.BlockSpec((tm, LANES), rope),
                  pl.BlockSpec((1, LANES), const), pl.BlockSpec((1, LANES), const),
                  pl.BlockSpec((1, LANES), const), pl.BlockSpec((1, LANES), const),
                  pl.BlockSpec((LANES, LANES), const)],
        out_specs=(pl.BlockSpec((tm, 512), row), pl.BlockSpec((tm, LANES), row), pl.BlockSpec((tm, LANES), row),
                   pl.BlockSpec((tm, 256), row), pl.BlockSpec((tm, LANES), row), pl.BlockSpec((tm, LANES), row),
                   pl.BlockSpec((tm, LANES), row), pl.BlockSpec((tm, LANES), row)),
        compiler_params=_cparams(("arbitrary",)),
        name="odd_post",
    )(zmain, ztail, cos_t, sin_t, qn, kn, kg, kb, bd)


INT_MIN = -2 ** 31
F32_MIN_NORMAL_BITS = 0x00800000


def _code_to_float(c):
    return pltpu.bitcast(c ^ ((c >> 31) & 0x7FFFFFFF), F32)


def _kth_largest(count_ge, k, shape):
    ans = jnp.full(shape, INT_MIN, I32)
    cand = jnp.zeros(shape, I32)
    ans = jnp.where(count_ge(_code_to_float(cand)) >= k, cand, ans)

    def body(i, ans):
        cand = ans | jnp.left_shift(jnp.int32(1), 30 - i)
        return jnp.where(count_ge(_code_to_float(cand)) >= k, cand, ans)

    ans = lax.fori_loop(0, 31, body, ans)
    ans = jnp.where((ans > 0) & (ans < F32_MIN_NORMAL_BITS), 0, ans)
    return _code_to_float(ans)


def _sum_leading(x):
    n = x.shape[0]
    extra = None
    while n > 1:
        if n % 2:
            extra = x[n - 1] if extra is None else extra + x[n - 1]
            n -= 1
        x = x[:n // 2] + x[n // 2:n]
        n //= 2
    return x[0] if extra is None else x[0] + extra


def _upper_tri_bf16(n):
    r = lax.broadcasted_iota(I32, (n, n), 0)
    c = lax.broadcasted_iota(I32, (n, n), 1)
    return jnp.where(r <= c, 1.0, 0.0).astype(BF16)


def _dsa_prompt_kernel(q_ref, qi_ref, wi_ref, k_ref, v_ref, kk_ref, o_ref, keys_ref, bias_ref, *,
                       q_base, s_eff, topk, qb):
    t0 = q_base + pl.program_id(1) * qb
    lane = lax.broadcasted_iota(I32, (qb, LANES), 1)
    row = lax.broadcasted_iota(I32, (qb, LANES), 0)
    lo = lane < HEAD_DIM
    hi = lane >= HEAD_DIM
    rpg = N_HEADS // N_KV

    kk = kk_ref[0]
    qi = qi_ref[0]
    wi = wi_ref[0]
    sc = None
    for h in range(IDX_HEADS):
        x = qi[:, (h // 2) * LANES:(h // 2 + 1) * LANES]
        x = jnp.where(lo if h % 2 == 0 else hi, x, 0.0)
        d = _bdot_nt(x, kk)
        term = jnp.maximum(d, 0.0) * wi[:, IDX_DIM + h:IDX_DIM + h + 1]
        sc = term if sc is None else sc + term
    kpos = lax.broadcasted_iota(I32, (qb, s_eff), 1)
    qpos = t0 + lax.broadcasted_iota(I32, (qb, s_eff), 0)
    adm_all = kpos <= qpos
    sc = jnp.where(sc == 0.0, 0.0, sc)
    keys_ref[...] = jnp.where(adm_all, sc, NEG)

    def count_ge(cand):
        return jnp.sum(jnp.where(keys_ref[...] >= cand, 1.0, 0.0), axis=-1, keepdims=True)

    thr = _kth_largest(count_ge, float(topk), (qb, 1))
    n_ge = count_ge(thr)
    n_adm = t0 + 1 + lax.broadcasted_iota(I32, (qb, 1), 0)
    simple = jnp.where((n_ge == float(topk)) | (n_adm < topk), 0.0, 1.0)
    no_cut = jnp.max(simple) == 0.0

    @pl.when(no_cut)
    def _():
        bias_ref[...] = jnp.where(adm_all, jnp.where(keys_ref[...] >= thr, 0.0, NEG), NEG)

    @pl.when(jnp.logical_not(no_cut))
    def _():
        n_gt = jnp.sum(jnp.where(keys_ref[...] > thr, 1.0, 0.0), axis=-1, keepdims=True)
        need = float(topk) - n_gt
        ut = _upper_tri_bf16(LANES)
        off = jnp.zeros((qb, 1), F32)
        for j in range(s_eff // LANES):
            kj = keys_ref[:, j * LANES:(j + 1) * LANES]
            eq = kj == thr
            pre = jnp.dot(jnp.where(eq, 1.0, 0.0).astype(BF16), ut, preferred_element_type=F32) + off
            adm = (j * LANES + lane) <= (t0 + row)
            keep = jnp.where(eq, jnp.where(pre <= need, 0.0, NEG), jnp.where(kj > thr, 0.0, NEG))
            bias_ref[:, j * LANES:(j + 1) * LANES] = jnp.where(adm, keep, NEG)
            off = pre[:, LANES - 1:LANES]

    kb = k_ref[0]
    vb = v_ref[0]
    lane_b = lax.broadcasted_iota(I32, (Q_BLOCK, LANES), 1)
    lo_b = lane_b < HEAD_DIM
    hi_b = lane_b >= HEAD_DIM
    lo4 = lax.broadcasted_iota(I32, (rpg * Q_BLOCK, LANES), 1) < HEAD_DIM
    for sub in range(qb // Q_BLOCK):
        rows = pl.ds(sub * Q_BLOCK, Q_BLOCK)
        bias = bias_ref[rows, :]
        q = q_ref[0, rows, :] * (HEAD_DIM ** -0.5)
        outs = []
        for g in range(N_KV):
            keep = lo_b if g == 0 else hi_b
            qg = jnp.concatenate([jnp.where(keep, q[:, r * LANES:(r + 1) * LANES], 0.0) for r in range(rpg)],
                                 axis=0)
            s = _bdot_nt(qg, kb).reshape(rpg, Q_BLOCK, s_eff) + bias[None]
            m = jnp.max(s, axis=-1, keepdims=True)
            e = jnp.exp(s - m)
            l = jnp.sum(e, axis=-1, keepdims=True).reshape(rpg * Q_BLOCK, 1)
            pv = jnp.dot(e.reshape(rpg * Q_BLOCK, s_eff).astype(BF16), vb, preferred_element_type=F32)
            outs.append(pv / l)
        oc = jnp.where(lo4, outs[0], outs[1])
        for r in range(rpg):
            o_ref[0, rows, r * LANES:(r + 1) * LANES] = oc[r * Q_BLOCK:(r + 1) * Q_BLOCK, :]


DSA_SEGMENT = 256


def dsa_prompt(q, qi, wi, k16, v16, kk16):
    bsz, s_len, _ = q.shape
    topk = min(TOPK_MAX, s_len // 4)
    seg = min(DSA_SEGMENT, s_len)
    outs = []
    for si in range(s_len // seg):
        s_eff = (si + 1) * seg
        kern = functools.partial(_dsa_prompt_kernel, q_base=si * seg, s_eff=s_eff, topk=topk, qb=seg)
        qblk = lambda w, si=si: pl.BlockSpec((1, seg, w), lambda b_, i: (b_, si, 0))
        keys = pl.BlockSpec((1, s_eff, LANES), lambda b_, i: (b_, 0, 0))
        outs.append(pl.pallas_call(
            kern,
            out_shape=jax.ShapeDtypeStruct((bsz, seg, 512), F32),
            grid=(bsz, 1),
            in_specs=[qblk(512), qblk(256), qblk(LANES), keys, keys, keys],
            out_specs=pl.BlockSpec((1, seg, 512), lambda b_, i: (b_, 0, 0)),
            scratch_shapes=[pltpu.VMEM((seg, s_eff), F32), pltpu.VMEM((seg, s_eff), F32)],
            compiler_params=_cparams(("arbitrary", "arbitrary")),
            name=f"dsa_prompt_{si}",
        )(q, qi, wi, k16, v16, kk16))
    return jnp.concatenate(outs, axis=1)


CARRY_D = 8


def _conv_d_kernel(din_ref, bg_ref, cg_ref, prev_ref, w_ref, y_ref, new_ref, buf_ref, *, tl):
    l = pl.program_id(1)

    @pl.when(l == 0)
    def _():
        buf_ref[pl.ds(0, CARRY_D), :] = prev_ref[0]

    buf_ref[pl.ds(CARRY_D, tl), :] = cg_ref[0] * din_ref[0]
    y_ref[0] = bg_ref[0] * _dwconv_from_buf(buf_ref, w_ref, CONV_D, CARRY_D, tl)
    hist = buf_ref[pl.ds(tl, CARRY_D), :]
    new_ref[0] = hist
    buf_ref[pl.ds(0, CARRY_D), :] = hist


def conv_d(zmain, prev8, w, *, tl):
    bsz, L, _ = zmain.shape
    kern = functools.partial(_conv_d_kernel, tl=tl)
    col = lambda c: pl.BlockSpec((1, tl, D_D), lambda b_, l: (b_, l, c // D_D))
    return pl.pallas_call(
        kern,
        out_shape=(jax.ShapeDtypeStruct((bsz, L, D_D), F32), jax.ShapeDtypeStruct((bsz, CARRY_D, D_D), F32)),
        grid=(bsz, L // tl),
        in_specs=[col(O_COL_DIN), col(O_COL_BG), col(O_COL_CG),
                  pl.BlockSpec((1, CARRY_D, D_D), lambda b_, l: (b_, 0, 0)),
                  pl.BlockSpec((CONV_D, D_D), lambda b_, l: (0, 0))],
        out_specs=(pl.BlockSpec((1, tl, D_D), lambda b_, l: (b_, l, 0)),
                   pl.BlockSpec((1, CARRY_D, D_D), lambda b_, l: (b_, 0, 0))),
        scratch_shapes=[pltpu.VMEM((CARRY_D + tl, D_D), F32)],
        compiler_params=_cparams(("arbitrary", "arbitrary")),
        name="conv_d",
    )(zmain, zmain, zmain, prev8, w)


def _router_kernel(x_ref, g_ref, whi_ref, wlo_ref, br_ref, xn_ref, comb_ref):
    x = x_ref[...]
    y = x * lax.rsqrt(jnp.mean(x * x, axis=-1, keepdims=True) + EPS) * g_ref[...]
    yhi = y.astype(BF16)
    xn_ref[...] = yhi
    ylo = (y - yhi.astype(F32)).astype(BF16)
    logits = jnp.dot(ylo, whi_ref[...], preferred_element_type=F32)
    logits = logits + jnp.dot(yhi, wlo_ref[...], preferred_element_type=F32)
    logits = logits + jnp.dot(yhi, whi_ref[...], preferred_element_type=F32) + br_ref[...]
    lane = lax.broadcasted_iota(I32, logits.shape, 1).astype(F32)
    m1 = jnp.max(logits, axis=-1, keepdims=True)
    i1 = jnp.min(jnp.where(logits == m1, lane, float(LANES)), axis=-1, keepdims=True)
    rest = jnp.where(lane == i1, -3e38, logits)
    m2 = jnp.max(rest, axis=-1, keepdims=True)
    i2 = jnp.min(jnp.where(rest == m2, lane, float(LANES)), axis=-1, keepdims=True)
    e2 = jnp.exp(m2 - m1)
    g1 = 1.0 / (1.0 + e2)
    g2 = e2 / (1.0 + e2)
    comb_ref[...] = jnp.where(lane == i1, g1, 0.0) + jnp.where(lane == i2, g2, 0.0)


def moe_router(x, g, w_hi, w_lo, b_pad, *, tm):
    m, d = x.shape
    return pl.pallas_call(
        _router_kernel,
        out_shape=(jax.ShapeDtypeStruct((m, d), BF16), jax.ShapeDtypeStruct((m, LANES), F32)),
        grid=(m // tm,),
        in_specs=[pl.BlockSpec((tm, d), lambda i: (i, 0)),
                  pl.BlockSpec((1, d), lambda i: (0, 0)),
                  pl.BlockSpec((d, LANES), lambda i: (0, 0)),
                  pl.BlockSpec((d, LANES), lambda i: (0, 0)),
                  pl.BlockSpec((1, LANES), lambda i: (0, 0))],
        out_specs=(pl.BlockSpec((tm, d), lambda i: (i, 0)), pl.BlockSpec((tm, LANES), lambda i: (i, 0))),
        compiler_params=_cparams(("arbitrary",)),
        name="moe_router",
    )(x, g, w_hi, w_lo, b_pad)


def _moe_dense_kernel(x_ref, xn_ref, comb_ref, wg_ref, wu_ref, wd_ref, o_ref, acc_ref, acce_ref):
    e = pl.program_id(1)
    f = pl.program_id(2)
    ne = pl.num_programs(1)
    nf = pl.num_programs(2)

    @pl.when((e == 0) & (f == 0))
    def _():
        acc_ref[...] = jnp.zeros_like(acc_ref)

    @pl.when(f == 0)
    def _():
        acce_ref[...] = jnp.zeros_like(acce_ref)

    xn = xn_ref[...]
    gate = jnp.dot(xn, wg_ref[0], preferred_element_type=F32)
    up = jnp.dot(xn, wu_ref[0], preferred_element_type=F32)
    hid = (_silu(gate) * up).astype(BF16)
    acce_ref[...] += jnp.dot(hid, wd_ref[0], preferred_element_type=F32)

    @pl.when(f == nf - 1)
    def _():
        comb = comb_ref[...]
        lane = lax.broadcasted_iota(I32, comb.shape, 1)
        c = jnp.sum(jnp.where(lane == e, comb, 0.0), axis=-1, keepdims=True)
        acc_ref[...] += c * acce_ref[...]

    @pl.when((e == ne - 1) & (f == nf - 1))
    def _():
        o_ref[...] = x_ref[...] + acc_ref[...]


def moe_dense(x, xn, comb, wg, wu, wd, *, tm, tf):
    m, d = x.shape
    ne, _, ff = wg.shape
    return pl.pallas_call(
        _moe_dense_kernel,
        out_shape=jax.ShapeDtypeStruct((m, d), F32),
        grid=(m // tm, ne, ff // tf),
        in_specs=[pl.BlockSpec((tm, d), lambda i, e, f: (i, 0)),
                  pl.BlockSpec((tm, d), lambda i, e, f: (i, 0)),
                  pl.BlockSpec((tm, LANES), lambda i, e, f: (i, 0)),
                  pl.BlockSpec((1, d, tf), lambda i, e, f: (e, 0, f)),
                  pl.BlockSpec((1, d, tf), lambda i, e, f: (e, 0, f)),
                  pl.BlockSpec((1, tf, d), lambda i, e, f: (e, f, 0))],
        out_specs=pl.BlockSpec((tm, d), lambda i, e, f: (i, 0)),
        scratch_shapes=[pltpu.VMEM((tm, d), F32), pltpu.VMEM((tm, d), F32)],
        compiler_params=_cparams(("arbitrary", "arbitrary", "arbitrary")),
        name="moe_dense",
    )(x, xn, comb, wg, wu, wd)


MOE_TILE = 512
DMA_ISSUE_UNROLL = 8
FF_TILE = D_FF // 2
R_I1, R_I2, R_G1, R_G2, R_RANK1, R_RANK2 = 0, 1, 2, 3, 4, 5


def _router_rank_kernel(x_ref, g_ref, whi_ref, wlo_ref, br_ref, info_ref, cnt_ref, run_ref):
    i = pl.program_id(0)

    @pl.when(i == 0)
    def _():
        run_ref[...] = jnp.zeros_like(run_ref)

    x = x_ref[...]
    tm = x.shape[0]
    y = x * lax.rsqrt(jnp.mean(x * x, axis=-1, keepdims=True) + EPS) * g_ref[...]
    yhi = y.astype(BF16)
    ylo = (y - yhi.astype(F32)).astype(BF16)
    logits = jnp.dot(ylo, whi_ref[...], preferred_element_type=F32)
    logits = logits + jnp.dot(yhi, wlo_ref[...], preferred_element_type=F32)
    logits = logits + jnp.dot(yhi, whi_ref[...], preferred_element_type=F32) + br_ref[...]
    lane = lax.broadcasted_iota(I32, logits.shape, 1).astype(F32)
    m1 = jnp.max(logits, axis=-1, keepdims=True)
    i1 = jnp.min(jnp.where(logits == m1, lane, float(LANES)), axis=-1, keepdims=True)
    rest = jnp.where(lane == i1, -3e38, logits)
    m2 = jnp.max(rest, axis=-1, keepdims=True)
    i2 = jnp.min(jnp.where(rest == m2, lane, float(LANES)), axis=-1, keepdims=True)
    e2 = jnp.exp(m2 - m1)
    g1 = 1.0 / (1.0 + e2)
    g2 = e2 / (1.0 + e2)
    oh1 = jnp.where(lane == i1, 1.0, 0.0)
    oh2 = jnp.where(lane == i2, 1.0, 0.0)
    both = oh1 + oh2
    r_i = lax.broadcasted_iota(I32, (tm, tm), 0)
    c_i = lax.broadcasted_iota(I32, (tm, tm), 1)
    strict_lower = jnp.where(c_i < r_i, 1.0, 0.0).astype(BF16)
    before = jnp.dot(strict_lower, both.astype(BF16), preferred_element_type=F32) + run_ref[...]
    rank1 = jnp.sum(before * oh1, axis=-1, keepdims=True)
    rank2 = jnp.sum(before * oh2, axis=-1, keepdims=True)
    run_ref[...] += jnp.sum(both, axis=0, keepdims=True)
    info = jnp.where(lane == R_I1, i1, 0.0)
    for col, val in ((R_I2, i2), (R_G1, g1), (R_G2, g2), (R_RANK1, rank1), (R_RANK2, rank2)):
        info = jnp.where(lane == col, val, info)
    info_ref[...] = info
    cnt_ref[...] = run_ref[...]


def moe_router_rank(x, g, w_hi, w_lo, b_pad, *, tm):
    m, d = x.shape
    return pl.pallas_call(
        _router_rank_kernel,
        out_shape=(jax.ShapeDtypeStruct((m, LANES), F32), jax.ShapeDtypeStruct((1, LANES), F32)),
        grid=(m // tm,),
        in_specs=[pl.BlockSpec((tm, d), lambda i: (i, 0)),
                  pl.BlockSpec((1, d), lambda i: (0, 0)),
                  pl.BlockSpec((d, LANES), lambda i: (0, 0)),
                  pl.BlockSpec((d, LANES), lambda i: (0, 0)),
                  pl.BlockSpec((1, LANES), lambda i: (0, 0))],
        out_specs=(pl.BlockSpec((tm, LANES), lambda i: (i, 0)), pl.BlockSpec((1, LANES), lambda i: (0, 0))),
        scratch_shapes=[pltpu.VMEM((1, LANES), F32)],
        compiler_params=_cparams(("arbitrary",)),
        name="moe_router_rank",
    )(x, g, w_hi, w_lo, b_pad)


def _row_copy(src_ref, si, dst_ref, di, sem):
    return pltpu.make_async_copy(src_ref.at[pl.ds(si, 1)], dst_ref.at[pl.ds(di, 1)], sem)


def _moe_dispatch_kernel(d1_ref, d2_ref, x_ref, xs_ref, sem, *, ct):
    def issue(t, carry):
        _row_copy(x_ref, t, xs_ref, d1_ref[0, 0, t], sem).start(priority=0)
        _row_copy(x_ref, t, xs_ref, d2_ref[0, 0, t], sem).start(priority=1)
        return carry

    lax.fori_loop(0, ct, issue, 0, unroll=DMA_ISSUE_UNROLL)
    tile_copy = pltpu.make_async_copy(x_ref, xs_ref.at[pl.ds(0, ct)], sem)
    tile_copy.wait()
    tile_copy.wait()


def moe_dispatch(x, dest1, dest2, *, ct):
    m, d = x.shape
    smem_row = pl.BlockSpec((1, 1, ct), lambda i: (i, 0, 0), memory_space=pltpu.SMEM)
    return pl.pallas_call(
        functools.partial(_moe_dispatch_kernel, ct=ct),
        out_shape=jax.ShapeDtypeStruct((2 * m, d), F32),
        grid=(m // ct,),
        in_specs=[smem_row, smem_row, pl.BlockSpec((ct, d), lambda i: (i, 0))],
        out_specs=pl.BlockSpec(memory_space=pl.ANY),
        scratch_shapes=[pltpu.SemaphoreType.DMA],
        compiler_params=_cparams(("arbitrary",)),
        name="moe_dispatch",
    )(dest1, dest2, x)


def _moe_grouped_kernel(tile_ref, exp_ref, first_ref, valid_ref, gs_ref, xs_ref, g_ref, wg_ref, wu_ref, wd_ref,
                        ys_ref, xn_ref, acc_ref):
    w = pl.program_id(0)
    f = pl.program_id(1)
    nf = pl.num_programs(1)

    @pl.when(valid_ref[w] == 1)
    def _():
        @pl.when(f == 0)
        def _():
            x = xs_ref[...]
            y = x * lax.rsqrt(jnp.mean(x * x, axis=-1, keepdims=True) + EPS) * g_ref[...]
            xn_ref[...] = y.astype(BF16)
            acc_ref[...] = jnp.zeros_like(acc_ref)

        xn = xn_ref[...]
        gate = jnp.dot(xn, wg_ref[0], preferred_element_type=F32)
        up = jnp.dot(xn, wu_ref[0], preferred_element_type=F32)
        hid = (_silu(gate) * up).astype(BF16)
        acc_ref[...] += jnp.dot(hid, wd_ref[0], preferred_element_type=F32)

        @pl.when(f == nf - 1)
        def _():
            e = exp_ref[w]
            row = tile_ref[w] * MOE_TILE + lax.broadcasted_iota(I32, acc_ref.shape, 0)
            mine = (row >= gs_ref[e]) & (row < gs_ref[e + 1])
            part = jnp.where(mine, acc_ref[...], 0.0)

            @pl.when(first_ref[w] == 1)
            def _():
                ys_ref[...] = part

            @pl.when(first_ref[w] == 0)
            def _():
                ys_ref[...] += part


def moe_grouped(item_tile, item_expert, item_first, item_valid, group_starts, xs, g, wg, wu, wd, *, tf):
    r, d = xs.shape
    ff = wg.shape[2]
    n_items = item_tile.shape[0]
    return pl.pallas_call(
        _moe_grouped_kernel,
        out_shape=jax.ShapeDtypeStruct((r, d), F32),
        grid_spec=pltpu.PrefetchScalarGridSpec(
            num_scalar_prefetch=5,
            grid=(n_items, ff // tf),
            in_specs=[pl.BlockSpec((MOE_TILE, d), lambda w, f, it, ie, i1, iv, gs: (it[w], 0)),
                      pl.BlockSpec((1, d), lambda w, f, it, ie, i1, iv, gs: (0, 0)),
                      pl.BlockSpec((1, d, tf), lambda w, f, it, ie, i1, iv, gs: (ie[w], 0, f)),
                      pl.BlockSpec((1, d, tf), lambda w, f, it, ie, i1, iv, gs: (ie[w], 0, f)),
                      pl.BlockSpec((1, tf, d), lambda w, f, it, ie, i1, iv, gs: (ie[w], f, 0))],
            out_specs=pl.BlockSpec((MOE_TILE, d), lambda w, f, it, ie, i1, iv, gs: (it[w], 0)),
            scratch_shapes=[pltpu.VMEM((MOE_TILE, d), BF16), pltpu.VMEM((MOE_TILE, d), F32)]),
        compiler_params=_cparams(("arbitrary", "arbitrary")),
        name="moe_grouped",
    )(item_tile, item_expert, item_first, item_valid, group_starts, xs, g, wg, wu, wd)


def _moe_combine_kernel(d1_ref, d2_ref, x_ref, info_ref, ys_ref, o_ref, y1_ref, y2_ref, sem, *, ct):
    def issue(t, carry):
        _row_copy(ys_ref, d1_ref[0, 0, t], y1_ref, t, sem).start(priority=0)
        _row_copy(ys_ref, d2_ref[0, 0, t], y2_ref, t, sem).start(priority=1)
        return carry

    lax.fori_loop(0, ct, issue, 0, unroll=DMA_ISSUE_UNROLL)
    pltpu.make_async_copy(ys_ref.at[pl.ds(0, ct)], y1_ref, sem).wait()
    pltpu.make_async_copy(ys_ref.at[pl.ds(0, ct)], y2_ref, sem).wait()
    info = info_ref[...]
    g1 = info[:, R_G1:R_G1 + 1]
    g2 = info[:, R_G2:R_G2 + 1]
    o_ref[...] = x_ref[...] + (g1 * y1_ref[...] + g2 * y2_ref[...])


def moe_combine(x, info, ys, dest1, dest2, *, ct):
    m, d = x.shape
    smem_row = pl.BlockSpec((1, 1, ct), lambda i: (i, 0, 0), memory_space=pltpu.SMEM)
    return pl.pallas_call(
        functools.partial(_moe_combine_kernel, ct=ct),
        out_shape=jax.ShapeDtypeStruct((m, d), F32),
        grid=(m // ct,),
        in_specs=[smem_row, smem_row, pl.BlockSpec((ct, d), lambda i: (i, 0)),
                  pl.BlockSpec((ct, LANES), lambda i: (i, 0)), pl.BlockSpec(memory_space=pl.ANY)],
        out_specs=pl.BlockSpec((ct, d), lambda i: (i, 0)),
        scratch_shapes=[pltpu.VMEM((ct, d), F32), pltpu.VMEM((ct, d), F32), pltpu.SemaphoreType.DMA],
        compiler_params=_cparams(("arbitrary",)),
        name="moe_combine",
    )(dest1, dest2, x, info, ys)


def _moe_work_items(counts, n_tiles):
    n_exp = counts.shape[0]
    n_items = n_tiles + n_exp - 1
    ends = jnp.cumsum(counts)
    starts = ends - counts
    expert_of_row = lambda rr: jnp.minimum(jnp.sum(ends[None, :] <= rr[:, None], axis=1), n_exp - 1).astype(I32)
    t = jnp.arange(n_tiles, dtype=I32)
    e_lo = expert_of_row(t * MOE_TILE)
    e_hi = expert_of_row(t * MOE_TILE + (MOE_TILE - 1))
    per_tile = e_hi - e_lo + 1
    item_end = jnp.cumsum(per_tile)
    item_start = item_end - per_tile
    total = item_end[-1]
    w = jnp.minimum(jnp.arange(n_items, dtype=I32), total - 1)
    tile = jnp.sum(item_end[None, :] <= w[:, None], axis=1).astype(I32)
    expert = (e_lo[tile] + (w - item_start[tile])).astype(I32)
    first = (w == item_start[tile]).astype(I32)
    valid = (jnp.arange(n_items, dtype=I32) < total).astype(I32)
    group_starts = jnp.concatenate([starts, ends[-1:]]).astype(I32)
    return tile, expert, first, valid, group_starts


def moe_routed(x, g, w_hi, w_lo, b_pad, wg, wu, wd, *, tm):
    m, d = x.shape
    n_exp = wg.shape[0]
    assert (2 * m) % MOE_TILE == 0
    info, cnt = moe_router_rank(x, g, w_hi, w_lo, b_pad, tm=tm)
    counts = cnt[0, :n_exp].astype(I32)
    starts = jnp.cumsum(counts) - counts
    i1 = info[:, R_I1].astype(I32)
    i2 = info[:, R_I2].astype(I32)
    dest1 = (starts[i1] + info[:, R_RANK1].astype(I32)).reshape(m // tm, 1, tm)
    dest2 = (starts[i2] + info[:, R_RANK2].astype(I32)).reshape(m // tm, 1, tm)
    xs = moe_dispatch(x, dest1, dest2, ct=tm)
    items = _moe_work_items(counts, (2 * m) // MOE_TILE)
    ys = moe_grouped(*items, xs, g, wg, wu, wd, tf=FF_TILE)
    return moe_combine(x, info, ys, dest1, dest2, ct=tm)


PAGES_PER_STEP = 32
PAGE_SLOTS = 2
SDSA_GROUP = 4


def _fetch_pages(pt_ref, cache_ref, buf_ref, sem_ref, step, nj):
    b = step // nj
    j0 = (step % nj) * PAGES_PER_STEP
    slot = step % PAGE_SLOTS
    for i in range(PAGES_PER_STEP):
        pltpu.make_async_copy(cache_ref.at[pt_ref[b, j0 + i]], buf_ref.at[slot, i], sem_ref.at[slot]).start()


def _wait_pages(cache_ref, buf_ref, sem_ref, step):
    slot = step % PAGE_SLOTS
    pltpu.make_async_copy(cache_ref.at[pl.ds(0, PAGES_PER_STEP)], buf_ref.at[slot], sem_ref.at[slot]).wait()


def _page_pipeline_step(pt_ref, streams, nj):
    step = pl.program_id(0) * pl.num_programs(1) + pl.program_id(1)
    total = pl.num_programs(0) * pl.num_programs(1)

    @pl.when(step == 0)
    def _():
        for cache_ref, buf_ref, sem_ref in streams:
            _fetch_pages(pt_ref, cache_ref, buf_ref, sem_ref, step, nj)

    @pl.when(step + 1 < total)
    def _():
        for cache_ref, buf_ref, sem_ref in streams:
            _fetch_pages(pt_ref, cache_ref, buf_ref, sem_ref, step + 1, nj)

    for cache_ref, buf_ref, sem_ref in streams:
        _wait_pages(cache_ref, buf_ref, sem_ref, step)
    return step % PAGE_SLOTS


def _sdsa_index_kernel(pt_ref, qi_ref, wi_ref, kknew_ref, cache_ref, mpast_ref, mnew_ref,
                       keys_ref, pre_ref, pbuf_ref, psem_ref, *, n_pages, t_len, topk, nb):
    nj = n_pages // PAGES_PER_STEP
    lb = pl.program_id(1) // nj
    j = pl.program_id(1) % nj
    rows = pl.ds(pl.multiple_of(lb * t_len, t_len), t_len)
    last_step = pl.program_id(1) == nb * nj - 1
    slot = _page_pipeline_step(pt_ref, [(cache_ref, pbuf_ref, psem_ref)], nj)
    lane = lax.broadcasted_iota(I32, (t_len, LANES), 1)
    row = lax.broadcasted_iota(I32, (t_len, LANES), 0)
    lo = lane < HEAD_DIM

    qi = qi_ref[0]
    wi = wi_ref[0]
    qs, ws = [], []
    for h in range(IDX_HEADS):
        x = qi[:, (h // 2) * LANES:(h // 2 + 1) * LANES]
        if h % 2 == 1:
            x = pltpu.roll(x, HEAD_DIM, 1)
        qs.append(x[:, :IDX_DIM])
        ws.append(wi[:, IDX_DIM + h:IDX_DIM + h + 1])
    q32 = jnp.concatenate(qs, axis=0).astype(BF16)
    w32 = jnp.concatenate(ws, axis=0)

    def scores(dots):
        d = jnp.maximum(dots, 0.0) * w32
        s = d[0:t_len]
        for h in range(1, IDX_HEADS):
            s = s + d[h * t_len:(h + 1) * t_len]
        return jnp.where(s == 0.0, 0.0, s)

    pages_t = jnp.concatenate([pbuf_ref[slot, i].astype(BF16) for i in range(PAGES_PER_STEP)], axis=1)
    s_wide = scores(jnp.dot(q32, pages_t, preferred_element_type=F32))
    for i in range(PAGES_PER_STEP):
        keys_ref[j * PAGES_PER_STEP + i, rows, :] = s_wide[:, i * PAGE_SIZE:(i + 1) * PAGE_SIZE]

    @pl.when(j == nj - 1)
    def _():
        knew = jnp.concatenate([kknew_ref[0][:, :IDX_DIM], jnp.zeros((LANES - t_len, IDX_DIM), F32)], axis=0)
        s_new = jnp.where((lane <= row) & (lane < t_len), scores(_bdot_nt(q32, knew)), NEG)
        keys_ref[n_pages, rows, :] = s_new

    @pl.when(last_step)
    def _():
        nrow = nb * t_len
        own = lambda x, g: x[..., g * t_len:(g + 1) * t_len, :]

        def count_ge(cand):
            part = _sum_leading(jnp.where(keys_ref[...] >= cand, 1.0, 0.0))
            return jnp.sum(part, axis=-1, keepdims=True)

        thr = _kth_largest(count_ge, float(topk), (nrow, 1))
        n_ge = count_ge(thr)
        no_cut = jnp.max(jnp.abs(n_ge - float(topk))) == 0.0

        @pl.when(no_cut)
        def _():
            bias = jnp.where(keys_ref[...] >= thr, 0.0, NEG)
            for g in range(nb):
                mpast_ref[g] = own(bias[:n_pages], g)
                mnew_ref[g] = own(bias[n_pages], g)

        @pl.when(jnp.logical_not(no_cut))
        def _():
            keys = keys_ref[...]
            n_gt = jnp.sum(_sum_leading(jnp.where(keys > thr, 1.0, 0.0)), axis=-1, keepdims=True)
            need = float(topk) - n_gt
            eqf = jnp.where(keys == thr, 1.0, 0.0).astype(BF16).reshape((n_pages + 1) * nrow, LANES)
            pre_ref[...] = jnp.dot(eqf, _upper_tri_bf16(LANES), preferred_element_type=F32).reshape(
                n_pages + 1, nrow, LANES)

            def body(p, off):
                kp = keys_ref[p]
                pre = pre_ref[p]
                bias = jnp.where(kp == thr, jnp.where((pre + off) <= need, 0.0, NEG),
                                 jnp.where(kp > thr, 0.0, NEG))

                @pl.when(p < n_pages)
                def _():
                    for g in range(nb):
                        mpast_ref[g, p] = own(bias, g)

                @pl.when(p == n_pages)
                def _():
                    for g in range(nb):
                        mnew_ref[g] = own(bias, g)

                return off + pre[:, LANES - 1:LANES]

            lax.fori_loop(0, n_pages + 1, body, jnp.zeros((nrow, 1), F32))


def sdsa_index(page_table, qi, wi, kk_new, cache_kidx):
    bsz, t_len, _ = qi.shape
    n_pages = page_table.shape[1]
    topk = min(TOPK_MAX, (n_pages * PAGE_SIZE + t_len) // 4)
    nb = math.gcd(bsz, SDSA_GROUP)
    nj = n_pages // PAGES_PER_STEP
    kern = functools.partial(_sdsa_index_kernel, n_pages=n_pages, t_len=t_len, topk=topk, nb=nb)
    tok = lambda w: pl.BlockSpec((1, t_len, w), lambda gi, jj, pt: (gi * nb + jj // nj, 0, 0))
    return pl.pallas_call(
        kern,
        out_shape=(jax.ShapeDtypeStruct((bsz, n_pages, t_len, LANES), F32),
                   jax.ShapeDtypeStruct((bsz, t_len, LANES), F32)),
        grid_spec=pltpu.PrefetchScalarGridSpec(
            num_scalar_prefetch=1,
            grid=(bsz // nb, nb * nj),
            in_specs=[tok(256), tok(LANES), tok(LANES), pl.BlockSpec(memory_space=pl.ANY)],
            out_specs=(pl.BlockSpec((nb, n_pages, t_len, LANES), lambda gi, jj, pt: (gi, 0, 0, 0)),
                       pl.BlockSpec((nb, t_len, LANES), lambda gi, jj, pt: (gi, 0, 0))),
            scratch_shapes=[pltpu.VMEM((n_pages + 1, nb * t_len, LANES), F32),
                            pltpu.VMEM((n_pages + 1, nb * t_len, LANES), F32),
                            pltpu.VMEM((PAGE_SLOTS, PAGES_PER_STEP, IDX_DIM, PAGE_SIZE), F32),
                            pltpu.SemaphoreType.DMA((PAGE_SLOTS,))]),
        compiler_params=_cparams(("arbitrary", "arbitrary")),
        name="sdsa_index",
    )(page_table, qi, wi, kk_new, cache_kidx)


def _sdsa_attn_kernel(pt_ref, q_ref, knew_ref, vnew_ref, mpast_ref, mnew_ref, ck_ref, cv_ref,
                      o_ref, qg_ref, m_ref, l_ref, acc_ref, kbuf_ref, vbuf_ref, ksem_ref, vsem_ref, *, t_len, n_pages):
    j = pl.program_id(1)
    nj = pl.num_programs(1)
    slot = _page_pipeline_step(pt_ref, [(ck_ref, kbuf_ref, ksem_ref), (cv_ref, vbuf_ref, vsem_ref)],
                               n_pages // PAGES_PER_STEP)
    rpg = N_HEADS // N_KV
    reps = rpg

    def group_lanes(x, g):
        if g == 1:
            x = pltpu.roll(x, HEAD_DIM, 1)
        return x[:, :HEAD_DIM]

    @pl.when(j == 0)
    def _():
        q = q_ref[0]
        for g in range(N_KV):
            parts = [group_lanes(q[:, r * LANES:(r + 1) * LANES], g) for r in range(rpg)]
            qg_ref[g] = (jnp.concatenate(parts, axis=0) * (HEAD_DIM ** -0.5)).astype(BF16)
        m_ref[...] = jnp.full(m_ref.shape, NEG, F32)
        l_ref[...] = jnp.zeros(l_ref.shape, F32)
        acc_ref[...] = jnp.zeros(acc_ref.shape, F32)

    def update(g, s, pv):
        m_old = m_ref[g]
        m_cur = jnp.maximum(m_old, jnp.max(s, axis=-1, keepdims=True))
        alpha = jnp.exp(m_old - m_cur)
        p = jnp.exp(s - m_cur)
        m_ref[g] = m_cur
        l_ref[g] = l_ref[g] * alpha + jnp.sum(p, axis=-1, keepdims=True)
        acc_ref[g] = acc_ref[g] * alpha + pv(p.astype(BF16))

    bias8 = jnp.concatenate([mpast_ref[0, i] for i in range(PAGES_PER_STEP)], axis=1)
    bias_wide = jnp.concatenate([bias8] * reps, axis=0)
    for g in range(N_KV):
        k_wide = jnp.concatenate([kbuf_ref[slot, i, g].astype(BF16) for i in range(PAGES_PER_STEP)], axis=1)
        v_wide = jnp.concatenate([vbuf_ref[slot, i, g].astype(BF16) for i in range(PAGES_PER_STEP)], axis=1)
        s = jnp.dot(qg_ref[g], k_wide, preferred_element_type=F32) + bias_wide
        update(g, s, lambda p, v_wide=v_wide: _bdot_nt(p, v_wide))

    @pl.when(j == nj - 1)
    def _():
        pad = jnp.zeros((LANES - t_len, HEAD_DIM), F32)
        bias = jnp.concatenate([mnew_ref[0]] * reps, axis=0)
        for g in range(N_KV):
            knew = jnp.concatenate([group_lanes(knew_ref[0], g), pad], axis=0)
            vnew = jnp.concatenate([group_lanes(vnew_ref[0], g), pad], axis=0)
            update(g, _bdot_nt(qg_ref[g], knew) + bias, lambda p, vnew=vnew: _bdot(p, vnew))
        o0 = acc_ref[0] / l_ref[0]
        o1 = acc_ref[1] / l_ref[1]
        for r in range(rpg):
            o_ref[0, :, r * LANES:(r + 1) * LANES] = jnp.concatenate(
                [o0[r * t_len:(r + 1) * t_len], o1[r * t_len:(r + 1) * t_len]], axis=-1)


def sdsa_attn(page_table, q, k_new, v_new, mask_past, mask_new, cache_k, cache_v):
    bsz, t_len, _ = q.shape
    n_pages = page_table.shape[1]
    kern = functools.partial(_sdsa_attn_kernel, t_len=t_len, n_pages=n_pages)
    tok = lambda w: pl.BlockSpec((1, t_len, w), lambda b_, j, pt: (b_, 0, 0))
    rows = (N_HEADS // N_KV) * t_len
    page_buf = pltpu.VMEM((PAGE_SLOTS, PAGES_PER_STEP, N_KV, HEAD_DIM, PAGE_SIZE), F32)
    return pl.pallas_call(
        kern,
        out_shape=jax.ShapeDtypeStruct((bsz, t_len, 512), F32),
        grid_spec=pltpu.PrefetchScalarGridSpec(
            num_scalar_prefetch=1,
            grid=(bsz, n_pages // PAGES_PER_STEP),
            in_specs=[tok(512), tok(LANES), tok(LANES),
                      pl.BlockSpec((1, PAGES_PER_STEP, t_len, LANES), lambda b_, j, pt: (b_, j, 0, 0)),
                      tok(LANES), pl.BlockSpec(memory_space=pl.ANY), pl.BlockSpec(memory_space=pl.ANY)],
            out_specs=pl.BlockSpec((1, t_len, 512), lambda b_, j, pt: (b_, 0, 0)),
            scratch_shapes=[pltpu.VMEM((N_KV, rows, HEAD_DIM), BF16), pltpu.VMEM((N_KV, rows, 1), F32),
                            pltpu.VMEM((N_KV, rows, 1), F32), pltpu.VMEM((N_KV, rows, HEAD_DIM), F32),
                            page_buf, page_buf,
                            pltpu.SemaphoreType.DMA((PAGE_SLOTS,)), pltpu.SemaphoreType.DMA((PAGE_SLOTS,))]),
        compiler_params=_cparams(("arbitrary", "arbitrary")),
        name="sdsa_attn",
    )(page_table, q, k_new, v_new, mask_past, mask_new, cache_k, cache_v)


HEAD_ORDER = (0, 4, 1, 5, 2, 6, 3, 7)


def _pad_cols(w, n):
    return jnp.pad(w, ((0, 0), (0, n - w.shape[1])))


def _row(v):
    return v.reshape(1, -1).astype(F32)


def _rope_tables(pos):
    half = ROT_DIM // 2
    inv = ROPE_THETA ** (-jnp.arange(half, dtype=F32) / half)
    ang = pos.astype(F32)[:, None] * inv[None, :]
    cos, sin = jnp.cos(ang), jnp.sin(ang)
    n = pos.shape[0]
    rest = HEAD_DIM - ROT_DIM
    c64 = jnp.concatenate([cos, cos, jnp.ones((n, rest), F32)], axis=1)
    s64 = jnp.concatenate([-sin, sin, jnp.zeros((n, rest), F32)], axis=1)
    return jnp.tile(c64, (1, 2)), jnp.tile(s64, (1, 2))


def _even_layer(x, prev_a, prev_ssm, prev_b, lc, p, *, tm, tl_a):
    bsz, L, d = x.shape
    m = bsz * L
    x2 = x.reshape(m, d)
    zmain, ztail = norm_proj(x2, p["e_norm_mix"], p["e_w_main"], p["e_w_tail"], tm=tm, tn=E_MAIN)
    zmain3 = zmain.reshape(bsz, L, E_MAIN)
    prev32 = jnp.pad(prev_a, ((0, 0), (CARRY_A - (CONV_A - 1), 0), (0, 0)))
    ya, new_a = conv_a(zmain3, prev32, p["e_conv_a_w"], p["e_conv_a_b"], p["e_ln_a_g"], p["e_ln_a_b"], tl=tl_a)
    prev8 = jnp.pad(prev_b, ((0, 0), (CARRY_B - (SSM_CONV - 1), 0), (0, 0)))
    yb, new_b, hout = ssd_mixer(zmain3, ztail.reshape(bsz, L, LANES), prev8,
                                prev_ssm.reshape(bsz, D_INNER, SSM_STATE),
                                p["e_conv_b_w"], p["e_conv_b_b"], p["e_dt_bias"], p["e_a_neg"], p["e_expand"],
                                p["e_dskip"], p["e_ssm_norm"], lc=lc)
    x2 = out_proj(x2, ya.reshape(m, D_A), yb.reshape(m, D_INNER), p["e_wo_a"], p["e_wo_b"], tm=min(m, 1024))
    x2 = swiglu_ffn(x2, p["e_norm_ffn"], p["e_w_gate"], p["e_w_up"], p["e_w_down"], tm=min(m, 512), tf=FF_TILE)
    return (x2.reshape(bsz, L, d), new_a[:, CARRY_A - (CONV_A - 1):], hout.reshape(bsz, SSM_HEADS, SSM_HEADDIM, SSM_STATE),
            new_b[:, CARRY_B - (SSM_CONV - 1):])


def _odd_project(x2, cos_t, sin_t, p, *, tm):
    zmain, ztail = norm_proj(x2, p["o_norm_mix"], p["o_w_main"], p["o_w_tail"], tm=tm, tn=O_MAIN)
    q, k, v, qi, kiw, k16, v16, kk16 = odd_post(zmain, ztail, cos_t, sin_t, p["o_q_norm"], p["o_k_norm"],
                                                p["o_kidx_g"], p["o_kidx_b"], p["o_bd"], tm=tm)
    return zmain, q, k, v, qi, kiw, (k16, v16, kk16)


def _odd_tail(x2, att, zmain3, prev_d, p, *, tm, tl_d):
    bsz, L, _ = zmain3.shape
    m = bsz * L
    prev8 = jnp.pad(prev_d, ((0, 0), (CARRY_D - (CONV_D - 1), 0), (0, 0)))
    dm, new_d = conv_d(zmain3, prev8, p["o_conv_d_w"], tl=tl_d)
    x2 = out_proj(x2, att.reshape(m, 512), dm.reshape(m, D_D), p["o_wo_a"], p["o_wo_b"], tm=min(m, 1024))
    if (2 * m) % MOE_TILE == 0 and 2 * m >= N_EXPERTS * MOE_TILE:
        x2 = moe_routed(x2, p["o_norm_ffn"], p["o_wr_hi"], p["o_wr_lo"], p["o_br"],
                        p["o_we_gate"], p["o_we_up"], p["o_we_down"], tm=tm)
    else:
        xn, comb = moe_router(x2, p["o_norm_ffn"], p["o_wr_hi"], p["o_wr_lo"], p["o_br"], tm=tm)
        x2 = moe_dense(x2, xn, comb, p["o_we_gate"], p["o_we_up"], p["o_we_down"], tm=min(m, 1024), tf=FF_TILE)
    return x2, new_d[:, CARRY_D - (CONV_D - 1):]


def kernel(x_prompt, x_sample, state_conv_a, state_ssm, state_conv_b, cache_k, cache_v, cache_kidx, state_conv_d,
           page_table, e_norm_mix, e_w_in, e_conv_a_w, e_conv_a_b, e_ln_a_g, e_ln_a_b, e_conv_b_w, e_conv_b_b,
           e_dt_bias, e_a_log, e_d_skip, e_ssm_norm, e_w_out, e_norm_ffn, e_w_gate, e_w_up, e_w_down,
           o_norm_mix, o_w_in, o_q_norm, o_k_norm, o_kidx_g, o_kidx_b, o_conv_d_w, o_w_out, o_norm_ffn,
           o_w_router, o_b_router, o_we_gate, o_we_up, o_we_down):
    bp, s_len, d = x_prompt.shape
    bd, t_len, _ = x_sample.shape
    n_pairs = e_w_in.shape[0]
    past = page_table.shape[1] * PAGE_SIZE
    n_pool = cache_k.shape[1]
    xp, xs = x_prompt, x_sample
    outs_p = [[] for _ in range(7)]
    outs_s = [[] for _ in range(7)]
    cos_p, sin_p = _rope_tables(jnp.arange(s_len))
    cos_s, sin_s = _rope_tables(jnp.tile(past + jnp.arange(t_len), bd))
    perm = np.concatenate([np.arange(h * HEAD_DIM, (h + 1) * HEAD_DIM) for h in HEAD_ORDER])
    expand = (jnp.arange(LANES)[:, None] == (jnp.arange(D_INNER)[None, :] // SSM_HEADDIM)).astype(BF16)
    blk = jnp.arange(LANES) // HEAD_DIM
    bdiag = jnp.where(blk[:, None] == blk[None, :], 1.0 / HEAD_DIM, 0.0).astype(BF16)
    for i in range(n_pairs):
        w = e_w_in[i]
        p = {
            "e_norm_mix": _row(e_norm_mix[i]),
            "e_w_main": jnp.concatenate([w[:, 2048:3584], w[:, 0:512], w[:, 1024:2048], w[:, 512:1024]],
                                        axis=1).astype(BF16),
            "e_w_tail": _pad_cols(w[:, 3584:3600], LANES).astype(BF16),
            "e_conv_a_w": e_conv_a_w[i], "e_conv_a_b": _row(e_conv_a_b[i]),
            "e_ln_a_g": _row(e_ln_a_g[i]), "e_ln_a_b": _row(e_ln_a_b[i]),
            "e_conv_b_w": e_conv_b_w[i], "e_conv_b_b": _row(e_conv_b_b[i]),
            "e_dt_bias": _pad_cols(_row(e_dt_bias[i]), LANES),
            "e_a_neg": _pad_cols(_row(-jnp.exp(e_a_log[i].astype(F32))), LANES),
            "e_expand": expand,
            "e_dskip": _row(jnp.repeat(e_d_skip[i], SSM_HEADDIM)),
            "e_ssm_norm": _row(e_ssm_norm[i]),
            "e_wo_a": e_w_out[i][:D_A].astype(BF16), "e_wo_b": e_w_out[i][D_A:].astype(BF16),
            "e_norm_ffn": _row(e_norm_ffn[i]),
            "e_w_gate": e_w_gate[i].astype(BF16), "e_w_up": e_w_up[i].astype(BF16),
            "e_w_down": e_w_down[i].astype(BF16),
        }
        zero_a = jnp.zeros((bp, CONV_A - 1, D_A), F32)
        zero_h = jnp.zeros((bp, SSM_HEADS, SSM_HEADDIM, SSM_STATE), F32)
        zero_b = jnp.zeros((bp, SSM_CONV - 1, CONV_DIM), F32)
        xp, ca, sm, cb = _even_layer(xp, zero_a, zero_h, zero_b, min(SSD_CHUNK, s_len), p, tm=512, tl_a=256)
        outs_p[0].append(ca); outs_p[1].append(sm); outs_p[2].append(cb)
        xs, ca, sm, cb = _even_layer(xs, state_conv_a[i], state_ssm[i], state_conv_b[i], t_len, p,
                                     tm=bd * t_len, tl_a=t_len)
        outs_s[0].append(ca); outs_s[1].append(sm); outs_s[2].append(cb)

        w = o_w_in[i]
        wo = o_w_out[i]
        p = {
            "o_norm_mix": _row(o_norm_mix[i]),
            "o_w_main": jnp.concatenate([w[:, 0:512][:, perm], w[:, 512:1024], w[:, 1092:2628]], axis=1).astype(BF16),
            "o_w_tail": _pad_cols(w[:, 1024:1092], LANES).astype(BF16),
            "o_q_norm": _row(jnp.tile(o_q_norm[i], 2)), "o_k_norm": _row(jnp.tile(o_k_norm[i], 2)),
            "o_kidx_g": _pad_cols(_row(o_kidx_g[i]), LANES), "o_kidx_b": _pad_cols(_row(o_kidx_b[i]), LANES),
            "o_bd": bdiag,
            "o_conv_d_w": o_conv_d_w[i],
            "o_wo_a": wo[:512][perm].astype(BF16), "o_wo_b": wo[512:].astype(BF16),
            "o_norm_ffn": _row(o_norm_ffn[i]),
            "o_br": jnp.concatenate([_row(o_b_router[i]), jnp.full((1, LANES - N_EXPERTS), NEG, F32)], axis=1),
            "o_we_gate": o_we_gate[i].astype(BF16), "o_we_up": o_we_up[i].astype(BF16),
            "o_we_down": o_we_down[i].astype(BF16),
        }
        wr = _pad_cols(o_w_router[i], LANES)
        p["o_wr_hi"] = wr.astype(BF16)
        p["o_wr_lo"] = (wr - p["o_wr_hi"].astype(F32)).astype(BF16)

        mp = bp * s_len
        x2 = xp.reshape(mp, d)
        zmain, q, k, v, qi, kiw, (k16, v16, kk16) = _odd_project(x2, cos_p, sin_p, p, tm=512)
        r3 = lambda a: a.reshape(bp, s_len, a.shape[-1])
        att = dsa_prompt(r3(q), r3(qi), r3(kiw), r3(k16), r3(v16), r3(kk16))
        x2, cd = _odd_tail(x2, att, r3(zmain), jnp.zeros((bp, CONV_D - 1, D_D), F32), p, tm=512, tl_d=512)
        xp = x2.reshape(bp, s_len, d)
        outs_p[3].append(k.reshape(bp, s_len, N_KV, HEAD_DIM)); outs_p[4].append(v.reshape(bp, s_len, N_KV, HEAD_DIM))
        outs_p[5].append(kiw[:, :IDX_DIM].reshape(bp, s_len, IDX_DIM)); outs_p[6].append(cd)

        ms = bd * t_len
        x2 = xs.reshape(ms, d)
        zmain, q, k, v, qi, kiw, _ = _odd_project(x2, cos_s, sin_s, p, tm=ms)
        r3 = lambda a: a.reshape(bd, t_len, a.shape[-1])
        mask_past, mask_new = sdsa_index(page_table, r3(qi), r3(kiw), r3(kiw),
                                         jnp.transpose(cache_kidx[i], (0, 2, 1)))
        att = sdsa_attn(page_table, r3(q), r3(k), r3(v), mask_past, mask_new,
                        jnp.transpose(cache_k[i], (0, 2, 3, 1)), jnp.transpose(cache_v[i], (0, 2, 3, 1)))
        x2, cd = _odd_tail(x2, att, r3(zmain), state_conv_d[i], p, tm=ms, tl_d=t_len)
        xs = x2.reshape(bd, t_len, d)
        outs_s[3].append(k.reshape(bd, t_len, N_KV, HEAD_DIM)); outs_s[4].append(v.reshape(bd, t_len, N_KV, HEAD_DIM))
        outs_s[5].append(kiw[:, :IDX_DIM].reshape(bd, t_len, IDX_DIM)); outs_s[6].append(cd)
    return (xp, xs) + tuple(jnp.stack(o) for o in outs_p) + tuple(jnp.stack(o) for o in outs_s)
```

```python
import functools
import math

import jax
import jax.numpy as jnp
import numpy as np
from jax import lax
from jax.experimental import pallas as pl
from jax.experimental.pallas import tpu as pltpu

F32 = jnp.float32
BF16 = jnp.bfloat16
I32 = jnp.int32

D_MODEL = 1024
D_A = 512
CONV_A = 31
D_INNER = 1024
SSM_HEADDIM = 64
SSM_HEADS = 16
SSM_GROUPS = 2
SSM_STATE = 128
SSM_CONV = 4
CONV_DIM = D_INNER + 2 * SSM_GROUPS * SSM_STATE
SSD_CHUNK = 128
HEAD_DIM = 64
N_HEADS = 8
N_KV = 2
ROT_DIM = 16
ROPE_THETA = 500000.0
IDX_HEADS = 4
IDX_DIM = 64
TOPK_MAX = 256
Q_BLOCK = 128
D_D = 512
CONV_D = 3
D_FF = 2816
N_EXPERTS = 8
PAGE_SIZE = 128
EPS = 1e-6
NEG = -1e30

LANES = 128
SUBLANES = 8
VMEM_LIMIT = 56 * 1024 * 1024

E_MAIN = D_A + D_A + D_INNER + CONV_DIM
E_COL_XBC, E_COL_VAL, E_COL_Z, E_COL_GATE = 0, 1536, 2048, 3072
O_MAIN = 512 + 128 + 128 + 256 + 3 * D_D


def _cparams(sem):
    return pltpu.CompilerParams(dimension_semantics=sem, vmem_limit_bytes=VMEM_LIMIT)


def _bdot(a, b):
    return jnp.dot(a.astype(BF16), b.astype(BF16), preferred_element_type=F32)


def _bdot_nt(a, b):
    return lax.dot_general(a.astype(BF16), b.astype(BF16), (((1,), (1,)), ((), ())),
                           preferred_element_type=F32)


def _split3(a):
    hi = a.astype(BF16)
    r1 = a - hi.astype(F32)
    mid = r1.astype(BF16)
    lo = (r1 - mid.astype(F32)).astype(BF16)
    return hi, mid, lo


def _dot_exact_rhs(a, b_bf16):
    hi, mid, lo = _split3(a)
    out = jnp.dot(lo, b_bf16, preferred_element_type=F32)
    out = out + jnp.dot(mid, b_bf16, preferred_element_type=F32)
    return out + jnp.dot(hi, b_bf16, preferred_element_type=F32)


def _dot_exact_lhs(a_bf16, b):
    hi, mid, lo = _split3(b)
    out = jnp.dot(a_bf16, lo, preferred_element_type=F32)
    out = out + jnp.dot(a_bf16, mid, preferred_element_type=F32)
    return out + jnp.dot(a_bf16, hi, preferred_element_type=F32)


def _sigmoid(x):
    return 1.0 / (1.0 + jnp.exp(-x))


def _silu(x):
    return x * _sigmoid(x)


def _softplus(x):
    return jnp.maximum(x, 0.0) + jnp.log(1.0 + jnp.exp(-jnp.abs(x)))


def _norm_proj_kernel(x_ref, g_ref, w_ref, wt_ref, main_ref, tail_ref, xn_ref):
    j = pl.program_id(1)

    @pl.when(j == 0)
    def _():
        x = x_ref[...]
        y = x * lax.rsqrt(jnp.mean(x * x, axis=-1, keepdims=True) + EPS) * g_ref[...]
        xn = y.astype(BF16)
        xn_ref[...] = xn
        tail_ref[...] = jnp.dot(xn, wt_ref[...], preferred_element_type=F32)

    main_ref[...] = jnp.dot(xn_ref[...], w_ref[...], preferred_element_type=F32)


def norm_proj(x, g, w_main, w_tail, *, tm, tn):
    m, k = x.shape
    n = w_main.shape[1]
    return pl.pallas_call(
        _norm_proj_kernel,
        out_shape=(jax.ShapeDtypeStruct((m, n), F32), jax.ShapeDtypeStruct((m, LANES), F32)),
        grid=(m // tm, n // tn),
        in_specs=[pl.BlockSpec((tm, k), lambda i, j: (i, 0)),
                  pl.BlockSpec((1, k), lambda i, j: (0, 0)),
                  pl.BlockSpec((k, tn), lambda i, j: (0, j)),
                  pl.BlockSpec((k, LANES), lambda i, j: (0, 0))],
        out_specs=(pl.BlockSpec((tm, tn), lambda i, j: (i, j)),
                   pl.BlockSpec((tm, LANES), lambda i, j: (i, 0))),
        scratch_shapes=[pltpu.VMEM((tm, k), BF16)],
        compiler_params=_cparams(("arbitrary", "arbitrary")),
        name="norm_proj",
    )(x, g, w_main, w_tail)


def _dwconv_from_buf(buf_ref, w_ref, kw, carry, tl):
    acc = None
    for k in range(kw):
        start = carry - (kw - 1) + k
        term = buf_ref[pl.ds(start, tl), :] * w_ref[pl.ds(k, 1), :]
        acc = term if acc is None else acc + term
    return acc


def _dwconv_phased(buf_ref, w_ref, kw, carry, tl):
    base = carry - (kw - 1)
    tz = tl + 2 * SUBLANES
    acc = None
    for b in range(SUBLANES):
        z = None
        for a in range((kw - 1 - b) // SUBLANES + 1):
            term = buf_ref[pl.ds(SUBLANES * a, tz), :] * w_ref[pl.ds(SUBLANES * a + b, 1), :]
            z = term if z is None else z + term
        shifted = z[base + b:base + b + tl]
        acc = shifted if acc is None else acc + shifted
    return acc


PHASE_PAD = 8


CARRY_A = 32


def _conv_a_kernel(val_ref, gate_ref, prev_ref, w_ref, b_ref, lg_ref, lb_ref, ya_ref, new_ref, buf_ref, *, tl):
    l = pl.program_id(1)

    @pl.when(l == 0)
    def _():
        buf_ref[pl.ds(0, CARRY_A), :] = prev_ref[0]
        buf_ref[pl.ds(CARRY_A + tl, PHASE_PAD), :] = jnp.zeros((PHASE_PAD, D_A), F32)

    u = val_ref[0] * _sigmoid(gate_ref[0])
    buf_ref[pl.ds(CARRY_A, tl), :] = u
    y = _dwconv_phased(buf_ref, w_ref, CONV_A, CARRY_A, tl) + b_ref[...]
    mu = jnp.mean(y, axis=-1, keepdims=True)
    yc = y - mu
    var = jnp.mean(yc * yc, axis=-1, keepdims=True)
    yn = yc * lax.rsqrt(var + EPS) * lg_ref[...] + lb_ref[...]
    ya_ref[0] = _silu(yn)
    hist = buf_ref[pl.ds(tl, CARRY_A), :]
    new_ref[0] = hist
    buf_ref[pl.ds(0, CARRY_A), :] = hist


def conv_a(zmain, prev32, w, b, lg, lb, *, tl):
    bsz, L, _ = zmain.shape
    kern = functools.partial(_conv_a_kernel, tl=tl)
    return pl.pallas_call(
        kern,
        out_shape=(jax.ShapeDtypeStruct((bsz, L, D_A), F32), jax.ShapeDtypeStruct((bsz, CARRY_A, D_A), F32)),
        grid=(bsz, L // tl),
        in_specs=[pl.BlockSpec((1, tl, D_A), lambda b_, l: (b_, l, E_COL_VAL // D_A)),
                  pl.BlockSpec((1, tl, D_A), lambda b_, l: (b_, l, E_COL_GATE // D_A)),
                  pl.BlockSpec((1, CARRY_A, D_A), lambda b_, l: (b_, 0, 0)),
                  pl.BlockSpec((CONV_A, D_A), lambda b_, l: (0, 0)),
                  pl.BlockSpec((1, D_A), lambda b_, l: (0, 0)),
                  pl.BlockSpec((1, D_A), lambda b_, l: (0, 0)),
                  pl.BlockSpec((1, D_A), lambda b_, l: (0, 0))],
        out_specs=(pl.BlockSpec((1, tl, D_A), lambda b_, l: (b_, l, 0)),
                   pl.BlockSpec((1, CARRY_A, D_A), lambda b_, l: (b_, 0, 0))),
        scratch_shapes=[pltpu.VMEM((CARRY_A + tl + PHASE_PAD, D_A), F32)],
        compiler_params=_cparams(("arbitrary", "arbitrary")),
        name="conv_a",
    )(zmain, zmain, prev32, w, b, lg, lb)


CARRY_B = 8
SSD_LP = 128


def _ssd_kernel(xbc_ref, z_ref, dt_ref, prevb_ref, h0_ref, cw_ref, cb_ref, dtb_ref, a_ref, expand_ref,
                dskip_ref, nrm_ref, y_ref, newb_ref, hout_ref, buf_ref, ht_ref, *, lc):
    c = pl.program_id(1)
    nc = pl.num_programs(1)
    lp = SSD_LP

    @pl.when(c == 0)
    def _():
        buf_ref[pl.ds(0, CARRY_B), :] = prevb_ref[0]
        ht_ref[...] = h0_ref[0].T
        if lc < lp:
            buf_ref[pl.ds(CARRY_B + lc, lp - lc), :] = jnp.zeros((lp - lc, CONV_DIM), F32)

    buf_ref[pl.ds(CARRY_B, lc), :] = xbc_ref[0]
    xbc = _silu(_dwconv_from_buf(buf_ref, cw_ref, SSM_CONV, CARRY_B, lp) + cb_ref[...])
    hist = buf_ref[pl.ds(lc, CARRY_B), :]
    newb_ref[0] = hist
    buf_ref[pl.ds(0, CARRY_B), :] = hist

    xs = xbc[:, :D_INNER]
    if lc < lp:
        dt_raw = jnp.concatenate([dt_ref[0], jnp.zeros((lp - lc, LANES), F32)], axis=0)
        zg = jnp.concatenate([z_ref[0], jnp.zeros((lp - lc, D_INNER), F32)], axis=0)
    else:
        dt_raw = dt_ref[0]
        zg = z_ref[0]
    dt = _softplus(dt_raw + dtb_ref[...])
    if lc < lp:
        row = lax.broadcasted_iota(I32, (lp, LANES), 0)
        dt = jnp.where(row < lc, dt, 0.0)
    lane = lax.broadcasted_iota(I32, (lp, LANES), 1)
    dt = jnp.where(lane < SSM_HEADS, dt, 0.0)
    da = dt * a_ref[...]

    r_i = lax.broadcasted_iota(I32, (lp, lp), 0)
    c_i = lax.broadcasted_iota(I32, (lp, lp), 1)
    causal = r_i >= c_i
    tri = jnp.where(causal, 1.0, 0.0).astype(BF16)
    a_col = _dot_exact_lhs(tri, da)
    a_row = a_col.T
    dt_row = dt.T
    a_last = a_col[lp - 1:lp, :]

    expand = expand_ref[...]
    a_exp = _dot_exact_rhs(a_col, expand)
    dt_exp = _dot_exact_rhs(dt, expand)
    al_exp = _dot_exact_rhs(a_last, expand)

    ht = ht_ref[...]
    y_parts = []
    for g in range(SSM_GROUPS):
        bm = xbc[:, D_INNER + g * SSM_STATE: D_INNER + (g + 1) * SSM_STATE]
        cm = xbc[:, D_INNER + (SSM_GROUPS + g) * SSM_STATE: D_INNER + (SSM_GROUPS + g + 1) * SSM_STATE]
        cb = _bdot_nt(cm, bm)
        gs = slice(g * 512, (g + 1) * 512)
        y_inter = _bdot(cm, ht[:, gs]) * jnp.exp(a_exp[:, gs])
        heads = []
        for r in range(SSM_HEADS // SSM_GROUPS):
            h = g * (SSM_HEADS // SSM_GROUPS) + r
            seg = a_col[:, h:h + 1] - a_row[h:h + 1, :]
            decay = jnp.where(causal, jnp.exp(jnp.where(causal, seg, 0.0)), 0.0)
            wts = cb * decay * dt_row[h:h + 1, :]
            heads.append(_bdot(wts, xs[:, h * SSM_HEADDIM:(h + 1) * SSM_HEADDIM]))
        y_parts.append(jnp.concatenate(heads, axis=-1) + y_inter)
        xw = xs[:, gs] * (jnp.exp(al_exp[:, gs] - a_exp[:, gs]) * dt_exp[:, gs])
        ht_ref[:, gs] = ht[:, gs] * jnp.exp(al_exp[:, gs]) + _bdot(bm.T, xw)
    y = jnp.concatenate(y_parts, axis=-1)
    y = y + dskip_ref[...] * xs
    y = y * _silu(zg)
    outs = []
    for g in range(SSM_GROUPS):
        yg = y[:, g * 512:(g + 1) * 512]
        outs.append(yg * lax.rsqrt(jnp.mean(yg * yg, axis=-1, keepdims=True) + EPS))
    yb = jnp.concatenate(outs, axis=-1) * nrm_ref[...]
    y_ref[0] = yb[:lc, :]

    @pl.when(c == nc - 1)
    def _():
        hout_ref[0] = ht_ref[...].T


def ssd_mixer(zmain, ztail, prev8, h0, cw, cb, dtb, a_neg, expand, dskip_exp, nrm, *, lc):
    bsz, L, _ = zmain.shape
    kern = functools.partial(_ssd_kernel, lc=lc)
    return pl.pallas_call(
        kern,
        out_shape=(jax.ShapeDtypeStruct((bsz, L, D_INNER), F32),
                   jax.ShapeDtypeStruct((bsz, CARRY_B, CONV_DIM), F32),
                   jax.ShapeDtypeStruct((bsz, D_INNER, SSM_STATE), F32)),
        grid=(bsz, L // lc),
        in_specs=[pl.BlockSpec((1, lc, CONV_DIM), lambda b_, c: (b_, c, E_COL_XBC // CONV_DIM)),
                  pl.BlockSpec((1, lc, D_INNER), lambda b_, c: (b_, c, E_COL_Z // D_INNER)),
                  pl.BlockSpec((1, lc, LANES), lambda b_, c: (b_, c, 0)),
                  pl.BlockSpec((1, CARRY_B, CONV_DIM), lambda b_, c: (b_, 0, 0)),
                  pl.BlockSpec((1, D_INNER, SSM_STATE), lambda b_, c: (b_, 0, 0)),
                  pl.BlockSpec((SSM_CONV, CONV_DIM), lambda b_, c: (0, 0)),
                  pl.BlockSpec((1, CONV_DIM), lambda b_, c: (0, 0)),
                  pl.BlockSpec((1, LANES), lambda b_, c: (0, 0)),
                  pl.BlockSpec((1, LANES), lambda b_, c: (0, 0)),
                  pl.BlockSpec((LANES, D_INNER), lambda b_, c: (0, 0)),
                  pl.BlockSpec((1, D_INNER), lambda b_, c: (0, 0)),
                  pl.BlockSpec((1, D_INNER), lambda b_, c: (0, 0))],
        out_specs=(pl.BlockSpec((1, lc, D_INNER), lambda b_, c: (b_, c, 0)),
                   pl.BlockSpec((1, CARRY_B, CONV_DIM), lambda b_, c: (b_, 0, 0)),
                   pl.BlockSpec((1, D_INNER, SSM_STATE), lambda b_, c: (b_, 0, 0))),
        scratch_shapes=[pltpu.VMEM((CARRY_B + SSD_LP, CONV_DIM), F32),
                        pltpu.VMEM((SSM_STATE, D_INNER), F32)],
        compiler_params=_cparams(("arbitrary", "arbitrary")),
        name="ssd_mixer",
    )(zmain, zmain, ztail, prev8, h0, cw, cb, dtb, a_neg, expand, dskip_exp, nrm)


def _out_proj_kernel(x_ref, a_ref, b_ref, wa_ref, wb_ref, o_ref):
    acc = jnp.dot(a_ref[...].astype(BF16), wa_ref[...], preferred_element_type=F32)
    acc = acc + jnp.dot(b_ref[...].astype(BF16), wb_ref[...], preferred_element_type=F32)
    o_ref[...] = x_ref[...] + acc


def out_proj(x, a, b, wa, wb, *, tm):
    m, d = x.shape
    ka, kb = a.shape[1], b.shape[1]
    return pl.pallas_call(
        _out_proj_kernel,
        out_shape=jax.ShapeDtypeStruct((m, d), F32),
        grid=(m // tm,),
        in_specs=[pl.BlockSpec((tm, d), lambda i: (i, 0)),
                  pl.BlockSpec((tm, ka), lambda i: (i, 0)),
                  pl.BlockSpec((tm, kb), lambda i: (i, 0)),
                  pl.BlockSpec((ka, d), lambda i: (0, 0)),
                  pl.BlockSpec((kb, d), lambda i: (0, 0))],
        out_specs=pl.BlockSpec((tm, d), lambda i: (i, 0)),
        compiler_params=_cparams(("arbitrary",)),
        name="out_proj",
    )(x, a, b, wa, wb)


def _swiglu_kernel(x_ref, g_ref, wg_ref, wu_ref, wd_ref, o_ref, xn_ref, acc_ref):
    f = pl.program_id(1)
    nf = pl.num_programs(1)

    @pl.when(f == 0)
    def _():
        x = x_ref[...]
        y = x * lax.rsqrt(jnp.mean(x * x, axis=-1, keepdims=True) + EPS) * g_ref[...]
        xn_ref[...] = y.astype(BF16)
        acc_ref[...] = jnp.zeros_like(acc_ref)

    xn = xn_ref[...]
    gate = jnp.dot(xn, wg_ref[...], preferred_element_type=F32)
    up = jnp.dot(xn, wu_ref[...], preferred_element_type=F32)
    hid = (_silu(gate) * up).astype(BF16)
    acc_ref[...] += jnp.dot(hid, wd_ref[...], preferred_element_type=F32)

    @pl.when(f == nf - 1)
    def _():
        o_ref[...] = x_ref[...] + acc_ref[...]


def swiglu_ffn(x, g, wg, wu, wd, *, tm, tf):
    m, d = x.shape
    ff = wg.shape[1]
    return pl.pallas_call(
        _swiglu_kernel,
        out_shape=jax.ShapeDtypeStruct((m, d), F32),
        grid=(m // tm, ff // tf),
        in_specs=[pl.BlockSpec((tm, d), lambda i, f: (i, 0)),
                  pl.BlockSpec((1, d), lambda i, f: (0, 0)),
                  pl.BlockSpec((d, tf), lambda i, f: (0, f)),
                  pl.BlockSpec((d, tf), lambda i, f: (0, f)),
                  pl.BlockSpec((tf, d), lambda i, f: (f, 0))],
        out_specs=pl.BlockSpec((tm, d), lambda i, f: (i, 0)),
        scratch_shapes=[pltpu.VMEM((tm, d), BF16), pltpu.VMEM((tm, d), F32)],
        compiler_params=_cparams(("arbitrary", "arbitrary")),
        name="swiglu_ffn",
    )(x, g, wg, wu, wd)


O_COL_Q, O_COL_K, O_COL_V, O_COL_QI, O_COL_DIN, O_COL_BG, O_COL_CG = 0, 512, 640, 768, 1024, 1536, 2048


def _rope128(x, cos_t, sin_t, lo_mask):
    up = pltpu.roll(x, LANES - ROT_DIM // 2, 1)
    dn = pltpu.roll(x, ROT_DIM // 2, 1)
    return x * cos_t + jnp.where(lo_mask, up, dn) * sin_t


def _odd_post_kernel(qkvq_ref, tail_ref, cos_ref, sin_ref, qn_ref, kn_ref, kg_ref, kb_ref, bd_ref,
                     q_ref, k_ref, v_ref, qi_ref, kiw_ref, k16_ref, v16_ref, kk16_ref, kit_ref=None):
    cos_t = cos_ref[...]
    sin_t = sin_ref[...]
    lane = lax.broadcasted_iota(I32, cos_t.shape, 1)
    lo_mask = (lane % HEAD_DIM) < (ROT_DIM // 2)
    bd = bd_ref[...]
    z = qkvq_ref[...]
    for c in range(4):
        x = z[:, c * LANES:(c + 1) * LANES]
        ms = _dot_exact_rhs(x * x, bd)
        xn = x * lax.rsqrt(ms + EPS) * qn_ref[...]
        q_ref[:, c * LANES:(c + 1) * LANES] = _rope128(xn, cos_t, sin_t, lo_mask)
    x = z[:, O_COL_K:O_COL_K + LANES]
    ms = _dot_exact_rhs(x * x, bd)
    xn = x * lax.rsqrt(ms + EPS) * kn_ref[...]
    kr = _rope128(xn, cos_t, sin_t, lo_mask)
    k16_ref[...] = kr.astype(BF16)
    vv = z[:, O_COL_V:O_COL_V + LANES]
    v16_ref[...] = vv.astype(BF16)
    if kit_ref is None:
        k_ref[...] = kr
        v_ref[...] = vv
    else:
        k_ref[0] = kr.T
        v_ref[0] = vv.T
    for c in range(2):
        x = z[:, O_COL_QI + c * LANES: O_COL_QI + (c + 1) * LANES]
        qi_ref[:, c * LANES:(c + 1) * LANES] = _rope128(x, cos_t, sin_t, lo_mask)
    t = tail_ref[...]
    lane_t = lax.broadcasted_iota(I32, t.shape, 1)
    is_ki = lane_t < IDX_DIM
    mu = _dot_exact_rhs(t, bd)
    tc = t - mu
    var = _dot_exact_rhs(tc * tc, bd)
    kin = tc * lax.rsqrt(var + EPS) * kg_ref[...] + kb_ref[...]
    kin = _rope128(kin, cos_t, sin_t, lo_mask)
    kiw_ref[...] = jnp.where(is_ki, kin, t * (IDX_HEADS ** -0.5 * IDX_DIM ** -0.5))
    kk16_ref[...] = jnp.where(is_ki, kin, pltpu.roll(kin, IDX_DIM, 1)).astype(BF16)
    if kit_ref is not None:
        kit_ref[0] = kin.T[:IDX_DIM]


def odd_post(zmain, ztail, cos_t, sin_t, qn, kn, kg, kb, bd, *, tm, seq_len=None):
    m = zmain.shape[0]
    nrope = cos_t.shape[0] // tm
    row = lambda i: (i, 0)
    rope = lambda i: (i % nrope, 0)
    const = lambda i: (0, 0)
    tok = lambda w, dt=F32: (jax.ShapeDtypeStruct((m, w), dt), pl.BlockSpec((tm, w), row))
    if seq_len is None:
        kv = [tok(LANES), tok(LANES)]
        extra = []
    else:
        nblk = seq_len // tm
        bsz = m // seq_len
        seq = lambda f: (jax.ShapeDtypeStruct((bsz, f, seq_len), F32),
                         pl.BlockSpec((1, f, tm), lambda i: (i // nblk, 0, i % nblk)))
        kv = [seq(LANES), seq(LANES)]
        extra = [seq(IDX_DIM)]
    outs = [tok(512)] + kv + [tok(256), tok(LANES), tok(LANES, BF16), tok(LANES, BF16), tok(LANES, BF16)] + extra
    return pl.pallas_call(
        _odd_post_kernel,
        out_shape=tuple(o[0] for o in outs),
        grid=(m // tm,),
        in_specs=[pl.BlockSpec((tm, 1024), row), pl.BlockSpec((tm, LANES), row),
                  pl.BlockSpec((tm, LANES), rope), pl.BlockSpec((tm, LANES), rope),
                  pl.BlockSpec((1, LANES), const), pl.BlockSpec((1, LANES), const),
                  pl.BlockSpec((1, LANES), const), pl.BlockSpec((1, LANES), const),
                  pl.BlockSpec((LANES, LANES), const)],
        out_specs=tuple(o[1] for o in outs),
        compiler_params=_cparams(("arbitrary",)),
        name="odd_post",
    )(zmain, ztail, cos_t, sin_t, qn, kn, kg, kb, bd)


INT_MIN = -2 ** 31
F32_MIN_NORMAL_BITS = 0x00800000


def _code_to_float(c):
    return pltpu.bitcast(c ^ ((c >> 31) & 0x7FFFFFFF), F32)


def _kth_largest(count_ge, k, shape):
    ans = jnp.full(shape, INT_MIN, I32)
    cand = jnp.zeros(shape, I32)
    ans = jnp.where(count_ge(_code_to_float(cand)) >= k, cand, ans)

    def body(i, ans):
        cand = ans | jnp.left_shift(jnp.int32(1), 30 - i)
        return jnp.where(count_ge(_code_to_float(cand)) >= k, cand, ans)

    ans = lax.fori_loop(0, 31, body, ans)
    ans = jnp.where((ans > 0) & (ans < F32_MIN_NORMAL_BITS), 0, ans)
    return _code_to_float(ans)


def _sum_leading(x):
    n = x.shape[0]
    extra = None
    while n > 1:
        if n % 2:
            extra = x[n - 1] if extra is None else extra + x[n - 1]
            n -= 1
        x = x[:n // 2] + x[n // 2:n]
        n //= 2
    return x[0] if extra is None else x[0] + extra


def _upper_tri_bf16(n):
    r = lax.broadcasted_iota(I32, (n, n), 0)
    c = lax.broadcasted_iota(I32, (n, n), 1)
    return jnp.where(r <= c, 1.0, 0.0).astype(BF16)


def _dsa_prompt_kernel(q_ref, qi_ref, wi_ref, k_ref, v_ref, kk_ref, o_ref, keys_ref, bias_ref, *,
                       q_base, s_eff, topk, qb):
    t0 = q_base + pl.program_id(1) * qb
    lane = lax.broadcasted_iota(I32, (qb, LANES), 1)
    row = lax.broadcasted_iota(I32, (qb, LANES), 0)
    lo = lane < HEAD_DIM
    hi = lane >= HEAD_DIM
    rpg = N_HEADS // N_KV

    kk = kk_ref[0]
    qi = qi_ref[0]
    wi = wi_ref[0]
    sc = None
    for h in range(IDX_HEADS):
        x = qi[:, (h // 2) * LANES:(h // 2 + 1) * LANES]
        x = jnp.where(lo if h % 2 == 0 else hi, x, 0.0)
        d = _bdot_nt(x, kk)
        term = jnp.maximum(d, 0.0) * wi[:, IDX_DIM + h:IDX_DIM + h + 1]
        sc = term if sc is None else sc + term
    kpos = lax.broadcasted_iota(I32, (qb, s_eff), 1)
    qpos = t0 + lax.broadcasted_iota(I32, (qb, s_eff), 0)
    adm_all = kpos <= qpos
    sc = jnp.where(sc == 0.0, 0.0, sc)
    keys_ref[...] = jnp.where(adm_all, sc, NEG)

    def count_ge(cand):
        return jnp.sum(jnp.where(keys_ref[...] >= cand, 1.0, 0.0), axis=-1, keepdims=True)

    thr = _kth_largest(count_ge, float(topk), (qb, 1))
    n_ge = count_ge(thr)
    n_adm = t0 + 1 + lax.broadcasted_iota(I32, (qb, 1), 0)
    simple = jnp.where((n_ge == float(topk)) | (n_adm < topk), 0.0, 1.0)
    no_cut = jnp.max(simple) == 0.0

    @pl.when(no_cut)
    def _():
        bias_ref[...] = jnp.where(adm_all, jnp.where(keys_ref[...] >= thr, 0.0, NEG), NEG)

    @pl.when(jnp.logical_not(no_cut))
    def _():
        n_gt = jnp.sum(jnp.where(keys_ref[...] > thr, 1.0, 0.0), axis=-1, keepdims=True)
        need = float(topk) - n_gt
        ut = _upper_tri_bf16(LANES)
        off = jnp.zeros((qb, 1), F32)
        for j in range(s_eff // LANES):
            kj = keys_ref[:, j * LANES:(j + 1) * LANES]
            eq = kj == thr
            pre = jnp.dot(jnp.where(eq, 1.0, 0.0).astype(BF16), ut, preferred_element_type=F32) + off
            adm = (j * LANES + lane) <= (t0 + row)
            keep = jnp.where(eq, jnp.where(pre <= need, 0.0, NEG), jnp.where(kj > thr, 0.0, NEG))
            bias_ref[:, j * LANES:(j + 1) * LANES] = jnp.where(adm, keep, NEG)
            off = pre[:, LANES - 1:LANES]

    kb = k_ref[0]
    vb = v_ref[0]
    lane_b = lax.broadcasted_iota(I32, (Q_BLOCK, LANES), 1)
    lo_b = lane_b < HEAD_DIM
    hi_b = lane_b >= HEAD_DIM
    lo4 = lax.broadcasted_iota(I32, (rpg * Q_BLOCK, LANES), 1) < HEAD_DIM
    for sub in range(qb // Q_BLOCK):
        rows = pl.ds(sub * Q_BLOCK, Q_BLOCK)
        bias = bias_ref[rows, :]
        q = q_ref[0, rows, :] * (HEAD_DIM ** -0.5)
        outs = []
        for g in range(N_KV):
            keep = lo_b if g == 0 else hi_b
            qg = jnp.concatenate([jnp.where(keep, q[:, r * LANES:(r + 1) * LANES], 0.0) for r in range(rpg)],
                                 axis=0)
            s = _bdot_nt(qg, kb).reshape(rpg, Q_BLOCK, s_eff) + bias[None]
            m = jnp.max(s, axis=-1, keepdims=True)
            e = jnp.exp(s - m)
            l = jnp.sum(e, axis=-1, keepdims=True).reshape(rpg * Q_BLOCK, 1)
            pv = jnp.dot(e.reshape(rpg * Q_BLOCK, s_eff).astype(BF16), vb, preferred_element_type=F32)
            outs.append(pv / l)
        oc = jnp.where(lo4, outs[0], outs[1])
        for r in range(rpg):
            o_ref[0, rows, r * LANES:(r + 1) * LANES] = oc[r * Q_BLOCK:(r + 1) * Q_BLOCK, :]


DSA_SEGMENT = 256


def dsa_prompt(q, qi, wi, k16, v16, kk16):
    bsz, s_len, _ = q.shape
    topk = min(TOPK_MAX, s_len // 4)
    seg = min(DSA_SEGMENT, s_len)
    outs = []
    for si in range(s_len // seg):
        s_eff = (si + 1) * seg
        kern = functools.partial(_dsa_prompt_kernel, q_base=si * seg, s_eff=s_eff, topk=topk, qb=seg)
        qblk = lambda w, si=si: pl.BlockSpec((1, seg, w), lambda b_, i: (b_, si, 0))
        keys = pl.BlockSpec((1, s_eff, LANES), lambda b_, i: (b_, 0, 0))
        outs.append(pl.pallas_call(
            kern,
            out_shape=jax.ShapeDtypeStruct((bsz, seg, 512), F32),
            grid=(bsz, 1),
            in_specs=[qblk(512), qblk(256), qblk(LANES), keys, keys, keys],
            out_specs=pl.BlockSpec((1, seg, 512), lambda b_, i: (b_, 0, 0)),
            scratch_shapes=[pltpu.VMEM((seg, s_eff), F32), pltpu.VMEM((seg, s_eff), F32)],
            compiler_params=_cparams(("arbitrary", "arbitrary")),
            name=f"dsa_prompt_{si}",
        )(q, qi, wi, k16, v16, kk16))
    return jnp.concatenate(outs, axis=1)


CARRY_D = 8


def _conv_d_kernel(din_ref, bg_ref, cg_ref, prev_ref, w_ref, y_ref, new_ref, buf_ref, *, tl):
    l = pl.program_id(1)

    @pl.when(l == 0)
    def _():
        buf_ref[pl.ds(0, CARRY_D), :] = prev_ref[0]

    buf_ref[pl.ds(CARRY_D, tl), :] = cg_ref[0] * din_ref[0]
    y_ref[0] = bg_ref[0] * _dwconv_from_buf(buf_ref, w_ref, CONV_D, CARRY_D, tl)
    hist = buf_ref[pl.ds(tl, CARRY_D), :]
    new_ref[0] = hist
    buf_ref[pl.ds(0, CARRY_D), :] = hist


def conv_d(zmain, prev8, w, *, tl):
    bsz, L, _ = zmain.shape
    kern = functools.partial(_conv_d_kernel, tl=tl)
    col = lambda c: pl.BlockSpec((1, tl, D_D), lambda b_, l: (b_, l, c // D_D))
    return pl.pallas_call(
        kern,
        out_shape=(jax.ShapeDtypeStruct((bsz, L, D_D), F32), jax.ShapeDtypeStruct((bsz, CARRY_D, D_D), F32)),
        grid=(bsz, L // tl),
        in_specs=[col(O_COL_DIN), col(O_COL_BG), col(O_COL_CG),
                  pl.BlockSpec((1, CARRY_D, D_D), lambda b_, l: (b_, 0, 0)),
                  pl.BlockSpec((CONV_D, D_D), lambda b_, l: (0, 0))],
        out_specs=(pl.BlockSpec((1, tl, D_D), lambda b_, l: (b_, l, 0)),
                   pl.BlockSpec((1, CARRY_D, D_D), lambda b_, l: (b_, 0, 0))),
        scratch_shapes=[pltpu.VMEM((CARRY_D + tl, D_D), F32)],
        compiler_params=_cparams(("arbitrary", "arbitrary")),
        name="conv_d",
    )(zmain, zmain, zmain, prev8, w)


def _router_kernel(x_ref, g_ref, whi_ref, wlo_ref, br_ref, xn_ref, comb_ref):
    x = x_ref[...]
    y = x * lax.rsqrt(jnp.mean(x * x, axis=-1, keepdims=True) + EPS) * g_ref[...]
    yhi = y.astype(BF16)
    xn_ref[...] = yhi
    ylo = (y - yhi.astype(F32)).astype(BF16)
    logits = jnp.dot(ylo, whi_ref[...], preferred_element_type=F32)
    logits = logits + jnp.dot(yhi, wlo_ref[...], preferred_element_type=F32)
    logits = logits + jnp.dot(yhi, whi_ref[...], preferred_element_type=F32) + br_ref[...]
    lane = lax.broadcasted_iota(I32, logits.shape, 1).astype(F32)
    m1 = jnp.max(logits, axis=-1, keepdims=True)
    i1 = jnp.min(jnp.where(logits == m1, lane, float(LANES)), axis=-1, keepdims=True)
    rest = jnp.where(lane == i1, -3e38, logits)
    m2 = jnp.max(rest, axis=-1, keepdims=True)
    i2 = jnp.min(jnp.where(rest == m2, lane, float(LANES)), axis=-1, keepdims=True)
    e2 = jnp.exp(m2 - m1)
    g1 = 1.0 / (1.0 + e2)
    g2 = e2 / (1.0 + e2)
    comb_ref[...] = jnp.where(lane == i1, g1, 0.0) + jnp.where(lane == i2, g2, 0.0)


def moe_router(x, g, w_hi, w_lo, b_pad, *, tm):
    m, d = x.shape
    return pl.pallas_call(
        _router_kernel,
        out_shape=(jax.ShapeDtypeStruct((m, d), BF16), jax.ShapeDtypeStruct((m, LANES), F32)),
        grid=(m // tm,),
        in_specs=[pl.BlockSpec((tm, d), lambda i: (i, 0)),
                  pl.BlockSpec((1, d), lambda i: (0, 0)),
                  pl.BlockSpec((d, LANES), lambda i: (0, 0)),
                  pl.BlockSpec((d, LANES), lambda i: (0, 0)),
                  pl.BlockSpec((1, LANES), lambda i: (0, 0))],
        out_specs=(pl.BlockSpec((tm, d), lambda i: (i, 0)), pl.BlockSpec((tm, LANES), lambda i: (i, 0))),
        compiler_params=_cparams(("arbitrary",)),
        name="moe_router",
    )(x, g, w_hi, w_lo, b_pad)


def _moe_dense_kernel(x_ref, xn_ref, comb_ref, wg_ref, wu_ref, wd_ref, o_ref, acc_ref, acce_ref):
    e = pl.program_id(1)
    f = pl.program_id(2)
    ne = pl.num_programs(1)
    nf = pl.num_programs(2)

    @pl.when((e == 0) & (f == 0))
    def _():
        acc_ref[...] = jnp.zeros_like(acc_ref)

    @pl.when(f == 0)
    def _():
        acce_ref[...] = jnp.zeros_like(acce_ref)

    xn = xn_ref[...]
    gate = jnp.dot(xn, wg_ref[0], preferred_element_type=F32)
    up = jnp.dot(xn, wu_ref[0], preferred_element_type=F32)
    hid = (_silu(gate) * up).astype(BF16)
    acce_ref[...] += jnp.dot(hid, wd_ref[0], preferred_element_type=F32)

    @pl.when(f == nf - 1)
    def _():
        comb = comb_ref[...]
        lane = lax.broadcasted_iota(I32, comb.shape, 1)
        c = jnp.sum(jnp.where(lane == e, comb, 0.0), axis=-1, keepdims=True)
        acc_ref[...] += c * acce_ref[...]

    @pl.when((e == ne - 1) & (f == nf - 1))
    def _():
        o_ref[...] = x_ref[...] + acc_ref[...]


def moe_dense(x, xn, comb, wg, wu, wd, *, tm, tf):
    m, d = x.shape
    ne, _, ff = wg.shape
    return pl.pallas_call(
        _moe_dense_kernel,
        out_shape=jax.ShapeDtypeStruct((m, d), F32),
        grid=(m // tm, ne, ff // tf),
        in_specs=[pl.BlockSpec((tm, d), lambda i, e, f: (i, 0)),
                  pl.BlockSpec((tm, d), lambda i, e, f: (i, 0)),
                  pl.BlockSpec((tm, LANES), lambda i, e, f: (i, 0)),
                  pl.BlockSpec((1, d, tf), lambda i, e, f: (e, 0, f)),
                  pl.BlockSpec((1, d, tf), lambda i, e, f: (e, 0, f)),
                  pl.BlockSpec((1, tf, d), lambda i, e, f: (e, f, 0))],
        out_specs=pl.BlockSpec((tm, d), lambda i, e, f: (i, 0)),
        scratch_shapes=[pltpu.VMEM((tm, d), F32), pltpu.VMEM((tm, d), F32)],
        compiler_params=_cparams(("arbitrary", "arbitrary", "arbitrary")),
        name="moe_dense",
    )(x, xn, comb, wg, wu, wd)


MOE_TILE = 512
DMA_ISSUE_UNROLL = 8
FF_TILE = D_FF // 2
R_I1, R_I2, R_G1, R_G2, R_RANK1, R_RANK2 = 0, 1, 2, 3, 4, 5


def _router_rank_kernel(x_ref, g_ref, whi_ref, wlo_ref, br_ref, info_ref, cnt_ref, run_ref):
    i = pl.program_id(0)

    @pl.when(i == 0)
    def _():
        run_ref[...] = jnp.zeros_like(run_ref)

    x = x_ref[...]
    tm = x.shape[0]
    y = x * lax.rsqrt(jnp.mean(x * x, axis=-1, keepdims=True) + EPS) * g_ref[...]
    yhi = y.astype(BF16)
    ylo = (y - yhi.astype(F32)).astype(BF16)
    logits = jnp.dot(ylo, whi_ref[...], preferred_element_type=F32)
    logits = logits + jnp.dot(yhi, wlo_ref[...], preferred_element_type=F32)
    logits = logits + jnp.dot(yhi, whi_ref[...], preferred_element_type=F32) + br_ref[...]
    lane = lax.broadcasted_iota(I32, logits.shape, 1).astype(F32)
    m1 = jnp.max(logits, axis=-1, keepdims=True)
    i1 = jnp.min(jnp.where(logits == m1, lane, float(LANES)), axis=-1, keepdims=True)
    rest = jnp.where(lane == i1, -3e38, logits)
    m2 = jnp.max(rest, axis=-1, keepdims=True)
    i2 = jnp.min(jnp.where(rest == m2, lane, float(LANES)), axis=-1, keepdims=True)
    e2 = jnp.exp(m2 - m1)
    g1 = 1.0 / (1.0 + e2)
    g2 = e2 / (1.0 + e2)
    oh1 = jnp.where(lane == i1, 1.0, 0.0)
    oh2 = jnp.where(lane == i2, 1.0, 0.0)
    both = oh1 + oh2
    r_i = lax.broadcasted_iota(I32, (tm, tm), 0)
    c_i = lax.broadcasted_iota(I32, (tm, tm), 1)
    strict_lower = jnp.where(c_i < r_i, 1.0, 0.0).astype(BF16)
    before = jnp.dot(strict_lower, both.astype(BF16), preferred_element_type=F32) + run_ref[...]
    rank1 = jnp.sum(before * oh1, axis=-1, keepdims=True)
    rank2 = jnp.sum(before * oh2, axis=-1, keepdims=True)
    run_ref[...] += jnp.sum(both, axis=0, keepdims=True)
    info = jnp.where(lane == R_I1, i1, 0.0)
    for col, val in ((R_I2, i2), (R_G1, g1), (R_G2, g2), (R_RANK1, rank1), (R_RANK2, rank2)):
        info = jnp.where(lane == col, val, info)
    info_ref[...] = info
    cnt_ref[...] = run_ref[...]


def moe_router_rank(x, g, w_hi, w_lo, b_pad, *, tm):
    m, d = x.shape
    return pl.pallas_call(
        _router_rank_kernel,
        out_shape=(jax.ShapeDtypeStruct((m, LANES), F32), jax.ShapeDtypeStruct((1, LANES), F32)),
        grid=(m // tm,),
        in_specs=[pl.BlockSpec((tm, d), lambda i: (i, 0)),
                  pl.BlockSpec((1, d), lambda i: (0, 0)),
                  pl.BlockSpec((d, LANES), lambda i: (0, 0)),
                  pl.BlockSpec((d, LANES), lambda i: (0, 0)),
                  pl.BlockSpec((1, LANES), lambda i: (0, 0))],
        out_specs=(pl.BlockSpec((tm, LANES), lambda i: (i, 0)), pl.BlockSpec((1, LANES), lambda i: (0, 0))),
        scratch_shapes=[pltpu.VMEM((1, LANES), F32)],
        compiler_params=_cparams(("arbitrary",)),
        name="moe_router_rank",
    )(x, g, w_hi, w_lo, b_pad)


def _row_copy(src_ref, si, dst_ref, di, sem):
    return pltpu.make_async_copy(src_ref.at[pl.ds(si, 1)], dst_ref.at[pl.ds(di, 1)], sem)


def _moe_dispatch_kernel(d1_ref, d2_ref, x_ref, xs_ref, sem, *, ct):
    def issue(t, carry):
        _row_copy(x_ref, t, xs_ref, d1_ref[0, 0, t], sem).start(priority=0)
        _row_copy(x_ref, t, xs_ref, d2_ref[0, 0, t], sem).start(priority=1)
        return carry

    lax.fori_loop(0, ct, issue, 0, unroll=DMA_ISSUE_UNROLL)
    tile_copy = pltpu.make_async_copy(x_ref, xs_ref.at[pl.ds(0, ct)], sem)
    tile_copy.wait()
    tile_copy.wait()


def moe_dispatch(x, dest1, dest2, *, ct):
    m, d = x.shape
    smem_row = pl.BlockSpec((1, 1, ct), lambda i: (i, 0, 0), memory_space=pltpu.SMEM)
    return pl.pallas_call(
        functools.partial(_moe_dispatch_kernel, ct=ct),
        out_shape=jax.ShapeDtypeStruct((2 * m, d), F32),
        grid=(m // ct,),
        in_specs=[smem_row, smem_row, pl.BlockSpec((ct, d), lambda i: (i, 0))],
        out_specs=pl.BlockSpec(memory_space=pl.ANY),
        scratch_shapes=[pltpu.SemaphoreType.DMA],
        compiler_params=_cparams(("arbitrary",)),
        name="moe_dispatch",
    )(dest1, dest2, x)


def _moe_grouped_kernel(tile_ref, exp_ref, first_ref, valid_ref, gs_ref, xs_ref, g_ref, wg_ref, wu_ref, wd_ref,
                        ys_ref, xn_ref, acc_ref):
    w = pl.program_id(0)
    f = pl.program_id(1)
    nf = pl.num_programs(1)

    @pl.when(valid_ref[w] == 1)
    def _():
        @pl.when(f == 0)
        def _():
            x = xs_ref[...]
            y = x * lax.rsqrt(jnp.mean(x * x, axis=-1, keepdims=True) + EPS) * g_ref[...]
            xn_ref[...] = y.astype(BF16)
            acc_ref[...] = jnp.zeros_like(acc_ref)

        xn = xn_ref[...]
        gate = jnp.dot(xn, wg_ref[0], preferred_element_type=F32)
        up = jnp.dot(xn, wu_ref[0], preferred_element_type=F32)
        hid = (_silu(gate) * up).astype(BF16)
        acc_ref[...] += jnp.dot(hid, wd_ref[0], preferred_element_type=F32)

        @pl.when(f == nf - 1)
        def _():
            e = exp_ref[w]
            row = tile_ref[w] * MOE_TILE + lax.broadcasted_iota(I32, acc_ref.shape, 0)
            mine = (row >= gs_ref[e]) & (row < gs_ref[e + 1])
            part = jnp.where(mine, acc_ref[...], 0.0)

            @pl.when(first_ref[w] == 1)
            def _():
                ys_ref[...] = part

            @pl.when(first_ref[w] == 0)
            def _():
                ys_ref[...] += part


def moe_grouped(item_tile, item_expert, item_first, item_valid, group_starts, xs, g, wg, wu, wd, *, tf):
    r, d = xs.shape
    ff = wg.shape[2]
    n_items = item_tile.shape[0]
    return pl.pallas_call(
        _moe_grouped_kernel,
        out_shape=jax.ShapeDtypeStruct((r, d), F32),
        grid_spec=pltpu.PrefetchScalarGridSpec(
            num_scalar_prefetch=5,
            grid=(n_items, ff // tf),
            in_specs=[pl.BlockSpec((MOE_TILE, d), lambda w, f, it, ie, i1, iv, gs: (it[w], 0)),
                      pl.BlockSpec((1, d), lambda w, f, it, ie, i1, iv, gs: (0, 0)),
                      pl.BlockSpec((1, d, tf), lambda w, f, it, ie, i1, iv, gs: (ie[w], 0, f)),
                      pl.BlockSpec((1, d, tf), lambda w, f, it, ie, i1, iv, gs: (ie[w], 0, f)),
                      pl.BlockSpec((1, tf, d), lambda w, f, it, ie, i1, iv, gs: (ie[w], f, 0))],
            out_specs=pl.BlockSpec((MOE_TILE, d), lambda w, f, it, ie, i1, iv, gs: (it[w], 0)),
            scratch_shapes=[pltpu.VMEM((MOE_TILE, d), BF16), pltpu.VMEM((MOE_TILE, d), F32)]),
        compiler_params=_cparams(("arbitrary", "arbitrary")),
        name="moe_grouped",
    )(item_tile, item_expert, item_first, item_valid, group_starts, xs, g, wg, wu, wd)


def _moe_combine_kernel(d1_ref, d2_ref, x_ref, info_ref, ys_ref, o_ref, y1_ref, y2_ref, sem, *, ct):
    def issue(t, carry):
        _row_copy(ys_ref, d1_ref[0, 0, t], y1_ref, t, sem).start(priority=0)
        _row_copy(ys_ref, d2_ref[0, 0, t], y2_ref, t, sem).start(priority=1)
        return carry

    lax.fori_loop(0, ct, issue, 0, unroll=DMA_ISSUE_UNROLL)
    pltpu.make_async_copy(ys_ref.at[pl.ds(0, ct)], y1_ref, sem).wait()
    pltpu.make_async_copy(ys_ref.at[pl.ds(0, ct)], y2_ref, sem).wait()
    info = info_ref[...]
    g1 = info[:, R_G1:R_G1 + 1]
    g2 = info[:, R_G2:R_G2 + 1]
    o_ref[...] = x_ref[...] + (g1 * y1_ref[...] + g2 * y2_ref[...])


def moe_combine(x, info, ys, dest1, dest2, *, ct):
    m, d = x.shape
    smem_row = pl.BlockSpec((1, 1, ct), lambda i: (i, 0, 0), memory_space=pltpu.SMEM)
    return pl.pallas_call(
        functools.partial(_moe_combine_kernel, ct=ct),
        out_shape=jax.ShapeDtypeStruct((m, d), F32),
        grid=(m // ct,),
        in_specs=[smem_row, smem_row, pl.BlockSpec((ct, d), lambda i: (i, 0)),
                  pl.BlockSpec((ct, LANES), lambda i: (i, 0)), pl.BlockSpec(memory_space=pl.ANY)],
        out_specs=pl.BlockSpec((ct, d), lambda i: (i, 0)),
        scratch_shapes=[pltpu.VMEM((ct, d), F32), pltpu.VMEM((ct, d), F32), pltpu.SemaphoreType.DMA],
        compiler_params=_cparams(("arbitrary",)),
        name="moe_combine",
    )(dest1, dest2, x, info, ys)


def _moe_work_items(counts, n_tiles):
    n_exp = counts.shape[0]
    n_items = n_tiles + n_exp - 1
    ends = jnp.cumsum(counts)
    starts = ends - counts
    expert_of_row = lambda rr: jnp.minimum(jnp.sum(ends[None, :] <= rr[:, None], axis=1), n_exp - 1).astype(I32)
    t = jnp.arange(n_tiles, dtype=I32)
    e_lo = expert_of_row(t * MOE_TILE)
    e_hi = expert_of_row(t * MOE_TILE + (MOE_TILE - 1))
    per_tile = e_hi - e_lo + 1
    item_end = jnp.cumsum(per_tile)
    item_start = item_end - per_tile
    total = item_end[-1]
    w = jnp.minimum(jnp.arange(n_items, dtype=I32), total - 1)
    tile = jnp.sum(item_end[None, :] <= w[:, None], axis=1).astype(I32)
    expert = (e_lo[tile] + (w - item_start[tile])).astype(I32)
    first = (w == item_start[tile]).astype(I32)
    valid = (jnp.arange(n_items, dtype=I32) < total).astype(I32)
    group_starts = jnp.concatenate([starts, ends[-1:]]).astype(I32)
    return tile, expert, first, valid, group_starts


def moe_routed(x, g, w_hi, w_lo, b_pad, wg, wu, wd, *, tm):
    m, d = x.shape
    n_exp = wg.shape[0]
    assert (2 * m) % MOE_TILE == 0
    info, cnt = moe_router_rank(x, g, w_hi, w_lo, b_pad, tm=tm)
    counts = cnt[0, :n_exp].astype(I32)
    starts = jnp.cumsum(counts) - counts
    i1 = info[:, R_I1].astype(I32)
    i2 = info[:, R_I2].astype(I32)
    dest1 = (starts[i1] + info[:, R_RANK1].astype(I32)).reshape(m // tm, 1, tm)
    dest2 = (starts[i2] + info[:, R_RANK2].astype(I32)).reshape(m // tm, 1, tm)
    xs = moe_dispatch(x, dest1, dest2, ct=tm)
    items = _moe_work_items(counts, (2 * m) // MOE_TILE)
    ys = moe_grouped(*items, xs, g, wg, wu, wd, tf=FF_TILE)
    return moe_combine(x, info, ys, dest1, dest2, ct=tm)


PAGES_PER_STEP = 32
PAGE_SLOTS = 2
SDSA_GROUP = 4


def _fetch_pages(pt_ref, cache_ref, buf_ref, sem_ref, step, nj):
    b = step // nj
    j0 = (step % nj) * PAGES_PER_STEP
    slot = step % PAGE_SLOTS
    for i in range(PAGES_PER_STEP):
        pltpu.make_async_copy(cache_ref.at[pt_ref[b, j0 + i]], buf_ref.at[slot, i], sem_ref.at[slot]).start()


def _wait_pages(cache_ref, buf_ref, sem_ref, step):
    slot = step % PAGE_SLOTS
    pltpu.make_async_copy(cache_ref.at[pl.ds(0, PAGES_PER_STEP)], buf_ref.at[slot], sem_ref.at[slot]).wait()


def _page_pipeline_step(pt_ref, streams, nj):
    step = pl.program_id(0) * pl.num_programs(1) + pl.program_id(1)
    total = pl.num_programs(0) * pl.num_programs(1)

    @pl.when(step == 0)
    def _():
        for cache_ref, buf_ref, sem_ref in streams:
            _fetch_pages(pt_ref, cache_ref, buf_ref, sem_ref, step, nj)

    @pl.when(step + 1 < total)
    def _():
        for cache_ref, buf_ref, sem_ref in streams:
            _fetch_pages(pt_ref, cache_ref, buf_ref, sem_ref, step + 1, nj)

    for cache_ref, buf_ref, sem_ref in streams:
        _wait_pages(cache_ref, buf_ref, sem_ref, step)
    return step % PAGE_SLOTS


def _sdsa_index_kernel(pt_ref, qi_ref, wi_ref, kknew_ref, cache_ref, mpast_ref, mnew_ref,
                       keys_ref, pre_ref, pbuf_ref, psem_ref, *, n_pages, t_len, topk, nb):
    nj = n_pages // PAGES_PER_STEP
    lb = pl.program_id(1) // nj
    j = pl.program_id(1) % nj
    rows = pl.ds(pl.multiple_of(lb * t_len, t_len), t_len)
    last_step = pl.program_id(1) == nb * nj - 1
    slot = _page_pipeline_step(pt_ref, [(cache_ref, pbuf_ref, psem_ref)], nj)
    lane = lax.broadcasted_iota(I32, (t_len, LANES), 1)
    row = lax.broadcasted_iota(I32, (t_len, LANES), 0)
    lo = lane < HEAD_DIM

    qi = qi_ref[0]
    wi = wi_ref[0]
    qs, ws = [], []
    for h in range(IDX_HEADS):
        x = qi[:, (h // 2) * LANES:(h // 2 + 1) * LANES]
        if h % 2 == 1:
            x = pltpu.roll(x, HEAD_DIM, 1)
        qs.append(x[:, :IDX_DIM])
        ws.append(wi[:, IDX_DIM + h:IDX_DIM + h + 1])
    q32 = jnp.concatenate(qs, axis=0).astype(BF16)
    w32 = jnp.concatenate(ws, axis=0)

    def scores(dots):
        d = jnp.maximum(dots, 0.0) * w32
        s = d[0:t_len]
        for h in range(1, IDX_HEADS):
            s = s + d[h * t_len:(h + 1) * t_len]
        return jnp.where(s == 0.0, 0.0, s)

    pages_t = jnp.concatenate([pbuf_ref[slot, i].astype(BF16) for i in range(PAGES_PER_STEP)], axis=1)
    s_wide = scores(jnp.dot(q32, pages_t, preferred_element_type=F32))
    for i in range(PAGES_PER_STEP):
        keys_ref[j * PAGES_PER_STEP + i, rows, :] = s_wide[:, i * PAGE_SIZE:(i + 1) * PAGE_SIZE]

    @pl.when(j == nj - 1)
    def _():
        knew = jnp.concatenate([kknew_ref[0][:, :IDX_DIM], jnp.zeros((LANES - t_len, IDX_DIM), F32)], axis=0)
        s_new = jnp.where((lane <= row) & (lane < t_len), scores(_bdot_nt(q32, knew)), NEG)
        keys_ref[n_pages, rows, :] = s_new

    @pl.when(last_step)
    def _():
        nrow = nb * t_len
        own = lambda x, g: x[..., g * t_len:(g + 1) * t_len, :]

        def count_ge(cand):
            part = _sum_leading(jnp.where(keys_ref[...] >= cand, 1.0, 0.0))
            return jnp.sum(part, axis=-1, keepdims=True)

        thr = _kth_largest(count_ge, float(topk), (nrow, 1))
        n_ge = count_ge(thr)
        no_cut = jnp.max(jnp.abs(n_ge - float(topk))) == 0.0

        @pl.when(no_cut)
        def _():
            bias = jnp.where(keys_ref[...] >= thr, 0.0, NEG)
            for g in range(nb):
                mpast_ref[g] = own(bias[:n_pages], g)
                mnew_ref[g] = own(bias[n_pages], g)

        @pl.when(jnp.logical_not(no_cut))
        def _():
            keys = keys_ref[...]
            n_gt = jnp.sum(_sum_leading(jnp.where(keys > thr, 1.0, 0.0)), axis=-1, keepdims=True)
            need = float(topk) - n_gt
            eqf = jnp.where(keys == thr, 1.0, 0.0).astype(BF16).reshape((n_pages + 1) * nrow, LANES)
            pre_ref[...] = jnp.dot(eqf, _upper_tri_bf16(LANES), preferred_element_type=F32).reshape(
                n_pages + 1, nrow, LANES)

            def body(p, off):
                kp = keys_ref[p]
                pre = pre_ref[p]
                bias = jnp.where(kp == thr, jnp.where((pre + off) <= need, 0.0, NEG),
                                 jnp.where(kp > thr, 0.0, NEG))

                @pl.when(p < n_pages)
                def _():
                    for g in range(nb):
                        mpast_ref[g, p] = own(bias, g)

                @pl.when(p == n_pages)
                def _():
                    for g in range(nb):
                        mnew_ref[g] = own(bias, g)

                return off + pre[:, LANES - 1:LANES]

            lax.fori_loop(0, n_pages + 1, body, jnp.zeros((nrow, 1), F32))


def sdsa_index(page_table, qi, wi, kk_new, cache_kidx):
    bsz, t_len, _ = qi.shape
    n_pages = page_table.shape[1]
    topk = min(TOPK_MAX, (n_pages * PAGE_SIZE + t_len) // 4)
    nb = math.gcd(bsz, SDSA_GROUP)
    nj = n_pages // PAGES_PER_STEP
    kern = functools.partial(_sdsa_index_kernel, n_pages=n_pages, t_len=t_len, topk=topk, nb=nb)
    tok = lambda w: pl.BlockSpec((1, t_len, w), lambda gi, jj, pt: (gi * nb + jj // nj, 0, 0))
    return pl.pallas_call(
        kern,
        out_shape=(jax.ShapeDtypeStruct((bsz, n_pages, t_len, LANES), F32),
                   jax.ShapeDtypeStruct((bsz, t_len, LANES), F32)),
        grid_spec=pltpu.PrefetchScalarGridSpec(
            num_scalar_prefetch=1,
            grid=(bsz // nb, nb * nj),
            in_specs=[tok(256), tok(LANES), tok(LANES), pl.BlockSpec(memory_space=pl.ANY)],
            out_specs=(pl.BlockSpec((nb, n_pages, t_len, LANES), lambda gi, jj, pt: (gi, 0, 0, 0)),
                       pl.BlockSpec((nb, t_len, LANES), lambda gi, jj, pt: (gi, 0, 0))),
            scratch_shapes=[pltpu.VMEM((n_pages + 1, nb * t_len, LANES), F32),
                            pltpu.VMEM((n_pages + 1, nb * t_len, LANES), F32),
                            pltpu.VMEM((PAGE_SLOTS, PAGES_PER_STEP, IDX_DIM, PAGE_SIZE), F32),
                            pltpu.SemaphoreType.DMA((PAGE_SLOTS,))]),
        compiler_params=_cparams(("arbitrary", "arbitrary")),
        name="sdsa_index",
    )(page_table, qi, wi, kk_new, cache_kidx)


def _sdsa_attn_kernel(pt_ref, q_ref, knew_ref, vnew_ref, mpast_ref, mnew_ref, ck_ref, cv_ref,
                      o_ref, qg_ref, m_ref, l_ref, acc_ref, kbuf_ref, vbuf_ref, ksem_ref, vsem_ref, *, t_len, n_pages):
    j = pl.program_id(1)
    nj = pl.num_programs(1)
    slot = _page_pipeline_step(pt_ref, [(ck_ref, kbuf_ref, ksem_ref), (cv_ref, vbuf_ref, vsem_ref)],
                               n_pages // PAGES_PER_STEP)
    rpg = N_HEADS // N_KV
    reps = rpg

    def group_lanes(x, g):
        if g == 1:
            x = pltpu.roll(x, HEAD_DIM, 1)
        return x[:, :HEAD_DIM]

    @pl.when(j == 0)
    def _():
        q = q_ref[0]
        for g in range(N_KV):
            parts = [group_lanes(q[:, r * LANES:(r + 1) * LANES], g) for r in range(rpg)]
            qg_ref[g] = (jnp.concatenate(parts, axis=0) * (HEAD_DIM ** -0.5)).astype(BF16)
        m_ref[...] = jnp.full(m_ref.shape, NEG, F32)
        l_ref[...] = jnp.zeros(l_ref.shape, F32)
        acc_ref[...] = jnp.zeros(acc_ref.shape, F32)

    def update(g, s, pv):
        m_old = m_ref[g]
        m_cur = jnp.maximum(m_old, jnp.max(s, axis=-1, keepdims=True))
        alpha = jnp.exp(m_old - m_cur)
        p = jnp.exp(s - m_cur)
        m_ref[g] = m_cur
        l_ref[g] = l_ref[g] * alpha + jnp.sum(p, axis=-1, keepdims=True)
        acc_ref[g] = acc_ref[g] * alpha + pv(p.astype(BF16))

    bias8 = jnp.concatenate([mpast_ref[0, i] for i in range(PAGES_PER_STEP)], axis=1)
    bias_wide = jnp.concatenate([bias8] * reps, axis=0)
    for g in range(N_KV):
        k_wide = jnp.concatenate([kbuf_ref[slot, i, g].astype(BF16) for i in range(PAGES_PER_STEP)], axis=1)
        v_wide = jnp.concatenate([vbuf_ref[slot, i, g].astype(BF16) for i in range(PAGES_PER_STEP)], axis=1)
        s = jnp.dot(qg_ref[g], k_wide, preferred_element_type=F32) + bias_wide
        update(g, s, lambda p, v_wide=v_wide: _bdot_nt(p, v_wide))

    @pl.when(j == nj - 1)
    def _():
        pad = jnp.zeros((LANES - t_len, HEAD_DIM), F32)
        bias = jnp.concatenate([mnew_ref[0]] * reps, axis=0)
        for g in range(N_KV):
            knew = jnp.concatenate([group_lanes(knew_ref[0], g), pad], axis=0)
            vnew = jnp.concatenate([group_lanes(vnew_ref[0], g), pad], axis=0)
            update(g, _bdot_nt(qg_ref[g], knew) + bias, lambda p, vnew=vnew: _bdot(p, vnew))
        o0 = acc_ref[0] / l_ref[0]
        o1 = acc_ref[1] / l_ref[1]
        for r in range(rpg):
            o_ref[0, :, r * LANES:(r + 1) * LANES] = jnp.concatenate(
                [o0[r * t_len:(r + 1) * t_len], o1[r * t_len:(r + 1) * t_len]], axis=-1)


def sdsa_attn(page_table, q, k_new, v_new, mask_past, mask_new, cache_k, cache_v):
    bsz, t_len, _ = q.shape
    n_pages = page_table.shape[1]
    kern = functools.partial(_sdsa_attn_kernel, t_len=t_len, n_pages=n_pages)
    tok = lambda w: pl.BlockSpec((1, t_len, w), lambda b_, j, pt: (b_, 0, 0))
    rows = (N_HEADS // N_KV) * t_len
    page_buf = pltpu.VMEM((PAGE_SLOTS, PAGES_PER_STEP, N_KV, HEAD_DIM, PAGE_SIZE), F32)
    return pl.pallas_call(
        kern,
        out_shape=jax.ShapeDtypeStruct((bsz, t_len, 512), F32),
        grid_spec=pltpu.PrefetchScalarGridSpec(
            num_scalar_prefetch=1,
            grid=(bsz, n_pages // PAGES_PER_STEP),
            in_specs=[tok(512), tok(LANES), tok(LANES),
                      pl.BlockSpec((1, PAGES_PER_STEP, t_len, LANES), lambda b_, j, pt: (b_, j, 0, 0)),
                      tok(LANES), pl.BlockSpec(memory_space=pl.ANY), pl.BlockSpec(memory_space=pl.ANY)],
            out_specs=pl.BlockSpec((1, t_len, 512), lambda b_, j, pt: (b_, 0, 0)),
            scratch_shapes=[pltpu.VMEM((N_KV, rows, HEAD_DIM), BF16), pltpu.VMEM((N_KV, rows, 1), F32),
                            pltpu.VMEM((N_KV, rows, 1), F32), pltpu.VMEM((N_KV, rows, HEAD_DIM), F32),
                            page_buf, page_buf,
                            pltpu.SemaphoreType.DMA((PAGE_SLOTS,)), pltpu.SemaphoreType.DMA((PAGE_SLOTS,))]),
        compiler_params=_cparams(("arbitrary", "arbitrary")),
        name="sdsa_attn",
    )(page_table, q, k_new, v_new, mask_past, mask_new, cache_k, cache_v)


HEAD_ORDER = (0, 4, 1, 5, 2, 6, 3, 7)


def _pad_cols(w, n):
    return jnp.pad(w, ((0, 0), (0, n - w.shape[1])))


def _row(v):
    return v.reshape(1, -1).astype(F32)


def _rope_tables(pos):
    half = ROT_DIM // 2
    inv = ROPE_THETA ** (-jnp.arange(half, dtype=F32) / half)
    ang = pos.astype(F32)[:, None] * inv[None, :]
    cos, sin = jnp.cos(ang), jnp.sin(ang)
    n = pos.shape[0]
    rest = HEAD_DIM - ROT_DIM
    c64 = jnp.concatenate([cos, cos, jnp.ones((n, rest), F32)], axis=1)
    s64 = jnp.concatenate([-sin, sin, jnp.zeros((n, rest), F32)], axis=1)
    return jnp.tile(c64, (1, 2)), jnp.tile(s64, (1, 2))


def _even_layer(x, prev_a, prev_ssm, prev_b, lc, p, *, tm, tl_a):
    bsz, L, d = x.shape
    m = bsz * L
    x2 = x.reshape(m, d)
    zmain, ztail = norm_proj(x2, p["e_norm_mix"], p["e_w_main"], p["e_w_tail"], tm=tm, tn=E_MAIN)
    zmain3 = zmain.reshape(bsz, L, E_MAIN)
    prev32 = jnp.pad(prev_a, ((0, 0), (CARRY_A - (CONV_A - 1), 0), (0, 0)))
    ya, new_a = conv_a(zmain3, prev32, p["e_conv_a_w"], p["e_conv_a_b"], p["e_ln_a_g"], p["e_ln_a_b"], tl=tl_a)
    prev8 = jnp.pad(prev_b, ((0, 0), (CARRY_B - (SSM_CONV - 1), 0), (0, 0)))
    yb, new_b, hout = ssd_mixer(zmain3, ztail.reshape(bsz, L, LANES), prev8,
                                prev_ssm.reshape(bsz, D_INNER, SSM_STATE),
                                p["e_conv_b_w"], p["e_conv_b_b"], p["e_dt_bias"], p["e_a_neg"], p["e_expand"],
                                p["e_dskip"], p["e_ssm_norm"], lc=lc)
    x2 = out_proj(x2, ya.reshape(m, D_A), yb.reshape(m, D_INNER), p["e_wo_a"], p["e_wo_b"], tm=min(m, 1024))
    x2 = swiglu_ffn(x2, p["e_norm_ffn"], p["e_w_gate"], p["e_w_up"], p["e_w_down"], tm=min(m, 512), tf=FF_TILE)
    return (x2.reshape(bsz, L, d), new_a[:, CARRY_A - (CONV_A - 1):], hout.reshape(bsz, SSM_HEADS, SSM_HEADDIM, SSM_STATE),
            new_b[:, CARRY_B - (SSM_CONV - 1):])


def _odd_project(x2, cos_t, sin_t, p, *, tm, seq_len=None):
    zmain, ztail = norm_proj(x2, p["o_norm_mix"], p["o_w_main"], p["o_w_tail"], tm=tm, tn=O_MAIN)
    q, k, v, qi, kiw, k16, v16, kk16, *kit = odd_post(zmain, ztail, cos_t, sin_t, p["o_q_norm"], p["o_k_norm"],
                                                      p["o_kidx_g"], p["o_kidx_b"], p["o_bd"], tm=tm,
                                                      seq_len=seq_len)
    return zmain, q, k, v, qi, kiw, (k16, v16, kk16, *kit)


def _odd_tail(x2, att, zmain3, prev_d, p, *, tm, tl_d):
    bsz, L, _ = zmain3.shape
    m = bsz * L
    prev8 = jnp.pad(prev_d, ((0, 0), (CARRY_D - (CONV_D - 1), 0), (0, 0)))
    dm, new_d = conv_d(zmain3, prev8, p["o_conv_d_w"], tl=tl_d)
    x2 = out_proj(x2, att.reshape(m, 512), dm.reshape(m, D_D), p["o_wo_a"], p["o_wo_b"], tm=min(m, 1024))
    if (2 * m) % MOE_TILE == 0 and 2 * m >= N_EXPERTS * MOE_TILE:
        x2 = moe_routed(x2, p["o_norm_ffn"], p["o_wr_hi"], p["o_wr_lo"], p["o_br"],
                        p["o_we_gate"], p["o_we_up"], p["o_we_down"], tm=tm)
    else:
        xn, comb = moe_router(x2, p["o_norm_ffn"], p["o_wr_hi"], p["o_wr_lo"], p["o_br"], tm=tm)
        x2 = moe_dense(x2, xn, comb, p["o_we_gate"], p["o_we_up"], p["o_we_down"], tm=min(m, 1024), tf=FF_TILE)
    return x2, new_d[:, CARRY_D - (CONV_D - 1):]


def kernel(x_prompt, x_sample, state_conv_a, state_ssm, state_conv_b, cache_k, cache_v, cache_kidx, state_conv_d,
           page_table, e_norm_mix, e_w_in, e_conv_a_w, e_conv_a_b, e_ln_a_g, e_ln_a_b, e_conv_b_w, e_conv_b_b,
           e_dt_bias, e_a_log, e_d_skip, e_ssm_norm, e_w_out, e_norm_ffn, e_w_gate, e_w_up, e_w_down,
           o_norm_mix, o_w_in, o_q_norm, o_k_norm, o_kidx_g, o_kidx_b, o_conv_d_w, o_w_out, o_norm_ffn,
           o_w_router, o_b_router, o_we_gate, o_we_up, o_we_down):
    bp, s_len, d = x_prompt.shape
    bd, t_len, _ = x_sample.shape
    n_pairs = e_w_in.shape[0]
    past = page_table.shape[1] * PAGE_SIZE
    n_pool = cache_k.shape[1]
    xp, xs = x_prompt, x_sample
    outs_p = [[] for _ in range(7)]
    outs_s = [[] for _ in range(7)]
    cos_p, sin_p = _rope_tables(jnp.arange(s_len))
    cos_s, sin_s = _rope_tables(jnp.tile(past + jnp.arange(t_len), bd))
    perm = np.concatenate([np.arange(h * HEAD_DIM, (h + 1) * HEAD_DIM) for h in HEAD_ORDER])
    expand = (jnp.arange(LANES)[:, None] == (jnp.arange(D_INNER)[None, :] // SSM_HEADDIM)).astype(BF16)
    blk = jnp.arange(LANES) // HEAD_DIM
    bdiag = jnp.where(blk[:, None] == blk[None, :], 1.0 / HEAD_DIM, 0.0).astype(BF16)
    for i in range(n_pairs):
        w = e_w_in[i]
        p = {
            "e_norm_mix": _row(e_norm_mix[i]),
            "e_w_main": jnp.concatenate([w[:, 2048:3584], w[:, 0:512], w[:, 1024:2048], w[:, 512:1024]],
                                        axis=1).astype(BF16),
            "e_w_tail": _pad_cols(w[:, 3584:3600], LANES).astype(BF16),
            "e_conv_a_w": e_conv_a_w[i], "e_conv_a_b": _row(e_conv_a_b[i]),
            "e_ln_a_g": _row(e_ln_a_g[i]), "e_ln_a_b": _row(e_ln_a_b[i]),
            "e_conv_b_w": e_conv_b_w[i], "e_conv_b_b": _row(e_conv_b_b[i]),
            "e_dt_bias": _pad_cols(_row(e_dt_bias[i]), LANES),
            "e_a_neg": _pad_cols(_row(-jnp.exp(e_a_log[i].astype(F32))), LANES),
            "e_expand": expand,
            "e_dskip": _row(jnp.repeat(e_d_skip[i], SSM_HEADDIM)),
            "e_ssm_norm": _row(e_ssm_norm[i]),
            "e_wo_a": e_w_out[i][:D_A].astype(BF16), "e_wo_b": e_w_out[i][D_A:].astype(BF16),
            "e_norm_ffn": _row(e_norm_ffn[i]),
            "e_w_gate": e_w_gate[i].astype(BF16), "e_w_up": e_w_up[i].astype(BF16),
            "e_w_down": e_w_down[i].astype(BF16),
        }
        zero_a = jnp.zeros((bp, CONV_A - 1, D_A), F32)
        zero_h = jnp.zeros((bp, SSM_HEADS, SSM_HEADDIM, SSM_STATE), F32)
        zero_b = jnp.zeros((bp, SSM_CONV - 1, CONV_DIM), F32)
        xp, ca, sm, cb = _even_layer(xp, zero_a, zero_h, zero_b, min(SSD_CHUNK, s_len), p, tm=512, tl_a=256)
        outs_p[0].append(ca); outs_p[1].append(sm); outs_p[2].append(cb)
        xs, ca, sm, cb = _even_layer(xs, state_conv_a[i], state_ssm[i], state_conv_b[i], t_len, p,
                                     tm=bd * t_len, tl_a=t_len)
        outs_s[0].append(ca); outs_s[1].append(sm); outs_s[2].append(cb)

        w = o_w_in[i]
        wo = o_w_out[i]
        p = {
            "o_norm_mix": _row(o_norm_mix[i]),
            "o_w_main": jnp.concatenate([w[:, 0:512][:, perm], w[:, 512:1024], w[:, 1092:2628]], axis=1).astype(BF16),
            "o_w_tail": _pad_cols(w[:, 1024:1092], LANES).astype(BF16),
            "o_q_norm": _row(jnp.tile(o_q_norm[i], 2)), "o_k_norm": _row(jnp.tile(o_k_norm[i], 2)),
            "o_kidx_g": _pad_cols(_row(o_kidx_g[i]), LANES), "o_kidx_b": _pad_cols(_row(o_kidx_b[i]), LANES),
            "o_bd": bdiag,
            "o_conv_d_w": o_conv_d_w[i],
            "o_wo_a": wo[:512][perm].astype(BF16), "o_wo_b": wo[512:].astype(BF16),
            "o_norm_ffn": _row(o_norm_ffn[i]),
            "o_br": jnp.concatenate([_row(o_b_router[i]), jnp.full((1, LANES - N_EXPERTS), NEG, F32)], axis=1),
            "o_we_gate": o_we_gate[i].astype(BF16), "o_we_up": o_we_up[i].astype(BF16),
            "o_we_down": o_we_down[i].astype(BF16),
        }
        wr = _pad_cols(o_w_router[i], LANES)
        p["o_wr_hi"] = wr.astype(BF16)
        p["o_wr_lo"] = (wr - p["o_wr_hi"].astype(F32)).astype(BF16)

        mp = bp * s_len
        x2 = xp.reshape(mp, d)
        zmain, q, k_t, v_t, qi, kiw, (k16, v16, kk16, ki_t) = _odd_project(x2, cos_p, sin_p, p, tm=512,
                                                                           seq_len=s_len)
        r3 = lambda a: a.reshape(bp, s_len, a.shape[-1])
        att = dsa_prompt(r3(q), r3(qi), r3(kiw), r3(k16), r3(v16), r3(kk16))
        x2, cd = _odd_tail(x2, att, r3(zmain), jnp.zeros((bp, CONV_D - 1, D_D), F32), p, tm=512, tl_d=512)
        xp = x2.reshape(bp, s_len, d)
        outs_p[3].append(jnp.transpose(k_t.reshape(bp, N_KV, HEAD_DIM, s_len), (0, 3, 1, 2)))
        outs_p[4].append(jnp.transpose(v_t.reshape(bp, N_KV, HEAD_DIM, s_len), (0, 3, 1, 2)))
        outs_p[5].append(jnp.transpose(ki_t, (0, 2, 1))); outs_p[6].append(cd)

        ms = bd * t_len
        x2 = xs.reshape(ms, d)
        zmain, q, k, v, qi, kiw, _ = _odd_project(x2, cos_s, sin_s, p, tm=ms)
        r3 = lambda a: a.reshape(bd, t_len, a.shape[-1])
        mask_past, mask_new = sdsa_index(page_table, r3(qi), r3(kiw), r3(kiw),
                                         jnp.transpose(cache_kidx[i], (0, 2, 1)))
        att = sdsa_attn(page_table, r3(q), r3(k), r3(v), mask_past, mask_new,
                        jnp.transpose(cache_k[i], (0, 2, 3, 1)), jnp.transpose(cache_v[i], (0, 2, 3, 1)))
        x2, cd = _odd_tail(x2, att, r3(zmain), state_conv_d[i], p, tm=ms, tl_d=t_len)
        xs = x2.reshape(bd, t_len, d)
        outs_s[3].append(k.reshape(bd, t_len, N_KV, HEAD_DIM)); outs_s[4].append(v.reshape(bd, t_len, N_KV, HEAD_DIM))
        outs_s[5].append(kiw[:, :IDX_DIM].reshape(bd, t_len, IDX_DIM)); outs_s[6].append(cd)
    return (xp, xs) + tuple(jnp.stack(o) for o in outs_p) + tuple(jnp.stack(o) for o in outs_s)
```

```python
import functools
import math

import jax
import jax.numpy as jnp
import numpy as np
from jax import lax
from jax.experimental import pallas as pl
from jax.experimental.pallas import tpu as pltpu

F32 = jnp.float32
BF16 = jnp.bfloat16
I32 = jnp.int32

D_MODEL = 1024
D_A = 512
CONV_A = 31
D_INNER = 1024
SSM_HEADDIM = 64
SSM_HEADS = 16
SSM_GROUPS = 2
SSM_STATE = 128
SSM_CONV = 4
CONV_DIM = D_INNER + 2 * SSM_GROUPS * SSM_STATE
SSD_CHUNK = 128
HEAD_DIM = 64
N_HEADS = 8
N_KV = 2
ROT_DIM = 16
ROPE_THETA = 500000.0
IDX_HEADS = 4
IDX_DIM = 64
TOPK_MAX = 256
Q_BLOCK = 128
D_D = 512
CONV_D = 3
D_FF = 2816
N_EXPERTS = 8
PAGE_SIZE = 128
EPS = 1e-6
NEG = -1e30

LANES = 128
SUBLANES = 8
VMEM_LIMIT = 56 * 1024 * 1024

E_MAIN = D_A + D_A + D_INNER + CONV_DIM
E_COL_XBC, E_COL_VAL, E_COL_Z, E_COL_GATE = 0, 1536, 2048, 3072
O_MAIN = 512 + 128 + 128 + 256 + 3 * D_D


def _cparams(sem):
    return pltpu.CompilerParams(dimension_semantics=sem, vmem_limit_bytes=VMEM_LIMIT)


def _bdot(a, b):
    return jnp.dot(a.astype(BF16), b.astype(BF16), preferred_element_type=F32)


def _bdot_nt(a, b):
    return lax.dot_general(a.astype(BF16), b.astype(BF16), (((1,), (1,)), ((), ())),
                           preferred_element_type=F32)


def _split3(a):
    hi = a.astype(BF16)
    r1 = a - hi.astype(F32)
    mid = r1.astype(BF16)
    lo = (r1 - mid.astype(F32)).astype(BF16)
    return hi, mid, lo


def _dot_exact_rhs(a, b_bf16):
    hi, mid, lo = _split3(a)
    out = jnp.dot(lo, b_bf16, preferred_element_type=F32)
    out = out + jnp.dot(mid, b_bf16, preferred_element_type=F32)
    return out + jnp.dot(hi, b_bf16, preferred_element_type=F32)


def _dot_exact_lhs(a_bf16, b):
    hi, mid, lo = _split3(b)
    out = jnp.dot(a_bf16, lo, preferred_element_type=F32)
    out = out + jnp.dot(a_bf16, mid, preferred_element_type=F32)
    return out + jnp.dot(a_bf16, hi, preferred_element_type=F32)


def _sigmoid(x):
    return 1.0 / (1.0 + jnp.exp(-x))


def _silu(x):
    return x * _sigmoid(x)


def _softplus(x):
    return jnp.maximum(x, 0.0) + jnp.log(1.0 + jnp.exp(-jnp.abs(x)))


def _norm_proj_kernel(x_ref, g_ref, w_ref, wt_ref, main_ref, tail_ref, xn_ref):
    j = pl.program_id(1)

    @pl.when(j == 0)
    def _():
        x = x_ref[...]
        y = x * lax.rsqrt(jnp.mean(x * x, axis=-1, keepdims=True) + EPS) * g_ref[...]
        xn = y.astype(BF16)
        xn_ref[...] = xn
        tail_ref[...] = jnp.dot(xn, wt_ref[...], preferred_element_type=F32)

    main_ref[...] = jnp.dot(xn_ref[...], w_ref[...], preferred_element_type=F32)


def norm_proj(x, g, w_main, w_tail, *, tm, tn):
    m, k = x.shape
    n = w_main.shape[1]
    return pl.pallas_call(
        _norm_proj_kernel,
        out_shape=(jax.ShapeDtypeStruct((m, n), F32), jax.ShapeDtypeStruct((m, LANES), F32)),
        grid=(m // tm, n // tn),
        in_specs=[pl.BlockSpec((tm, k), lambda i, j: (i, 0)),
                  pl.BlockSpec((1, k), lambda i, j: (0, 0)),
                  pl.BlockSpec((k, tn), lambda i, j: (0, j)),
                  pl.BlockSpec((k, LANES), lambda i, j: (0, 0))],
        out_specs=(pl.BlockSpec((tm, tn), lambda i, j: (i, j)),
                   pl.BlockSpec((tm, LANES), lambda i, j: (i, 0))),
        scratch_shapes=[pltpu.VMEM((tm, k), BF16)],
        compiler_params=_cparams(("arbitrary", "arbitrary")),
        name="norm_proj",
    )(x, g, w_main, w_tail)


def _dwconv_from_buf(buf_ref, w_ref, kw, carry, tl):
    acc = None
    for k in range(kw):
        start = carry - (kw - 1) + k
        term = buf_ref[pl.ds(start, tl), :] * w_ref[pl.ds(k, 1), :]
        acc = term if acc is None else acc + term
    return acc


def _dwconv_phased(buf_ref, w_ref, kw, carry, tl):
    base = carry - (kw - 1)
    tz = tl + 2 * SUBLANES
    acc = None
    for b in range(SUBLANES):
        z = None
        for a in range((kw - 1 - b) // SUBLANES + 1):
            term = buf_ref[pl.ds(SUBLANES * a, tz), :] * w_ref[pl.ds(SUBLANES * a + b, 1), :]
            z = term if z is None else z + term
        shifted = z[base + b:base + b + tl]
        acc = shifted if acc is None else acc + shifted
    return acc


PHASE_PAD = 8


CARRY_A = 32


def _conv_a_kernel(val_ref, gate_ref, prev_ref, w_ref, b_ref, lg_ref, lb_ref, ya_ref, new_ref, buf_ref, *, tl):
    l = pl.program_id(1)

    @pl.when(l == 0)
    def _():
        buf_ref[pl.ds(0, CARRY_A), :] = prev_ref[0]
        buf_ref[pl.ds(CARRY_A + tl, PHASE_PAD), :] = jnp.zeros((PHASE_PAD, D_A), F32)

    u = val_ref[0] * _sigmoid(gate_ref[0])
    buf_ref[pl.ds(CARRY_A, tl), :] = u
    y = _dwconv_phased(buf_ref, w_ref, CONV_A, CARRY_A, tl) + b_ref[...]
    mu = jnp.mean(y, axis=-1, keepdims=True)
    yc = y - mu
    var = jnp.mean(yc * yc, axis=-1, keepdims=True)
    yn = yc * lax.rsqrt(var + EPS) * lg_ref[...] + lb_ref[...]
    ya_ref[0] = _silu(yn)
    hist = buf_ref[pl.ds(tl, CARRY_A), :]
    new_ref[0] = hist
    buf_ref[pl.ds(0, CARRY_A), :] = hist


def conv_a(zmain, prev32, w, b, lg, lb, *, tl):
    bsz, L, _ = zmain.shape
    kern = functools.partial(_conv_a_kernel, tl=tl)
    return pl.pallas_call(
        kern,
        out_shape=(jax.ShapeDtypeStruct((bsz, L, D_A), F32), jax.ShapeDtypeStruct((bsz, CARRY_A, D_A), F32)),
        grid=(bsz, L // tl),
        in_specs=[pl.BlockSpec((1, tl, D_A), lambda b_, l: (b_, l, E_COL_VAL // D_A)),
                  pl.BlockSpec((1, tl, D_A), lambda b_, l: (b_, l, E_COL_GATE // D_A)),
                  pl.BlockSpec((1, CARRY_A, D_A), lambda b_, l: (b_, 0, 0)),
                  pl.BlockSpec((CONV_A, D_A), lambda b_, l: (0, 0)),
                  pl.BlockSpec((1, D_A), lambda b_, l: (0, 0)),
                  pl.BlockSpec((1, D_A), lambda b_, l: (0, 0)),
                  pl.BlockSpec((1, D_A), lambda b_, l: (0, 0))],
        out_specs=(pl.BlockSpec((1, tl, D_A), lambda b_, l: (b_, l, 0)),
                   pl.BlockSpec((1, CARRY_A, D_A), lambda b_, l: (b_, 0, 0))),
        scratch_shapes=[pltpu.VMEM((CARRY_A + tl + PHASE_PAD, D_A), F32)],
        compiler_params=_cparams(("arbitrary", "arbitrary")),
        name="conv_a",
    )(zmain, zmain, prev32, w, b, lg, lb)


CARRY_B = 8
SSD_LP = 128


def _ssd_kernel(xbc_ref, z_ref, dt_ref, prevb_ref, h0_ref, cw_ref, cb_ref, dtb_ref, a_ref, expand_ref,
                dskip_ref, nrm_ref, y_ref, newb_ref, hout_ref, buf_ref, ht_ref, *, lc):
    c = pl.program_id(1)
    nc = pl.num_programs(1)
    lp = SSD_LP

    @pl.when(c == 0)
    def _():
        buf_ref[pl.ds(0, CARRY_B), :] = prevb_ref[0]
        ht_ref[...] = h0_ref[0].T
        if lc < lp:
            buf_ref[pl.ds(CARRY_B + lc, lp - lc), :] = jnp.zeros((lp - lc, CONV_DIM), F32)

    buf_ref[pl.ds(CARRY_B, lc), :] = xbc_ref[0]
    xbc = _silu(_dwconv_from_buf(buf_ref, cw_ref, SSM_CONV, CARRY_B, lp) + cb_ref[...])
    hist = buf_ref[pl.ds(lc, CARRY_B), :]
    newb_ref[0] = hist
    buf_ref[pl.ds(0, CARRY_B), :] = hist

    xs = xbc[:, :D_INNER]
    if lc < lp:
        dt_raw = jnp.concatenate([dt_ref[0], jnp.zeros((lp - lc, LANES), F32)], axis=0)
        zg = jnp.concatenate([z_ref[0], jnp.zeros((lp - lc, D_INNER), F32)], axis=0)
    else:
        dt_raw = dt_ref[0]
        zg = z_ref[0]
    dt = _softplus(dt_raw + dtb_ref[...])
    if lc < lp:
        row = lax.broadcasted_iota(I32, (lp, LANES), 0)
        dt = jnp.where(row < lc, dt, 0.0)
    lane = lax.broadcasted_iota(I32, (lp, LANES), 1)
    dt = jnp.where(lane < SSM_HEADS, dt, 0.0)
    da = dt * a_ref[...]

    r_i = lax.broadcasted_iota(I32, (lp, lp), 0)
    c_i = lax.broadcasted_iota(I32, (lp, lp), 1)
    causal = r_i >= c_i
    tri = jnp.where(causal, 1.0, 0.0).astype(BF16)
    a_col = _dot_exact_lhs(tri, da)
    a_row = a_col.T
    dt_row = dt.T
    a_last = a_col[lp - 1:lp, :]

    expand = expand_ref[...]
    a_exp = _dot_exact_rhs(a_col, expand)
    dt_exp = _dot_exact_rhs(dt, expand)
    al_exp = _dot_exact_rhs(a_last, expand)

    ht = ht_ref[...]
    y_parts = []
    for g in range(SSM_GROUPS):
        bm = xbc[:, D_INNER + g * SSM_STATE: D_INNER + (g + 1) * SSM_STATE]
        cm = xbc[:, D_INNER + (SSM_GROUPS + g) * SSM_STATE: D_INNER + (SSM_GROUPS + g + 1) * SSM_STATE]
        cb = _bdot_nt(cm, bm)
        gs = slice(g * 512, (g + 1) * 512)
        y_inter = _bdot(cm, ht[:, gs]) * jnp.exp(a_exp[:, gs])
        heads = []
        for r in range(SSM_HEADS // SSM_GROUPS):
            h = g * (SSM_HEADS // SSM_GROUPS) + r
            seg = a_col[:, h:h + 1] - a_row[h:h + 1, :]
            decay = jnp.where(causal, jnp.exp(jnp.where(causal, seg, 0.0)), 0.0)
            wts = cb * decay * dt_row[h:h + 1, :]
            heads.append(_bdot(wts, xs[:, h * SSM_HEADDIM:(h + 1) * SSM_HEADDIM]))
        y_parts.append(jnp.concatenate(heads, axis=-1) + y_inter)
        xw = xs[:, gs] * (jnp.exp(al_exp[:, gs] - a_exp[:, gs]) * dt_exp[:, gs])
        ht_ref[:, gs] = ht[:, gs] * jnp.exp(al_exp[:, gs]) + _bdot(bm.T, xw)
    y = jnp.concatenate(y_parts, axis=-1)
    y = y + dskip_ref[...] * xs
    y = y * _silu(zg)
    outs = []
    for g in range(SSM_GROUPS):
        yg = y[:, g * 512:(g + 1) * 512]
        outs.append(yg * lax.rsqrt(jnp.mean(yg * yg, axis=-1, keepdims=True) + EPS))
    yb = jnp.concatenate(outs, axis=-1) * nrm_ref[...]
    y_ref[0] = yb[:lc, :]

    @pl.when(c == nc - 1)
    def _():
        hout_ref[0] = ht_ref[...].T


def ssd_mixer(zmain, ztail, prev8, h0, cw, cb, dtb, a_neg, expand, dskip_exp, nrm, *, lc):
    bsz, L, _ = zmain.shape
    kern = functools.partial(_ssd_kernel, lc=lc)
    return pl.pallas_call(
        kern,
        out_shape=(jax.ShapeDtypeStruct((bsz, L, D_INNER), F32),
                   jax.ShapeDtypeStruct((bsz, CARRY_B, CONV_DIM), F32),
                   jax.ShapeDtypeStruct((bsz, D_INNER, SSM_STATE), F32)),
        grid=(bsz, L // lc),
        in_specs=[pl.BlockSpec((1, lc, CONV_DIM), lambda b_, c: (b_, c, E_COL_XBC // CONV_DIM)),
                  pl.BlockSpec((1, lc, D_INNER), lambda b_, c: (b_, c, E_COL_Z // D_INNER)),
                  pl.BlockSpec((1, lc, LANES), lambda b_, c: (b_, c, 0)),
                  pl.BlockSpec((1, CARRY_B, CONV_DIM), lambda b_, c: (b_, 0, 0)),
                  pl.BlockSpec((1, D_INNER, SSM_STATE), lambda b_, c: (b_, 0, 0)),
                  pl.BlockSpec((SSM_CONV, CONV_DIM), lambda b_, c: (0, 0)),
                  pl.BlockSpec((1, CONV_DIM), lambda b_, c: (0, 0)),
                  pl.BlockSpec((1, LANES), lambda b_, c: (0, 0)),
                  pl.BlockSpec((1, LANES), lambda b_, c: (0, 0)),
                  pl.BlockSpec((LANES, D_INNER), lambda b_, c: (0, 0)),
                  pl.BlockSpec((1, D_INNER), lambda b_, c: (0, 0)),
                  pl.BlockSpec((1, D_INNER), lambda b_, c: (0, 0))],
        out_specs=(pl.BlockSpec((1, lc, D_INNER), lambda b_, c: (b_, c, 0)),
                   pl.BlockSpec((1, CARRY_B, CONV_DIM), lambda b_, c: (b_, 0, 0)),
                   pl.BlockSpec((1, D_INNER, SSM_STATE), lambda b_, c: (b_, 0, 0))),
        scratch_shapes=[pltpu.VMEM((CARRY_B + SSD_LP, CONV_DIM), F32),
                        pltpu.VMEM((SSM_STATE, D_INNER), F32)],
        compiler_params=_cparams(("arbitrary", "arbitrary")),
        name="ssd_mixer",
    )(zmain, zmain, ztail, prev8, h0, cw, cb, dtb, a_neg, expand, dskip_exp, nrm)


def _out_proj_kernel(x_ref, a_ref, b_ref, wa_ref, wb_ref, o_ref):
    acc = jnp.dot(a_ref[...].astype(BF16), wa_ref[...], preferred_element_type=F32)
    acc = acc + jnp.dot(b_ref[...].astype(BF16), wb_ref[...], preferred_element_type=F32)
    o_ref[...] = x_ref[...] + acc


def out_proj(x, a, b, wa, wb, *, tm):
    m, d = x.shape
    ka, kb = a.shape[1], b.shape[1]
    return pl.pallas_call(
        _out_proj_kernel,
        out_shape=jax.ShapeDtypeStruct((m, d), F32),
        grid=(m // tm,),
        in_specs=[pl.BlockSpec((tm, d), lambda i: (i, 0)),
                  pl.BlockSpec((tm, ka), lambda i: (i, 0)),
                  pl.BlockSpec((tm, kb), lambda i: (i, 0)),
                  pl.BlockSpec((ka, d), lambda i: (0, 0)),
                  pl.BlockSpec((kb, d), lambda i: (0, 0))],
        out_specs=pl.BlockSpec((tm, d), lambda i: (i, 0)),
        compiler_params=_cparams(("arbitrary",)),
        name="out_proj",
    )(x, a, b, wa, wb)


def _swiglu_kernel(x_ref, g_ref, wg_ref, wu_ref, wd_ref, o_ref, xn_ref, acc_ref):
    f = pl.program_id(1)
    nf = pl.num_programs(1)

    @pl.when(f == 0)
    def _():
        x = x_ref[...]
        y = x * lax.rsqrt(jnp.mean(x * x, axis=-1, keepdims=True) + EPS) * g_ref[...]
        xn_ref[...] = y.astype(BF16)
        acc_ref[...] = jnp.zeros_like(acc_ref)

    xn = xn_ref[...]
    gate = jnp.dot(xn, wg_ref[...], preferred_element_type=F32)
    up = jnp.dot(xn, wu_ref[...], preferred_element_type=F32)
    hid = (_silu(gate) * up).astype(BF16)
    acc_ref[...] += jnp.dot(hid, wd_ref[...], preferred_element_type=F32)

    @pl.when(f == nf - 1)
    def _():
        o_ref[...] = x_ref[...] + acc_ref[...]


def swiglu_ffn(x, g, wg, wu, wd, *, tm, tf):
    m, d = x.shape
    ff = wg.shape[1]
    return pl.pallas_call(
        _swiglu_kernel,
        out_shape=jax.ShapeDtypeStruct((m, d), F32),
        grid=(m // tm, ff // tf),
        in_specs=[pl.BlockSpec((tm, d), lambda i, f: (i, 0)),
                  pl.BlockSpec((1, d), lambda i, f: (0, 0)),
                  pl.BlockSpec((d, tf), lambda i, f: (0, f)),
                  pl.BlockSpec((d, tf), lambda i, f: (0, f)),
                  pl.BlockSpec((tf, d), lambda i, f: (f, 0))],
        out_specs=pl.BlockSpec((tm, d), lambda i, f: (i, 0)),
        scratch_shapes=[pltpu.VMEM((tm, d), BF16), pltpu.VMEM((tm, d), F32)],
        compiler_params=_cparams(("arbitrary", "arbitrary")),
        name="swiglu_ffn",
    )(x, g, wg, wu, wd)


O_COL_Q, O_COL_K, O_COL_V, O_COL_QI, O_COL_DIN, O_COL_BG, O_COL_CG = 0, 512, 640, 768, 1024, 1536, 2048


def _rope128(x, cos_t, sin_t, lo_mask):
    up = pltpu.roll(x, LANES - ROT_DIM // 2, 1)
    dn = pltpu.roll(x, ROT_DIM // 2, 1)
    return x * cos_t + jnp.where(lo_mask, up, dn) * sin_t


def _odd_post_kernel(qkvq_ref, tail_ref, cos_ref, sin_ref, qn_ref, kn_ref, kg_ref, kb_ref, bd_ref,
                     q_ref, k_ref, v_ref, qi_ref, kiw_ref, k16_ref, v16_ref, kk16_ref, kit_ref=None):
    cos_t = cos_ref[...]
    sin_t = sin_ref[...]
    lane = lax.broadcasted_iota(I32, cos_t.shape, 1)
    lo_mask = (lane % HEAD_DIM) < (ROT_DIM // 2)
    bd = bd_ref[...]
    z = qkvq_ref[...]
    for c in range(4):
        x = z[:, c * LANES:(c + 1) * LANES]
        ms = _dot_exact_rhs(x * x, bd)
        xn = x * lax.rsqrt(ms + EPS) * qn_ref[...]
        q_ref[:, c * LANES:(c + 1) * LANES] = _rope128(xn, cos_t, sin_t, lo_mask)
    x = z[:, O_COL_K:O_COL_K + LANES]
    ms = _dot_exact_rhs(x * x, bd)
    xn = x * lax.rsqrt(ms + EPS) * kn_ref[...]
    kr = _rope128(xn, cos_t, sin_t, lo_mask)
    k16_ref[...] = kr.astype(BF16)
    vv = z[:, O_COL_V:O_COL_V + LANES]
    v16_ref[...] = vv.astype(BF16)
    if kit_ref is None:
        k_ref[...] = kr
        v_ref[...] = vv
    else:
        k_ref[0] = kr.T
        v_ref[0] = vv.T
    for c in range(2):
        x = z[:, O_COL_QI + c * LANES: O_COL_QI + (c + 1) * LANES]
        qi_ref[:, c * LANES:(c + 1) * LANES] = _rope128(x, cos_t, sin_t, lo_mask)
    t = tail_ref[...]
    lane_t = lax.broadcasted_iota(I32, t.shape, 1)
    is_ki = lane_t < IDX_DIM
    mu = _dot_exact_rhs(t, bd)
    tc = t - mu
    var = _dot_exact_rhs(tc * tc, bd)
    kin = tc * lax.rsqrt(var + EPS) * kg_ref[...] + kb_ref[...]
    kin = _rope128(kin, cos_t, sin_t, lo_mask)
    kiw_ref[...] = jnp.where(is_ki, kin, t * (IDX_HEADS ** -0.5 * IDX_DIM ** -0.5))
    kk16_ref[...] = jnp.where(is_ki, kin, pltpu.roll(kin, IDX_DIM, 1)).astype(BF16)
    if kit_ref is not None:
        kit_ref[0] = kin.T[:IDX_DIM]


def odd_post(zmain, ztail, cos_t, sin_t, qn, kn, kg, kb, bd, *, tm, seq_len=None):
    m = zmain.shape[0]
    nrope = cos_t.shape[0] // tm
    row = lambda i: (i, 0)
    rope = lambda i: (i % nrope, 0)
    const = lambda i: (0, 0)
    tok = lambda w, dt=F32: (jax.ShapeDtypeStruct((m, w), dt), pl.BlockSpec((tm, w), row))
    if seq_len is None:
        kv = [tok(LANES), tok(LANES)]
        extra = []
    else:
        nblk = seq_len // tm
        bsz = m // seq_len
        seq = lambda f: (jax.ShapeDtypeStruct((bsz, f, seq_len), F32),
                         pl.BlockSpec((1, f, tm), lambda i: (i // nblk, 0, i % nblk)))
        kv = [seq(LANES), seq(LANES)]
        extra = [seq(IDX_DIM)]
    outs = [tok(512)] + kv + [tok(256), tok(LANES), tok(LANES, BF16), tok(LANES, BF16), tok(LANES, BF16)] + extra
    return pl.pallas_call(
        _odd_post_kernel,
        out_shape=tuple(o[0] for o in outs),
        grid=(m // tm,),
        in_specs=[pl.BlockSpec((tm, 1024), row), pl.BlockSpec((tm, LANES), row),
                  pl.BlockSpec((tm, LANES), rope), pl.BlockSpec((tm, LANES), rope),
                  pl.BlockSpec((1, LANES), const), pl.BlockSpec((1, LANES), const),
                  pl.BlockSpec((1, LANES), const), pl.BlockSpec((1, LANES), const),
                  pl.BlockSpec((LANES, LANES), const)],
        out_specs=tuple(o[1] for o in outs),
        compiler_params=_cparams(("arbitrary",)),
        name="odd_post",
    )(zmain, ztail, cos_t, sin_t, qn, kn, kg, kb, bd)


INT_MIN = -2 ** 31
F32_MIN_NORMAL_BITS = 0x00800000


def _code_to_float(c):
    return pltpu.bitcast(c ^ ((c >> 31) & 0x7FFFFFFF), F32)


def _kth_largest(count_ge, k, shape):
    ans = jnp.full(shape, INT_MIN, I32)
    cand = jnp.zeros(shape, I32)
    ans = jnp.where(count_ge(_code_to_float(cand)) >= k, cand, ans)

    def body(i, ans):
        cand = ans | jnp.left_shift(jnp.int32(1), 30 - i)
        return jnp.where(count_ge(_code_to_float(cand)) >= k, cand, ans)

    ans = lax.fori_loop(0, 31, body, ans)
    ans = jnp.where((ans > 0) & (ans < F32_MIN_NORMAL_BITS), 0, ans)
    return _code_to_float(ans)


def _sum_leading(x):
    n = x.shape[0]
    extra = None
    while n > 1:
        if n % 2:
            extra = x[n - 1] if extra is None else extra + x[n - 1]
            n -= 1
        x = x[:n // 2] + x[n // 2:n]
        n //= 2
    return x[0] if extra is None else x[0] + extra


def _upper_tri_bf16(n):
    r = lax.broadcasted_iota(I32, (n, n), 0)
    c = lax.broadcasted_iota(I32, (n, n), 1)
    return jnp.where(r <= c, 1.0, 0.0).astype(BF16)


def _dsa_prompt_kernel(q_ref, qi_ref, wi_ref, k_ref, v_ref, kk_ref, o_ref, keys_ref, bias_ref, *,
                       q_base, s_eff, topk, qb):
    t0 = q_base + pl.program_id(1) * qb
    lane = lax.broadcasted_iota(I32, (qb, LANES), 1)
    row = lax.broadcasted_iota(I32, (qb, LANES), 0)
    lo = lane < HEAD_DIM
    hi = lane >= HEAD_DIM
    rpg = N_HEADS // N_KV

    kk = kk_ref[0]
    qi = qi_ref[0]
    wi = wi_ref[0]
    sc = None
    for h in range(IDX_HEADS):
        x = qi[:, (h // 2) * LANES:(h // 2 + 1) * LANES]
        x = jnp.where(lo if h % 2 == 0 else hi, x, 0.0)
        d = _bdot_nt(x, kk)
        term = jnp.maximum(d, 0.0) * wi[:, IDX_DIM + h:IDX_DIM + h + 1]
        sc = term if sc is None else sc + term
    kpos = lax.broadcasted_iota(I32, (qb, s_eff), 1)
    qpos = t0 + lax.broadcasted_iota(I32, (qb, s_eff), 0)
    adm_all = kpos <= qpos
    sc = jnp.where(sc == 0.0, 0.0, sc)
    keys_ref[...] = jnp.where(adm_all, sc, NEG)

    def count_ge(cand):
        acc = None
        for j in range(s_eff // LANES):
            hit = jnp.where(keys_ref[:, j * LANES:(j + 1) * LANES] >= cand, 1.0, 0.0)
            acc = hit if acc is None else acc + hit
        return jnp.sum(acc, axis=-1, keepdims=True)

    thr = _kth_largest(count_ge, float(topk), (qb, 1))
    n_ge = count_ge(thr)
    n_adm = t0 + 1 + lax.broadcasted_iota(I32, (qb, 1), 0)
    simple = jnp.where((n_ge == float(topk)) | (n_adm < topk), 0.0, 1.0)
    no_cut = jnp.max(simple) == 0.0

    @pl.when(no_cut)
    def _():
        bias_ref[...] = jnp.where(adm_all, jnp.where(keys_ref[...] >= thr, 0.0, NEG), NEG)

    @pl.when(jnp.logical_not(no_cut))
    def _():
        n_gt = jnp.sum(jnp.where(keys_ref[...] > thr, 1.0, 0.0), axis=-1, keepdims=True)
        need = float(topk) - n_gt
        ut = _upper_tri_bf16(LANES)
        off = jnp.zeros((qb, 1), F32)
        for j in range(s_eff // LANES):
            kj = keys_ref[:, j * LANES:(j + 1) * LANES]
            eq = kj == thr
            pre = jnp.dot(jnp.where(eq, 1.0, 0.0).astype(BF16), ut, preferred_element_type=F32) + off
            adm = (j * LANES + lane) <= (t0 + row)
            keep = jnp.where(eq, jnp.where(pre <= need, 0.0, NEG), jnp.where(kj > thr, 0.0, NEG))
            bias_ref[:, j * LANES:(j + 1) * LANES] = jnp.where(adm, keep, NEG)
            off = pre[:, LANES - 1:LANES]

    kb = k_ref[0]
    vb = v_ref[0]
    lane_b = lax.broadcasted_iota(I32, (Q_BLOCK, LANES), 1)
    lo_b = lane_b < HEAD_DIM
    hi_b = lane_b >= HEAD_DIM
    lo4 = lax.broadcasted_iota(I32, (rpg * Q_BLOCK, LANES), 1) < HEAD_DIM
    for sub in range(qb // Q_BLOCK):
        rows = pl.ds(sub * Q_BLOCK, Q_BLOCK)
        bias = bias_ref[rows, :]
        q = q_ref[0, rows, :] * (HEAD_DIM ** -0.5)
        outs = []
        for g in range(N_KV):
            keep = lo_b if g == 0 else hi_b
            qg = jnp.concatenate([jnp.where(keep, q[:, r * LANES:(r + 1) * LANES], 0.0) for r in range(rpg)],
                                 axis=0)
            s = _bdot_nt(qg, kb).reshape(rpg, Q_BLOCK, s_eff) + bias[None]
            m = jnp.max(s, axis=-1, keepdims=True)
            e = jnp.exp(s - m)
            l = jnp.sum(e, axis=-1, keepdims=True).reshape(rpg * Q_BLOCK, 1)
            pv = jnp.dot(e.reshape(rpg * Q_BLOCK, s_eff).astype(BF16), vb, preferred_element_type=F32)
            outs.append(pv / l)
        oc = jnp.where(lo4, outs[0], outs[1])
        for r in range(rpg):
            o_ref[0, rows, r * LANES:(r + 1) * LANES] = oc[r * Q_BLOCK:(r + 1) * Q_BLOCK, :]


DSA_SEGMENT = 256


def dsa_prompt(q, qi, wi, k16, v16, kk16):
    bsz, s_len, _ = q.shape
    topk = min(TOPK_MAX, s_len // 4)
    seg = min(DSA_SEGMENT, s_len)
    outs = []
    for si in range(s_len // seg):
        s_eff = (si + 1) * seg
        kern = functools.partial(_dsa_prompt_kernel, q_base=si * seg, s_eff=s_eff, topk=topk, qb=seg)
        qblk = lambda w, si=si: pl.BlockSpec((1, seg, w), lambda b_, i: (b_, si, 0))
        keys = pl.BlockSpec((1, s_eff, LANES), lambda b_, i: (b_, 0, 0))
        outs.append(pl.pallas_call(
            kern,
            out_shape=jax.ShapeDtypeStruct((bsz, seg, 512), F32),
            grid=(bsz, 1),
            in_specs=[qblk(512), qblk(256), qblk(LANES), keys, keys, keys],
            out_specs=pl.BlockSpec((1, seg, 512), lambda b_, i: (b_, 0, 0)),
            scratch_shapes=[pltpu.VMEM((seg, s_eff), F32), pltpu.VMEM((seg, s_eff), F32)],
            compiler_params=_cparams(("arbitrary", "arbitrary")),
            name=f"dsa_prompt_{si}",
        )(q, qi, wi, k16, v16, kk16))
    return jnp.concatenate(outs, axis=1)


CARRY_D = 8


def _conv_d_kernel(din_ref, bg_ref, cg_ref, prev_ref, w_ref, y_ref, new_ref, buf_ref, *, tl):
    l = pl.program_id(1)

    @pl.when(l == 0)
    def _():
        buf_ref[pl.ds(0, CARRY_D), :] = prev_ref[0]

    buf_ref[pl.ds(CARRY_D, tl), :] = cg_ref[0] * din_ref[0]
    y_ref[0] = bg_ref[0] * _dwconv_from_buf(buf_ref, w_ref, CONV_D, CARRY_D, tl)
    hist = buf_ref[pl.ds(tl, CARRY_D), :]
    new_ref[0] = hist
    buf_ref[pl.ds(0, CARRY_D), :] = hist


def conv_d(zmain, prev8, w, *, tl):
    bsz, L, _ = zmain.shape
    kern = functools.partial(_conv_d_kernel, tl=tl)
    col = lambda c: pl.BlockSpec((1, tl, D_D), lambda b_, l: (b_, l, c // D_D))
    return pl.pallas_call(
        kern,
        out_shape=(jax.ShapeDtypeStruct((bsz, L, D_D), F32), jax.ShapeDtypeStruct((bsz, CARRY_D, D_D), F32)),
        grid=(bsz, L // tl),
        in_specs=[col(O_COL_DIN), col(O_COL_BG), col(O_COL_CG),
                  pl.BlockSpec((1, CARRY_D, D_D), lambda b_, l: (b_, 0, 0)),
                  pl.BlockSpec((CONV_D, D_D), lambda b_, l: (0, 0))],
        out_specs=(pl.BlockSpec((1, tl, D_D), lambda b_, l: (b_, l, 0)),
                   pl.BlockSpec((1, CARRY_D, D_D), lambda b_, l: (b_, 0, 0))),
        scratch_shapes=[pltpu.VMEM((CARRY_D + tl, D_D), F32)],
        compiler_params=_cparams(("arbitrary", "arbitrary")),
        name="conv_d",
    )(zmain, zmain, zmain, prev8, w)


def _router_kernel(x_ref, g_ref, whi_ref, wlo_ref, br_ref, xn_ref, comb_ref):
    x = x_ref[...]
    y = x * lax.rsqrt(jnp.mean(x * x, axis=-1, keepdims=True) + EPS) * g_ref[...]
    yhi = y.astype(BF16)
    xn_ref[...] = yhi
    ylo = (y - yhi.astype(F32)).astype(BF16)
    logits = jnp.dot(ylo, whi_ref[...], preferred_element_type=F32)
    logits = logits + jnp.dot(yhi, wlo_ref[...], preferred_element_type=F32)
    logits = logits + jnp.dot(yhi, whi_ref[...], preferred_element_type=F32) + br_ref[...]
    lane = lax.broadcasted_iota(I32, logits.shape, 1).astype(F32)
    m1 = jnp.max(logits, axis=-1, keepdims=True)
    i1 = jnp.min(jnp.where(logits == m1, lane, float(LANES)), axis=-1, keepdims=True)
    rest = jnp.where(lane == i1, -3e38, logits)
    m2 = jnp.max(rest, axis=-1, keepdims=True)
    i2 = jnp.min(jnp.where(rest == m2, lane, float(LANES)), axis=-1, keepdims=True)
    e2 = jnp.exp(m2 - m1)
    g1 = 1.0 / (1.0 + e2)
    g2 = e2 / (1.0 + e2)
    comb_ref[...] = jnp.where(lane == i1, g1, 0.0) + jnp.where(lane == i2, g2, 0.0)


def moe_router(x, g, w_hi, w_lo, b_pad, *, tm):
    m, d = x.shape
    return pl.pallas_call(
        _router_kernel,
        out_shape=(jax.ShapeDtypeStruct((m, d), BF16), jax.ShapeDtypeStruct((m, LANES), F32)),
        grid=(m // tm,),
        in_specs=[pl.BlockSpec((tm, d), lambda i: (i, 0)),
                  pl.BlockSpec((1, d), lambda i: (0, 0)),
                  pl.BlockSpec((d, LANES), lambda i: (0, 0)),
                  pl.BlockSpec((d, LANES), lambda i: (0, 0)),
                  pl.BlockSpec((1, LANES), lambda i: (0, 0))],
        out_specs=(pl.BlockSpec((tm, d), lambda i: (i, 0)), pl.BlockSpec((tm, LANES), lambda i: (i, 0))),
        compiler_params=_cparams(("arbitrary",)),
        name="moe_router",
    )(x, g, w_hi, w_lo, b_pad)


def _moe_dense_kernel(x_ref, xn_ref, comb_ref, wg_ref, wu_ref, wd_ref, o_ref, acc_ref, acce_ref):
    e = pl.program_id(1)
    f = pl.program_id(2)
    ne = pl.num_programs(1)
    nf = pl.num_programs(2)

    @pl.when((e == 0) & (f == 0))
    def _():
        acc_ref[...] = jnp.zeros_like(acc_ref)

    @pl.when(f == 0)
    def _():
        acce_ref[...] = jnp.zeros_like(acce_ref)

    xn = xn_ref[...]
    gate = jnp.dot(xn, wg_ref[0], preferred_element_type=F32)
    up = jnp.dot(xn, wu_ref[0], preferred_element_type=F32)
    hid = (_silu(gate) * up).astype(BF16)
    acce_ref[...] += jnp.dot(hid, wd_ref[0], preferred_element_type=F32)

    @pl.when(f == nf - 1)
    def _():
        comb = comb_ref[...]
        lane = lax.broadcasted_iota(I32, comb.shape, 1)
        c = jnp.sum(jnp.where(lane == e, comb, 0.0), axis=-1, keepdims=True)
        acc_ref[...] += c * acce_ref[...]

    @pl.when((e == ne - 1) & (f == nf - 1))
    def _():
        o_ref[...] = x_ref[...] + acc_ref[...]


def moe_dense(x, xn, comb, wg, wu, wd, *, tm, tf):
    m, d = x.shape
    ne, _, ff = wg.shape
    return pl.pallas_call(
        _moe_dense_kernel,
        out_shape=jax.ShapeDtypeStruct((m, d), F32),
        grid=(m // tm, ne, ff // tf),
        in_specs=[pl.BlockSpec((tm, d), lambda i, e, f: (i, 0)),
                  pl.BlockSpec((tm, d), lambda i, e, f: (i, 0)),
                  pl.BlockSpec((tm, LANES), lambda i, e, f: (i, 0)),
                  pl.BlockSpec((1, d, tf), lambda i, e, f: (e, 0, f)),
                  pl.BlockSpec((1, d, tf), lambda i, e, f: (e, 0, f)),
                  pl.BlockSpec((1, tf, d), lambda i, e, f: (e, f, 0))],
        out_specs=pl.BlockSpec((tm, d), lambda i, e, f: (i, 0)),
        scratch_shapes=[pltpu.VMEM((tm, d), F32), pltpu.VMEM((tm, d), F32)],
        compiler_params=_cparams(("arbitrary", "arbitrary", "arbitrary")),
        name="moe_dense",
    )(x, xn, comb, wg, wu, wd)


MOE_TILE = 512
DMA_ISSUE_UNROLL = 8
FF_TILE = D_FF // 2
R_I1, R_I2, R_G1, R_G2, R_RANK1, R_RANK2 = 0, 1, 2, 3, 4, 5


def _router_rank_kernel(x_ref, g_ref, whi_ref, wlo_ref, br_ref, info_ref, cnt_ref, run_ref):
    i = pl.program_id(0)

    @pl.when(i == 0)
    def _():
        run_ref[...] = jnp.zeros_like(run_ref)

    x = x_ref[...]
    tm = x.shape[0]
    y = x * lax.rsqrt(jnp.mean(x * x, axis=-1, keepdims=True) + EPS) * g_ref[...]
    yhi = y.astype(BF16)
    ylo = (y - yhi.astype(F32)).astype(BF16)
    logits = jnp.dot(ylo, whi_ref[...], preferred_element_type=F32)
    logits = logits + jnp.dot(yhi, wlo_ref[...], preferred_element_type=F32)
    logits = logits + jnp.dot(yhi, whi_ref[...], preferred_element_type=F32) + br_ref[...]
    lane = lax.broadcasted_iota(I32, logits.shape, 1).astype(F32)
    m1 = jnp.max(logits, axis=-1, keepdims=True)
    i1 = jnp.min(jnp.where(logits == m1, lane, float(LANES)), axis=-1, keepdims=True)
    rest = jnp.where(lane == i1, -3e38, logits)
    m2 = jnp.max(rest, axis=-1, keepdims=True)
    i2 = jnp.min(jnp.where(rest == m2, lane, float(LANES)), axis=-1, keepdims=True)
    e2 = jnp.exp(m2 - m1)
    g1 = 1.0 / (1.0 + e2)
    g2 = e2 / (1.0 + e2)
    oh1 = jnp.where(lane == i1, 1.0, 0.0)
    oh2 = jnp.where(lane == i2, 1.0, 0.0)
    both = oh1 + oh2
    r_i = lax.broadcasted_iota(I32, (tm, tm), 0)
    c_i = lax.broadcasted_iota(I32, (tm, tm), 1)
    strict_lower = jnp.where(c_i < r_i, 1.0, 0.0).astype(BF16)
    before = jnp.dot(strict_lower, both.astype(BF16), preferred_element_type=F32) + run_ref[...]
    rank1 = jnp.sum(before * oh1, axis=-1, keepdims=True)
    rank2 = jnp.sum(before * oh2, axis=-1, keepdims=True)
    run_ref[...] += jnp.sum(both, axis=0, keepdims=True)
    info = jnp.where(lane == R_I1, i1, 0.0)
    for col, val in ((R_I2, i2), (R_G1, g1), (R_G2, g2), (R_RANK1, rank1), (R_RANK2, rank2)):
        info = jnp.where(lane == col, val, info)
    info_ref[...] = info
    cnt_ref[...] = run_ref[...]


def moe_router_rank(x, g, w_hi, w_lo, b_pad, *, tm):
    m, d = x.shape
    return pl.pallas_call(
        _router_rank_kernel,
        out_shape=(jax.ShapeDtypeStruct((m, LANES), F32), jax.ShapeDtypeStruct((1, LANES), F32)),
        grid=(m // tm,),
        in_specs=[pl.BlockSpec((tm, d), lambda i: (i, 0)),
                  pl.BlockSpec((1, d), lambda i: (0, 0)),
                  pl.BlockSpec((d, LANES), lambda i: (0, 0)),
                  pl.BlockSpec((d, LANES), lambda i: (0, 0)),
                  pl.BlockSpec((1, LANES), lambda i: (0, 0))],
        out_specs=(pl.BlockSpec((tm, LANES), lambda i: (i, 0)), pl.BlockSpec((1, LANES), lambda i: (0, 0))),
        scratch_shapes=[pltpu.VMEM((1, LANES), F32)],
        compiler_params=_cparams(("arbitrary",)),
        name="moe_router_rank",
    )(x, g, w_hi, w_lo, b_pad)


def _row_copy(src_ref, si, dst_ref, di, sem):
    return pltpu.make_async_copy(src_ref.at[pl.ds(si, 1)], dst_ref.at[pl.ds(di, 1)], sem)


def _moe_dispatch_kernel(d1_ref, d2_ref, x_ref, xs_ref, sem, *, ct):
    def issue(t, carry):
        _row_copy(x_ref, t, xs_ref, d1_ref[0, 0, t], sem).start(priority=0)
        _row_copy(x_ref, t, xs_ref, d2_ref[0, 0, t], sem).start(priority=1)
        return carry

    lax.fori_loop(0, ct, issue, 0, unroll=DMA_ISSUE_UNROLL)
    tile_copy = pltpu.make_async_copy(x_ref, xs_ref.at[pl.ds(0, ct)], sem)
    tile_copy.wait()
    tile_copy.wait()


def moe_dispatch(x, dest1, dest2, *, ct):
    m, d = x.shape
    smem_row = pl.BlockSpec((1, 1, ct), lambda i: (i, 0, 0), memory_space=pltpu.SMEM)
    return pl.pallas_call(
        functools.partial(_moe_dispatch_kernel, ct=ct),
        out_shape=jax.ShapeDtypeStruct((2 * m, d), F32),
        grid=(m // ct,),
        in_specs=[smem_row, smem_row, pl.BlockSpec((ct, d), lambda i: (i, 0))],
        out_specs=pl.BlockSpec(memory_space=pl.ANY),
        scratch_shapes=[pltpu.SemaphoreType.DMA],
        compiler_params=_cparams(("arbitrary",)),
        name="moe_dispatch",
    )(dest1, dest2, x)


def _moe_grouped_kernel(tile_ref, exp_ref, first_ref, valid_ref, gs_ref, xs_ref, g_ref, wg_ref, wu_ref, wd_ref,
                        ys_ref, xn_ref, acc_ref):
    w = pl.program_id(0)
    f = pl.program_id(1)
    nf = pl.num_programs(1)

    @pl.when(valid_ref[w] == 1)
    def _():
        @pl.when(f == 0)
        def _():
            x = xs_ref[...]
            y = x * lax.rsqrt(jnp.mean(x * x, axis=-1, keepdims=True) + EPS) * g_ref[...]
            xn_ref[...] = y.astype(BF16)
            acc_ref[...] = jnp.zeros_like(acc_ref)

        xn = xn_ref[...]
        gate = jnp.dot(xn, wg_ref[0], preferred_element_type=F32)
        up = jnp.dot(xn, wu_ref[0], preferred_element_type=F32)
        hid = (_silu(gate) * up).astype(BF16)
        acc_ref[...] += jnp.dot(hid, wd_ref[0], preferred_element_type=F32)

        @pl.when(f == nf - 1)
        def _():
            e = exp_ref[w]
            row = tile_ref[w] * MOE_TILE + lax.broadcasted_iota(I32, acc_ref.shape, 0)
            mine = (row >= gs_ref[e]) & (row < gs_ref[e + 1])
            part = jnp.where(mine, acc_ref[...], 0.0)

            @pl.when(first_ref[w] == 1)
            def _():
                ys_ref[...] = part

            @pl.when(first_ref[w] == 0)
            def _():
                ys_ref[...] += part


def moe_grouped(item_tile, item_expert, item_first, item_valid, group_starts, xs, g, wg, wu, wd, *, tf):
    r, d = xs.shape
    ff = wg.shape[2]
    n_items = item_tile.shape[0]
    return pl.pallas_call(
        _moe_grouped_kernel,
        out_shape=jax.ShapeDtypeStruct((r, d), F32),
        grid_spec=pltpu.PrefetchScalarGridSpec(
            num_scalar_prefetch=5,
            grid=(n_items, ff // tf),
            in_specs=[pl.BlockSpec((MOE_TILE, d), lambda w, f, it, ie, i1, iv, gs: (it[w], 0)),
                      pl.BlockSpec((1, d), lambda w, f, it, ie, i1, iv, gs: (0, 0)),
                      pl.BlockSpec((1, d, tf), lambda w, f, it, ie, i1, iv, gs: (ie[w], 0, f)),
                      pl.BlockSpec((1, d, tf), lambda w, f, it, ie, i1, iv, gs: (ie[w], 0, f)),
                      pl.BlockSpec((1, tf, d), lambda w, f, it, ie, i1, iv, gs: (ie[w], f, 0))],
            out_specs=pl.BlockSpec((MOE_TILE, d), lambda w, f, it, ie, i1, iv, gs: (it[w], 0)),
            scratch_shapes=[pltpu.VMEM((MOE_TILE, d), BF16), pltpu.VMEM((MOE_TILE, d), F32)]),
        compiler_params=_cparams(("arbitrary", "arbitrary")),
        name="moe_grouped",
    )(item_tile, item_expert, item_first, item_valid, group_starts, xs, g, wg, wu, wd)


def _moe_combine_kernel(d1_ref, d2_ref, x_ref, info_ref, ys_ref, o_ref, y1_ref, y2_ref, sem, *, ct):
    def issue(t, carry):
        _row_copy(ys_ref, d1_ref[0, 0, t], y1_ref, t, sem).start(priority=0)
        _row_copy(ys_ref, d2_ref[0, 0, t], y2_ref, t, sem).start(priority=1)
        return carry

    lax.fori_loop(0, ct, issue, 0, unroll=DMA_ISSUE_UNROLL)
    pltpu.make_async_copy(ys_ref.at[pl.ds(0, ct)], y1_ref, sem).wait()
    pltpu.make_async_copy(ys_ref.at[pl.ds(0, ct)], y2_ref, sem).wait()
    info = info_ref[...]
    g1 = info[:, R_G1:R_G1 + 1]
    g2 = info[:, R_G2:R_G2 + 1]
    o_ref[...] = x_ref[...] + (g1 * y1_ref[...] + g2 * y2_ref[...])


def moe_combine(x, info, ys, dest1, dest2, *, ct):
    m, d = x.shape
    smem_row = pl.BlockSpec((1, 1, ct), lambda i: (i, 0, 0), memory_space=pltpu.SMEM)
    return pl.pallas_call(
        functools.partial(_moe_combine_kernel, ct=ct),
        out_shape=jax.ShapeDtypeStruct((m, d), F32),
        grid=(m // ct,),
        in_specs=[smem_row, smem_row, pl.BlockSpec((ct, d), lambda i: (i, 0)),
                  pl.BlockSpec((ct, LANES), lambda i: (i, 0)), pl.BlockSpec(memory_space=pl.ANY)],
        out_specs=pl.BlockSpec((ct, d), lambda i: (i, 0)),
        scratch_shapes=[pltpu.VMEM((ct, d), F32), pltpu.VMEM((ct, d), F32), pltpu.SemaphoreType.DMA],
        compiler_params=_cparams(("arbitrary",)),
        name="moe_combine",
    )(dest1, dest2, x, info, ys)


def _moe_work_items(counts, n_tiles):
    n_exp = counts.shape[0]
    n_items = n_tiles + n_exp - 1
    ends = jnp.cumsum(counts)
    starts = ends - counts
    expert_of_row = lambda rr: jnp.minimum(jnp.sum(ends[None, :] <= rr[:, None], axis=1), n_exp - 1).astype(I32)
    t = jnp.arange(n_tiles, dtype=I32)
    e_lo = expert_of_row(t * MOE_TILE)
    e_hi = expert_of_row(t * MOE_TILE + (MOE_TILE - 1))
    per_tile = e_hi - e_lo + 1
    item_end = jnp.cumsum(per_tile)
    item_start = item_end - per_tile
    total = item_end[-1]
    w = jnp.minimum(jnp.arange(n_items, dtype=I32), total - 1)
    tile = jnp.sum(item_end[None, :] <= w[:, None], axis=1).astype(I32)
    expert = (e_lo[tile] + (w - item_start[tile])).astype(I32)
    first = (w == item_start[tile]).astype(I32)
    valid = (jnp.arange(n_items, dtype=I32) < total).astype(I32)
    group_starts = jnp.concatenate([starts, ends[-1:]]).astype(I32)
    return tile, expert, first, valid, group_starts


def moe_routed(x, g, w_hi, w_lo, b_pad, wg, wu, wd, *, tm):
    m, d = x.shape
    n_exp = wg.shape[0]
    assert (2 * m) % MOE_TILE == 0
    info, cnt = moe_router_rank(x, g, w_hi, w_lo, b_pad, tm=tm)
    counts = cnt[0, :n_exp].astype(I32)
    starts = jnp.cumsum(counts) - counts
    i1 = info[:, R_I1].astype(I32)
    i2 = info[:, R_I2].astype(I32)
    dest1 = (starts[i1] + info[:, R_RANK1].astype(I32)).reshape(m // tm, 1, tm)
    dest2 = (starts[i2] + info[:, R_RANK2].astype(I32)).reshape(m // tm, 1, tm)
    xs = moe_dispatch(x, dest1, dest2, ct=tm)
    items = _moe_work_items(counts, (2 * m) // MOE_TILE)
    ys = moe_grouped(*items, xs, g, wg, wu, wd, tf=FF_TILE)
    return moe_combine(x, info, ys, dest1, dest2, ct=tm)


PAGES_PER_STEP = 32
PAGE_SLOTS = 3
SDSA_GROUP = 4
COUNT_STREAMS = 4


def _fetch_pages(pt_ref, cache_ref, buf_ref, sem_ref, step, nj):
    b = step // nj
    j0 = (step % nj) * PAGES_PER_STEP
    slot = step % PAGE_SLOTS
    for i in range(PAGES_PER_STEP):
        pltpu.make_async_copy(cache_ref.at[pt_ref[b, j0 + i]], buf_ref.at[slot, i], sem_ref.at[slot]).start()


def _wait_pages(cache_ref, buf_ref, sem_ref, step):
    slot = step % PAGE_SLOTS
    pltpu.make_async_copy(cache_ref.at[pl.ds(0, PAGES_PER_STEP)], buf_ref.at[slot], sem_ref.at[slot]).wait()


def _page_pipeline_step(pt_ref, streams, nj):
    step = pl.program_id(0) * pl.num_programs(1) + pl.program_id(1)
    total = pl.num_programs(0) * pl.num_programs(1)

    ahead = PAGE_SLOTS - 1

    @pl.when(step == 0)
    def _():
        for d in range(ahead):
            @pl.when(d < total)
            def _():
                for cache_ref, buf_ref, sem_ref in streams:
                    _fetch_pages(pt_ref, cache_ref, buf_ref, sem_ref, step + d, nj)

    @pl.when(step + ahead < total)
    def _():
        for cache_ref, buf_ref, sem_ref in streams:
            _fetch_pages(pt_ref, cache_ref, buf_ref, sem_ref, step + ahead, nj)

    for cache_ref, buf_ref, sem_ref in streams:
        _wait_pages(cache_ref, buf_ref, sem_ref, step)
    return step % PAGE_SLOTS


def _sdsa_index_kernel(pt_ref, qi_ref, wi_ref, kknew_ref, cache_ref, mpast_ref, mnew_ref,
                       keys_ref, pre_ref, pbuf_ref, psem_ref, *, n_pages, t_len, topk, nb):
    nj = n_pages // PAGES_PER_STEP
    lb = pl.program_id(1) // nj
    j = pl.program_id(1) % nj
    rows = pl.ds(pl.multiple_of(lb * t_len, t_len), t_len)
    last_step = pl.program_id(1) == nb * nj - 1
    slot = _page_pipeline_step(pt_ref, [(cache_ref, pbuf_ref, psem_ref)], nj)
    lane = lax.broadcasted_iota(I32, (t_len, LANES), 1)
    row = lax.broadcasted_iota(I32, (t_len, LANES), 0)
    lo = lane < HEAD_DIM

    qi = qi_ref[0]
    wi = wi_ref[0]
    qs, ws = [], []
    for h in range(IDX_HEADS):
        x = qi[:, (h // 2) * LANES:(h // 2 + 1) * LANES]
        if h % 2 == 1:
            x = pltpu.roll(x, HEAD_DIM, 1)
        qs.append(x[:, :IDX_DIM])
        ws.append(wi[:, IDX_DIM + h:IDX_DIM + h + 1])
    q32 = jnp.concatenate(qs, axis=0).astype(BF16)
    w32 = jnp.concatenate(ws, axis=0)

    def scores(dots):
        d = jnp.maximum(dots, 0.0) * w32
        s = d[0:t_len]
        for h in range(1, IDX_HEADS):
            s = s + d[h * t_len:(h + 1) * t_len]
        return jnp.where(s == 0.0, 0.0, s)

    pages_t = jnp.concatenate([pbuf_ref[slot, i].astype(BF16) for i in range(PAGES_PER_STEP)], axis=1)
    s_wide = scores(jnp.dot(q32, pages_t, preferred_element_type=F32))
    for i in range(PAGES_PER_STEP):
        keys_ref[j * PAGES_PER_STEP + i, rows, :] = s_wide[:, i * PAGE_SIZE:(i + 1) * PAGE_SIZE]

    @pl.when(j == nj - 1)
    def _():
        knew = jnp.concatenate([kknew_ref[0][:, :IDX_DIM], jnp.zeros((LANES - t_len, IDX_DIM), F32)], axis=0)
        s_new = jnp.where((lane <= row) & (lane < t_len), scores(_bdot_nt(q32, knew)), NEG)
        keys_ref[n_pages, rows, :] = s_new

    @pl.when(last_step)
    def _():
        nrow = nb * t_len
        own = lambda x, g: x[..., g * t_len:(g + 1) * t_len, :]

        def count_ge(cand):
            accs = [None] * COUNT_STREAMS
            for p in range(n_pages + 1):
                hit = jnp.where(keys_ref[p] >= cand, 1.0, 0.0)
                accs[p % COUNT_STREAMS] = hit if accs[p % COUNT_STREAMS] is None else accs[p % COUNT_STREAMS] + hit
            part = accs[0]
            for a in accs[1:]:
                part = part + a
            return jnp.sum(part, axis=-1, keepdims=True)

        thr = _kth_largest(count_ge, float(topk), (nrow, 1))
        n_ge = count_ge(thr)
        no_cut = jnp.max(jnp.abs(n_ge - float(topk))) == 0.0

        @pl.when(no_cut)
        def _():
            bias = jnp.where(keys_ref[...] >= thr, 0.0, NEG)
            for g in range(nb):
                mpast_ref[g] = own(bias[:n_pages], g)
                mnew_ref[g] = own(bias[n_pages], g)

        @pl.when(jnp.logical_not(no_cut))
        def _():
            keys = keys_ref[...]
            n_gt = jnp.sum(_sum_leading(jnp.where(keys > thr, 1.0, 0.0)), axis=-1, keepdims=True)
            need = float(topk) - n_gt
            eqf = jnp.where(keys == thr, 1.0, 0.0).astype(BF16).reshape((n_pages + 1) * nrow, LANES)
            pre_ref[...] = jnp.dot(eqf, _upper_tri_bf16(LANES), preferred_element_type=F32).reshape(
                n_pages + 1, nrow, LANES)

            def body(p, off):
                kp = keys_ref[p]
                pre = pre_ref[p]
                bias = jnp.where(kp == thr, jnp.where((pre + off) <= need, 0.0, NEG),
                                 jnp.where(kp > thr, 0.0, NEG))

                @pl.when(p < n_pages)
                def _():
                    for g in range(nb):
                        mpast_ref[g, p] = own(bias, g)

                @pl.when(p == n_pages)
                def _():
                    for g in range(nb):
                        mnew_ref[g] = own(bias, g)

                return off + pre[:, LANES - 1:LANES]

            lax.fori_loop(0, n_pages + 1, body, jnp.zeros((nrow, 1), F32))


def sdsa_index(page_table, qi, wi, kk_new, cache_kidx):
    bsz, t_len, _ = qi.shape
    n_pages = page_table.shape[1]
    topk = min(TOPK_MAX, (n_pages * PAGE_SIZE + t_len) // 4)
    nb = math.gcd(bsz, SDSA_GROUP)
    nj = n_pages // PAGES_PER_STEP
    kern = functools.partial(_sdsa_index_kernel, n_pages=n_pages, t_len=t_len, topk=topk, nb=nb)
    tok = lambda w: pl.BlockSpec((1, t_len, w), lambda gi, jj, pt: (gi * nb + jj // nj, 0, 0))
    return pl.pallas_call(
        kern,
        out_shape=(jax.ShapeDtypeStruct((bsz, n_pages, t_len, LANES), F32),
                   jax.ShapeDtypeStruct((bsz, t_len, LANES), F32)),
        grid_spec=pltpu.PrefetchScalarGridSpec(
            num_scalar_prefetch=1,
            grid=(bsz // nb, nb * nj),
            in_specs=[tok(256), tok(LANES), tok(LANES), pl.BlockSpec(memory_space=pl.ANY)],
            out_specs=(pl.BlockSpec((nb, n_pages, t_len, LANES), lambda gi, jj, pt: (gi, 0, 0, 0)),
                       pl.BlockSpec((nb, t_len, LANES), lambda gi, jj, pt: (gi, 0, 0))),
            scratch_shapes=[pltpu.VMEM((n_pages + 1, nb * t_len, LANES), F32),
                            pltpu.VMEM((n_pages + 1, nb * t_len, LANES), F32),
                            pltpu.VMEM((PAGE_SLOTS, PAGES_PER_STEP, IDX_DIM, PAGE_SIZE), F32),
                            pltpu.SemaphoreType.DMA((PAGE_SLOTS,))]),
        compiler_params=_cparams(("arbitrary", "arbitrary")),
        name="sdsa_index",
    )(page_table, qi, wi, kk_new, cache_kidx)


def _sdsa_attn_kernel(pt_ref, q_ref, knew_ref, vnew_ref, mpast_ref, mnew_ref, ck_ref, cv_ref,
                      o_ref, qg_ref, m_ref, l_ref, acc_ref, kbuf_ref, vbuf_ref, ksem_ref, vsem_ref, *, t_len, n_pages):
    j = pl.program_id(1)
    nj = pl.num_programs(1)
    slot = _page_pipeline_step(pt_ref, [(ck_ref, kbuf_ref, ksem_ref), (cv_ref, vbuf_ref, vsem_ref)],
                               n_pages // PAGES_PER_STEP)
    rpg = N_HEADS // N_KV
    reps = rpg

    def group_lanes(x, g):
        if g == 1:
            x = pltpu.roll(x, HEAD_DIM, 1)
        return x[:, :HEAD_DIM]

    @pl.when(j == 0)
    def _():
        q = q_ref[0]
        for g in range(N_KV):
            parts = [group_lanes(q[:, r * LANES:(r + 1) * LANES], g) for r in range(rpg)]
            qg_ref[g] = (jnp.concatenate(parts, axis=0) * (HEAD_DIM ** -0.5)).astype(BF16)
        m_ref[...] = jnp.full(m_ref.shape, NEG, F32)
        l_ref[...] = jnp.zeros(l_ref.shape, F32)
        acc_ref[...] = jnp.zeros(acc_ref.shape, F32)

    def update(g, s, pv):
        m_old = m_ref[g]
        m_cur = jnp.maximum(m_old, jnp.max(s, axis=-1, keepdims=True))
        alpha = jnp.exp(m_old - m_cur)
        p = jnp.exp(s - m_cur)
        m_ref[g] = m_cur
        l_ref[g] = l_ref[g] * alpha + jnp.sum(p, axis=-1, keepdims=True)
        acc_ref[g] = acc_ref[g] * alpha + pv(p.astype(BF16))

    bias8 = jnp.concatenate([mpast_ref[0, i] for i in range(PAGES_PER_STEP)], axis=1)
    bias_wide = jnp.concatenate([bias8] * reps, axis=0)
    for g in range(N_KV):
        k_wide = jnp.concatenate([kbuf_ref[slot, i, g].astype(BF16) for i in range(PAGES_PER_STEP)], axis=1)
        v_wide = jnp.concatenate([vbuf_ref[slot, i, g].astype(BF16) for i in range(PAGES_PER_STEP)], axis=1)
        s = jnp.dot(qg_ref[g], k_wide, preferred_element_type=F32) + bias_wide
        update(g, s, lambda p, v_wide=v_wide: _bdot_nt(p, v_wide))

    @pl.when(j == nj - 1)
    def _():
        pad = jnp.zeros((LANES - t_len, HEAD_DIM), F32)
        bias = jnp.concatenate([mnew_ref[0]] * reps, axis=0)
        for g in range(N_KV):
            knew = jnp.concatenate([group_lanes(knew_ref[0], g), pad], axis=0)
            vnew = jnp.concatenate([group_lanes(vnew_ref[0], g), pad], axis=0)
            update(g, _bdot_nt(qg_ref[g], knew) + bias, lambda p, vnew=vnew: _bdot(p, vnew))
        o0 = acc_ref[0] / l_ref[0]
        o1 = acc_ref[1] / l_ref[1]
        for r in range(rpg):
            o_ref[0, :, r * LANES:(r + 1) * LANES] = jnp.concatenate(
                [o0[r * t_len:(r + 1) * t_len], o1[r * t_len:(r + 1) * t_len]], axis=-1)


def sdsa_attn(page_table, q, k_new, v_new, mask_past, mask_new, cache_k, cache_v):
    bsz, t_len, _ = q.shape
    n_pages = page_table.shape[1]
    kern = functools.partial(_sdsa_attn_kernel, t_len=t_len, n_pages=n_pages)
    tok = lambda w: pl.BlockSpec((1, t_len, w), lambda b_, j, pt: (b_, 0, 0))
    rows = (N_HEADS // N_KV) * t_len
    page_buf = pltpu.VMEM((PAGE_SLOTS, PAGES_PER_STEP, N_KV, HEAD_DIM, PAGE_SIZE), F32)
    return pl.pallas_call(
        kern,
        out_shape=jax.ShapeDtypeStruct((bsz, t_len, 512), F32),
        grid_spec=pltpu.PrefetchScalarGridSpec(
            num_scalar_prefetch=1,
            grid=(bsz, n_pages // PAGES_PER_STEP),
            in_specs=[tok(512), tok(LANES), tok(LANES),
                      pl.BlockSpec((1, PAGES_PER_STEP, t_len, LANES), lambda b_, j, pt: (b_, j, 0, 0)),
                      tok(LANES), pl.BlockSpec(memory_space=pl.ANY), pl.BlockSpec(memory_space=pl.ANY)],
            out_specs=pl.BlockSpec((1, t_len, 512), lambda b_, j, pt: (b_, 0, 0)),
            scratch_shapes=[pltpu.VMEM((N_KV, rows, HEAD_DIM), BF16), pltpu.VMEM((N_KV, rows, 1), F32),
                            pltpu.VMEM((N_KV, rows, 1), F32), pltpu.VMEM((N_KV, rows, HEAD_DIM), F32),
                            page_buf, page_buf,
                            pltpu.SemaphoreType.DMA((PAGE_SLOTS,)), pltpu.SemaphoreType.DMA((PAGE_SLOTS,))]),
        compiler_params=_cparams(("arbitrary", "arbitrary")),
        name="sdsa_attn",
    )(page_table, q, k_new, v_new, mask_past, mask_new, cache_k, cache_v)


HEAD_ORDER = (0, 4, 1, 5, 2, 6, 3, 7)


def _pad_cols(w, n):
    return jnp.pad(w, ((0, 0), (0, n - w.shape[1])))


def _row(v):
    return v.reshape(1, -1).astype(F32)


def _rope_tables(pos):
    half = ROT_DIM // 2
    inv = ROPE_THETA ** (-jnp.arange(half, dtype=F32) / half)
    ang = pos.astype(F32)[:, None] * inv[None, :]
    cos, sin = jnp.cos(ang), jnp.sin(ang)
    n = pos.shape[0]
    rest = HEAD_DIM - ROT_DIM
    c64 = jnp.concatenate([cos, cos, jnp.ones((n, rest), F32)], axis=1)
    s64 = jnp.concatenate([-sin, sin, jnp.zeros((n, rest), F32)], axis=1)
    return jnp.tile(c64, (1, 2)), jnp.tile(s64, (1, 2))


def _even_layer(x, prev_a, prev_ssm, prev_b, lc, p, *, tm, tl_a):
    bsz, L, d = x.shape
    m = bsz * L
    x2 = x.reshape(m, d)
    zmain, ztail = norm_proj(x2, p["e_norm_mix"], p["e_w_main"], p["e_w_tail"], tm=tm, tn=E_MAIN)
    zmain3 = zmain.reshape(bsz, L, E_MAIN)
    prev32 = jnp.pad(prev_a, ((0, 0), (CARRY_A - (CONV_A - 1), 0), (0, 0)))
    ya, new_a = conv_a(zmain3, prev32, p["e_conv_a_w"], p["e_conv_a_b"], p["e_ln_a_g"], p["e_ln_a_b"], tl=tl_a)
    prev8 = jnp.pad(prev_b, ((0, 0), (CARRY_B - (SSM_CONV - 1), 0), (0, 0)))
    yb, new_b, hout = ssd_mixer(zmain3, ztail.reshape(bsz, L, LANES), prev8,
                                prev_ssm.reshape(bsz, D_INNER, SSM_STATE),
                                p["e_conv_b_w"], p["e_conv_b_b"], p["e_dt_bias"], p["e_a_neg"], p["e_expand"],
                                p["e_dskip"], p["e_ssm_norm"], lc=lc)
    x2 = out_proj(x2, ya.reshape(m, D_A), yb.reshape(m, D_INNER), p["e_wo_a"], p["e_wo_b"], tm=min(m, 1024))
    x2 = swiglu_ffn(x2, p["e_norm_ffn"], p["e_w_gate"], p["e_w_up"], p["e_w_down"], tm=min(m, 512), tf=FF_TILE)
    return (x2.reshape(bsz, L, d), new_a[:, CARRY_A - (CONV_A - 1):], hout.reshape(bsz, SSM_HEADS, SSM_HEADDIM, SSM_STATE),
            new_b[:, CARRY_B - (SSM_CONV - 1):])


def _odd_project(x2, cos_t, sin_t, p, *, tm, seq_len=None):
    zmain, ztail = norm_proj(x2, p["o_norm_mix"], p["o_w_main"], p["o_w_tail"], tm=tm, tn=O_MAIN)
    q, k, v, qi, kiw, k16, v16, kk16, *kit = odd_post(zmain, ztail, cos_t, sin_t, p["o_q_norm"], p["o_k_norm"],
                                                      p["o_kidx_g"], p["o_kidx_b"], p["o_bd"], tm=tm,
                                                      seq_len=seq_len)
    return zmain, q, k, v, qi, kiw, (k16, v16, kk16, *kit)


def _odd_tail(x2, att, zmain3, prev_d, p, *, tm, tl_d):
    bsz, L, _ = zmain3.shape
    m = bsz * L
    prev8 = jnp.pad(prev_d, ((0, 0), (CARRY_D - (CONV_D - 1), 0), (0, 0)))
    dm, new_d = conv_d(zmain3, prev8, p["o_conv_d_w"], tl=tl_d)
    x2 = out_proj(x2, att.reshape(m, 512), dm.reshape(m, D_D), p["o_wo_a"], p["o_wo_b"], tm=min(m, 1024))
    if (2 * m) % MOE_TILE == 0 and 2 * m >= N_EXPERTS * MOE_TILE:
        x2 = moe_routed(x2, p["o_norm_ffn"], p["o_wr_hi"], p["o_wr_lo"], p["o_br"],
                        p["o_we_gate"], p["o_we_up"], p["o_we_down"], tm=tm)
    else:
        xn, comb = moe_router(x2, p["o_norm_ffn"], p["o_wr_hi"], p["o_wr_lo"], p["o_br"], tm=tm)
        x2 = moe_dense(x2, xn, comb, p["o_we_gate"], p["o_we_up"], p["o_we_down"], tm=min(m, 1024), tf=FF_TILE)
    return x2, new_d[:, CARRY_D - (CONV_D - 1):]


def kernel(x_prompt, x_sample, state_conv_a, state_ssm, state_conv_b, cache_k, cache_v, cache_kidx, state_conv_d,
           page_table, e_norm_mix, e_w_in, e_conv_a_w, e_conv_a_b, e_ln_a_g, e_ln_a_b, e_conv_b_w, e_conv_b_b,
           e_dt_bias, e_a_log, e_d_skip, e_ssm_norm, e_w_out, e_norm_ffn, e_w_gate, e_w_up, e_w_down,
           o_norm_mix, o_w_in, o_q_norm, o_k_norm, o_kidx_g, o_kidx_b, o_conv_d_w, o_w_out, o_norm_ffn,
           o_w_router, o_b_router, o_we_gate, o_we_up, o_we_down):
    bp, s_len, d = x_prompt.shape
    bd, t_len, _ = x_sample.shape
    n_pairs = e_w_in.shape[0]
    past = page_table.shape[1] * PAGE_SIZE
    n_pool = cache_k.shape[1]
    xp, xs = x_prompt, x_sample
    outs_p = [[] for _ in range(7)]
    outs_s = [[] for _ in range(7)]
    cos_p, sin_p = _rope_tables(jnp.arange(s_len))
    cos_s, sin_s = _rope_tables(jnp.tile(past + jnp.arange(t_len), bd))
    perm = np.concatenate([np.arange(h * HEAD_DIM, (h + 1) * HEAD_DIM) for h in HEAD_ORDER])
    expand = (jnp.arange(LANES)[:, None] == (jnp.arange(D_INNER)[None, :] // SSM_HEADDIM)).astype(BF16)
    blk = jnp.arange(LANES) // HEAD_DIM
    bdiag = jnp.where(blk[:, None] == blk[None, :], 1.0 / HEAD_DIM, 0.0).astype(BF16)
    for i in range(n_pairs):
        w = e_w_in[i]
        p = {
            "e_norm_mix": _row(e_norm_mix[i]),
            "e_w_main": jnp.concatenate([w[:, 2048:3584], w[:, 0:512], w[:, 1024:2048], w[:, 512:1024]],
                                        axis=1).astype(BF16),
            "e_w_tail": _pad_cols(w[:, 3584:3600], LANES).astype(BF16),
            "e_conv_a_w": e_conv_a_w[i], "e_conv_a_b": _row(e_conv_a_b[i]),
            "e_ln_a_g": _row(e_ln_a_g[i]), "e_ln_a_b": _row(e_ln_a_b[i]),
            "e_conv_b_w": e_conv_b_w[i], "e_conv_b_b": _row(e_conv_b_b[i]),
            "e_dt_bias": _pad_cols(_row(e_dt_bias[i]), LANES),
            "e_a_neg": _pad_cols(_row(-jnp.exp(e_a_log[i].astype(F32))), LANES),
            "e_expand": expand,
            "e_dskip": _row(jnp.repeat(e_d_skip[i], SSM_HEADDIM)),
            "e_ssm_norm": _row(e_ssm_norm[i]),
            "e_wo_a": e_w_out[i][:D_A].astype(BF16), "e_wo_b": e_w_out[i][D_A:].astype(BF16),
            "e_norm_ffn": _row(e_norm_ffn[i]),
            "e_w_gate": e_w_gate[i].astype(BF16), "e_w_up": e_w_up[i].astype(BF16),
            "e_w_down": e_w_down[i].astype(BF16),
        }
        zero_a = jnp.zeros((bp, CONV_A - 1, D_A), F32)
        zero_h = jnp.zeros((bp, SSM_HEADS, SSM_HEADDIM, SSM_STATE), F32)
        zero_b = jnp.zeros((bp, SSM_CONV - 1, CONV_DIM), F32)
        xp, ca, sm, cb = _even_layer(xp, zero_a, zero_h, zero_b, min(SSD_CHUNK, s_len), p, tm=512, tl_a=256)
        outs_p[0].append(ca); outs_p[1].append(sm); outs_p[2].append(cb)
        xs, ca, sm, cb = _even_layer(xs, state_conv_a[i], state_ssm[i], state_conv_b[i], t_len, p,
                                     tm=bd * t_len, tl_a=t_len)
        outs_s[0].append(ca); outs_s[1].append(sm); outs_s[2].append(cb)

        w = o_w_in[i]
        wo = o_w_out[i]
        p = {
            "o_norm_mix": _row(o_norm_mix[i]),
            "o_w_main": jnp.concatenate([w[:, 0:512][:, perm], w[:, 512:1024], w[:, 1092:2628]], axis=1).astype(BF16),
            "o_w_tail": _pad_cols(w[:, 1024:1092], LANES).astype(BF16),
            "o_q_norm": _row(jnp.tile(o_q_norm[i], 2)), "o_k_norm": _row(jnp.tile(o_k_norm[i], 2)),
            "o_kidx_g": _pad_cols(_row(o_kidx_g[i]), LANES), "o_kidx_b": _pad_cols(_row(o_kidx_b[i]), LANES),
            "o_bd": bdiag,
            "o_conv_d_w": o_conv_d_w[i],
            "o_wo_a": wo[:512][perm].astype(BF16), "o_wo_b": wo[512:].astype(BF16),
            "o_norm_ffn": _row(o_norm_ffn[i]),
            "o_br": jnp.concatenate([_row(o_b_router[i]), jnp.full((1, LANES - N_EXPERTS), NEG, F32)], axis=1),
            "o_we_gate": o_we_gate[i].astype(BF16), "o_we_up": o_we_up[i].astype(BF16),
            "o_we_down": o_we_down[i].astype(BF16),
        }
        wr = _pad_cols(o_w_router[i], LANES)
        p["o_wr_hi"] = wr.astype(BF16)
        p["o_wr_lo"] = (wr - p["o_wr_hi"].astype(F32)).astype(BF16)

        mp = bp * s_len
        x2 = xp.reshape(mp, d)
        zmain, q, k_t, v_t, qi, kiw, (k16, v16, kk16, ki_t) = _odd_project(x2, cos_p, sin_p, p, tm=512,
                                                                           seq_len=s_len)
        r3 = lambda a: a.reshape(bp, s_len, a.shape[-1])
        att = dsa_prompt(r3(q), r3(qi), r3(kiw), r3(k16), r3(v16), r3(kk16))
        x2, cd = _odd_tail(x2, att, r3(zmain), jnp.zeros((bp, CONV_D - 1, D_D), F32), p, tm=512, tl_d=512)
        xp = x2.reshape(bp, s_len, d)
        outs_p[3].append(jnp.transpose(k_t.reshape(bp, N_KV, HEAD_DIM, s_len), (0, 3, 1, 2)))
        outs_p[4].append(jnp.transpose(v_t.reshape(bp, N_KV, HEAD_DIM, s_len), (0, 3, 1, 2)))
        outs_p[5].append(jnp.transpose(ki_t, (0, 2, 1))); outs_p[6].append(cd)

        ms = bd * t_len
        x2 = xs.reshape(ms, d)
        zmain, q, k, v, qi, kiw, _ = _odd_project(x2, cos_s, sin_s, p, tm=ms)
        r3 = lambda a: a.reshape(bd, t_len, a.shape[-1])
        mask_past, mask_new = sdsa_index(page_table, r3(qi), r3(kiw), r3(kiw),
                                         jnp.transpose(cache_kidx[i], (0, 2, 1)))
        att = sdsa_attn(page_table, r3(q), r3(k), r3(v), mask_past, mask_new,
                        jnp.transpose(cache_k[i], (0, 2, 3, 1)), jnp.transpose(cache_v[i], (0, 2, 3, 1)))
        x2, cd = _odd_tail(x2, att, r3(zmain), state_conv_d[i], p, tm=ms, tl_d=t_len)
        xs = x2.reshape(bd, t_len, d)
        outs_s[3].append(k.reshape(bd, t_len, N_KV, HEAD_DIM)); outs_s[4].append(v.reshape(bd, t_len, N_KV, HEAD_DIM))
        outs_s[5].append(kiw[:, :IDX_DIM].reshape(bd, t_len, IDX_DIM)); outs_s[6].append(cd)
    return (xp, xs) + tuple(jnp.stack(o) for o in outs_p) + tuple(jnp.stack(o) for o in outs_s)
```

```python
import functools
import math

import jax
import jax.numpy as jnp
import numpy as np
from jax import lax
from jax.experimental import pallas as pl
from jax.experimental.pallas import tpu as pltpu

F32 = jnp.float32
BF16 = jnp.bfloat16
I32 = jnp.int32

D_MODEL = 1024
D_A = 512
CONV_A = 31
D_INNER = 1024
SSM_HEADDIM = 64
SSM_HEADS = 16
SSM_GROUPS = 2
SSM_STATE = 128
SSM_CONV = 4
CONV_DIM = D_INNER + 2 * SSM_GROUPS * SSM_STATE
SSD_CHUNK = 128
HEAD_DIM = 64
N_HEADS = 8
N_KV = 2
ROT_DIM = 16
ROPE_THETA = 500000.0
IDX_HEADS = 4
IDX_DIM = 64
TOPK_MAX = 256
Q_BLOCK = 128
D_D = 512
CONV_D = 3
D_FF = 2816
N_EXPERTS = 8
PAGE_SIZE = 128
EPS = 1e-6
NEG = -1e30

LANES = 128
SUBLANES = 8
VMEM_LIMIT = 56 * 1024 * 1024

E_MAIN = D_A + D_A + D_INNER + CONV_DIM
E_COL_XBC, E_COL_VAL, E_COL_Z, E_COL_GATE = 0, 1536, 2048, 3072
O_MAIN = 512 + 128 + 128 + 256 + 3 * D_D


def _cparams(sem):
    return pltpu.CompilerParams(dimension_semantics=sem, vmem_limit_bytes=VMEM_LIMIT)


def _bdot(a, b):
    return jnp.dot(a.astype(BF16), b.astype(BF16), preferred_element_type=F32)


def _bdot_nt(a, b):
    return lax.dot_general(a.astype(BF16), b.astype(BF16), (((1,), (1,)), ((), ())),
                           preferred_element_type=F32)


def _split3(a):
    hi = a.astype(BF16)
    r1 = a - hi.astype(F32)
    mid = r1.astype(BF16)
    lo = (r1 - mid.astype(F32)).astype(BF16)
    return hi, mid, lo


def _dot_exact_rhs(a, b_bf16):
    hi, mid, lo = _split3(a)
    out = jnp.dot(lo, b_bf16, preferred_element_type=F32)
    out = out + jnp.dot(mid, b_bf16, preferred_element_type=F32)
    return out + jnp.dot(hi, b_bf16, preferred_element_type=F32)


def _dot_exact_lhs(a_bf16, b):
    hi, mid, lo = _split3(b)
    out = jnp.dot(a_bf16, lo, preferred_element_type=F32)
    out = out + jnp.dot(a_bf16, mid, preferred_element_type=F32)
    return out + jnp.dot(a_bf16, hi, preferred_element_type=F32)


def _sigmoid(x):
    return 1.0 / (1.0 + jnp.exp(-x))


def _silu(x):
    return x * _sigmoid(x)


def _softplus(x):
    return jnp.maximum(x, 0.0) + jnp.log(1.0 + jnp.exp(-jnp.abs(x)))


def _norm_proj_kernel(x_ref, g_ref, w_ref, wt_ref, main_ref, tail_ref, xn_ref):
    j = pl.program_id(1)

    @pl.when(j == 0)
    def _():
        x = x_ref[...]
        y = x * lax.rsqrt(jnp.mean(x * x, axis=-1, keepdims=True) + EPS) * g_ref[...]
        xn = y.astype(BF16)
        xn_ref[...] = xn
        tail_ref[...] = jnp.dot(xn, wt_ref[...], preferred_element_type=F32)

    main_ref[...] = jnp.dot(xn_ref[...], w_ref[...], preferred_element_type=F32)


def norm_proj(x, g, w_main, w_tail, *, tm, tn):
    m, k = x.shape
    n = w_main.shape[1]
    return pl.pallas_call(
        _norm_proj_kernel,
        out_shape=(jax.ShapeDtypeStruct((m, n), F32), jax.ShapeDtypeStruct((m, LANES), F32)),
        grid=(m // tm, n // tn),
        in_specs=[pl.BlockSpec((tm, k), lambda i, j: (i, 0)),
                  pl.BlockSpec((1, k), lambda i, j: (0, 0)),
                  pl.BlockSpec((k, tn), lambda i, j: (0, j)),
                  pl.BlockSpec((k, LANES), lambda i, j: (0, 0))],
        out_specs=(pl.BlockSpec((tm, tn), lambda i, j: (i, j)),
                   pl.BlockSpec((tm, LANES), lambda i, j: (i, 0))),
        scratch_shapes=[pltpu.VMEM((tm, k), BF16)],
        compiler_params=_cparams(("arbitrary", "arbitrary")),
        name="norm_proj",
    )(x, g, w_main, w_tail)


def _dwconv_from_buf(buf_ref, w_ref, kw, carry, tl):
    acc = None
    for k in range(kw):
        start = carry - (kw - 1) + k
        term = buf_ref[pl.ds(start, tl), :] * w_ref[pl.ds(k, 1), :]
        acc = term if acc is None else acc + term
    return acc


def _dwconv_phased(buf_ref, w_ref, kw, carry, tl):
    base = carry - (kw - 1)
    tz = tl + 2 * SUBLANES
    acc = None
    for b in range(SUBLANES):
        z = None
        for a in range((kw - 1 - b) // SUBLANES + 1):
            term = buf_ref[pl.ds(SUBLANES * a, tz), :] * w_ref[pl.ds(SUBLANES * a + b, 1), :]
            z = term if z is None else z + term
        shifted = z[base + b:base + b + tl]
        acc = shifted if acc is None else acc + shifted
    return acc


PHASE_PAD = 8


CARRY_A = 32


def _conv_a_kernel(val_ref, gate_ref, prev_ref, w_ref, b_ref, lg_ref, lb_ref, ya_ref, new_ref, buf_ref, *, tl):
    l = pl.program_id(1)

    @pl.when(l == 0)
    def _():
        buf_ref[pl.ds(0, CARRY_A), :] = prev_ref[0]
        buf_ref[pl.ds(CARRY_A + tl, PHASE_PAD), :] = jnp.zeros((PHASE_PAD, D_A), F32)

    u = val_ref[0] * _sigmoid(gate_ref[0])
    buf_ref[pl.ds(CARRY_A, tl), :] = u
    y = _dwconv_phased(buf_ref, w_ref, CONV_A, CARRY_A, tl) + b_ref[...]
    mu = jnp.mean(y, axis=-1, keepdims=True)
    yc = y - mu
    var = jnp.mean(yc * yc, axis=-1, keepdims=True)
    yn = yc * lax.rsqrt(var + EPS) * lg_ref[...] + lb_ref[...]
    ya_ref[0] = _silu(yn)
    hist = buf_ref[pl.ds(tl, CARRY_A), :]
    new_ref[0] = hist
    buf_ref[pl.ds(0, CARRY_A), :] = hist


def conv_a(zmain, prev32, w, b, lg, lb, *, tl):
    bsz, L, _ = zmain.shape
    kern = functools.partial(_conv_a_kernel, tl=tl)
    return pl.pallas_call(
        kern,
        out_shape=(jax.ShapeDtypeStruct((bsz, L, D_A), F32), jax.ShapeDtypeStruct((bsz, CARRY_A, D_A), F32)),
        grid=(bsz, L // tl),
        in_specs=[pl.BlockSpec((1, tl, D_A), lambda b_, l: (b_, l, E_COL_VAL // D_A)),
                  pl.BlockSpec((1, tl, D_A), lambda b_, l: (b_, l, E_COL_GATE // D_A)),
                  pl.BlockSpec((1, CARRY_A, D_A), lambda b_, l: (b_, 0, 0)),
                  pl.BlockSpec((CONV_A, D_A), lambda b_, l: (0, 0)),
                  pl.BlockSpec((1, D_A), lambda b_, l: (0, 0)),
                  pl.BlockSpec((1, D_A), lambda b_, l: (0, 0)),
                  pl.BlockSpec((1, D_A), lambda b_, l: (0, 0))],
        out_specs=(pl.BlockSpec((1, tl, D_A), lambda b_, l: (b_, l, 0)),
                   pl.BlockSpec((1, CARRY_A, D_A), lambda b_, l: (b_, 0, 0))),
        scratch_shapes=[pltpu.VMEM((CARRY_A + tl + PHASE_PAD, D_A), F32)],
        compiler_params=_cparams(("arbitrary", "arbitrary")),
        name="conv_a",
    )(zmain, zmain, prev32, w, b, lg, lb)


CARRY_B = 8
SSD_LP = 128


def _ssd_kernel(xbc_ref, z_ref, dt_ref, prevb_ref, h0_ref, cw_ref, cb_ref, dtb_ref, a_ref, expand_ref,
                dskip_ref, nrm_ref, y_ref, newb_ref, hout_ref, buf_ref, ht_ref, *, lc):
    c = pl.program_id(1)
    nc = pl.num_programs(1)
    lp = SSD_LP

    @pl.when(c == 0)
    def _():
        buf_ref[pl.ds(0, CARRY_B), :] = prevb_ref[0]
        ht_ref[...] = h0_ref[0].T
        if lc < lp:
            buf_ref[pl.ds(CARRY_B + lc, lp - lc), :] = jnp.zeros((lp - lc, CONV_DIM), F32)

    buf_ref[pl.ds(CARRY_B, lc), :] = xbc_ref[0]
    xbc = _silu(_dwconv_from_buf(buf_ref, cw_ref, SSM_CONV, CARRY_B, lp) + cb_ref[...])
    hist = buf_ref[pl.ds(lc, CARRY_B), :]
    newb_ref[0] = hist
    buf_ref[pl.ds(0, CARRY_B), :] = hist

    xs = xbc[:, :D_INNER]
    if lc < lp:
        dt_raw = jnp.concatenate([dt_ref[0], jnp.zeros((lp - lc, LANES), F32)], axis=0)
        zg = jnp.concatenate([z_ref[0], jnp.zeros((lp - lc, D_INNER), F32)], axis=0)
    else:
        dt_raw = dt_ref[0]
        zg = z_ref[0]
    dt = _softplus(dt_raw + dtb_ref[...])
    if lc < lp:
        row = lax.broadcasted_iota(I32, (lp, LANES), 0)
        dt = jnp.where(row < lc, dt, 0.0)
    lane = lax.broadcasted_iota(I32, (lp, LANES), 1)
    dt = jnp.where(lane < SSM_HEADS, dt, 0.0)
    da = dt * a_ref[...]

    r_i = lax.broadcasted_iota(I32, (lp, lp), 0)
    c_i = lax.broadcasted_iota(I32, (lp, lp), 1)
    causal = r_i >= c_i
    tri = jnp.where(causal, 1.0, 0.0).astype(BF16)
    a_col = _dot_exact_lhs(tri, da)
    a_row = a_col.T
    dt_row = dt.T
    a_last = a_col[lp - 1:lp, :]

    expand = expand_ref[...]
    a_exp = _dot_exact_rhs(a_col, expand)
    dt_exp = _dot_exact_rhs(dt, expand)
    al_exp = _dot_exact_rhs(a_last, expand)

    ht = ht_ref[...]
    y_parts = []
    for g in range(SSM_GROUPS):
        bm = xbc[:, D_INNER + g * SSM_STATE: D_INNER + (g + 1) * SSM_STATE]
        cm = xbc[:, D_INNER + (SSM_GROUPS + g) * SSM_STATE: D_INNER + (SSM_GROUPS + g + 1) * SSM_STATE]
        cb = _bdot_nt(cm, bm)
        gs = slice(g * 512, (g + 1) * 512)
        y_inter = _bdot(cm, ht[:, gs]) * jnp.exp(a_exp[:, gs])
        heads = []
        for r in range(SSM_HEADS // SSM_GROUPS):
            h = g * (SSM_HEADS // SSM_GROUPS) + r
            seg = a_col[:, h:h + 1] - a_row[h:h + 1, :]
            decay = jnp.where(causal, jnp.exp(jnp.where(causal, seg, 0.0)), 0.0)
            wts = cb * decay * dt_row[h:h + 1, :]
            heads.append(_bdot(wts, xs[:, h * SSM_HEADDIM:(h + 1) * SSM_HEADDIM]))
        y_parts.append(jnp.concatenate(heads, axis=-1) + y_inter)
        xw = xs[:, gs] * (jnp.exp(al_exp[:, gs] - a_exp[:, gs]) * dt_exp[:, gs])
        ht_ref[:, gs] = ht[:, gs] * jnp.exp(al_exp[:, gs]) + _bdot(bm.T, xw)
    y = jnp.concatenate(y_parts, axis=-1)
    y = y + dskip_ref[...] * xs
    y = y * _silu(zg)
    outs = []
    for g in range(SSM_GROUPS):
        yg = y[:, g * 512:(g + 1) * 512]
        outs.append(yg * lax.rsqrt(jnp.mean(yg * yg, axis=-1, keepdims=True) + EPS))
    yb = jnp.concatenate(outs, axis=-1) * nrm_ref[...]
    y_ref[0] = yb[:lc, :]

    @pl.when(c == nc - 1)
    def _():
        hout_ref[0] = ht_ref[...].T


def ssd_mixer(zmain, ztail, prev8, h0, cw, cb, dtb, a_neg, expand, dskip_exp, nrm, *, lc):
    bsz, L, _ = zmain.shape
    kern = functools.partial(_ssd_kernel, lc=lc)
    return pl.pallas_call(
        kern,
        out_shape=(jax.ShapeDtypeStruct((bsz, L, D_INNER), F32),
                   jax.ShapeDtypeStruct((bsz, CARRY_B, CONV_DIM), F32),
                   jax.ShapeDtypeStruct((bsz, D_INNER, SSM_STATE), F32)),
        grid=(bsz, L // lc),
        in_specs=[pl.BlockSpec((1, lc, CONV_DIM), lambda b_, c: (b_, c, E_COL_XBC // CONV_DIM)),
                  pl.BlockSpec((1, lc, D_INNER), lambda b_, c: (b_, c, E_COL_Z // D_INNER)),
                  pl.BlockSpec((1, lc, LANES), lambda b_, c: (b_, c, 0)),
                  pl.BlockSpec((1, CARRY_B, CONV_DIM), lambda b_, c: (b_, 0, 0)),
                  pl.BlockSpec((1, D_INNER, SSM_STATE), lambda b_, c: (b_, 0, 0)),
                  pl.BlockSpec((SSM_CONV, CONV_DIM), lambda b_, c: (0, 0)),
                  pl.BlockSpec((1, CONV_DIM), lambda b_, c: (0, 0)),
                  pl.BlockSpec((1, LANES), lambda b_, c: (0, 0)),
                  pl.BlockSpec((1, LANES), lambda b_, c: (0, 0)),
                  pl.BlockSpec((LANES, D_INNER), lambda b_, c: (0, 0)),
                  pl.BlockSpec((1, D_INNER), lambda b_, c: (0, 0)),
                  pl.BlockSpec((1, D_INNER), lambda b_, c: (0, 0))],
        out_specs=(pl.BlockSpec((1, lc, D_INNER), lambda b_, c: (b_, c, 0)),
                   pl.BlockSpec((1, CARRY_B, CONV_DIM), lambda b_, c: (b_, 0, 0)),
                   pl.BlockSpec((1, D_INNER, SSM_STATE), lambda b_, c: (b_, 0, 0))),
        scratch_shapes=[pltpu.VMEM((CARRY_B + SSD_LP, CONV_DIM), F32),
                        pltpu.VMEM((SSM_STATE, D_INNER), F32)],
        compiler_params=_cparams(("arbitrary", "arbitrary")),
        name="ssd_mixer",
    )(zmain, zmain, ztail, prev8, h0, cw, cb, dtb, a_neg, expand, dskip_exp, nrm)


def _out_proj_kernel(x_ref, a_ref, b_ref, wa_ref, wb_ref, o_ref):
    acc = jnp.dot(a_ref[...].astype(BF16), wa_ref[...], preferred_element_type=F32)
    acc = acc + jnp.dot(b_ref[...].astype(BF16), wb_ref[...], preferred_element_type=F32)
    o_ref[...] = x_ref[...] + acc


def out_proj(x, a, b, wa, wb, *, tm):
    m, d = x.shape
    ka, kb = a.shape[1], b.shape[1]
    return pl.pallas_call(
        _out_proj_kernel,
        out_shape=jax.ShapeDtypeStruct((m, d), F32),
        grid=(m // tm,),
        in_specs=[pl.BlockSpec((tm, d), lambda i: (i, 0)),
                  pl.BlockSpec((tm, ka), lambda i: (i, 0)),
                  pl.BlockSpec((tm, kb), lambda i: (i, 0)),
                  pl.BlockSpec((ka, d), lambda i: (0, 0)),
                  pl.BlockSpec((kb, d), lambda i: (0, 0))],
        out_specs=pl.BlockSpec((tm, d), lambda i: (i, 0)),
        compiler_params=_cparams(("arbitrary",)),
        name="out_proj",
    )(x, a, b, wa, wb)


def _swiglu_kernel(x_ref, g_ref, wg_ref, wu_ref, wd_ref, o_ref, xn_ref, acc_ref):
    f = pl.program_id(1)
    nf = pl.num_programs(1)

    @pl.when(f == 0)
    def _():
        x = x_ref[...]
        y = x * lax.rsqrt(jnp.mean(x * x, axis=-1, keepdims=True) + EPS) * g_ref[...]
        xn_ref[...] = y.astype(BF16)
        acc_ref[...] = jnp.zeros_like(acc_ref)

    xn = xn_ref[...]
    gate = jnp.dot(xn, wg_ref[...], preferred_element_type=F32)
    up = jnp.dot(xn, wu_ref[...], preferred_element_type=F32)
    hid = (_silu(gate) * up).astype(BF16)
    acc_ref[...] += jnp.dot(hid, wd_ref[...], preferred_element_type=F32)

    @pl.when(f == nf - 1)
    def _():
        o_ref[...] = x_ref[...] + acc_ref[...]


def swiglu_ffn(x, g, wg, wu, wd, *, tm, tf):
    m, d = x.shape
    ff = wg.shape[1]
    return pl.pallas_call(
        _swiglu_kernel,
        out_shape=jax.ShapeDtypeStruct((m, d), F32),
        grid=(m // tm, ff // tf),
        in_specs=[pl.BlockSpec((tm, d), lambda i, f: (i, 0)),
                  pl.BlockSpec((1, d), lambda i, f: (0, 0)),
                  pl.BlockSpec((d, tf), lambda i, f: (0, f)),
                  pl.BlockSpec((d, tf), lambda i, f: (0, f)),
                  pl.BlockSpec((tf, d), lambda i, f: (f, 0))],
        out_specs=pl.BlockSpec((tm, d), lambda i, f: (i, 0)),
        scratch_shapes=[pltpu.VMEM((tm, d), BF16), pltpu.VMEM((tm, d), F32)],
        compiler_params=_cparams(("arbitrary", "arbitrary")),
        name="swiglu_ffn",
    )(x, g, wg, wu, wd)


O_COL_Q, O_COL_K, O_COL_V, O_COL_QI, O_COL_DIN, O_COL_BG, O_COL_CG = 0, 512, 640, 768, 1024, 1536, 2048


def _rope128(x, cos_t, sin_t, lo_mask):
    up = pltpu.roll(x, LANES - ROT_DIM // 2, 1)
    dn = pltpu.roll(x, ROT_DIM // 2, 1)
    return x * cos_t + jnp.where(lo_mask, up, dn) * sin_t


def _odd_post_kernel(qkvq_ref, tail_ref, cos_ref, sin_ref, qn_ref, kn_ref, kg_ref, kb_ref, bd_ref,
                     q_ref, k_ref, v_ref, qi_ref, kiw_ref, k16_ref, v16_ref, kk16_ref, kit_ref=None):
    cos_t = cos_ref[...]
    sin_t = sin_ref[...]
    lane = lax.broadcasted_iota(I32, cos_t.shape, 1)
    lo_mask = (lane % HEAD_DIM) < (ROT_DIM // 2)
    bd = bd_ref[...]
    z = qkvq_ref[...]
    for c in range(4):
        x = z[:, c * LANES:(c + 1) * LANES]
        ms = _dot_exact_rhs(x * x, bd)
        xn = x * lax.rsqrt(ms + EPS) * qn_ref[...]
        q_ref[:, c * LANES:(c + 1) * LANES] = _rope128(xn, cos_t, sin_t, lo_mask)
    x = z[:, O_COL_K:O_COL_K + LANES]
    ms = _dot_exact_rhs(x * x, bd)
    xn = x * lax.rsqrt(ms + EPS) * kn_ref[...]
    kr = _rope128(xn, cos_t, sin_t, lo_mask)
    k16_ref[...] = kr.astype(BF16)
    vv = z[:, O_COL_V:O_COL_V + LANES]
    v16_ref[...] = vv.astype(BF16)
    if kit_ref is None:
        k_ref[...] = kr
        v_ref[...] = vv
    else:
        k_ref[0] = kr.T
        v_ref[0] = vv.T
    for c in range(2):
        x = z[:, O_COL_QI + c * LANES: O_COL_QI + (c + 1) * LANES]
        qi_ref[:, c * LANES:(c + 1) * LANES] = _rope128(x, cos_t, sin_t, lo_mask)
    t = tail_ref[...]
    lane_t = lax.broadcasted_iota(I32, t.shape, 1)
    is_ki = lane_t < IDX_DIM
    mu = _dot_exact_rhs(t, bd)
    tc = t - mu
    var = _dot_exact_rhs(tc * tc, bd)
    kin = tc * lax.rsqrt(var + EPS) * kg_ref[...] + kb_ref[...]
    kin = _rope128(kin, cos_t, sin_t, lo_mask)
    kiw_ref[...] = jnp.where(is_ki, kin, t * (IDX_HEADS ** -0.5 * IDX_DIM ** -0.5))
    kk16_ref[...] = jnp.where(is_ki, kin, pltpu.roll(kin, IDX_DIM, 1)).astype(BF16)
    if kit_ref is not None:
        kit_ref[0] = kin.T[:IDX_DIM]


def odd_post(zmain, ztail, cos_t, sin_t, qn, kn, kg, kb, bd, *, tm, seq_len=None):
    m = zmain.shape[0]
    nrope = cos_t.shape[0] // tm
    row = lambda i: (i, 0)
    rope = lambda i: (i % nrope, 0)
    const = lambda i: (0, 0)
    tok = lambda w, dt=F32: (jax.ShapeDtypeStruct((m, w), dt), pl.BlockSpec((tm, w), row))
    if seq_len is None:
        kv = [tok(LANES), tok(LANES)]
        extra = []
    else:
        nblk = seq_len // tm
        bsz = m // seq_len
        seq = lambda f: (jax.ShapeDtypeStruct((bsz, f, seq_len), F32),
                         pl.BlockSpec((1, f, tm), lambda i: (i // nblk, 0, i % nblk)))
        kv = [seq(LANES), seq(LANES)]
        extra = [seq(IDX_DIM)]
    outs = [tok(512)] + kv + [tok(256), tok(LANES), tok(LANES, BF16), tok(LANES, BF16), tok(LANES, BF16)] + extra
    return pl.pallas_call(
        _odd_post_kernel,
        out_shape=tuple(o[0] for o in outs),
        grid=(m // tm,),
        in_specs=[pl.BlockSpec((tm, 1024), row), pl.BlockSpec((tm, LANES), row),
                  pl.BlockSpec((tm, LANES), rope), pl.BlockSpec((tm, LANES), rope),
                  pl.BlockSpec((1, LANES), const), pl.BlockSpec((1, LANES), const),
                  pl.BlockSpec((1, LANES), const), pl.BlockSpec((1, LANES), const),
                  pl.BlockSpec((LANES, LANES), const)],
        out_specs=tuple(o[1] for o in outs),
        compiler_params=_cparams(("arbitrary",)),
        name="odd_post",
    )(zmain, ztail, cos_t, sin_t, qn, kn, kg, kb, bd)


INT_MIN = -2 ** 31
F32_MIN_NORMAL_BITS = 0x00800000


def _code_to_float(c):
    return pltpu.bitcast(c ^ ((c >> 31) & 0x7FFFFFFF), F32)


def _kth_largest(count_ge, k, shape):
    ans = jnp.full(shape, INT_MIN, I32)
    cand = jnp.zeros(shape, I32)
    ans = jnp.where(count_ge(_code_to_float(cand)) >= k, cand, ans)

    def body(i, ans):
        cand = ans | jnp.left_shift(jnp.int32(1), 30 - i)
        return jnp.where(count_ge(_code_to_float(cand)) >= k, cand, ans)

    ans = lax.fori_loop(0, 31, body, ans)
    ans = jnp.where((ans > 0) & (ans < F32_MIN_NORMAL_BITS), 0, ans)
    return _code_to_float(ans)


def _sum_leading(x):
    n = x.shape[0]
    extra = None
    while n > 1:
        if n % 2:
            extra = x[n - 1] if extra is None else extra + x[n - 1]
            n -= 1
        x = x[:n // 2] + x[n // 2:n]
        n //= 2
    return x[0] if extra is None else x[0] + extra


def _upper_tri_bf16(n):
    r = lax.broadcasted_iota(I32, (n, n), 0)
    c = lax.broadcasted_iota(I32, (n, n), 1)
    return jnp.where(r <= c, 1.0, 0.0).astype(BF16)


def _dsa_prompt_kernel(q_ref, qi_ref, wi_ref, k_ref, v_ref, kk_ref, o_ref, keys_ref, bias_ref, *,
                       q_base, s_eff, topk, qb):
    t0 = q_base + pl.program_id(1) * qb
    lane = lax.broadcasted_iota(I32, (qb, LANES), 1)
    row = lax.broadcasted_iota(I32, (qb, LANES), 0)
    lo = lane < HEAD_DIM
    hi = lane >= HEAD_DIM
    rpg = N_HEADS // N_KV

    kk = kk_ref[0]
    qi = qi_ref[0]
    wi = wi_ref[0]
    sc = None
    for h in range(IDX_HEADS):
        x = qi[:, (h // 2) * LANES:(h // 2 + 1) * LANES]
        x = jnp.where(lo if h % 2 == 0 else hi, x, 0.0)
        d = _bdot_nt(x, kk)
        term = jnp.maximum(d, 0.0) * wi[:, IDX_DIM + h:IDX_DIM + h + 1]
        sc = term if sc is None else sc + term
    kpos = lax.broadcasted_iota(I32, (qb, s_eff), 1)
    qpos = t0 + lax.broadcasted_iota(I32, (qb, s_eff), 0)
    adm_all = kpos <= qpos
    sc = jnp.where(sc == 0.0, 0.0, sc)
    keys_ref[...] = jnp.where(adm_all, sc, NEG)

    def count_ge(cand):
        acc = None
        for j in range(s_eff // LANES):
            hit = jnp.where(keys_ref[:, j * LANES:(j + 1) * LANES] >= cand, 1.0, 0.0)
            acc = hit if acc is None else acc + hit
        return jnp.sum(acc, axis=-1, keepdims=True)

    thr = _kth_largest(count_ge, float(topk), (qb, 1))
    n_ge = count_ge(thr)
    n_adm = t0 + 1 + lax.broadcasted_iota(I32, (qb, 1), 0)
    simple = jnp.where((n_ge == float(topk)) | (n_adm < topk), 0.0, 1.0)
    no_cut = jnp.max(simple) == 0.0

    @pl.when(no_cut)
    def _():
        bias_ref[...] = jnp.where(adm_all, jnp.where(keys_ref[...] >= thr, 0.0, NEG), NEG)

    @pl.when(jnp.logical_not(no_cut))
    def _():
        n_gt = jnp.sum(jnp.where(keys_ref[...] > thr, 1.0, 0.0), axis=-1, keepdims=True)
        need = float(topk) - n_gt
        ut = _upper_tri_bf16(LANES)
        off = jnp.zeros((qb, 1), F32)
        for j in range(s_eff // LANES):
            kj = keys_ref[:, j * LANES:(j + 1) * LANES]
            eq = kj == thr
            pre = jnp.dot(jnp.where(eq, 1.0, 0.0).astype(BF16), ut, preferred_element_type=F32) + off
            adm = (j * LANES + lane) <= (t0 + row)
            keep = jnp.where(eq, jnp.where(pre <= need, 0.0, NEG), jnp.where(kj > thr, 0.0, NEG))
            bias_ref[:, j * LANES:(j + 1) * LANES] = jnp.where(adm, keep, NEG)
            off = pre[:, LANES - 1:LANES]

    kb = k_ref[0]
    vb = v_ref[0]
    lane_b = lax.broadcasted_iota(I32, (Q_BLOCK, LANES), 1)
    lo_b = lane_b < HEAD_DIM
    hi_b = lane_b >= HEAD_DIM
    lo4 = lax.broadcasted_iota(I32, (rpg * Q_BLOCK, LANES), 1) < HEAD_DIM
    for sub in range(qb // Q_BLOCK):
        rows = pl.ds(sub * Q_BLOCK, Q_BLOCK)
        bias = bias_ref[rows, :]
        q = q_ref[0, rows, :] * (HEAD_DIM ** -0.5)
        outs = []
        for g in range(N_KV):
            keep = lo_b if g == 0 else hi_b
            qg = jnp.concatenate([jnp.where(keep, q[:, r * LANES:(r + 1) * LANES], 0.0) for r in range(rpg)],
                                 axis=0)
            s = _bdot_nt(qg, kb).reshape(rpg, Q_BLOCK, s_eff) + bias[None]
            m = jnp.max(s, axis=-1, keepdims=True)
            e = jnp.exp(s - m)
            l = jnp.sum(e, axis=-1, keepdims=True).reshape(rpg * Q_BLOCK, 1)
            pv = jnp.dot(e.reshape(rpg * Q_BLOCK, s_eff).astype(BF16), vb, preferred_element_type=F32)
            outs.append(pv / l)
        oc = jnp.where(lo4, outs[0], outs[1])
        for r in range(rpg):
            o_ref[0, rows, r * LANES:(r + 1) * LANES] = oc[r * Q_BLOCK:(r + 1) * Q_BLOCK, :]


DSA_SEGMENT = 256


def dsa_prompt(q, qi, wi, k16, v16, kk16):
    bsz, s_len, _ = q.shape
    topk = min(TOPK_MAX, s_len // 4)
    seg = min(DSA_SEGMENT, s_len)
    outs = []
    for si in range(s_len // seg):
        s_eff = (si + 1) * seg
        kern = functools.partial(_dsa_prompt_kernel, q_base=si * seg, s_eff=s_eff, topk=topk, qb=seg)
        qblk = lambda w, si=si: pl.BlockSpec((1, seg, w), lambda b_, i: (b_, si, 0))
        keys = pl.BlockSpec((1, s_eff, LANES), lambda b_, i: (b_, 0, 0))
        outs.append(pl.pallas_call(
            kern,
            out_shape=jax.ShapeDtypeStruct((bsz, seg, 512), F32),
            grid=(bsz, 1),
            in_specs=[qblk(512), qblk(256), qblk(LANES), keys, keys, keys],
            out_specs=pl.BlockSpec((1, seg, 512), lambda b_, i: (b_, 0, 0)),
            scratch_shapes=[pltpu.VMEM((seg, s_eff), F32), pltpu.VMEM((seg, s_eff), F32)],
            compiler_params=_cparams(("arbitrary", "arbitrary")),
            name=f"dsa_prompt_{si}",
        )(q, qi, wi, k16, v16, kk16))
    return jnp.concatenate(outs, axis=1)


CARRY_D = 8


def _conv_d_kernel(din_ref, bg_ref, cg_ref, prev_ref, w_ref, y_ref, new_ref, buf_ref, *, tl):
    l = pl.program_id(1)

    @pl.when(l == 0)
    def _():
        buf_ref[pl.ds(0, CARRY_D), :] = prev_ref[0]

    buf_ref[pl.ds(CARRY_D, tl), :] = cg_ref[0] * din_ref[0]
    y_ref[0] = bg_ref[0] * _dwconv_from_buf(buf_ref, w_ref, CONV_D, CARRY_D, tl)
    hist = buf_ref[pl.ds(tl, CARRY_D), :]
    new_ref[0] = hist
    buf_ref[pl.ds(0, CARRY_D), :] = hist


def conv_d(zmain, prev8, w, *, tl):
    bsz, L, _ = zmain.shape
    kern = functools.partial(_conv_d_kernel, tl=tl)
    col = lambda c: pl.BlockSpec((1, tl, D_D), lambda b_, l: (b_, l, c // D_D))
    return pl.pallas_call(
        kern,
        out_shape=(jax.ShapeDtypeStruct((bsz, L, D_D), F32), jax.ShapeDtypeStruct((bsz, CARRY_D, D_D), F32)),
        grid=(bsz, L // tl),
        in_specs=[col(O_COL_DIN), col(O_COL_BG), col(O_COL_CG),
                  pl.BlockSpec((1, CARRY_D, D_D), lambda b_, l: (b_, 0, 0)),
                  pl.BlockSpec((CONV_D, D_D), lambda b_, l: (0, 0))],
        out_specs=(pl.BlockSpec((1, tl, D_D), lambda b_, l: (b_, l, 0)),
                   pl.BlockSpec((1, CARRY_D, D_D), lambda b_, l: (b_, 0, 0))),
        scratch_shapes=[pltpu.VMEM((CARRY_D + tl, D_D), F32)],
        compiler_params=_cparams(("arbitrary", "arbitrary")),
        name="conv_d",
    )(zmain, zmain, zmain, prev8, w)


def _router_kernel(x_ref, g_ref, whi_ref, wlo_ref, br_ref, xn_ref, comb_ref):
    x = x_ref[...]
    y = x * lax.rsqrt(jnp.mean(x * x, axis=-1, keepdims=True) + EPS) * g_ref[...]
    yhi = y.astype(BF16)
    xn_ref[...] = yhi
    ylo = (y - yhi.astype(F32)).astype(BF16)
    logits = jnp.dot(ylo, whi_ref[...], preferred_element_type=F32)
    logits = logits + jnp.dot(yhi, wlo_ref[...], preferred_element_type=F32)
    logits = logits + jnp.dot(yhi, whi_ref[...], preferred_element_type=F32) + br_ref[...]
    lane = lax.broadcasted_iota(I32, logits.shape, 1).astype(F32)
    m1 = jnp.max(logits, axis=-1, keepdims=True)
    i1 = jnp.min(jnp.where(logits == m1, lane, float(LANES)), axis=-1, keepdims=True)
    rest = jnp.where(lane == i1, -3e38, logits)
    m2 = jnp.max(rest, axis=-1, keepdims=True)
    i2 = jnp.min(jnp.where(rest == m2, lane, float(LANES)), axis=-1, keepdims=True)
    e2 = jnp.exp(m2 - m1)
    g1 = 1.0 / (1.0 + e2)
    g2 = e2 / (1.0 + e2)
    comb_ref[...] = jnp.where(lane == i1, g1, 0.0) + jnp.where(lane == i2, g2, 0.0)


def moe_router(x, g, w_hi, w_lo, b_pad, *, tm):
    m, d = x.shape
    return pl.pallas_call(
        _router_kernel,
        out_shape=(jax.ShapeDtypeStruct((m, d), BF16), jax.ShapeDtypeStruct((m, LANES), F32)),
        grid=(m // tm,),
        in_specs=[pl.BlockSpec((tm, d), lambda i: (i, 0)),
                  pl.BlockSpec((1, d), lambda i: (0, 0)),
                  pl.BlockSpec((d, LANES), lambda i: (0, 0)),
                  pl.BlockSpec((d, LANES), lambda i: (0, 0)),
                  pl.BlockSpec((1, LANES), lambda i: (0, 0))],
        out_specs=(pl.BlockSpec((tm, d), lambda i: (i, 0)), pl.BlockSpec((tm, LANES), lambda i: (i, 0))),
        compiler_params=_cparams(("arbitrary",)),
        name="moe_router",
    )(x, g, w_hi, w_lo, b_pad)


def _moe_dense_kernel(x_ref, xn_ref, comb_ref, wg_ref, wu_ref, wd_ref, o_ref, acc_ref, acce_ref):
    e = pl.program_id(1)
    f = pl.program_id(2)
    ne = pl.num_programs(1)
    nf = pl.num_programs(2)

    @pl.when((e == 0) & (f == 0))
    def _():
        acc_ref[...] = jnp.zeros_like(acc_ref)

    @pl.when(f == 0)
    def _():
        acce_ref[...] = jnp.zeros_like(acce_ref)

    xn = xn_ref[...]
    gate = jnp.dot(xn, wg_ref[0], preferred_element_type=F32)
    up = jnp.dot(xn, wu_ref[0], preferred_element_type=F32)
    hid = (_silu(gate) * up).astype(BF16)
    acce_ref[...] += jnp.dot(hid, wd_ref[0], preferred_element_type=F32)

    @pl.when(f == nf - 1)
    def _():
        comb = comb_ref[...]
        lane = lax.broadcasted_iota(I32, comb.shape, 1)
        c = jnp.sum(jnp.where(lane == e, comb, 0.0), axis=-1, keepdims=True)
        acc_ref[...] += c * acce_ref[...]

    @pl.when((e == ne - 1) & (f == nf - 1))
    def _():
        o_ref[...] = x_ref[...] + acc_ref[...]


def moe_dense(x, xn, comb, wg, wu, wd, *, tm, tf):
    m, d = x.shape
    ne, _, ff = wg.shape
    return pl.pallas_call(
        _moe_dense_kernel,
        out_shape=jax.ShapeDtypeStruct((m, d), F32),
        grid=(m // tm, ne, ff // tf),
        in_specs=[pl.BlockSpec((tm, d), lambda i, e, f: (i, 0)),
                  pl.BlockSpec((tm, d), lambda i, e, f: (i, 0)),
                  pl.BlockSpec((tm, LANES), lambda i, e, f: (i, 0)),
                  pl.BlockSpec((1, d, tf), lambda i, e, f: (e, 0, f)),
                  pl.BlockSpec((1, d, tf), lambda i, e, f: (e, 0, f)),
                  pl.BlockSpec((1, tf, d), lambda i, e, f: (e, f, 0))],
        out_specs=pl.BlockSpec((tm, d), lambda i, e, f: (i, 0)),
        scratch_shapes=[pltpu.VMEM((tm, d), F32), pltpu.VMEM((tm, d), F32)],
        compiler_params=_cparams(("arbitrary", "arbitrary", "arbitrary")),
        name="moe_dense",
    )(x, xn, comb, wg, wu, wd)


MOE_TILE = 512
DMA_ISSUE_UNROLL = 8
FF_TILE = D_FF // 2
R_I1, R_I2, R_G1, R_G2, R_RANK1, R_RANK2 = 0, 1, 2, 3, 4, 5


def _router_rank_kernel(x_ref, g_ref, whi_ref, wlo_ref, br_ref, info_ref, cnt_ref, run_ref):
    i = pl.program_id(0)

    @pl.when(i == 0)
    def _():
        run_ref[...] = jnp.zeros_like(run_ref)

    x = x_ref[...]
    tm = x.shape[0]
    y = x * lax.rsqrt(jnp.mean(x * x, axis=-1, keepdims=True) + EPS) * g_ref[...]
    yhi = y.astype(BF16)
    ylo = (y - yhi.astype(F32)).astype(BF16)
    logits = jnp.dot(ylo, whi_ref[...], preferred_element_type=F32)
    logits = logits + jnp.dot(yhi, wlo_ref[...], preferred_element_type=F32)
    logits = logits + jnp.dot(yhi, whi_ref[...], preferred_element_type=F32) + br_ref[...]
    lane = lax.broadcasted_iota(I32, logits.shape, 1).astype(F32)
    m1 = jnp.max(logits, axis=-1, keepdims=True)
    i1 = jnp.min(jnp.where(logits == m1, lane, float(LANES)), axis=-1, keepdims=True)
    rest = jnp.where(lane == i1, -3e38, logits)
    m2 = jnp.max(rest, axis=-1, keepdims=True)
    i2 = jnp.min(jnp.where(rest == m2, lane, float(LANES)), axis=-1, keepdims=True)
    e2 = jnp.exp(m2 - m1)
    g1 = 1.0 / (1.0 + e2)
    g2 = e2 / (1.0 + e2)
    oh1 = jnp.where(lane == i1, 1.0, 0.0)
    oh2 = jnp.where(lane == i2, 1.0, 0.0)
    both = oh1 + oh2
    r_i = lax.broadcasted_iota(I32, (tm, tm), 0)
    c_i = lax.broadcasted_iota(I32, (tm, tm), 1)
    strict_lower = jnp.where(c_i < r_i, 1.0, 0.0).astype(BF16)
    before = jnp.dot(strict_lower, both.astype(BF16), preferred_element_type=F32) + run_ref[...]
    rank1 = jnp.sum(before * oh1, axis=-1, keepdims=True)
    rank2 = jnp.sum(before * oh2, axis=-1, keepdims=True)
    run_ref[...] += jnp.sum(both, axis=0, keepdims=True)
    info = jnp.where(lane == R_I1, i1, 0.0)
    for col, val in ((R_I2, i2), (R_G1, g1), (R_G2, g2), (R_RANK1, rank1), (R_RANK2, rank2)):
        info = jnp.where(lane == col, val, info)
    info_ref[...] = info
    cnt_ref[...] = run_ref[...]


def moe_router_rank(x, g, w_hi, w_lo, b_pad, *, tm):
    m, d = x.shape
    return pl.pallas_call(
        _router_rank_kernel,
        out_shape=(jax.ShapeDtypeStruct((m, LANES), F32), jax.ShapeDtypeStruct((1, LANES), F32)),
        grid=(m // tm,),
        in_specs=[pl.BlockSpec((tm, d), lambda i: (i, 0)),
                  pl.BlockSpec((1, d), lambda i: (0, 0)),
                  pl.BlockSpec((d, LANES), lambda i: (0, 0)),
                  pl.BlockSpec((d, LANES), lambda i: (0, 0)),
                  pl.BlockSpec((1, LANES), lambda i: (0, 0))],
        out_specs=(pl.BlockSpec((tm, LANES), lambda i: (i, 0)), pl.BlockSpec((1, LANES), lambda i: (0, 0))),
        scratch_shapes=[pltpu.VMEM((1, LANES), F32)],
        compiler_params=_cparams(("arbitrary",)),
        name="moe_router_rank",
    )(x, g, w_hi, w_lo, b_pad)


def _row_copy(src_ref, si, dst_ref, di, sem):
    return pltpu.make_async_copy(src_ref.at[pl.ds(si, 1)], dst_ref.at[pl.ds(di, 1)], sem)


def _moe_dispatch_kernel(d1_ref, d2_ref, x_ref, xs_ref, sem, *, ct):
    def issue(t, carry):
        _row_copy(x_ref, t, xs_ref, d1_ref[0, 0, t], sem).start(priority=0)
        _row_copy(x_ref, t, xs_ref, d2_ref[0, 0, t], sem).start(priority=1)
        return carry

    lax.fori_loop(0, ct, issue, 0, unroll=DMA_ISSUE_UNROLL)
    tile_copy = pltpu.make_async_copy(x_ref, xs_ref.at[pl.ds(0, ct)], sem)
    tile_copy.wait()
    tile_copy.wait()


def moe_dispatch(x, dest1, dest2, *, ct):
    m, d = x.shape
    smem_row = pl.BlockSpec((1, 1, ct), lambda i: (i, 0, 0), memory_space=pltpu.SMEM)
    return pl.pallas_call(
        functools.partial(_moe_dispatch_kernel, ct=ct),
        out_shape=jax.ShapeDtypeStruct((2 * m, d), F32),
        grid=(m // ct,),
        in_specs=[smem_row, smem_row, pl.BlockSpec((ct, d), lambda i: (i, 0))],
        out_specs=pl.BlockSpec(memory_space=pl.ANY),
        scratch_shapes=[pltpu.SemaphoreType.DMA],
        compiler_params=_cparams(("arbitrary",)),
        name="moe_dispatch",
    )(dest1, dest2, x)


def _moe_grouped_kernel(tile_ref, exp_ref, first_ref, valid_ref, gs_ref, xs_ref, g_ref, wg_ref, wu_ref, wd_ref,
                        ys_ref, xn_ref, acc_ref):
    w = pl.program_id(0)
    f = pl.program_id(1)
    nf = pl.num_programs(1)

    @pl.when(valid_ref[w] == 1)
    def _():
        @pl.when(f == 0)
        def _():
            x = xs_ref[...]
            y = x * lax.rsqrt(jnp.mean(x * x, axis=-1, keepdims=True) + EPS) * g_ref[...]
            xn_ref[...] = y.astype(BF16)
            acc_ref[...] = jnp.zeros_like(acc_ref)

        xn = xn_ref[...]
        gate = jnp.dot(xn, wg_ref[0], preferred_element_type=F32)
        up = jnp.dot(xn, wu_ref[0], preferred_element_type=F32)
        hid = (_silu(gate) * up).astype(BF16)
        acc_ref[...] += jnp.dot(hid, wd_ref[0], preferred_element_type=F32)

        @pl.when(f == nf - 1)
        def _():
            e = exp_ref[w]
            row = tile_ref[w] * MOE_TILE + lax.broadcasted_iota(I32, acc_ref.shape, 0)
            mine = (row >= gs_ref[e]) & (row < gs_ref[e + 1])
            part = jnp.where(mine, acc_ref[...], 0.0)

            @pl.when(first_ref[w] == 1)
            def _():
                ys_ref[...] = part

            @pl.when(first_ref[w] == 0)
            def _():
                ys_ref[...] += part


def moe_grouped(item_tile, item_expert, item_first, item_valid, group_starts, xs, g, wg, wu, wd, *, tf):
    r, d = xs.shape
    ff = wg.shape[2]
    n_items = item_tile.shape[0]
    return pl.pallas_call(
        _moe_grouped_kernel,
        out_shape=jax.ShapeDtypeStruct((r, d), F32),
        grid_spec=pltpu.PrefetchScalarGridSpec(
            num_scalar_prefetch=5,
            grid=(n_items, ff // tf),
            in_specs=[pl.BlockSpec((MOE_TILE, d), lambda w, f, it, ie, i1, iv, gs: (it[w], 0)),
                      pl.BlockSpec((1, d), lambda w, f, it, ie, i1, iv, gs: (0, 0)),
                      pl.BlockSpec((1, d, tf), lambda w, f, it, ie, i1, iv, gs: (ie[w], 0, f)),
                      pl.BlockSpec((1, d, tf), lambda w, f, it, ie, i1, iv, gs: (ie[w], 0, f)),
                      pl.BlockSpec((1, tf, d), lambda w, f, it, ie, i1, iv, gs: (ie[w], f, 0))],
            out_specs=pl.BlockSpec((MOE_TILE, d), lambda w, f, it, ie, i1, iv, gs: (it[w], 0)),
            scratch_shapes=[pltpu.VMEM((MOE_TILE, d), BF16), pltpu.VMEM((MOE_TILE, d), F32)]),
        compiler_params=_cparams(("arbitrary", "arbitrary")),
        name="moe_grouped",
    )(item_tile, item_expert, item_first, item_valid, group_starts, xs, g, wg, wu, wd)


def _moe_combine_kernel(d1_ref, d2_ref, x_ref, info_ref, ys_ref, o_ref, y1_ref, y2_ref, sem, *, ct):
    def issue(t, carry):
        _row_copy(ys_ref, d1_ref[0, 0, t], y1_ref, t, sem).start(priority=0)
        _row_copy(ys_ref, d2_ref[0, 0, t], y2_ref, t, sem).start(priority=1)
        return carry

    lax.fori_loop(0, ct, issue, 0, unroll=DMA_ISSUE_UNROLL)
    pltpu.make_async_copy(ys_ref.at[pl.ds(0, ct)], y1_ref, sem).wait()
    pltpu.make_async_copy(ys_ref.at[pl.ds(0, ct)], y2_ref, sem).wait()
    info = info_ref[...]
    g1 = info[:, R_G1:R_G1 + 1]
    g2 = info[:, R_G2:R_G2 + 1]
    o_ref[...] = x_ref[...] + (g1 * y1_ref[...] + g2 * y2_ref[...])


def moe_combine(x, info, ys, dest1, dest2, *, ct):
    m, d = x.shape
    smem_row = pl.BlockSpec((1, 1, ct), lambda i: (i, 0, 0), memory_space=pltpu.SMEM)
    return pl.pallas_call(
        functools.partial(_moe_combine_kernel, ct=ct),
        out_shape=jax.ShapeDtypeStruct((m, d), F32),
        grid=(m // ct,),
        in_specs=[smem_row, smem_row, pl.BlockSpec((ct, d), lambda i: (i, 0)),
                  pl.BlockSpec((ct, LANES), lambda i: (i, 0)), pl.BlockSpec(memory_space=pl.ANY)],
        out_specs=pl.BlockSpec((ct, d), lambda i: (i, 0)),
        scratch_shapes=[pltpu.VMEM((ct, d), F32), pltpu.VMEM((ct, d), F32), pltpu.SemaphoreType.DMA],
        compiler_params=_cparams(("arbitrary",)),
        name="moe_combine",
    )(dest1, dest2, x, info, ys)


def _moe_work_items(counts, n_tiles):
    n_exp = counts.shape[0]
    n_items = n_tiles + n_exp - 1
    ends = jnp.cumsum(counts)
    starts = ends - counts
    expert_of_row = lambda rr: jnp.minimum(jnp.sum(ends[None, :] <= rr[:, None], axis=1), n_exp - 1).astype(I32)
    t = jnp.arange(n_tiles, dtype=I32)
    e_lo = expert_of_row(t * MOE_TILE)
    e_hi = expert_of_row(t * MOE_TILE + (MOE_TILE - 1))
    per_tile = e_hi - e_lo + 1
    item_end = jnp.cumsum(per_tile)
    item_start = item_end - per_tile
    total = item_end[-1]
    w = jnp.minimum(jnp.arange(n_items, dtype=I32), total - 1)
    tile = jnp.sum(item_end[None, :] <= w[:, None], axis=1).astype(I32)
    expert = (e_lo[tile] + (w - item_start[tile])).astype(I32)
    first = (w == item_start[tile]).astype(I32)
    valid = (jnp.arange(n_items, dtype=I32) < total).astype(I32)
    group_starts = jnp.concatenate([starts, ends[-1:]]).astype(I32)
    return tile, expert, first, valid, group_starts


def moe_routed(x, g, w_hi, w_lo, b_pad, wg, wu, wd, *, tm):
    m, d = x.shape
    n_exp = wg.shape[0]
    assert (2 * m) % MOE_TILE == 0
    info, cnt = moe_router_rank(x, g, w_hi, w_lo, b_pad, tm=tm)
    counts = cnt[0, :n_exp].astype(I32)
    starts = jnp.cumsum(counts) - counts
    i1 = info[:, R_I1].astype(I32)
    i2 = info[:, R_I2].astype(I32)
    dest1 = (starts[i1] + info[:, R_RANK1].astype(I32)).reshape(m // tm, 1, tm)
    dest2 = (starts[i2] + info[:, R_RANK2].astype(I32)).reshape(m // tm, 1, tm)
    xs = moe_dispatch(x, dest1, dest2, ct=tm)
    items = _moe_work_items(counts, (2 * m) // MOE_TILE)
    ys = moe_grouped(*items, xs, g, wg, wu, wd, tf=FF_TILE)
    return moe_combine(x, info, ys, dest1, dest2, ct=tm)


PAGES_PER_STEP = 64
PAGE_SLOTS = 3
SDSA_GROUP = 4
COUNT_STREAMS = 4


def _fetch_pages(pt_ref, cache_ref, buf_ref, sem_ref, step, nj):
    b = step // nj
    j0 = (step % nj) * PAGES_PER_STEP
    slot = step % PAGE_SLOTS
    for i in range(PAGES_PER_STEP):
        pltpu.make_async_copy(cache_ref.at[pt_ref[b, j0 + i]], buf_ref.at[slot, i], sem_ref.at[slot]).start()


def _wait_pages(cache_ref, buf_ref, sem_ref, step):
    slot = step % PAGE_SLOTS
    pltpu.make_async_copy(cache_ref.at[pl.ds(0, PAGES_PER_STEP)], buf_ref.at[slot], sem_ref.at[slot]).wait()


def _page_pipeline_step(pt_ref, streams, nj):
    step = pl.program_id(0) * pl.num_programs(1) + pl.program_id(1)
    total = pl.num_programs(0) * pl.num_programs(1)

    ahead = PAGE_SLOTS - 1

    @pl.when(step == 0)
    def _():
        for d in range(ahead):
            @pl.when(d < total)
            def _():
                for cache_ref, buf_ref, sem_ref in streams:
                    _fetch_pages(pt_ref, cache_ref, buf_ref, sem_ref, step + d, nj)

    @pl.when(step + ahead < total)
    def _():
        for cache_ref, buf_ref, sem_ref in streams:
            _fetch_pages(pt_ref, cache_ref, buf_ref, sem_ref, step + ahead, nj)

    for cache_ref, buf_ref, sem_ref in streams:
        _wait_pages(cache_ref, buf_ref, sem_ref, step)
    return step % PAGE_SLOTS


def _sdsa_index_kernel(pt_ref, qi_ref, wi_ref, kknew_ref, cache_ref, mpast_ref, mnew_ref,
                       keys_ref, pre_ref, pbuf_ref, psem_ref, *, n_pages, t_len, topk, nb):
    nj = n_pages // PAGES_PER_STEP
    lb = pl.program_id(1) // nj
    j = pl.program_id(1) % nj
    rows = pl.ds(pl.multiple_of(lb * t_len, t_len), t_len)
    last_step = pl.program_id(1) == nb * nj - 1
    slot = _page_pipeline_step(pt_ref, [(cache_ref, pbuf_ref, psem_ref)], nj)
    lane = lax.broadcasted_iota(I32, (t_len, LANES), 1)
    row = lax.broadcasted_iota(I32, (t_len, LANES), 0)
    lo = lane < HEAD_DIM

    qi = qi_ref[0]
    wi = wi_ref[0]
    qs, ws = [], []
    for h in range(IDX_HEADS):
        x = qi[:, (h // 2) * LANES:(h // 2 + 1) * LANES]
        if h % 2 == 1:
            x = pltpu.roll(x, HEAD_DIM, 1)
        qs.append(x[:, :IDX_DIM])
        ws.append(wi[:, IDX_DIM + h:IDX_DIM + h + 1])
    q32 = jnp.concatenate(qs, axis=0).astype(BF16)
    w32 = jnp.concatenate(ws, axis=0)

    def scores(dots):
        d = jnp.maximum(dots, 0.0) * w32
        s = d[0:t_len]
        for h in range(1, IDX_HEADS):
            s = s + d[h * t_len:(h + 1) * t_len]
        return jnp.where(s == 0.0, 0.0, s)

    pages_t = jnp.concatenate([pbuf_ref[slot, i].astype(BF16) for i in range(PAGES_PER_STEP)], axis=1)
    s_wide = scores(jnp.dot(q32, pages_t, preferred_element_type=F32))
    for i in range(PAGES_PER_STEP):
        keys_ref[j * PAGES_PER_STEP + i, rows, :] = s_wide[:, i * PAGE_SIZE:(i + 1) * PAGE_SIZE]

    @pl.when(j == nj - 1)
    def _():
        knew = jnp.concatenate([kknew_ref[0][:, :IDX_DIM], jnp.zeros((LANES - t_len, IDX_DIM), F32)], axis=0)
        s_new = jnp.where((lane <= row) & (lane < t_len), scores(_bdot_nt(q32, knew)), NEG)
        keys_ref[n_pages, rows, :] = s_new

    @pl.when(last_step)
    def _():
        nrow = nb * t_len
        own = lambda x, g: x[..., g * t_len:(g + 1) * t_len, :]

        def count_ge(cand):
            accs = [None] * COUNT_STREAMS
            for p in range(n_pages + 1):
                hit = jnp.where(keys_ref[p] >= cand, 1.0, 0.0)
                accs[p % COUNT_STREAMS] = hit if accs[p % COUNT_STREAMS] is None else accs[p % COUNT_STREAMS] + hit
            part = accs[0]
            for a in accs[1:]:
                part = part + a
            return jnp.sum(part, axis=-1, keepdims=True)

        thr = _kth_largest(count_ge, float(topk), (nrow, 1))
        n_ge = count_ge(thr)
        no_cut = jnp.max(jnp.abs(n_ge - float(topk))) == 0.0

        @pl.when(no_cut)
        def _():
            bias = jnp.where(keys_ref[...] >= thr, 0.0, NEG)
            for g in range(nb):
                mpast_ref[g] = own(bias[:n_pages], g)
                mnew_ref[g] = own(bias[n_pages], g)

        @pl.when(jnp.logical_not(no_cut))
        def _():
            keys = keys_ref[...]
            n_gt = jnp.sum(_sum_leading(jnp.where(keys > thr, 1.0, 0.0)), axis=-1, keepdims=True)
            need = float(topk) - n_gt
            eqf = jnp.where(keys == thr, 1.0, 0.0).astype(BF16).reshape((n_pages + 1) * nrow, LANES)
            pre_ref[...] = jnp.dot(eqf, _upper_tri_bf16(LANES), preferred_element_type=F32).reshape(
                n_pages + 1, nrow, LANES)

            def body(p, off):
                kp = keys_ref[p]
                pre = pre_ref[p]
                bias = jnp.where(kp == thr, jnp.where((pre + off) <= need, 0.0, NEG),
                                 jnp.where(kp > thr, 0.0, NEG))

                @pl.when(p < n_pages)
                def _():
                    for g in range(nb):
                        mpast_ref[g, p] = own(bias, g)

                @pl.when(p == n_pages)
                def _():
                    for g in range(nb):
                        mnew_ref[g] = own(bias, g)

                return off + pre[:, LANES - 1:LANES]

            lax.fori_loop(0, n_pages + 1, body, jnp.zeros((nrow, 1), F32))


def sdsa_index(page_table, qi, wi, kk_new, cache_kidx):
    bsz, t_len, _ = qi.shape
    n_pages = page_table.shape[1]
    topk = min(TOPK_MAX, (n_pages * PAGE_SIZE + t_len) // 4)
    nb = math.gcd(bsz, SDSA_GROUP)
    nj = n_pages // PAGES_PER_STEP
    kern = functools.partial(_sdsa_index_kernel, n_pages=n_pages, t_len=t_len, topk=topk, nb=nb)
    tok = lambda w: pl.BlockSpec((1, t_len, w), lambda gi, jj, pt: (gi * nb + jj // nj, 0, 0))
    return pl.pallas_call(
        kern,
        out_shape=(jax.ShapeDtypeStruct((bsz, n_pages, t_len, LANES), F32),
                   jax.ShapeDtypeStruct((bsz, t_len, LANES), F32)),
        grid_spec=pltpu.PrefetchScalarGridSpec(
            num_scalar_prefetch=1,
            grid=(bsz // nb, nb * nj),
            in_specs=[tok(256), tok(LANES), tok(LANES), pl.BlockSpec(memory_space=pl.ANY)],
            out_specs=(pl.BlockSpec((nb, n_pages, t_len, LANES), lambda gi, jj, pt: (gi, 0, 0, 0)),
                       pl.BlockSpec((nb, t_len, LANES), lambda gi, jj, pt: (gi, 0, 0))),
            scratch_shapes=[pltpu.VMEM((n_pages + 1, nb * t_len, LANES), F32),
                            pltpu.VMEM((n_pages + 1, nb * t_len, LANES), F32),
                            pltpu.VMEM((PAGE_SLOTS, PAGES_PER_STEP, IDX_DIM, PAGE_SIZE), F32),
                            pltpu.SemaphoreType.DMA((PAGE_SLOTS,))]),
        compiler_params=_cparams(("arbitrary", "arbitrary")),
        name="sdsa_index",
    )(page_table, qi, wi, kk_new, cache_kidx)


def _sdsa_attn_kernel(pt_ref, q_ref, knew_ref, vnew_ref, mpast_ref, mnew_ref, ck_ref, cv_ref,
                      o_ref, qg_ref, m_ref, l_ref, acc_ref, kbuf_ref, vbuf_ref, ksem_ref, vsem_ref, *, t_len, n_pages):
    j = pl.program_id(1)
    nj = pl.num_programs(1)
    slot = _page_pipeline_step(pt_ref, [(ck_ref, kbuf_ref, ksem_ref), (cv_ref, vbuf_ref, vsem_ref)],
                               n_pages // PAGES_PER_STEP)
    rpg = N_HEADS // N_KV
    reps = rpg

    def group_lanes(x, g):
        if g == 1:
            x = pltpu.roll(x, HEAD_DIM, 1)
        return x[:, :HEAD_DIM]

    @pl.when(j == 0)
    def _():
        q = q_ref[0]
        for g in range(N_KV):
            parts = [group_lanes(q[:, r * LANES:(r + 1) * LANES], g) for r in range(rpg)]
            qg_ref[g] = (jnp.concatenate(parts, axis=0) * (HEAD_DIM ** -0.5)).astype(BF16)
        m_ref[...] = jnp.full(m_ref.shape, NEG, F32)
        l_ref[...] = jnp.zeros(l_ref.shape, F32)
        acc_ref[...] = jnp.zeros(acc_ref.shape, F32)

    def update(g, s, pv):
        m_old = m_ref[g]
        m_cur = jnp.maximum(m_old, jnp.max(s, axis=-1, keepdims=True))
        alpha = jnp.exp(m_old - m_cur)
        p = jnp.exp(s - m_cur)
        m_ref[g] = m_cur
        l_ref[g] = l_ref[g] * alpha + jnp.sum(p, axis=-1, keepdims=True)
        acc_ref[g] = acc_ref[g] * alpha + pv(p.astype(BF16))

    bias8 = jnp.concatenate([mpast_ref[0, i] for i in range(PAGES_PER_STEP)], axis=1)
    bias_wide = jnp.concatenate([bias8] * reps, axis=0)
    for g in range(N_KV):
        k_wide = jnp.concatenate([kbuf_ref[slot, i, g].astype(BF16) for i in range(PAGES_PER_STEP)], axis=1)
        v_wide = jnp.concatenate([vbuf_ref[slot, i, g].astype(BF16) for i in range(PAGES_PER_STEP)], axis=1)
        s = jnp.dot(qg_ref[g], k_wide, preferred_element_type=F32) + bias_wide
        update(g, s, lambda p, v_wide=v_wide: _bdot_nt(p, v_wide))

    @pl.when(j == nj - 1)
    def _():
        pad = jnp.zeros((LANES - t_len, HEAD_DIM), F32)
        bias = jnp.concatenate([mnew_ref[0]] * reps, axis=0)
        for g in range(N_KV):
            knew = jnp.concatenate([group_lanes(knew_ref[0], g), pad], axis=0)
            vnew = jnp.concatenate([group_lanes(vnew_ref[0], g), pad], axis=0)
            update(g, _bdot_nt(qg_ref[g], knew) + bias, lambda p, vnew=vnew: _bdot(p, vnew))
        o0 = acc_ref[0] / l_ref[0]
        o1 = acc_ref[1] / l_ref[1]
        for r in range(rpg):
            o_ref[0, :, r * LANES:(r + 1) * LANES] = jnp.concatenate(
                [o0[r * t_len:(r + 1) * t_len], o1[r * t_len:(r + 1) * t_len]], axis=-1)


def sdsa_attn(page_table, q, k_new, v_new, mask_past, mask_new, cache_k, cache_v):
    bsz, t_len, _ = q.shape
    n_pages = page_table.shape[1]
    kern = functools.partial(_sdsa_attn_kernel, t_len=t_len, n_pages=n_pages)
    tok = lambda w: pl.BlockSpec((1, t_len, w), lambda b_, j, pt: (b_, 0, 0))
    rows = (N_HEADS // N_KV) * t_len
    page_buf = pltpu.VMEM((PAGE_SLOTS, PAGES_PER_STEP, N_KV, HEAD_DIM, PAGE_SIZE), F32)
    return pl.pallas_call(
        kern,
        out_shape=jax.ShapeDtypeStruct((bsz, t_len, 512), F32),
        grid_spec=pltpu.PrefetchScalarGridSpec(
            num_scalar_prefetch=1,
            grid=(bsz, n_pages // PAGES_PER_STEP),
            in_specs=[tok(512), tok(LANES), tok(LANES),
                      pl.BlockSpec((1, PAGES_PER_STEP, t_len, LANES), lambda b_, j, pt: (b_, j, 0, 0)),
                      tok(LANES), pl.BlockSpec(memory_space=pl.ANY), pl.BlockSpec(memory_space=pl.ANY)],
            out_specs=pl.BlockSpec((1, t_len, 512), lambda b_, j, pt: (b_, 0, 0)),
            scratch_shapes=[pltpu.VMEM((N_KV, rows, HEAD_DIM), BF16), pltpu.VMEM((N_KV, rows, 1), F32),
                            pltpu.VMEM((N_KV, rows, 1), F32), pltpu.VMEM((N_KV, rows, HEAD_DIM), F32),
                            page_buf, page_buf,
                            pltpu.SemaphoreType.DMA((PAGE_SLOTS,)), pltpu.SemaphoreType.DMA((PAGE_SLOTS,))]),
        compiler_params=_cparams(("arbitrary", "arbitrary")),
        name="sdsa_attn",
    )(page_table, q, k_new, v_new, mask_past, mask_new, cache_k, cache_v)


HEAD_ORDER = (0, 4, 1, 5, 2, 6, 3, 7)


def _pad_cols(w, n):
    return jnp.pad(w, ((0, 0), (0, n - w.shape[1])))


def _row(v):
    return v.reshape(1, -1).astype(F32)


def _rope_tables(pos):
    half = ROT_DIM // 2
    inv = ROPE_THETA ** (-jnp.arange(half, dtype=F32) / half)
    ang = pos.astype(F32)[:, None] * inv[None, :]
    cos, sin = jnp.cos(ang), jnp.sin(ang)
    n = pos.shape[0]
    rest = HEAD_DIM - ROT_DIM
    c64 = jnp.concatenate([cos, cos, jnp.ones((n, rest), F32)], axis=1)
    s64 = jnp.concatenate([-sin, sin, jnp.zeros((n, rest), F32)], axis=1)
    return jnp.tile(c64, (1, 2)), jnp.tile(s64, (1, 2))


def _even_layer(x, prev_a, prev_ssm, prev_b, lc, p, *, tm, tl_a):
    bsz, L, d = x.shape
    m = bsz * L
    x2 = x.reshape(m, d)
    zmain, ztail = norm_proj(x2, p["e_norm_mix"], p["e_w_main"], p["e_w_tail"], tm=tm, tn=E_MAIN)
    zmain3 = zmain.reshape(bsz, L, E_MAIN)
    prev32 = jnp.pad(prev_a, ((0, 0), (CARRY_A - (CONV_A - 1), 0), (0, 0)))
    ya, new_a = conv_a(zmain3, prev32, p["e_conv_a_w"], p["e_conv_a_b"], p["e_ln_a_g"], p["e_ln_a_b"], tl=tl_a)
    prev8 = jnp.pad(prev_b, ((0, 0), (CARRY_B - (SSM_CONV - 1), 0), (0, 0)))
    yb, new_b, hout = ssd_mixer(zmain3, ztail.reshape(bsz, L, LANES), prev8,
                                prev_ssm.reshape(bsz, D_INNER, SSM_STATE),
                                p["e_conv_b_w"], p["e_conv_b_b"], p["e_dt_bias"], p["e_a_neg"], p["e_expand"],
                                p["e_dskip"], p["e_ssm_norm"], lc=lc)
    x2 = out_proj(x2, ya.reshape(m, D_A), yb.reshape(m, D_INNER), p["e_wo_a"], p["e_wo_b"], tm=min(m, 1024))
    x2 = swiglu_ffn(x2, p["e_norm_ffn"], p["e_w_gate"], p["e_w_up"], p["e_w_down"], tm=min(m, 512), tf=FF_TILE)
    return (x2.reshape(bsz, L, d), new_a[:, CARRY_A - (CONV_A - 1):], hout.reshape(bsz, SSM_HEADS, SSM_HEADDIM, SSM_STATE),
            new_b[:, CARRY_B - (SSM_CONV - 1):])


def _odd_project(x2, cos_t, sin_t, p, *, tm, seq_len=None):
    zmain, ztail = norm_proj(x2, p["o_norm_mix"], p["o_w_main"], p["o_w_tail"], tm=tm, tn=O_MAIN)
    q, k, v, qi, kiw, k16, v16, kk16, *kit = odd_post(zmain, ztail, cos_t, sin_t, p["o_q_norm"], p["o_k_norm"],
                                                      p["o_kidx_g"], p["o_kidx_b"], p["o_bd"], tm=tm,
                                                      seq_len=seq_len)
    return zmain, q, k, v, qi, kiw, (k16, v16, kk16, *kit)


def _odd_tail(x2, att, zmain3, prev_d, p, *, tm, tl_d):
    bsz, L, _ = zmain3.shape
    m = bsz * L
    prev8 = jnp.pad(prev_d, ((0, 0), (CARRY_D - (CONV_D - 1), 0), (0, 0)))
    dm, new_d = conv_d(zmain3, prev8, p["o_conv_d_w"], tl=tl_d)
    x2 = out_proj(x2, att.reshape(m, 512), dm.reshape(m, D_D), p["o_wo_a"], p["o_wo_b"], tm=min(m, 1024))
    if (2 * m) % MOE_TILE == 0 and 2 * m >= N_EXPERTS * MOE_TILE:
        x2 = moe_routed(x2, p["o_norm_ffn"], p["o_wr_hi"], p["o_wr_lo"], p["o_br"],
                        p["o_we_gate"], p["o_we_up"], p["o_we_down"], tm=tm)
    else:
        xn, comb = moe_router(x2, p["o_norm_ffn"], p["o_wr_hi"], p["o_wr_lo"], p["o_br"], tm=tm)
        x2 = moe_dense(x2, xn, comb, p["o_we_gate"], p["o_we_up"], p["o_we_down"], tm=min(m, 1024), tf=FF_TILE)
    return x2, new_d[:, CARRY_D - (CONV_D - 1):]


def kernel(x_prompt, x_sample, state_conv_a, state_ssm, state_conv_b, cache_k, cache_v, cache_kidx, state_conv_d,
           page_table, e_norm_mix, e_w_in, e_conv_a_w, e_conv_a_b, e_ln_a_g, e_ln_a_b, e_conv_b_w, e_conv_b_b,
           e_dt_bias, e_a_log, e_d_skip, e_ssm_norm, e_w_out, e_norm_ffn, e_w_gate, e_w_up, e_w_down,
           o_norm_mix, o_w_in, o_q_norm, o_k_norm, o_kidx_g, o_kidx_b, o_conv_d_w, o_w_out, o_norm_ffn,
           o_w_router, o_b_router, o_we_gate, o_we_up, o_we_down):
    bp, s_len, d = x_prompt.shape
    bd, t_len, _ = x_sample.shape
    n_pairs = e_w_in.shape[0]
    past = page_table.shape[1] * PAGE_SIZE
    n_pool = cache_k.shape[1]
    xp, xs = x_prompt, x_sample
    outs_p = [[] for _ in range(7)]
    outs_s = [[] for _ in range(7)]
    cos_p, sin_p = _rope_tables(jnp.arange(s_len))
    cos_s, sin_s = _rope_tables(jnp.tile(past + jnp.arange(t_len), bd))
    perm = np.concatenate([np.arange(h * HEAD_DIM, (h + 1) * HEAD_DIM) for h in HEAD_ORDER])
    expand = (jnp.arange(LANES)[:, None] == (jnp.arange(D_INNER)[None, :] // SSM_HEADDIM)).astype(BF16)
    blk = jnp.arange(LANES) // HEAD_DIM
    bdiag = jnp.where(blk[:, None] == blk[None, :], 1.0 / HEAD_DIM, 0.0).astype(BF16)
    for i in range(n_pairs):
        w = e_w_in[i]
        p = {
            "e_norm_mix": _row(e_norm_mix[i]),
            "e_w_main": jnp.concatenate([w[:, 2048:3584], w[:, 0:512], w[:, 1024:2048], w[:, 512:1024]],
                                        axis=1).astype(BF16),
            "e_w_tail": _pad_cols(w[:, 3584:3600], LANES).astype(BF16),
            "e_conv_a_w": e_conv_a_w[i], "e_conv_a_b": _row(e_conv_a_b[i]),
            "e_ln_a_g": _row(e_ln_a_g[i]), "e_ln_a_b": _row(e_ln_a_b[i]),
            "e_conv_b_w": e_conv_b_w[i], "e_conv_b_b": _row(e_conv_b_b[i]),
            "e_dt_bias": _pad_cols(_row(e_dt_bias[i]), LANES),
            "e_a_neg": _pad_cols(_row(-jnp.exp(e_a_log[i].astype(F32))), LANES),
            "e_expand": expand,
            "e_dskip": _row(jnp.repeat(e_d_skip[i], SSM_HEADDIM)),
            "e_ssm_norm": _row(e_ssm_norm[i]),
            "e_wo_a": e_w_out[i][:D_A].astype(BF16), "e_wo_b": e_w_out[i][D_A:].astype(BF16),
            "e_norm_ffn": _row(e_norm_ffn[i]),
            "e_w_gate": e_w_gate[i].astype(BF16), "e_w_up": e_w_up[i].astype(BF16),
            "e_w_down": e_w_down[i].astype(BF16),
        }
        zero_a = jnp.zeros((bp, CONV_A - 1, D_A), F32)
        zero_h = jnp.zeros((bp, SSM_HEADS, SSM_HEADDIM, SSM_STATE), F32)
        zero_b = jnp.zeros((bp, SSM_CONV - 1, CONV_DIM), F32)
        xp, ca, sm, cb = _even_layer(xp, zero_a, zero_h, zero_b, min(SSD_CHUNK, s_len), p, tm=512, tl_a=256)
        outs_p[0].append(ca); outs_p[1].append(sm); outs_p[2].append(cb)
        xs, ca, sm, cb = _even_layer(xs, state_conv_a[i], state_ssm[i], state_conv_b[i], t_len, p,
                                     tm=bd * t_len, tl_a=t_len)
        outs_s[0].append(ca); outs_s[1].append(sm); outs_s[2].append(cb)

        w = o_w_in[i]
        wo = o_w_out[i]
        p = {
            "o_norm_mix": _row(o_norm_mix[i]),
            "o_w_main": jnp.concatenate([w[:, 0:512][:, perm], w[:, 512:1024], w[:, 1092:2628]], axis=1).astype(BF16),
            "o_w_tail": _pad_cols(w[:, 1024:1092], LANES).astype(BF16),
            "o_q_norm": _row(jnp.tile(o_q_norm[i], 2)), "o_k_norm": _row(jnp.tile(o_k_norm[i], 2)),
            "o_kidx_g": _pad_cols(_row(o_kidx_g[i]), LANES), "o_kidx_b": _pad_cols(_row(o_kidx_b[i]), LANES),
            "o_bd": bdiag,
            "o_conv_d_w": o_conv_d_w[i],
            "o_wo_a": wo[:512][perm].astype(BF16), "o_wo_b": wo[512:].astype(BF16),
            "o_norm_ffn": _row(o_norm_ffn[i]),
            "o_br": jnp.concatenate([_row(o_b_router[i]), jnp.full((1, LANES - N_EXPERTS), NEG, F32)], axis=1),
            "o_we_gate": o_we_gate[i].astype(BF16), "o_we_up": o_we_up[i].astype(BF16),
            "o_we_down": o_we_down[i].astype(BF16),
        }
        wr = _pad_cols(o_w_router[i], LANES)
        p["o_wr_hi"] = wr.astype(BF16)
        p["o_wr_lo"] = (wr - p["o_wr_hi"].astype(F32)).astype(BF16)

        mp = bp * s_len
        x2 = xp.reshape(mp, d)
        zmain, q, k_t, v_t, qi, kiw, (k16, v16, kk16, ki_t) = _odd_project(x2, cos_p, sin_p, p, tm=512,
                                                                           seq_len=s_len)
        r3 = lambda a: a.reshape(bp, s_len, a.shape[-1])
        att = dsa_prompt(r3(q), r3(qi), r3(kiw), r3(k16), r3(v16), r3(kk16))
        x2, cd = _odd_tail(x2, att, r3(zmain), jnp.zeros((bp, CONV_D - 1, D_D), F32), p, tm=512, tl_d=512)
        xp = x2.reshape(bp, s_len, d)
        outs_p[3].append(jnp.transpose(k_t.reshape(bp, N_KV, HEAD_DIM, s_len), (0, 3, 1, 2)))
        outs_p[4].append(jnp.transpose(v_t.reshape(bp, N_KV, HEAD_DIM, s_len), (0, 3, 1, 2)))
        outs_p[5].append(jnp.transpose(ki_t, (0, 2, 1))); outs_p[6].append(cd)

        ms = bd * t_len
        x2 = xs.reshape(ms, d)
        zmain, q, k, v, qi, kiw, _ = _odd_project(x2, cos_s, sin_s, p, tm=ms)
        r3 = lambda a: a.reshape(bd, t_len, a.shape[-1])
        mask_past, mask_new = sdsa_index(page_table, r3(qi), r3(kiw), r3(kiw),
                                         jnp.transpose(cache_kidx[i], (0, 2, 1)))
        att = sdsa_attn(page_table, r3(q), r3(k), r3(v), mask_past, mask_new,
                        jnp.transpose(cache_k[i], (0, 2, 3, 1)), jnp.transpose(cache_v[i], (0, 2, 3, 1)))
        x2, cd = _odd_tail(x2, att, r3(zmain), state_conv_d[i], p, tm=ms, tl_d=t_len)
        xs = x2.reshape(bd, t_len, d)
        outs_s[3].append(k.reshape(bd, t_len, N_KV, HEAD_DIM)); outs_s[4].append(v.reshape(bd, t_len, N_KV, HEAD_DIM))
        outs_s[5].append(kiw[:, :IDX_DIM].reshape(bd, t_len, IDX_DIM)); outs_s[6].append(cd)
    return (xp, xs) + tuple(jnp.stack(o) for o in outs_p) + tuple(jnp.stack(o) for o in outs_s)
```

```python
import functools
import math

import jax
import jax.numpy as jnp
import numpy as np
from jax import lax
from jax.experimental import pallas as pl
from jax.experimental.pallas import tpu as pltpu

F32 = jnp.float32
BF16 = jnp.bfloat16
I32 = jnp.int32

D_MODEL = 1024
D_A = 512
CONV_A = 31
D_INNER = 1024
SSM_HEADDIM = 64
SSM_HEADS = 16
SSM_GROUPS = 2
SSM_STATE = 128
SSM_CONV = 4
CONV_DIM = D_INNER + 2 * SSM_GROUPS * SSM_STATE
SSD_CHUNK = 128
HEAD_DIM = 64
N_HEADS = 8
N_KV = 2
ROT_DIM = 16
ROPE_THETA = 500000.0
IDX_HEADS = 4
IDX_DIM = 64
TOPK_MAX = 256
Q_BLOCK = 128
D_D = 512
CONV_D = 3
D_FF = 2816
N_EXPERTS = 8
PAGE_SIZE = 128
EPS = 1e-6
NEG = -1e30

LANES = 128
SUBLANES = 8
VMEM_LIMIT = 56 * 1024 * 1024

E_MAIN = D_A + D_A + D_INNER + CONV_DIM
E_COL_XBC, E_COL_VAL, E_COL_Z, E_COL_GATE = 0, 1536, 2048, 3072
O_MAIN = 512 + 128 + 128 + 256 + 3 * D_D


def _cparams(sem):
    return pltpu.CompilerParams(dimension_semantics=sem, vmem_limit_bytes=VMEM_LIMIT)


def _bdot(a, b):
    return jnp.dot(a.astype(BF16), b.astype(BF16), preferred_element_type=F32)


def _bdot_nt(a, b):
    return lax.dot_general(a.astype(BF16), b.astype(BF16), (((1,), (1,)), ((), ())),
                           preferred_element_type=F32)


def _split3(a):
    hi = a.astype(BF16)
    r1 = a - hi.astype(F32)
    mid = r1.astype(BF16)
    lo = (r1 - mid.astype(F32)).astype(BF16)
    return hi, mid, lo


def _dot_exact_rhs(a, b_bf16):
    hi, mid, lo = _split3(a)
    out = jnp.dot(lo, b_bf16, preferred_element_type=F32)
    out = out + jnp.dot(mid, b_bf16, preferred_element_type=F32)
    return out + jnp.dot(hi, b_bf16, preferred_element_type=F32)


def _dot_exact_lhs(a_bf16, b):
    hi, mid, lo = _split3(b)
    out = jnp.dot(a_bf16, lo, preferred_element_type=F32)
    out = out + jnp.dot(a_bf16, mid, preferred_element_type=F32)
    return out + jnp.dot(a_bf16, hi, preferred_element_type=F32)


def _sigmoid(x):
    return 1.0 / (1.0 + jnp.exp(-x))


def _silu(x):
    return x * _sigmoid(x)


def _softplus(x):
    return jnp.maximum(x, 0.0) + jnp.log(1.0 + jnp.exp(-jnp.abs(x)))


def _norm_proj_kernel(x_ref, g_ref, w_ref, wt_ref, main_ref, tail_ref, xn_ref):
    j = pl.program_id(1)

    @pl.when(j == 0)
    def _():
        x = x_ref[...]
        y = x * lax.rsqrt(jnp.mean(x * x, axis=-1, keepdims=True) + EPS) * g_ref[...]
        xn = y.astype(BF16)
        xn_ref[...] = xn
        tail_ref[...] = jnp.dot(xn, wt_ref[...], preferred_element_type=F32)

    main_ref[...] = jnp.dot(xn_ref[...], w_ref[...], preferred_element_type=F32)


def norm_proj(x, g, w_main, w_tail, *, tm, tn):
    m, k = x.shape
    n = w_main.shape[1]
    return pl.pallas_call(
        _norm_proj_kernel,
        out_shape=(jax.ShapeDtypeStruct((m, n), F32), jax.ShapeDtypeStruct((m, LANES), F32)),
        grid=(m // tm, n // tn),
        in_specs=[pl.BlockSpec((tm, k), lambda i, j: (i, 0)),
                  pl.BlockSpec((1, k), lambda i, j: (0, 0)),
                  pl.BlockSpec((k, tn), lambda i, j: (0, j)),
                  pl.BlockSpec((k, LANES), lambda i, j: (0, 0))],
        out_specs=(pl.BlockSpec((tm, tn), lambda i, j: (i, j)),
                   pl.BlockSpec((tm, LANES), lambda i, j: (i, 0))),
        scratch_shapes=[pltpu.VMEM((tm, k), BF16)],
        compiler_params=_cparams(("arbitrary", "arbitrary")),
        name="norm_proj",
    )(x, g, w_main, w_tail)


def _dwconv_from_buf(buf_ref, w_ref, kw, carry, tl):
    acc = None
    for k in range(kw):
        start = carry - (kw - 1) + k
        term = buf_ref[pl.ds(start, tl), :] * w_ref[pl.ds(k, 1), :]
        acc = term if acc is None else acc + term
    return acc


def _dwconv_phased(buf_ref, w_ref, kw, carry, tl):
    base = carry - (kw - 1)
    tz = tl + 2 * SUBLANES
    acc = None
    for b in range(SUBLANES):
        z = None
        for a in range((kw - 1 - b) // SUBLANES + 1):
            term = buf_ref[pl.ds(SUBLANES * a, tz), :] * w_ref[pl.ds(SUBLANES * a + b, 1), :]
            z = term if z is None else z + term
        shifted = z[base + b:base + b + tl]
        acc = shifted if acc is None else acc + shifted
    return acc


PHASE_PAD = 8


CARRY_A = 32


def _conv_a_kernel(val_ref, gate_ref, prev_ref, w_ref, b_ref, lg_ref, lb_ref, ya_ref, new_ref, buf_ref, *, tl):
    l = pl.program_id(1)

    @pl.when(l == 0)
    def _():
        buf_ref[pl.ds(0, CARRY_A), :] = prev_ref[0]
        buf_ref[pl.ds(CARRY_A + tl, PHASE_PAD), :] = jnp.zeros((PHASE_PAD, D_A), F32)

    u = val_ref[0] * _sigmoid(gate_ref[0])
    buf_ref[pl.ds(CARRY_A, tl), :] = u
    y = _dwconv_phased(buf_ref, w_ref, CONV_A, CARRY_A, tl) + b_ref[...]
    mu = jnp.mean(y, axis=-1, keepdims=True)
    yc = y - mu
    var = jnp.mean(yc * yc, axis=-1, keepdims=True)
    yn = yc * lax.rsqrt(var + EPS) * lg_ref[...] + lb_ref[...]
    ya_ref[0] = _silu(yn)
    hist = buf_ref[pl.ds(tl, CARRY_A), :]
    new_ref[0] = hist
    buf_ref[pl.ds(0, CARRY_A), :] = hist


def conv_a(zmain, prev32, w, b, lg, lb, *, tl):
    bsz, L, _ = zmain.shape
    kern = functools.partial(_conv_a_kernel, tl=tl)
    return pl.pallas_call(
        kern,
        out_shape=(jax.ShapeDtypeStruct((bsz, L, D_A), F32), jax.ShapeDtypeStruct((bsz, CARRY_A, D_A), F32)),
        grid=(bsz, L // tl),
        in_specs=[pl.BlockSpec((1, tl, D_A), lambda b_, l: (b_, l, E_COL_VAL // D_A)),
                  pl.BlockSpec((1, tl, D_A), lambda b_, l: (b_, l, E_COL_GATE // D_A)),
                  pl.BlockSpec((1, CARRY_A, D_A), lambda b_, l: (b_, 0, 0)),
                  pl.BlockSpec((CONV_A, D_A), lambda b_, l: (0, 0)),
                  pl.BlockSpec((1, D_A), lambda b_, l: (0, 0)),
                  pl.BlockSpec((1, D_A), lambda b_, l: (0, 0)),
                  pl.BlockSpec((1, D_A), lambda b_, l: (0, 0))],
        out_specs=(pl.BlockSpec((1, tl, D_A), lambda b_, l: (b_, l, 0)),
                   pl.BlockSpec((1, CARRY_A, D_A), lambda b_, l: (b_, 0, 0))),
        scratch_shapes=[pltpu.VMEM((CARRY_A + tl + PHASE_PAD, D_A), F32)],
        compiler_params=_cparams(("arbitrary", "arbitrary")),
        name="conv_a",
    )(zmain, zmain, prev32, w, b, lg, lb)


CARRY_B = 8
SSD_LP = 128


def _ssd_kernel(xbc_ref, z_ref, dt_ref, prevb_ref, h0_ref, cw_ref, cb_ref, dtb_ref, a_ref, expand_ref,
                dskip_ref, nrm_ref, y_ref, newb_ref, hout_ref, buf_ref, ht_ref, *, lc):
    c = pl.program_id(1)
    nc = pl.num_programs(1)
    lp = SSD_LP

    @pl.when(c == 0)
    def _():
        buf_ref[pl.ds(0, CARRY_B), :] = prevb_ref[0]
        ht_ref[...] = h0_ref[0].T
        if lc < lp:
            buf_ref[pl.ds(CARRY_B + lc, lp - lc), :] = jnp.zeros((lp - lc, CONV_DIM), F32)

    buf_ref[pl.ds(CARRY_B, lc), :] = xbc_ref[0]
    xbc = _silu(_dwconv_from_buf(buf_ref, cw_ref, SSM_CONV, CARRY_B, lp) + cb_ref[...])
    hist = buf_ref[pl.ds(lc, CARRY_B), :]
    newb_ref[0] = hist
    buf_ref[pl.ds(0, CARRY_B), :] = hist

    xs = xbc[:, :D_INNER]
    if lc < lp:
        dt_raw = jnp.concatenate([dt_ref[0], jnp.zeros((lp - lc, LANES), F32)], axis=0)
        zg = jnp.concatenate([z_ref[0], jnp.zeros((lp - lc, D_INNER), F32)], axis=0)
    else:
        dt_raw = dt_ref[0]
        zg = z_ref[0]
    dt = _softplus(dt_raw + dtb_ref[...])
    if lc < lp:
        row = lax.broadcasted_iota(I32, (lp, LANES), 0)
        dt = jnp.where(row < lc, dt, 0.0)
    lane = lax.broadcasted_iota(I32, (lp, LANES), 1)
    dt = jnp.where(lane < SSM_HEADS, dt, 0.0)
    da = dt * a_ref[...]

    r_i = lax.broadcasted_iota(I32, (lp, lp), 0)
    c_i = lax.broadcasted_iota(I32, (lp, lp), 1)
    causal = r_i >= c_i
    tri = jnp.where(causal, 1.0, 0.0).astype(BF16)
    a_col = _dot_exact_lhs(tri, da)
    a_row = a_col.T
    dt_row = dt.T
    a_last = a_col[lp - 1:lp, :]

    expand = expand_ref[...]
    a_exp = _dot_exact_rhs(a_col, expand)
    dt_exp = _dot_exact_rhs(dt, expand)
    al_exp = _dot_exact_rhs(a_last, expand)

    ht = ht_ref[...]
    y_parts = []
    for g in range(SSM_GROUPS):
        bm = xbc[:, D_INNER + g * SSM_STATE: D_INNER + (g + 1) * SSM_STATE]
        cm = xbc[:, D_INNER + (SSM_GROUPS + g) * SSM_STATE: D_INNER + (SSM_GROUPS + g + 1) * SSM_STATE]
        cb = _bdot_nt(cm, bm)
        gs = slice(g * 512, (g + 1) * 512)
        y_inter = _bdot(cm, ht[:, gs]) * jnp.exp(a_exp[:, gs])
        heads = []
        for r in range(SSM_HEADS // SSM_GROUPS):
            h = g * (SSM_HEADS // SSM_GROUPS) + r
            seg = a_col[:, h:h + 1] - a_row[h:h + 1, :]
            decay = jnp.where(causal, jnp.exp(jnp.where(causal, seg, 0.0)), 0.0)
            wts = cb * decay * dt_row[h:h + 1, :]
            heads.append(_bdot(wts, xs[:, h * SSM_HEADDIM:(h + 1) * SSM_HEADDIM]))
        y_parts.append(jnp.concatenate(heads, axis=-1) + y_inter)
        xw = xs[:, gs] * (jnp.exp(al_exp[:, gs] - a_exp[:, gs]) * dt_exp[:, gs])
        ht_ref[:, gs] = ht[:, gs] * jnp.exp(al_exp[:, gs]) + _bdot(bm.T, xw)
    y = jnp.concatenate(y_parts, axis=-1)
    y = y + dskip_ref[...] * xs
    y = y * _silu(zg)
    outs = []
    for g in range(SSM_GROUPS):
        yg = y[:, g * 512:(g + 1) * 512]
        outs.append(yg * lax.rsqrt(jnp.mean(yg * yg, axis=-1, keepdims=True) + EPS))
    yb = jnp.concatenate(outs, axis=-1) * nrm_ref[...]
    y_ref[0] = yb[:lc, :]

    @pl.when(c == nc - 1)
    def _():
        hout_ref[0] = ht_ref[...].T


def ssd_mixer(zmain, ztail, prev8, h0, cw, cb, dtb, a_neg, expand, dskip_exp, nrm, *, lc):
    bsz, L, _ = zmain.shape
    kern = functools.partial(_ssd_kernel, lc=lc)
    return pl.pallas_call(
        kern,
        out_shape=(jax.ShapeDtypeStruct((bsz, L, D_INNER), F32),
                   jax.ShapeDtypeStruct((bsz, CARRY_B, CONV_DIM), F32),
                   jax.ShapeDtypeStruct((bsz, D_INNER, SSM_STATE), F32)),
        grid=(bsz, L // lc),
        in_specs=[pl.BlockSpec((1, lc, CONV_DIM), lambda b_, c: (b_, c, E_COL_XBC // CONV_DIM)),
                  pl.BlockSpec((1, lc, D_INNER), lambda b_, c: (b_, c, E_COL_Z // D_INNER)),
                  pl.BlockSpec((1, lc, LANES), lambda b_, c: (b_, c, 0)),
                  pl.BlockSpec((1, CARRY_B, CONV_DIM), lambda b_, c: (b_, 0, 0)),
                  pl.BlockSpec((1, D_INNER, SSM_STATE), lambda b_, c: (b_, 0, 0)),
                  pl.BlockSpec((SSM_CONV, CONV_DIM), lambda b_, c: (0, 0)),
                  pl.BlockSpec((1, CONV_DIM), lambda b_, c: (0, 0)),
                  pl.BlockSpec((1, LANES), lambda b_, c: (0, 0)),
                  pl.BlockSpec((1, LANES), lambda b_, c: (0, 0)),
                  pl.BlockSpec((LANES, D_INNER), lambda b_, c: (0, 0)),
                  pl.BlockSpec((1, D_INNER), lambda b_, c: (0, 0)),
                  pl.BlockSpec((1, D_INNER), lambda b_, c: (0, 0))],
        out_specs=(pl.BlockSpec((1, lc, D_INNER), lambda b_, c: (b_, c, 0)),
                   pl.BlockSpec((1, CARRY_B, CONV_DIM), lambda b_, c: (b_, 0, 0)),
                   pl.BlockSpec((1, D_INNER, SSM_STATE), lambda b_, c: (b_, 0, 0))),
        scratch_shapes=[pltpu.VMEM((CARRY_B + SSD_LP, CONV_DIM), F32),
                        pltpu.VMEM((SSM_STATE, D_INNER), F32)],
        compiler_params=_cparams(("arbitrary", "arbitrary")),
        name="ssd_mixer",
    )(zmain, zmain, ztail, prev8, h0, cw, cb, dtb, a_neg, expand, dskip_exp, nrm)


def _out_proj_kernel(x_ref, a_ref, b_ref, wa_ref, wb_ref, o_ref):
    acc = jnp.dot(a_ref[...].astype(BF16), wa_ref[...], preferred_element_type=F32)
    acc = acc + jnp.dot(b_ref[...].astype(BF16), wb_ref[...], preferred_element_type=F32)
    o_ref[...] = x_ref[...] + acc


def out_proj(x, a, b, wa, wb, *, tm):
    m, d = x.shape
    ka, kb = a.shape[1], b.shape[1]
    return pl.pallas_call(
        _out_proj_kernel,
        out_shape=jax.ShapeDtypeStruct((m, d), F32),
        grid=(m // tm,),
        in_specs=[pl.BlockSpec((tm, d), lambda i: (i, 0)),
                  pl.BlockSpec((tm, ka), lambda i: (i, 0)),
                  pl.BlockSpec((tm, kb), lambda i: (i, 0)),
                  pl.BlockSpec((ka, d), lambda i: (0, 0)),
                  pl.BlockSpec((kb, d), lambda i: (0, 0))],
        out_specs=pl.BlockSpec((tm, d), lambda i: (i, 0)),
        compiler_params=_cparams(("arbitrary",)),
        name="out_proj",
    )(x, a, b, wa, wb)


def _swiglu_kernel(x_ref, g_ref, wg_ref, wu_ref, wd_ref, o_ref, xn_ref, acc_ref):
    f = pl.program_id(1)
    nf = pl.num_programs(1)

    @pl.when(f == 0)
    def _():
        x = x_ref[...]
        y = x * lax.rsqrt(jnp.mean(x * x, axis=-1, keepdims=True) + EPS) * g_ref[...]
        xn_ref[...] = y.astype(BF16)
        acc_ref[...] = jnp.zeros_like(acc_ref)

    xn = xn_ref[...]
    gate = jnp.dot(xn, wg_ref[...], preferred_element_type=F32)
    up = jnp.dot(xn, wu_ref[...], preferred_element_type=F32)
    hid = (_silu(gate) * up).astype(BF16)
    acc_ref[...] += jnp.dot(hid, wd_ref[...], preferred_element_type=F32)

    @pl.when(f == nf - 1)
    def _():
        o_ref[...] = x_ref[...] + acc_ref[...]


def swiglu_ffn(x, g, wg, wu, wd, *, tm, tf):
    m, d = x.shape
    ff = wg.shape[1]
    return pl.pallas_call(
        _swiglu_kernel,
        out_shape=jax.ShapeDtypeStruct((m, d), F32),
        grid=(m // tm, ff // tf),
        in_specs=[pl.BlockSpec((tm, d), lambda i, f: (i, 0)),
                  pl.BlockSpec((1, d), lambda i, f: (0, 0)),
                  pl.BlockSpec((d, tf), lambda i, f: (0, f)),
                  pl.BlockSpec((d, tf), lambda i, f: (0, f)),
                  pl.BlockSpec((tf, d), lambda i, f: (f, 0))],
        out_specs=pl.BlockSpec((tm, d), lambda i, f: (i, 0)),
        scratch_shapes=[pltpu.VMEM((tm, d), BF16), pltpu.VMEM((tm, d), F32)],
        compiler_params=_cparams(("arbitrary", "arbitrary")),
        name="swiglu_ffn",
    )(x, g, wg, wu, wd)


O_COL_Q, O_COL_K, O_COL_V, O_COL_QI, O_COL_DIN, O_COL_BG, O_COL_CG = 0, 512, 640, 768, 1024, 1536, 2048


def _rope128(x, cos_t, sin_t, lo_mask):
    up = pltpu.roll(x, LANES - ROT_DIM // 2, 1)
    dn = pltpu.roll(x, ROT_DIM // 2, 1)
    return x * cos_t + jnp.where(lo_mask, up, dn) * sin_t


def _odd_post_kernel(qkvq_ref, tail_ref, cos_ref, sin_ref, qn_ref, kn_ref, kg_ref, kb_ref, bd_ref,
                     q_ref, k_ref, v_ref, qi_ref, kiw_ref, k16_ref, v16_ref, kk16_ref, kit_ref=None):
    cos_t = cos_ref[...]
    sin_t = sin_ref[...]
    lane = lax.broadcasted_iota(I32, cos_t.shape, 1)
    lo_mask = (lane % HEAD_DIM) < (ROT_DIM // 2)
    bd = bd_ref[...]
    z = qkvq_ref[...]
    for c in range(4):
        x = z[:, c * LANES:(c + 1) * LANES]
        ms = _dot_exact_rhs(x * x, bd)
        xn = x * lax.rsqrt(ms + EPS) * qn_ref[...]
        q_ref[:, c * LANES:(c + 1) * LANES] = _rope128(xn, cos_t, sin_t, lo_mask)
    x = z[:, O_COL_K:O_COL_K + LANES]
    ms = _dot_exact_rhs(x * x, bd)
    xn = x * lax.rsqrt(ms + EPS) * kn_ref[...]
    kr = _rope128(xn, cos_t, sin_t, lo_mask)
    k16_ref[...] = kr.astype(BF16)
    vv = z[:, O_COL_V:O_COL_V + LANES]
    v16_ref[...] = vv.astype(BF16)
    if kit_ref is None:
        k_ref[...] = kr
        v_ref[...] = vv
    else:
        k_ref[0] = kr.T
        v_ref[0] = vv.T
    for c in range(2):
        x = z[:, O_COL_QI + c * LANES: O_COL_QI + (c + 1) * LANES]
        qi_ref[:, c * LANES:(c + 1) * LANES] = _rope128(x, cos_t, sin_t, lo_mask)
    t = tail_ref[...]
    lane_t = lax.broadcasted_iota(I32, t.shape, 1)
    is_ki = lane_t < IDX_DIM
    mu = _dot_exact_rhs(t, bd)
    tc = t - mu
    var = _dot_exact_rhs(tc * tc, bd)
    kin = tc * lax.rsqrt(var + EPS) * kg_ref[...] + kb_ref[...]
    kin = _rope128(kin, cos_t, sin_t, lo_mask)
    kiw_ref[...] = jnp.where(is_ki, kin, t * (IDX_HEADS ** -0.5 * IDX_DIM ** -0.5))
    kk16_ref[...] = jnp.where(is_ki, kin, pltpu.roll(kin, IDX_DIM, 1)).astype(BF16)
    if kit_ref is not None:
        kit_ref[0] = kin.T[:IDX_DIM]


def odd_post(zmain, ztail, cos_t, sin_t, qn, kn, kg, kb, bd, *, tm, seq_len=None):
    m = zmain.shape[0]
    nrope = cos_t.shape[0] // tm
    row = lambda i: (i, 0)
    rope = lambda i: (i % nrope, 0)
    const = lambda i: (0, 0)
    tok = lambda w, dt=F32: (jax.ShapeDtypeStruct((m, w), dt), pl.BlockSpec((tm, w), row))
    if seq_len is None:
        kv = [tok(LANES), tok(LANES)]
        extra = []
    else:
        nblk = seq_len // tm
        bsz = m // seq_len
        seq = lambda f: (jax.ShapeDtypeStruct((bsz, f, seq_len), F32),
                         pl.BlockSpec((1, f, tm), lambda i: (i // nblk, 0, i % nblk)))
        kv = [seq(LANES), seq(LANES)]
        extra = [seq(IDX_DIM)]
    outs = [tok(512)] + kv + [tok(256), tok(LANES), tok(LANES, BF16), tok(LANES, BF16), tok(LANES, BF16)] + extra
    return pl.pallas_call(
        _odd_post_kernel,
        out_shape=tuple(o[0] for o in outs),
        grid=(m // tm,),
        in_specs=[pl.BlockSpec((tm, 1024), row), pl.BlockSpec((tm, LANES), row),
                  pl.BlockSpec((tm, LANES), rope), pl.BlockSpec((tm, LANES), rope),
                  pl.BlockSpec((1, LANES), const), pl.BlockSpec((1, LANES), const),
                  pl.BlockSpec((1, LANES), const), pl.BlockSpec((1, LANES), const),
                  pl.BlockSpec((LANES, LANES), const)],
        out_specs=tuple(o[1] for o in outs),
        compiler_params=_cparams(("arbitrary",)),
        name="odd_post",
    )(zmain, ztail, cos_t, sin_t, qn, kn, kg, kb, bd)


INT_MIN = -2 ** 31
F32_MIN_NORMAL_BITS = 0x00800000


def _code_to_float(c):
    return pltpu.bitcast(c ^ ((c >> 31) & 0x7FFFFFFF), F32)


def _kth_largest(count_ge, k, shape):
    ans = jnp.full(shape, INT_MIN, I32)
    cand = jnp.zeros(shape, I32)
    ans = jnp.where(count_ge(_code_to_float(cand)) >= k, cand, ans)

    def body(i, ans):
        cand = ans | jnp.left_shift(jnp.int32(1), 30 - i)
        return jnp.where(count_ge(_code_to_float(cand)) >= k, cand, ans)

    ans = lax.fori_loop(0, 31, body, ans)
    ans = jnp.where((ans > 0) & (ans < F32_MIN_NORMAL_BITS), 0, ans)
    return _code_to_float(ans)


def _sum_leading(x):
    n = x.shape[0]
    extra = None
    while n > 1:
        if n % 2:
            extra = x[n - 1] if extra is None else extra + x[n - 1]
            n -= 1
        x = x[:n // 2] + x[n // 2:n]
        n //= 2
    return x[0] if extra is None else x[0] + extra


def _upper_tri_bf16(n):
    r = lax.broadcasted_iota(I32, (n, n), 0)
    c = lax.broadcasted_iota(I32, (n, n), 1)
    return jnp.where(r <= c, 1.0, 0.0).astype(BF16)


def _dsa_prompt_kernel(q_ref, qi_ref, wi_ref, k_ref, v_ref, kk_ref, o_ref, keys_ref, bias_ref, *,
                       q_base, s_eff, topk, qb):
    t0 = q_base + pl.program_id(1) * qb
    lane = lax.broadcasted_iota(I32, (qb, LANES), 1)
    row = lax.broadcasted_iota(I32, (qb, LANES), 0)
    lo = lane < HEAD_DIM
    hi = lane >= HEAD_DIM
    rpg = N_HEADS // N_KV

    kk = kk_ref[0]
    qi = qi_ref[0]
    wi = wi_ref[0]
    sc = None
    for h in range(IDX_HEADS):
        x = qi[:, (h // 2) * LANES:(h // 2 + 1) * LANES]
        x = jnp.where(lo if h % 2 == 0 else hi, x, 0.0)
        d = _bdot_nt(x, kk)
        term = jnp.maximum(d, 0.0) * wi[:, IDX_DIM + h:IDX_DIM + h + 1]
        sc = term if sc is None else sc + term
    kpos = lax.broadcasted_iota(I32, (qb, s_eff), 1)
    qpos = t0 + lax.broadcasted_iota(I32, (qb, s_eff), 0)
    adm_all = kpos <= qpos
    sc = jnp.where(sc == 0.0, 0.0, sc)
    keys_ref[...] = jnp.where(adm_all, sc, NEG)

    def count_ge(cand):
        acc = None
        for j in range(s_eff // LANES):
            hit = jnp.where(keys_ref[:, j * LANES:(j + 1) * LANES] >= cand, 1.0, 0.0)
            acc = hit if acc is None else acc + hit
        return jnp.sum(acc, axis=-1, keepdims=True)

    thr = _kth_largest(count_ge, float(topk), (qb, 1))
    n_ge = count_ge(thr)
    n_adm = t0 + 1 + lax.broadcasted_iota(I32, (qb, 1), 0)
    simple = jnp.where((n_ge == float(topk)) | (n_adm < topk), 0.0, 1.0)
    no_cut = jnp.max(simple) == 0.0

    @pl.when(no_cut)
    def _():
        bias_ref[...] = jnp.where(adm_all, jnp.where(keys_ref[...] >= thr, 0.0, NEG), NEG)

    @pl.when(jnp.logical_not(no_cut))
    def _():
        n_gt = jnp.sum(jnp.where(keys_ref[...] > thr, 1.0, 0.0), axis=-1, keepdims=True)
        need = float(topk) - n_gt
        ut = _upper_tri_bf16(LANES)
        off = jnp.zeros((qb, 1), F32)
        for j in range(s_eff // LANES):
            kj = keys_ref[:, j * LANES:(j + 1) * LANES]
            eq = kj == thr
            pre = jnp.dot(jnp.where(eq, 1.0, 0.0).astype(BF16), ut, preferred_element_type=F32) + off
            adm = (j * LANES + lane) <= (t0 + row)
            keep = jnp.where(eq, jnp.where(pre <= need, 0.0, NEG), jnp.where(kj > thr, 0.0, NEG))
            bias_ref[:, j * LANES:(j + 1) * LANES] = jnp.where(adm, keep, NEG)
            off = pre[:, LANES - 1:LANES]

    kb = k_ref[0]
    vb = v_ref[0]
    lane_b = lax.broadcasted_iota(I32, (Q_BLOCK, LANES), 1)
    lo_b = lane_b < HEAD_DIM
    hi_b = lane_b >= HEAD_DIM
    lo4 = lax.broadcasted_iota(I32, (rpg * Q_BLOCK, LANES), 1) < HEAD_DIM
    for sub in range(qb // Q_BLOCK):
        rows = pl.ds(sub * Q_BLOCK, Q_BLOCK)
        bias = bias_ref[rows, :]
        q = q_ref[0, rows, :] * (HEAD_DIM ** -0.5)
        outs = []
        for g in range(N_KV):
            keep = lo_b if g == 0 else hi_b
            qg = jnp.concatenate([jnp.where(keep, q[:, r * LANES:(r + 1) * LANES], 0.0) for r in range(rpg)],
                                 axis=0)
            s = _bdot_nt(qg, kb).reshape(rpg, Q_BLOCK, s_eff) + bias[None]
            m = jnp.max(s, axis=-1, keepdims=True)
            e = jnp.exp(s - m)
            l = jnp.sum(e, axis=-1, keepdims=True).reshape(rpg * Q_BLOCK, 1)
            pv = jnp.dot(e.reshape(rpg * Q_BLOCK, s_eff).astype(BF16), vb, preferred_element_type=F32)
            outs.append(pv / l)
        oc = jnp.where(lo4, outs[0], outs[1])
        for r in range(rpg):
            o_ref[0, rows, r * LANES:(r + 1) * LANES] = oc[r * Q_BLOCK:(r + 1) * Q_BLOCK, :]


DSA_SEGMENT = 256


def dsa_prompt(q, qi, wi, k16, v16, kk16):
    bsz, s_len, _ = q.shape
    topk = min(TOPK_MAX, s_len // 4)
    seg = min(DSA_SEGMENT, s_len)
    outs = []
    for si in range(s_len // seg):
        s_eff = (si + 1) * seg
        kern = functools.partial(_dsa_prompt_kernel, q_base=si * seg, s_eff=s_eff, topk=topk, qb=seg)
        qblk = lambda w, si=si: pl.BlockSpec((1, seg, w), lambda b_, i: (b_, si, 0))
        keys = pl.BlockSpec((1, s_eff, LANES), lambda b_, i: (b_, 0, 0))
        outs.append(pl.pallas_call(
            kern,
            out_shape=jax.ShapeDtypeStruct((bsz, seg, 512), F32),
            grid=(bsz, 1),
            in_specs=[qblk(512), qblk(256), qblk(LANES), keys, keys, keys],
            out_specs=pl.BlockSpec((1, seg, 512), lambda b_, i: (b_, 0, 0)),
            scratch_shapes=[pltpu.VMEM((seg, s_eff), F32), pltpu.VMEM((seg, s_eff), F32)],
            compiler_params=_cparams(("arbitrary", "arbitrary")),
            name=f"dsa_prompt_{si}",
        )(q, qi, wi, k16, v16, kk16))
    return jnp.concatenate(outs, axis=1)


CARRY_D = 8


def _conv_d_kernel(din_ref, bg_ref, cg_ref, prev_ref, w_ref, y_ref, new_ref, buf_ref, *, tl):
    l = pl.program_id(1)

    @pl.when(l == 0)
    def _():
        buf_ref[pl.ds(0, CARRY_D), :] = prev_ref[0]

    buf_ref[pl.ds(CARRY_D, tl), :] = cg_ref[0] * din_ref[0]
    y_ref[0] = bg_ref[0] * _dwconv_from_buf(buf_ref, w_ref, CONV_D, CARRY_D, tl)
    hist = buf_ref[pl.ds(tl, CARRY_D), :]
    new_ref[0] = hist
    buf_ref[pl.ds(0, CARRY_D), :] = hist


def conv_d(zmain, prev8, w, *, tl):
    bsz, L, _ = zmain.shape
    kern = functools.partial(_conv_d_kernel, tl=tl)
    col = lambda c: pl.BlockSpec((1, tl, D_D), lambda b_, l: (b_, l, c // D_D))
    return pl.pallas_call(
        kern,
        out_shape=(jax.ShapeDtypeStruct((bsz, L, D_D), F32), jax.ShapeDtypeStruct((bsz, CARRY_D, D_D), F32)),
        grid=(bsz, L // tl),
        in_specs=[col(O_COL_DIN), col(O_COL_BG), col(O_COL_CG),
                  pl.BlockSpec((1, CARRY_D, D_D), lambda b_, l: (b_, 0, 0)),
                  pl.BlockSpec((CONV_D, D_D), lambda b_, l: (0, 0))],
        out_specs=(pl.BlockSpec((1, tl, D_D), lambda b_, l: (b_, l, 0)),
                   pl.BlockSpec((1, CARRY_D, D_D), lambda b_, l: (b_, 0, 0))),
        scratch_shapes=[pltpu.VMEM((CARRY_D + tl, D_D), F32)],
        compiler_params=_cparams(("arbitrary", "arbitrary")),
        name="conv_d",
    )(zmain, zmain, zmain, prev8, w)


def _router_kernel(x_ref, g_ref, whi_ref, wlo_ref, br_ref, xn_ref, comb_ref):
    x = x_ref[...]
    y = x * lax.rsqrt(jnp.mean(x * x, axis=-1, keepdims=True) + EPS) * g_ref[...]
    yhi = y.astype(BF16)
    xn_ref[...] = yhi
    ylo = (y - yhi.astype(F32)).astype(BF16)
    logits = jnp.dot(ylo, whi_ref[...], preferred_element_type=F32)
    logits = logits + jnp.dot(yhi, wlo_ref[...], preferred_element_type=F32)
    logits = logits + jnp.dot(yhi, whi_ref[...], preferred_element_type=F32) + br_ref[...]
    lane = lax.broadcasted_iota(I32, logits.shape, 1).astype(F32)
    m1 = jnp.max(logits, axis=-1, keepdims=True)
    i1 = jnp.min(jnp.where(logits == m1, lane, float(LANES)), axis=-1, keepdims=True)
    rest = jnp.where(lane == i1, -3e38, logits)
    m2 = jnp.max(rest, axis=-1, keepdims=True)
    i2 = jnp.min(jnp.where(rest == m2, lane, float(LANES)), axis=-1, keepdims=True)
    e2 = jnp.exp(m2 - m1)
    g1 = 1.0 / (1.0 + e2)
    g2 = e2 / (1.0 + e2)
    comb_ref[...] = jnp.where(lane == i1, g1, 0.0) + jnp.where(lane == i2, g2, 0.0)


def moe_router(x, g, w_hi, w_lo, b_pad, *, tm):
    m, d = x.shape
    return pl.pallas_call(
        _router_kernel,
        out_shape=(jax.ShapeDtypeStruct((m, d), BF16), jax.ShapeDtypeStruct((m, LANES), F32)),
        grid=(m // tm,),
        in_specs=[pl.BlockSpec((tm, d), lambda i: (i, 0)),
                  pl.BlockSpec((1, d), lambda i: (0, 0)),
                  pl.BlockSpec((d, LANES), lambda i: (0, 0)),
                  pl.BlockSpec((d, LANES), lambda i: (0, 0)),
                  pl.BlockSpec((1, LANES), lambda i: (0, 0))],
        out_specs=(pl.BlockSpec((tm, d), lambda i: (i, 0)), pl.BlockSpec((tm, LANES), lambda i: (i, 0))),
        compiler_params=_cparams(("arbitrary",)),
        name="moe_router",
    )(x, g, w_hi, w_lo, b_pad)


def _moe_dense_kernel(x_ref, xn_ref, comb_ref, wg_ref, wu_ref, wd_ref, o_ref, acc_ref, acce_ref):
    e = pl.program_id(1)
    f = pl.program_id(2)
    ne = pl.num_programs(1)
    nf = pl.num_programs(2)

    @pl.when((e == 0) & (f == 0))
    def _():
        acc_ref[...] = jnp.zeros_like(acc_ref)

    @pl.when(f == 0)
    def _():
        acce_ref[...] = jnp.zeros_like(acce_ref)

    xn = xn_ref[...]
    gate = jnp.dot(xn, wg_ref[0], preferred_element_type=F32)
    up = jnp.dot(xn, wu_ref[0], preferred_element_type=F32)
    hid = (_silu(gate) * up).astype(BF16)
    acce_ref[...] += jnp.dot(hid, wd_ref[0], preferred_element_type=F32)

    @pl.when(f == nf - 1)
    def _():
        comb = comb_ref[...]
        lane = lax.broadcasted_iota(I32, comb.shape, 1)
        c = jnp.sum(jnp.where(lane == e, comb, 0.0), axis=-1, keepdims=True)
        acc_ref[...] += c * acce_ref[...]

    @pl.when((e == ne - 1) & (f == nf - 1))
    def _():
        o_ref[...] = x_ref[...] + acc_ref[...]


def moe_dense(x, xn, comb, wg, wu, wd, *, tm, tf):
    m, d = x.shape
    ne, _, ff = wg.shape
    return pl.pallas_call(
        _moe_dense_kernel,
        out_shape=jax.ShapeDtypeStruct((m, d), F32),
        grid=(m // tm, ne, ff // tf),
        in_specs=[pl.BlockSpec((tm, d), lambda i, e, f: (i, 0)),
                  pl.BlockSpec((tm, d), lambda i, e, f: (i, 0)),
                  pl.BlockSpec((tm, LANES), lambda i, e, f: (i, 0)),
                  pl.BlockSpec((1, d, tf), lambda i, e, f: (e, 0, f)),
                  pl.BlockSpec((1, d, tf), lambda i, e, f: (e, 0, f)),
                  pl.BlockSpec((1, tf, d), lambda i, e, f: (e, f, 0))],
        out_specs=pl.BlockSpec((tm, d), lambda i, e, f: (i, 0)),
        scratch_shapes=[pltpu.VMEM((tm, d), F32), pltpu.VMEM((tm, d), F32)],
        compiler_params=_cparams(("arbitrary", "arbitrary", "arbitrary")),
        name="moe_dense",
    )(x, xn, comb, wg, wu, wd)


MOE_TILE = 512
DMA_ISSUE_UNROLL = 8
FF_TILE = D_FF // 2
R_I1, R_I2, R_G1, R_G2, R_RANK1, R_RANK2 = 0, 1, 2, 3, 4, 5


def _proj_router_rank_kernel(xin_ref, a_ref, b_ref, wa_ref, wb_ref, g_ref, whi_ref, wlo_ref, br_ref,
                             x_ref, info_ref, cnt_ref, run_ref):
    i = pl.program_id(0)

    @pl.when(i == 0)
    def _():
        run_ref[...] = jnp.zeros_like(run_ref)

    proj = jnp.dot(a_ref[...].astype(BF16), wa_ref[...], preferred_element_type=F32)
    proj = proj + jnp.dot(b_ref[...].astype(BF16), wb_ref[...], preferred_element_type=F32)
    x = xin_ref[...] + proj
    x_ref[...] = x
    tm = x.shape[0]
    y = x * lax.rsqrt(jnp.mean(x * x, axis=-1, keepdims=True) + EPS) * g_ref[...]
    yhi = y.astype(BF16)
    ylo = (y - yhi.astype(F32)).astype(BF16)
    logits = jnp.dot(ylo, whi_ref[...], preferred_element_type=F32)
    logits = logits + jnp.dot(yhi, wlo_ref[...], preferred_element_type=F32)
    logits = logits + jnp.dot(yhi, whi_ref[...], preferred_element_type=F32) + br_ref[...]
    lane = lax.broadcasted_iota(I32, logits.shape, 1).astype(F32)
    m1 = jnp.max(logits, axis=-1, keepdims=True)
    i1 = jnp.min(jnp.where(logits == m1, lane, float(LANES)), axis=-1, keepdims=True)
    rest = jnp.where(lane == i1, -3e38, logits)
    m2 = jnp.max(rest, axis=-1, keepdims=True)
    i2 = jnp.min(jnp.where(rest == m2, lane, float(LANES)), axis=-1, keepdims=True)
    e2 = jnp.exp(m2 - m1)
    g1 = 1.0 / (1.0 + e2)
    g2 = e2 / (1.0 + e2)
    oh1 = jnp.where(lane == i1, 1.0, 0.0)
    oh2 = jnp.where(lane == i2, 1.0, 0.0)
    both = oh1 + oh2
    r_i = lax.broadcasted_iota(I32, (tm, tm), 0)
    c_i = lax.broadcasted_iota(I32, (tm, tm), 1)
    strict_lower = jnp.where(c_i < r_i, 1.0, 0.0).astype(BF16)
    before = jnp.dot(strict_lower, both.astype(BF16), preferred_element_type=F32) + run_ref[...]
    rank1 = jnp.sum(before * oh1, axis=-1, keepdims=True)
    rank2 = jnp.sum(before * oh2, axis=-1, keepdims=True)
    run_ref[...] += jnp.sum(both, axis=0, keepdims=True)
    info = jnp.where(lane == R_I1, i1, 0.0)
    for col, val in ((R_I2, i2), (R_G1, g1), (R_G2, g2), (R_RANK1, rank1), (R_RANK2, rank2)):
        info = jnp.where(lane == col, val, info)
    info_ref[...] = info
    cnt_ref[...] = run_ref[...]


def proj_router_rank(x, a, b, wa, wb, g, w_hi, w_lo, b_pad, *, tm):
    m, d = x.shape
    ka, kb = a.shape[1], b.shape[1]
    row = lambda w: pl.BlockSpec((tm, w), lambda i: (i, 0))
    const = lambda r, c: pl.BlockSpec((r, c), lambda i: (0, 0))
    return pl.pallas_call(
        _proj_router_rank_kernel,
        out_shape=(jax.ShapeDtypeStruct((m, d), F32), jax.ShapeDtypeStruct((m, LANES), F32),
                   jax.ShapeDtypeStruct((1, LANES), F32)),
        grid=(m // tm,),
        in_specs=[row(d), row(ka), row(kb), const(ka, d), const(kb, d),
                  const(1, d), const(d, LANES), const(d, LANES), const(1, LANES)],
        out_specs=(row(d), row(LANES), const(1, LANES)),
        scratch_shapes=[pltpu.VMEM((1, LANES), F32)],
        compiler_params=_cparams(("arbitrary",)),
        name="proj_router_rank",
    )(x, a, b, wa, wb, g, w_hi, w_lo, b_pad)


def _row_copy(src_ref, si, dst_ref, di, sem):
    return pltpu.make_async_copy(src_ref.at[pl.ds(si, 1)], dst_ref.at[pl.ds(di, 1)], sem)


def _moe_dispatch_kernel(d1_ref, d2_ref, x_ref, xs_ref, sem, *, ct):
    def issue(t, carry):
        _row_copy(x_ref, t, xs_ref, d1_ref[0, 0, t], sem).start(priority=0)
        _row_copy(x_ref, t, xs_ref, d2_ref[0, 0, t], sem).start(priority=1)
        return carry

    lax.fori_loop(0, ct, issue, 0, unroll=DMA_ISSUE_UNROLL)
    tile_copy = pltpu.make_async_copy(x_ref, xs_ref.at[pl.ds(0, ct)], sem)
    tile_copy.wait()
    tile_copy.wait()


def moe_dispatch(x, dest1, dest2, *, ct):
    m, d = x.shape
    smem_row = pl.BlockSpec((1, 1, ct), lambda i: (i, 0, 0), memory_space=pltpu.SMEM)
    return pl.pallas_call(
        functools.partial(_moe_dispatch_kernel, ct=ct),
        out_shape=jax.ShapeDtypeStruct((2 * m, d), F32),
        grid=(m // ct,),
        in_specs=[smem_row, smem_row, pl.BlockSpec((ct, d), lambda i: (i, 0))],
        out_specs=pl.BlockSpec(memory_space=pl.ANY),
        scratch_shapes=[pltpu.SemaphoreType.DMA],
        compiler_params=_cparams(("arbitrary",)),
        name="moe_dispatch",
    )(dest1, dest2, x)


def _moe_grouped_kernel(tile_ref, exp_ref, first_ref, valid_ref, gs_ref, xs_ref, g_ref, wg_ref, wu_ref, wd_ref,
                        ys_ref, xn_ref, acc_ref):
    w = pl.program_id(0)
    f = pl.program_id(1)
    nf = pl.num_programs(1)

    @pl.when(valid_ref[w] == 1)
    def _():
        @pl.when(f == 0)
        def _():
            x = xs_ref[...]
            y = x * lax.rsqrt(jnp.mean(x * x, axis=-1, keepdims=True) + EPS) * g_ref[...]
            xn_ref[...] = y.astype(BF16)
            acc_ref[...] = jnp.zeros_like(acc_ref)

        xn = xn_ref[...]
        gate = jnp.dot(xn, wg_ref[0], preferred_element_type=F32)
        up = jnp.dot(xn, wu_ref[0], preferred_element_type=F32)
        hid = (_silu(gate) * up).astype(BF16)
        acc_ref[...] += jnp.dot(hid, wd_ref[0], preferred_element_type=F32)

        @pl.when(f == nf - 1)
        def _():
            e = exp_ref[w]
            row = tile_ref[w] * MOE_TILE + lax.broadcasted_iota(I32, acc_ref.shape, 0)
            mine = (row >= gs_ref[e]) & (row < gs_ref[e + 1])
            part = jnp.where(mine, acc_ref[...], 0.0)

            @pl.when(first_ref[w] == 1)
            def _():
                ys_ref[...] = part

            @pl.when(first_ref[w] == 0)
            def _():
                ys_ref[...] += part


def moe_grouped(item_tile, item_expert, item_first, item_valid, group_starts, xs, g, wg, wu, wd, *, tf):
    r, d = xs.shape
    ff = wg.shape[2]
    n_items = item_tile.shape[0]
    return pl.pallas_call(
        _moe_grouped_kernel,
        out_shape=jax.ShapeDtypeStruct((r, d), F32),
        grid_spec=pltpu.PrefetchScalarGridSpec(
            num_scalar_prefetch=5,
            grid=(n_items, ff // tf),
            in_specs=[pl.BlockSpec((MOE_TILE, d), lambda w, f, it, ie, i1, iv, gs: (it[w], 0)),
                      pl.BlockSpec((1, d), lambda w, f, it, ie, i1, iv, gs: (0, 0)),
                      pl.BlockSpec((1, d, tf), lambda w, f, it, ie, i1, iv, gs: (ie[w], 0, f)),
                      pl.BlockSpec((1, d, tf), lambda w, f, it, ie, i1, iv, gs: (ie[w], 0, f)),
                      pl.BlockSpec((1, tf, d), lambda w, f, it, ie, i1, iv, gs: (ie[w], f, 0))],
            out_specs=pl.BlockSpec((MOE_TILE, d), lambda w, f, it, ie, i1, iv, gs: (it[w], 0)),
            scratch_shapes=[pltpu.VMEM((MOE_TILE, d), BF16), pltpu.VMEM((MOE_TILE, d), F32)]),
        compiler_params=_cparams(("arbitrary", "arbitrary")),
        name="moe_grouped",
    )(item_tile, item_expert, item_first, item_valid, group_starts, xs, g, wg, wu, wd)


def _moe_combine_kernel(d1_ref, d2_ref, x_ref, info_ref, ys_ref, o_ref, y1_ref, y2_ref, sem, *, ct):
    def issue(t, carry):
        _row_copy(ys_ref, d1_ref[0, 0, t], y1_ref, t, sem).start(priority=0)
        _row_copy(ys_ref, d2_ref[0, 0, t], y2_ref, t, sem).start(priority=1)
        return carry

    lax.fori_loop(0, ct, issue, 0, unroll=DMA_ISSUE_UNROLL)
    pltpu.make_async_copy(ys_ref.at[pl.ds(0, ct)], y1_ref, sem).wait()
    pltpu.make_async_copy(ys_ref.at[pl.ds(0, ct)], y2_ref, sem).wait()
    info = info_ref[...]
    g1 = info[:, R_G1:R_G1 + 1]
    g2 = info[:, R_G2:R_G2 + 1]
    o_ref[...] = x_ref[...] + (g1 * y1_ref[...] + g2 * y2_ref[...])


def moe_combine(x, info, ys, dest1, dest2, *, ct):
    m, d = x.shape
    smem_row = pl.BlockSpec((1, 1, ct), lambda i: (i, 0, 0), memory_space=pltpu.SMEM)
    return pl.pallas_call(
        functools.partial(_moe_combine_kernel, ct=ct),
        out_shape=jax.ShapeDtypeStruct((m, d), F32),
        grid=(m // ct,),
        in_specs=[smem_row, smem_row, pl.BlockSpec((ct, d), lambda i: (i, 0)),
                  pl.BlockSpec((ct, LANES), lambda i: (i, 0)), pl.BlockSpec(memory_space=pl.ANY)],
        out_specs=pl.BlockSpec((ct, d), lambda i: (i, 0)),
        scratch_shapes=[pltpu.VMEM((ct, d), F32), pltpu.VMEM((ct, d), F32), pltpu.SemaphoreType.DMA],
        compiler_params=_cparams(("arbitrary",)),
        name="moe_combine",
    )(dest1, dest2, x, info, ys)


def _moe_work_items(counts, n_tiles):
    n_exp = counts.shape[0]
    n_items = n_tiles + n_exp - 1
    ends = jnp.cumsum(counts)
    starts = ends - counts
    expert_of_row = lambda rr: jnp.minimum(jnp.sum(ends[None, :] <= rr[:, None], axis=1), n_exp - 1).astype(I32)
    t = jnp.arange(n_tiles, dtype=I32)
    e_lo = expert_of_row(t * MOE_TILE)
    e_hi = expert_of_row(t * MOE_TILE + (MOE_TILE - 1))
    per_tile = e_hi - e_lo + 1
    item_end = jnp.cumsum(per_tile)
    item_start = item_end - per_tile
    total = item_end[-1]
    w = jnp.minimum(jnp.arange(n_items, dtype=I32), total - 1)
    tile = jnp.sum(item_end[None, :] <= w[:, None], axis=1).astype(I32)
    expert = (e_lo[tile] + (w - item_start[tile])).astype(I32)
    first = (w == item_start[tile]).astype(I32)
    valid = (jnp.arange(n_items, dtype=I32) < total).astype(I32)
    group_starts = jnp.concatenate([starts, ends[-1:]]).astype(I32)
    return tile, expert, first, valid, group_starts


def moe_routed(x_in, a, b, wa, wb, g, w_hi, w_lo, b_pad, wg, wu, wd, *, tm):
    m, d = x_in.shape
    n_exp = wg.shape[0]
    assert (2 * m) % MOE_TILE == 0
    x, info, cnt = proj_router_rank(x_in, a, b, wa, wb, g, w_hi, w_lo, b_pad, tm=tm)
    counts = cnt[0, :n_exp].astype(I32)
    starts = jnp.cumsum(counts) - counts
    i1 = info[:, R_I1].astype(I32)
    i2 = info[:, R_I2].astype(I32)
    dest1 = (starts[i1] + info[:, R_RANK1].astype(I32)).reshape(m // tm, 1, tm)
    dest2 = (starts[i2] + info[:, R_RANK2].astype(I32)).reshape(m // tm, 1, tm)
    xs = moe_dispatch(x, dest1, dest2, ct=tm)
    items = _moe_work_items(counts, (2 * m) // MOE_TILE)
    ys = moe_grouped(*items, xs, g, wg, wu, wd, tf=FF_TILE)
    return moe_combine(x, info, ys, dest1, dest2, ct=tm)


PAGES_PER_STEP = 64
PAGE_SLOTS = 3
SDSA_GROUP = 4
COUNT_STREAMS = 4


def _fetch_pages(pt_ref, cache_ref, buf_ref, sem_ref, step, nj):
    b = step // nj
    j0 = (step % nj) * PAGES_PER_STEP
    slot = step % PAGE_SLOTS
    for i in range(PAGES_PER_STEP):
        pltpu.make_async_copy(cache_ref.at[pt_ref[b, j0 + i]], buf_ref.at[slot, i], sem_ref.at[slot]).start()


def _wait_pages(cache_ref, buf_ref, sem_ref, step):
    slot = step % PAGE_SLOTS
    pltpu.make_async_copy(cache_ref.at[pl.ds(0, PAGES_PER_STEP)], buf_ref.at[slot], sem_ref.at[slot]).wait()


def _page_pipeline_step(pt_ref, streams, nj):
    step = pl.program_id(0) * pl.num_programs(1) + pl.program_id(1)
    total = pl.num_programs(0) * pl.num_programs(1)

    ahead = PAGE_SLOTS - 1

    @pl.when(step == 0)
    def _():
        for d in range(ahead):
            @pl.when(d < total)
            def _():
                for cache_ref, buf_ref, sem_ref in streams:
                    _fetch_pages(pt_ref, cache_ref, buf_ref, sem_ref, step + d, nj)

    @pl.when(step + ahead < total)
    def _():
        for cache_ref, buf_ref, sem_ref in streams:
            _fetch_pages(pt_ref, cache_ref, buf_ref, sem_ref, step + ahead, nj)

    for cache_ref, buf_ref, sem_ref in streams:
        _wait_pages(cache_ref, buf_ref, sem_ref, step)
    return step % PAGE_SLOTS


def _sdsa_index_kernel(pt_ref, qi_ref, wi_ref, kknew_ref, cache_ref, mpast_ref, mnew_ref,
                       keys_ref, pre_ref, pbuf_ref, psem_ref, *, n_pages, t_len, topk, nb):
    nj = n_pages // PAGES_PER_STEP
    lb = pl.program_id(1) // nj
    j = pl.program_id(1) % nj
    rows = pl.ds(pl.multiple_of(lb * t_len, t_len), t_len)
    last_step = pl.program_id(1) == nb * nj - 1
    slot = _page_pipeline_step(pt_ref, [(cache_ref, pbuf_ref, psem_ref)], nj)
    lane = lax.broadcasted_iota(I32, (t_len, LANES), 1)
    row = lax.broadcasted_iota(I32, (t_len, LANES), 0)
    lo = lane < HEAD_DIM

    qi = qi_ref[0]
    wi = wi_ref[0]
    qs, ws = [], []
    for h in range(IDX_HEADS):
        x = qi[:, (h // 2) * LANES:(h // 2 + 1) * LANES]
        if h % 2 == 1:
            x = pltpu.roll(x, HEAD_DIM, 1)
        qs.append(x[:, :IDX_DIM])
        ws.append(wi[:, IDX_DIM + h:IDX_DIM + h + 1])
    q32 = jnp.concatenate(qs, axis=0).astype(BF16)
    w32 = jnp.concatenate(ws, axis=0)

    def scores(dots):
        d = jnp.maximum(dots, 0.0) * w32
        s = d[0:t_len]
        for h in range(1, IDX_HEADS):
            s = s + d[h * t_len:(h + 1) * t_len]
        return jnp.where(s == 0.0, 0.0, s)

    pages_t = jnp.concatenate([pbuf_ref[slot, i].astype(BF16) for i in range(PAGES_PER_STEP)], axis=1)
    s_wide = scores(jnp.dot(q32, pages_t, preferred_element_type=F32))
    for i in range(PAGES_PER_STEP):
        keys_ref[j * PAGES_PER_STEP + i, rows, :] = s_wide[:, i * PAGE_SIZE:(i + 1) * PAGE_SIZE]

    @pl.when(j == nj - 1)
    def _():
        knew = jnp.concatenate([kknew_ref[0][:, :IDX_DIM], jnp.zeros((LANES - t_len, IDX_DIM), F32)], axis=0)
        s_new = jnp.where((lane <= row) & (lane < t_len), scores(_bdot_nt(q32, knew)), NEG)
        keys_ref[n_pages, rows, :] = s_new

    @pl.when(last_step)
    def _():
        nrow = nb * t_len
        own = lambda x, g: x[..., g * t_len:(g + 1) * t_len, :]

        def count_ge(cand):
            accs = [None] * COUNT_STREAMS
            for p in range(n_pages + 1):
                hit = jnp.where(keys_ref[p] >= cand, 1.0, 0.0)
                accs[p % COUNT_STREAMS] = hit if accs[p % COUNT_STREAMS] is None else accs[p % COUNT_STREAMS] + hit
            part = accs[0]
            for a in accs[1:]:
                part = part + a
            return jnp.sum(part, axis=-1, keepdims=True)

        thr = _kth_largest(count_ge, float(topk), (nrow, 1))
        n_ge = count_ge(thr)
        no_cut = jnp.max(jnp.abs(n_ge - float(topk))) == 0.0

        @pl.when(no_cut)
        def _():
            bias = jnp.where(keys_ref[...] >= thr, 0.0, NEG)
            for g in range(nb):
                mpast_ref[g] = own(bias[:n_pages], g)
                mnew_ref[g] = own(bias[n_pages], g)

        @pl.when(jnp.logical_not(no_cut))
        def _():
            keys = keys_ref[...]
            n_gt = jnp.sum(_sum_leading(jnp.where(keys > thr, 1.0, 0.0)), axis=-1, keepdims=True)
            need = float(topk) - n_gt
            eqf = jnp.where(keys == thr, 1.0, 0.0).astype(BF16).reshape((n_pages + 1) * nrow, LANES)
            pre_ref[...] = jnp.dot(eqf, _upper_tri_bf16(LANES), preferred_element_type=F32).reshape(
                n_pages + 1, nrow, LANES)

            def body(p, off):
                kp = keys_ref[p]
                pre = pre_ref[p]
                bias = jnp.where(kp == thr, jnp.where((pre + off) <= need, 0.0, NEG),
                                 jnp.where(kp > thr, 0.0, NEG))

                @pl.when(p < n_pages)
                def _():
                    for g in range(nb):
                        mpast_ref[g, p] = own(bias, g)

                @pl.when(p == n_pages)
                def _():
                    for g in range(nb):
                        mnew_ref[g] = own(bias, g)

                return off + pre[:, LANES - 1:LANES]

            lax.fori_loop(0, n_pages + 1, body, jnp.zeros((nrow, 1), F32))


def sdsa_index(page_table, qi, wi, kk_new, cache_kidx):
    bsz, t_len, _ = qi.shape
    n_pages = page_table.shape[1]
    topk = min(TOPK_MAX, (n_pages * PAGE_SIZE + t_len) // 4)
    nb = math.gcd(bsz, SDSA_GROUP)
    nj = n_pages // PAGES_PER_STEP
    kern = functools.partial(_sdsa_index_kernel, n_pages=n_pages, t_len=t_len, topk=topk, nb=nb)
    tok = lambda w: pl.BlockSpec((1, t_len, w), lambda gi, jj, pt: (gi * nb + jj // nj, 0, 0))
    return pl.pallas_call(
        kern,
        out_shape=(jax.ShapeDtypeStruct((bsz, n_pages, t_len, LANES), F32),
                   jax.ShapeDtypeStruct((bsz, t_len, LANES), F32)),
        grid_spec=pltpu.PrefetchScalarGridSpec(
            num_scalar_prefetch=1,
            grid=(bsz // nb, nb * nj),
            in_specs=[tok(256), tok(LANES), tok(LANES), pl.BlockSpec(memory_space=pl.ANY)],
            out_specs=(pl.BlockSpec((nb, n_pages, t_len, LANES), lambda gi, jj, pt: (gi, 0, 0, 0)),
                       pl.BlockSpec((nb, t_len, LANES), lambda gi, jj, pt: (gi, 0, 0))),
            scratch_shapes=[pltpu.VMEM((n_pages + 1, nb * t_len, LANES), F32),
                            pltpu.VMEM((n_pages + 1, nb * t_len, LANES), F32),
                            pltpu.VMEM((PAGE_SLOTS, PAGES_PER_STEP, IDX_DIM, PAGE_SIZE), F32),
                            pltpu.SemaphoreType.DMA((PAGE_SLOTS,))]),
        compiler_params=_cparams(("arbitrary", "arbitrary")),
        name="sdsa_index",
    )(page_table, qi, wi, kk_new, cache_kidx)


def _sdsa_attn_kernel(pt_ref, q_ref, knew_ref, vnew_ref, mpast_ref, mnew_ref, ck_ref, cv_ref,
                      o_ref, qg_ref, m_ref, l_ref, acc_ref, kbuf_ref, vbuf_ref, ksem_ref, vsem_ref, *, t_len, n_pages):
    j = pl.program_id(1)
    nj = pl.num_programs(1)
    slot = _page_pipeline_step(pt_ref, [(ck_ref, kbuf_ref, ksem_ref), (cv_ref, vbuf_ref, vsem_ref)],
                               n_pages // PAGES_PER_STEP)
    rpg = N_HEADS // N_KV
    reps = rpg

    def group_lanes(x, g):
        if g == 1:
            x = pltpu.roll(x, HEAD_DIM, 1)
        return x[:, :HEAD_DIM]

    @pl.when(j == 0)
    def _():
        q = q_ref[0]
        for g in range(N_KV):
            parts = [group_lanes(q[:, r * LANES:(r + 1) * LANES], g) for r in range(rpg)]
            qg_ref[g] = (jnp.concatenate(parts, axis=0) * (HEAD_DIM ** -0.5)).astype(BF16)
        m_ref[...] = jnp.full(m_ref.shape, NEG, F32)
        l_ref[...] = jnp.zeros(l_ref.shape, F32)
        acc_ref[...] = jnp.zeros(acc_ref.shape, F32)

    def update(g, s, pv):
        m_old = m_ref[g]
        m_cur = jnp.maximum(m_old, jnp.max(s, axis=-1, keepdims=True))
        alpha = jnp.exp(m_old - m_cur)
        p = jnp.exp(s - m_cur)
        m_ref[g] = m_cur
        l_ref[g] = l_ref[g] * alpha + jnp.sum(p, axis=-1, keepdims=True)
        acc_ref[g] = acc_ref[g] * alpha + pv(p.astype(BF16))

    bias8 = jnp.concatenate([mpast_ref[0, i] for i in range(PAGES_PER_STEP)], axis=1)
    bias_wide = jnp.concatenate([bias8] * reps, axis=0)
    for g in range(N_KV):
        k_wide = jnp.concatenate([kbuf_ref[slot, i, g].astype(BF16) for i in range(PAGES_PER_STEP)], axis=1)
        v_wide = jnp.concatenate([vbuf_ref[slot, i, g].astype(BF16) for i in range(PAGES_PER_STEP)], axis=1)
        s = jnp.dot(qg_ref[g], k_wide, preferred_element_type=F32) + bias_wide
        update(g, s, lambda p, v_wide=v_wide: _bdot_nt(p, v_wide))

    @pl.when(j == nj - 1)
    def _():
        pad = jnp.zeros((LANES - t_len, HEAD_DIM), F32)
        bias = jnp.concatenate([mnew_ref[0]] * reps, axis=0)
        for g in range(N_KV):
            knew = jnp.concatenate([group_lanes(knew_ref[0], g), pad], axis=0)
            vnew = jnp.concatenate([group_lanes(vnew_ref[0], g), pad], axis=0)
            update(g, _bdot_nt(qg_ref[g], knew) + bias, lambda p, vnew=vnew: _bdot(p, vnew))
        o0 = acc_ref[0] / l_ref[0]
        o1 = acc_ref[1] / l_ref[1]
        for r in range(rpg):
            o_ref[0, :, r * LANES:(r + 1) * LANES] = jnp.concatenate(
                [o0[r * t_len:(r + 1) * t_len], o1[r * t_len:(r + 1) * t_len]], axis=-1)


def sdsa_attn(page_table, q, k_new, v_new, mask_past, mask_new, cache_k, cache_v):
    bsz, t_len, _ = q.shape
    n_pages = page_table.shape[1]
    kern = functools.partial(_sdsa_attn_kernel, t_len=t_len, n_pages=n_pages)
    tok = lambda w: pl.BlockSpec((1, t_len, w), lambda b_, j, pt: (b_, 0, 0))
    rows = (N_HEADS // N_KV) * t_len
    page_buf = pltpu.VMEM((PAGE_SLOTS, PAGES_PER_STEP, N_KV, HEAD_DIM, PAGE_SIZE), F32)
    return pl.pallas_call(
        kern,
        out_shape=jax.ShapeDtypeStruct((bsz, t_len, 512), F32),
        grid_spec=pltpu.PrefetchScalarGridSpec(
            num_scalar_prefetch=1,
            grid=(bsz, n_pages // PAGES_PER_STEP),
            in_specs=[tok(512), tok(LANES), tok(LANES),
                      pl.BlockSpec((1, PAGES_PER_STEP, t_len, LANES), lambda b_, j, pt: (b_, j, 0, 0)),
                      tok(LANES), pl.BlockSpec(memory_space=pl.ANY), pl.BlockSpec(memory_space=pl.ANY)],
            out_specs=pl.BlockSpec((1, t_len, 512), lambda b_, j, pt: (b_, 0, 0)),
            scratch_shapes=[pltpu.VMEM((N_KV, rows, HEAD_DIM), BF16), pltpu.VMEM((N_KV, rows, 1), F32),
                            pltpu.VMEM((N_KV, rows, 1), F32), pltpu.VMEM((N_KV, rows, HEAD_DIM), F32),
                            page_buf, page_buf,
                            pltpu.SemaphoreType.DMA((PAGE_SLOTS,)), pltpu.SemaphoreType.DMA((PAGE_SLOTS,))]),
        compiler_params=_cparams(("arbitrary", "arbitrary")),
        name="sdsa_attn",
    )(page_table, q, k_new, v_new, mask_past, mask_new, cache_k, cache_v)


HEAD_ORDER = (0, 4, 1, 5, 2, 6, 3, 7)


def _pad_cols(w, n):
    return jnp.pad(w, ((0, 0), (0, n - w.shape[1])))


def _row(v):
    return v.reshape(1, -1).astype(F32)


def _rope_tables(pos):
    half = ROT_DIM // 2
    inv = ROPE_THETA ** (-jnp.arange(half, dtype=F32) / half)
    ang = pos.astype(F32)[:, None] * inv[None, :]
    cos, sin = jnp.cos(ang), jnp.sin(ang)
    n = pos.shape[0]
    rest = HEAD_DIM - ROT_DIM
    c64 = jnp.concatenate([cos, cos, jnp.ones((n, rest), F32)], axis=1)
    s64 = jnp.concatenate([-sin, sin, jnp.zeros((n, rest), F32)], axis=1)
    return jnp.tile(c64, (1, 2)), jnp.tile(s64, (1, 2))


def _even_layer(x, prev_a, prev_ssm, prev_b, lc, p, *, tm, tl_a):
    bsz, L, d = x.shape
    m = bsz * L
    x2 = x.reshape(m, d)
    zmain, ztail = norm_proj(x2, p["e_norm_mix"], p["e_w_main"], p["e_w_tail"], tm=tm, tn=E_MAIN)
    zmain3 = zmain.reshape(bsz, L, E_MAIN)
    prev32 = jnp.pad(prev_a, ((0, 0), (CARRY_A - (CONV_A - 1), 0), (0, 0)))
    ya, new_a = conv_a(zmain3, prev32, p["e_conv_a_w"], p["e_conv_a_b"], p["e_ln_a_g"], p["e_ln_a_b"], tl=tl_a)
    prev8 = jnp.pad(prev_b, ((0, 0), (CARRY_B - (SSM_CONV - 1), 0), (0, 0)))
    yb, new_b, hout = ssd_mixer(zmain3, ztail.reshape(bsz, L, LANES), prev8,
                                prev_ssm.reshape(bsz, D_INNER, SSM_STATE),
                                p["e_conv_b_w"], p["e_conv_b_b"], p["e_dt_bias"], p["e_a_neg"], p["e_expand"],
                                p["e_dskip"], p["e_ssm_norm"], lc=lc)
    x2 = out_proj(x2, ya.reshape(m, D_A), yb.reshape(m, D_INNER), p["e_wo_a"], p["e_wo_b"], tm=min(m, 1024))
    x2 = swiglu_ffn(x2, p["e_norm_ffn"], p["e_w_gate"], p["e_w_up"], p["e_w_down"], tm=min(m, 512), tf=FF_TILE)
    return (x2.reshape(bsz, L, d), new_a[:, CARRY_A - (CONV_A - 1):], hout.reshape(bsz, SSM_HEADS, SSM_HEADDIM, SSM_STATE),
            new_b[:, CARRY_B - (SSM_CONV - 1):])


def _odd_project(x2, cos_t, sin_t, p, *, tm, seq_len=None):
    zmain, ztail = norm_proj(x2, p["o_norm_mix"], p["o_w_main"], p["o_w_tail"], tm=tm, tn=O_MAIN)
    q, k, v, qi, kiw, k16, v16, kk16, *kit = odd_post(zmain, ztail, cos_t, sin_t, p["o_q_norm"], p["o_k_norm"],
                                                      p["o_kidx_g"], p["o_kidx_b"], p["o_bd"], tm=tm,
                                                      seq_len=seq_len)
    return zmain, q, k, v, qi, kiw, (k16, v16, kk16, *kit)


def _odd_tail(x2, att, zmain3, prev_d, p, *, tm, tl_d):
    bsz, L, _ = zmain3.shape
    m = bsz * L
    prev8 = jnp.pad(prev_d, ((0, 0), (CARRY_D - (CONV_D - 1), 0), (0, 0)))
    dm, new_d = conv_d(zmain3, prev8, p["o_conv_d_w"], tl=tl_d)
    if (2 * m) % MOE_TILE == 0 and 2 * m >= N_EXPERTS * MOE_TILE:
        x2 = moe_routed(x2, att.reshape(m, 512), dm.reshape(m, D_D), p["o_wo_a"], p["o_wo_b"],
                        p["o_norm_ffn"], p["o_wr_hi"], p["o_wr_lo"], p["o_br"],
                        p["o_we_gate"], p["o_we_up"], p["o_we_down"], tm=tm)
    else:
        x2 = out_proj(x2, att.reshape(m, 512), dm.reshape(m, D_D), p["o_wo_a"], p["o_wo_b"], tm=min(m, 1024))
        xn, comb = moe_router(x2, p["o_norm_ffn"], p["o_wr_hi"], p["o_wr_lo"], p["o_br"], tm=tm)
        x2 = moe_dense(x2, xn, comb, p["o_we_gate"], p["o_we_up"], p["o_we_down"], tm=min(m, 1024), tf=FF_TILE)
    return x2, new_d[:, CARRY_D - (CONV_D - 1):]


def kernel(x_prompt, x_sample, state_conv_a, state_ssm, state_conv_b, cache_k, cache_v, cache_kidx, state_conv_d,
           page_table, e_norm_mix, e_w_in, e_conv_a_w, e_conv_a_b, e_ln_a_g, e_ln_a_b, e_conv_b_w, e_conv_b_b,
           e_dt_bias, e_a_log, e_d_skip, e_ssm_norm, e_w_out, e_norm_ffn, e_w_gate, e_w_up, e_w_down,
           o_norm_mix, o_w_in, o_q_norm, o_k_norm, o_kidx_g, o_kidx_b, o_conv_d_w, o_w_out, o_norm_ffn,
           o_w_router, o_b_router, o_we_gate, o_we_up, o_we_down):
    bp, s_len, d = x_prompt.shape
    bd, t_len, _ = x_sample.shape
    n_pairs = e_w_in.shape[0]
    past = page_table.shape[1] * PAGE_SIZE
    n_pool = cache_k.shape[1]
    xp, xs = x_prompt, x_sample
    outs_p = [[] for _ in range(7)]
    outs_s = [[] for _ in range(7)]
    cos_p, sin_p = _rope_tables(jnp.arange(s_len))
    cos_s, sin_s = _rope_tables(jnp.tile(past + jnp.arange(t_len), bd))
    perm = np.concatenate([np.arange(h * HEAD_DIM, (h + 1) * HEAD_DIM) for h in HEAD_ORDER])
    expand = (jnp.arange(LANES)[:, None] == (jnp.arange(D_INNER)[None, :] // SSM_HEADDIM)).astype(BF16)
    blk = jnp.arange(LANES) // HEAD_DIM
    bdiag = jnp.where(blk[:, None] == blk[None, :], 1.0 / HEAD_DIM, 0.0).astype(BF16)
    for i in range(n_pairs):
        w = e_w_in[i]
        p = {
            "e_norm_mix": _row(e_norm_mix[i]),
            "e_w_main": jnp.concatenate([w[:, 2048:3584], w[:, 0:512], w[:, 1024:2048], w[:, 512:1024]],
                                        axis=1).astype(BF16),
            "e_w_tail": _pad_cols(w[:, 3584:3600], LANES).astype(BF16),
            "e_conv_a_w": e_conv_a_w[i], "e_conv_a_b": _row(e_conv_a_b[i]),
            "e_ln_a_g": _row(e_ln_a_g[i]), "e_ln_a_b": _row(e_ln_a_b[i]),
            "e_conv_b_w": e_conv_b_w[i], "e_conv_b_b": _row(e_conv_b_b[i]),
            "e_dt_bias": _pad_cols(_row(e_dt_bias[i]), LANES),
            "e_a_neg": _pad_cols(_row(-jnp.exp(e_a_log[i].astype(F32))), LANES),
            "e_expand": expand,
            "e_dskip": _row(jnp.repeat(e_d_skip[i], SSM_HEADDIM)),
            "e_ssm_norm": _row(e_ssm_norm[i]),
            "e_wo_a": e_w_out[i][:D_A].astype(BF16), "e_wo_b": e_w_out[i][D_A:].astype(BF16),
            "e_norm_ffn": _row(e_norm_ffn[i]),
            "e_w_gate": e_w_gate[i].astype(BF16), "e_w_up": e_w_up[i].astype(BF16),
            "e_w_down": e_w_down[i].astype(BF16),
        }
        zero_a = jnp.zeros((bp, CONV_A - 1, D_A), F32)
        zero_h = jnp.zeros((bp, SSM_HEADS, SSM_HEADDIM, SSM_STATE), F32)
        zero_b = jnp.zeros((bp, SSM_CONV - 1, CONV_DIM), F32)
        xp, ca, sm, cb = _even_layer(xp, zero_a, zero_h, zero_b, min(SSD_CHUNK, s_len), p, tm=512, tl_a=256)
        outs_p[0].append(ca); outs_p[1].append(sm); outs_p[2].append(cb)
        xs, ca, sm, cb = _even_layer(xs, state_conv_a[i], state_ssm[i], state_conv_b[i], t_len, p,
                                     tm=bd * t_len, tl_a=t_len)
        outs_s[0].append(ca); outs_s[1].append(sm); outs_s[2].append(cb)

        w = o_w_in[i]
        wo = o_w_out[i]
        p = {
            "o_norm_mix": _row(o_norm_mix[i]),
            "o_w_main": jnp.concatenate([w[:, 0:512][:, perm], w[:, 512:1024], w[:, 1092:2628]], axis=1).astype(BF16),
            "o_w_tail": _pad_cols(w[:, 1024:1092], LANES).astype(BF16),
            "o_q_norm": _row(jnp.tile(o_q_norm[i], 2)), "o_k_norm": _row(jnp.tile(o_k_norm[i], 2)),
            "o_kidx_g": _pad_cols(_row(o_kidx_g[i]), LANES), "o_kidx_b": _pad_cols(_row(o_kidx_b[i]), LANES),
            "o_bd": bdiag,
            "o_conv_d_w": o_conv_d_w[i],
            "o_wo_a": wo[:512][perm].astype(BF16), "o_wo_b": wo[512:].astype(BF16),
            "o_norm_ffn": _row(o_norm_ffn[i]),
            "o_br": jnp.concatenate([_row(o_b_router[i]), jnp.full((1, LANES - N_EXPERTS), NEG, F32)], axis=1),
            "o_we_gate": o_we_gate[i].astype(BF16), "o_we_up": o_we_up[i].astype(BF16),
            "o_we_down": o_we_down[i].astype(BF16),
        }
        wr = _pad_cols(o_w_router[i], LANES)
        p["o_wr_hi"] = wr.astype(BF16)
        p["o_wr_lo"] = (wr - p["o_wr_hi"].astype(F32)).astype(BF16)

        mp = bp * s_len
        x2 = xp.reshape(mp, d)
        zmain, q, k_t, v_t, qi, kiw, (k16, v16, kk16, ki_t) = _odd_project(x2, cos_p, sin_p, p, tm=512,
                                                                           seq_len=s_len)
        r3 = lambda a: a.reshape(bp, s_len, a.shape[-1])
        att = dsa_prompt(r3(q), r3(qi), r3(kiw), r3(k16), r3(v16), r3(kk16))
        x2, cd = _odd_tail(x2, att, r3(zmain), jnp.zeros((bp, CONV_D - 1, D_D), F32), p, tm=512, tl_d=512)
        xp = x2.reshape(bp, s_len, d)
        outs_p[3].append(jnp.transpose(k_t.reshape(bp, N_KV, HEAD_DIM, s_len), (0, 3, 1, 2)))
        outs_p[4].append(jnp.transpose(v_t.reshape(bp, N_KV, HEAD_DIM, s_len), (0, 3, 1, 2)))
        outs_p[5].append(jnp.transpose(ki_t, (0, 2, 1))); outs_p[6].append(cd)

        ms = bd * t_len
        x2 = xs.reshape(ms, d)
        zmain, q, k, v, qi, kiw, _ = _odd_project(x2, cos_s, sin_s, p, tm=ms)
        r3 = lambda a: a.reshape(bd, t_len, a.shape[-1])
        mask_past, mask_new = sdsa_index(page_table, r3(qi), r3(kiw), r3(kiw),
                                         jnp.transpose(cache_kidx[i], (0, 2, 1)))
        att = sdsa_attn(page_table, r3(q), r3(k), r3(v), mask_past, mask_new,
                        jnp.transpose(cache_k[i], (0, 2, 3, 1)), jnp.transpose(cache_v[i], (0, 2, 3, 1)))
        x2, cd = _odd_tail(x2, att, r3(zmain), state_conv_d[i], p, tm=ms, tl_d=t_len)
        xs = x2.reshape(bd, t_len, d)
        outs_s[3].append(k.reshape(bd, t_len, N_KV, HEAD_DIM)); outs_s[4].append(v.reshape(bd, t_len, N_KV, HEAD_DIM))
        outs_s[5].append(kiw[:, :IDX_DIM].reshape(bd, t_len, IDX_DIM)); outs_s[6].append(cd)
    return (xp, xs) + tuple(jnp.stack(o) for o in outs_p) + tuple(jnp.stack(o) for o in outs_s)
```
